```python
import math
import jax, jax.numpy as jnp
from jax import lax
import numpy as np

D_MODEL = 1024
BATCH = 8
SEQ = 2048
DEPTH = 1
DEC_BATCH = 128
DEC_SEQ = 4
PAST_LEN = 2048
PAGE_SIZE = 128

RWKV_HEAD = 64
D_RWKV = D_MODEL
RWKV_HEADS = D_RWKV // RWKV_HEAD
R_DECAY = 64
R_ICL = 64
LNX_EPS = 64e-5
ATT_HEAD = 64
D_ATT = D_MODEL
ATT_HEADS = D_ATT // ATT_HEAD
ATT_KV_HEADS = 2
ATT_GROUP = ATT_HEADS // ATT_KV_HEADS
IDX_HEADS = 8
IDX_DIM = 64
TOPK_MAX = 256
QBLOCK = 128
REL_BUCKETS = 32
REL_MAX_DIST = 128
NORM_EPS = 1e-6
POOL_FACTOR = 1.25

RWKV_SIZES = (D_RWKV, D_RWKV, D_RWKV, D_RWKV, R_DECAY, R_ICL)
RWKV_COLS = sum(RWKV_SIZES)
ATT_SIZES = (D_ATT, ATT_KV_HEADS * ATT_HEAD, ATT_KV_HEADS * ATT_HEAD,
             IDX_HEADS * IDX_DIM, IDX_DIM, IDX_HEADS, D_ATT)
ATT_COLS = sum(ATT_SIZES)
GATE_SIZES = (D_MODEL, D_MODEL)
N_COLS = RWKV_COLS + ATT_COLS + sum(GATE_SIZES)

kernel_name = 'rwkv7_dsa_gated_hybrid_step'


def _split(z, sizes):
    idx = np.cumsum(sizes)[:-1].tolist()
    return jnp.split(z, idx, axis=-1)


def _rmsnorm(x, g, eps):
    xf = x.astype(jnp.float32)
    y = xf * lax.rsqrt(jnp.mean(xf * xf, axis=-1, keepdims=True) + eps)
    return (y * g.astype(jnp.float32)).astype(x.dtype)


def _t5_bucket(dist):
    max_exact = REL_BUCKETS // 2
    d = jnp.maximum(dist, 0)
    df = jnp.maximum(d, 1).astype(jnp.float32)
    large = max_exact + (jnp.log(df / max_exact) / math.log(REL_MAX_DIST / max_exact)
                         * (REL_BUCKETS - max_exact)).astype(jnp.int32)
    large = jnp.minimum(large, REL_BUCKETS - 1)
    return jnp.where(d < max_exact, d, large)


def _wkv7_scan(r, w, k, v, kk, a, s0):
    def step(s, inp):
        r_t, w_t, k_t, v_t, kk_t, a_t = inp
        sa = jnp.einsum('bhij,bhj->bhi', s, -kk_t)
        s = (s * w_t[:, :, None, :] + sa[..., None] * (kk_t * a_t)[:, :, None, :]
             + v_t[..., None] * k_t[:, :, None, :])
        return s, jnp.einsum('bhij,bhj->bhi', s, r_t)
    xs = tuple(jnp.swapaxes(t, 0, 1) for t in (r, w, k, v, kk, a))
    s_T, out = lax.scan(step, s0, xs)
    return jnp.swapaxes(out, 0, 1), s_T


def _rwkv_branch(zr, shift_prev, s0, mu, w0, w2, a0, a2, k_k, k_a, r_k, lnx_g, lnx_b):
    B, T, _ = zr.shape
    f32 = jnp.float32
    prev = jnp.concatenate([shift_prev[:, None, :].astype(zr.dtype), zr[:, :-1]], axis=1)
    zs = zr + (prev - zr) * mu
    r, k, v, g, wd, ad = _split(zs, RWKV_SIZES)
    heads = lambda t: t.astype(f32).reshape(B, T, RWKV_HEADS, RWKV_HEAD)
    w_log = -jax.nn.softplus(-(w0 + jnp.tanh(wd) @ w2).astype(f32)) - 0.5
    decay = jnp.exp(-jnp.exp(w_log))
    a = jax.nn.sigmoid((a0 + ad @ a2).astype(f32))
    kk = heads(k * k_k)
    kk = kk / jnp.maximum(jnp.sqrt(jnp.sum(kk * kk, axis=-1, keepdims=True)), 1e-12)
    k_mod = heads(k.astype(f32) * (1.0 + (a - 1.0) * k_a.astype(f32)))
    r_h, v_h, a_h = heads(r), heads(v), heads(a)
    out, s_T = _wkv7_scan(r_h, heads(decay), k_mod, v_h, kk, a_h, s0.astype(f32))
    mean = jnp.mean(out, axis=-1, keepdims=True)
    var = jnp.mean(jnp.square(out - mean), axis=-1, keepdims=True)
    o = ((out - mean) * lax.rsqrt(var + LNX_EPS)).reshape(B, T, D_RWKV)
    o = o * lnx_g.astype(f32) + lnx_b.astype(f32)
    bonus = jnp.sum(r_h * k_mod * r_k.astype(f32), axis=-1, keepdims=True) * v_h
    o = (o + bonus.reshape(B, T, D_RWKV)) * jax.nn.silu(g.astype(f32))
    return o.astype(zr.dtype), s_T.astype(s0.dtype), zr[:, -1]


def _sparse_attend_block(q, qi, wi, qpos, k, v, ki, topk, rel_bias):
    B, Tq = q.shape[:2]
    L = k.shape[1]
    f32 = jnp.float32
    idx_logits = jnp.einsum('bthd,bsd->bths', qi.astype(f32), ki.astype(f32))
    score = jnp.einsum('bths,bth->bts', jax.nn.relu(idx_logits), wi.astype(f32))
    kpos = jnp.arange(L, dtype=jnp.int32)
    visible = kpos[None, :] <= qpos[:, None]
    score = jnp.where(visible[None], score, -jnp.inf)
    _, sel = lax.top_k(score, topk)
    gather = jax.vmap(lambda rows, ids: rows[ids])
    k_sel = gather(k, sel)
    v_sel = gather(v, sel)
    qg = q.reshape(B, Tq, ATT_KV_HEADS, ATT_GROUP, ATT_HEAD)
    logits = jnp.einsum('btkgd,btskd->btkgs', qg.astype(f32), k_sel.astype(f32)) * (ATT_HEAD ** -0.5)
    dist = qpos[None, :, None] - sel
    bias = rel_bias.astype(f32)[_t5_bucket(dist)]
    bias = bias.reshape(B, Tq, topk, ATT_KV_HEADS, ATT_GROUP).transpose(0, 1, 3, 4, 2)
    valid = (dist >= 0)[:, :, None, None, :]
    logits = jnp.where(valid, logits + bias, -jnp.inf)
    p = jax.nn.softmax(logits, axis=-1)
    o = jnp.einsum('btkgs,btskd->btkgd', p, v_sel.astype(f32))
    return o.reshape(B, Tq, D_ATT).astype(q.dtype)


def _attn_branch(za, k_past, v_past, ki_past, pos0, topk, qn_g, kn_g, rel_bias):
    B, T, _ = za.shape
    q, kn, vn, qi, kin, wi, g = _split(za, ATT_SIZES)
    q = _rmsnorm(q.reshape(B, T, ATT_HEADS, ATT_HEAD), qn_g, NORM_EPS)
    kn = _rmsnorm(kn.reshape(B, T, ATT_KV_HEADS, ATT_HEAD), kn_g, NORM_EPS)
    vn = vn.reshape(B, T, ATT_KV_HEADS, ATT_HEAD)
    qi = qi.reshape(B, T, IDX_HEADS, IDX_DIM) * (IDX_DIM ** -0.5)
    wi = wi * (IDX_HEADS ** -0.5)
    if k_past is None:
        k_all, v_all, ki_all = kn, vn, kin
    else:
        k_all = jnp.concatenate([k_past.astype(kn.dtype), kn], axis=1)
        v_all = jnp.concatenate([v_past.astype(vn.dtype), vn], axis=1)
        ki_all = jnp.concatenate([ki_past.astype(kin.dtype), kin], axis=1)
    blk = min(QBLOCK, T)
    nblk = T // blk
    qpos = pos0 + jnp.arange(T, dtype=jnp.int32)

    def to_blocks(t):
        return jnp.moveaxis(t.reshape(B, nblk, blk, *t.shape[2:]), 1, 0)

    def body(args):
        qb, qib, wib, pb = args
        return _sparse_attend_block(qb, qib, wib, pb, k_all, v_all, ki_all, topk, rel_bias)

    ob = lax.map(body, (to_blocks(q), to_blocks(qi), to_blocks(wi), qpos.reshape(nblk, blk)))
    o = jnp.moveaxis(ob, 0, 1).reshape(B, T, D_ATT)
    return o * jax.nn.silu(g), kn, vn, kin


def _layer(x, shift_prev, s0, k_past, v_past, ki_past, pos0, topk,
           norm_g, w_in, mu, w0, w2, a0, a2, k_k, k_a, r_k, lnx_g, lnx_b,
           qn_g, kn_g, rel_bias, w_pa, w_pb, w_out):
    xn = _rmsnorm(x, norm_g, NORM_EPS)
    z = xn @ w_in
    zr, za, zg = _split(z, (RWKV_COLS, ATT_COLS, sum(GATE_SIZES)))
    oa, s_T, last = _rwkv_branch(zr, shift_prev, s0, mu, w0, w2, a0, a2, k_k, k_a, r_k, lnx_g, lnx_b)
    ob, kn, vn, kin = _attn_branch(za, k_past, v_past, ki_past, pos0, topk, qn_g, kn_g, rel_bias)
    ga, gb = _split(zg, GATE_SIZES)
    merged = jax.nn.sigmoid(ga) * (oa @ w_pa) + jax.nn.sigmoid(gb) * (ob @ w_pb)
    return x + merged @ w_out, (kn, vn, kin, s_T, last)


def setup_inputs(seed: int = 0) -> dict:
    key = jax.random.key(seed)
    ks = jax.random.split(key, 32)
    f32 = jnp.float32
    n_pages = PAST_LEN // PAGE_SIZE
    n_phys = int(math.ceil(POOL_FACTOR * DEC_BATCH * n_pages))
    nrm = lambda k, shape, s: jax.random.normal(k, shape, f32) * s
    x_prompt = nrm(ks[0], (BATCH, SEQ, D_MODEL), 1.0)
    x_sample = nrm(ks[1], (DEC_BATCH, DEC_SEQ, D_MODEL), 1.0)
    cache_k = nrm(ks[2], (DEPTH, n_phys, PAGE_SIZE, ATT_KV_HEADS, ATT_HEAD), 1.0)
    cache_v = nrm(ks[3], (DEPTH, n_phys, PAGE_SIZE, ATT_KV_HEADS, ATT_HEAD), 1.0)
    cache_kidx = nrm(ks[4], (DEPTH, n_phys, PAGE_SIZE, IDX_DIM), 1.0)
    state_wkv = nrm(ks[5], (DEPTH, DEC_BATCH, RWKV_HEADS, RWKV_HEAD, RWKV_HEAD), 0.3)
    state_shift = nrm(ks[6], (DEPTH, DEC_BATCH, RWKV_COLS), 1.0)
    page_table = jax.random.permutation(ks[7], n_phys)[:DEC_BATCH * n_pages]
    page_table = page_table.reshape(DEC_BATCH, n_pages).astype(jnp.int32)
    norm_g = 1.0 + nrm(ks[8], (DEPTH, D_MODEL), 0.02)
    w_in = nrm(ks[9], (DEPTH, D_MODEL, N_COLS), D_MODEL ** -0.5)
    shift_mu = jax.random.uniform(ks[10], (DEPTH, RWKV_COLS), f32, 0.1, 0.9)
    w0 = jax.random.uniform(ks[11], (DEPTH, D_RWKV), f32, -5.0, 1.0)
    w2 = nrm(ks[12], (DEPTH, R_DECAY, D_RWKV), 0.1 * R_DECAY ** -0.5)
    a0 = nrm(ks[13], (DEPTH, D_RWKV), 0.1)
    a2 = nrm(ks[14], (DEPTH, R_ICL, D_RWKV), 0.5 * R_ICL ** -0.5)
    k_k = 0.85 + nrm(ks[15], (DEPTH, D_RWKV), 0.02)
    k_a = 1.0 + nrm(ks[16], (DEPTH, D_RWKV), 0.02)
    r_k = nrm(ks[17], (DEPTH, RWKV_HEADS, RWKV_HEAD), 0.1)
    lnx_g = 1.0 + nrm(ks[18], (DEPTH, D_RWKV), 0.02)
    lnx_b = nrm(ks[19], (DEPTH, D_RWKV), 0.02)
    q_norm_g = 1.0 + nrm(ks[20], (DEPTH, ATT_HEAD), 0.02)
    k_norm_g = 1.0 + nrm(ks[21], (DEPTH, ATT_HEAD), 0.02)
    rel_bias = nrm(ks[22], (REL_BUCKETS, ATT_HEADS), 0.5)
    w_pa = nrm(ks[23], (DEPTH, D_RWKV, D_MODEL), D_RWKV ** -0.5)
    w_pb = nrm(ks[24], (DEPTH, D_ATT, D_MODEL), D_ATT ** -0.5)
    w_out = nrm(ks[25], (DEPTH, D_MODEL, D_MODEL), D_MODEL ** -0.5)
    return {'x_prompt': x_prompt, 'x_sample': x_sample, 'cache_k': cache_k, 'cache_v': cache_v,
            'cache_kidx': cache_kidx, 'state_wkv': state_wkv, 'state_shift': state_shift,
            'page_table': page_table, 'norm_g': norm_g, 'w_in': w_in, 'shift_mu': shift_mu,
            'w0': w0, 'w2': w2, 'a0': a0, 'a2': a2, 'k_k': k_k, 'k_a': k_a, 'r_k': r_k,
            'lnx_g': lnx_g, 'lnx_b': lnx_b, 'q_norm_g': q_norm_g, 'k_norm_g': k_norm_g,
            'rel_bias': rel_bias, 'w_pa': w_pa, 'w_pb': w_pb, 'w_out': w_out}


def reference(x_prompt, x_sample, cache_k, cache_v, cache_kidx, state_wkv, state_shift, page_table,
              norm_g, w_in, shift_mu, w0, w2, a0, a2, k_k, k_a, r_k, lnx_g, lnx_b,
              q_norm_g, k_norm_g, rel_bias, w_pa, w_pb, w_out):
    bsz, seq = x_prompt.shape[0], x_prompt.shape[1]
    dec_bsz, dec_seq = x_sample.shape[0], x_sample.shape[1]
    past_len = page_table.shape[1] * cache_k.shape[2]
    topk_p = min(TOPK_MAX, seq // 4)
    topk_s = min(TOPK_MAX, (past_len + dec_seq) // 4)
    zero_shift = jnp.zeros((bsz, RWKV_COLS), x_prompt.dtype)
    zero_wkv = jnp.zeros((bsz, RWKV_HEADS, RWKV_HEAD, RWKV_HEAD), x_prompt.dtype)
    hp, hs = x_prompt, x_sample
    outs_p, outs_s = [], []
    for l in range(DEPTH):
        params = (norm_g[l], w_in[l], shift_mu[l], w0[l], w2[l], a0[l], a2[l], k_k[l], k_a[l], r_k[l],
                  lnx_g[l], lnx_b[l], q_norm_g[l], k_norm_g[l], rel_bias, w_pa[l], w_pb[l], w_out[l])
        hp, st_p = _layer(hp, zero_shift, zero_wkv, None, None, None, 0, topk_p, *params)
        gather_past = lambda pool: pool[l][page_table].reshape(dec_bsz, past_len, *pool.shape[3:])
        hs, st_s = _layer(hs, state_shift[l], state_wkv[l], gather_past(cache_k), gather_past(cache_v),
                          gather_past(cache_kidx), past_len, topk_s, *params)
        outs_p.append(st_p)
        outs_s.append(st_s)
    stk = lambda outs, i: jnp.stack([o[i] for o in outs], axis=0)
    return (hp, hs,
            stk(outs_p, 0), stk(outs_p, 1), stk(outs_p, 2), stk(outs_p, 3), stk(outs_p, 4),
            stk(outs_s, 0), stk(outs_s, 1), stk(outs_s, 2), stk(outs_s, 3), stk(outs_s, 4))
```

```python
import functools
import math

import numpy as np
import jax
import jax.numpy as jnp
from jax import lax
from jax.experimental import pallas as pl
from jax.experimental.pallas import tpu as pltpu

F32 = jnp.float32
BF16 = jnp.bfloat16

D_MODEL = 1024
HEAD = 64
RWKV_HEADS = D_MODEL // HEAD
LORA = 64
LNX_EPS = 64e-5
ATT_HEADS = D_MODEL // HEAD
ATT_KV_HEADS = 2
ATT_GROUP = ATT_HEADS // ATT_KV_HEADS
IDX_HEADS = 8
TOPK_MAX = 256
REL_BUCKETS = 32
REL_MAX_DIST = 128
NORM_EPS = 1e-6
RWKV_COLS = 4 * D_MODEL + 2 * LORA

LANES = 128
SUBLANES = 8
VMEM_LIMIT_BYTES = 48 * 1024 * 1024

ZA_Q = 0
ZA_QI = 1024
ZA_KV = 1536
ZA_KIW = 1792
ZA_GATT = 2048
ZA_GA = 3072
ZA_GB = 4096
ZA_COLS = 5120

_NEG_INF = float("-inf")
_POS_INF = float("inf")


def _cparams(n_axes):
    return pltpu.CompilerParams(dimension_semantics=("arbitrary",) * n_axes,
                                vmem_limit_bytes=VMEM_LIMIT_BYTES)


def _sigmoid(x):
    return 1.0 / (1.0 + jnp.exp(-x))


def _dot_nt(a, b):
    return lax.dot_general(a, b, (((1,), (1,)), ((), ())), preferred_element_type=F32)


def _rmsnorm_kernel(x_ref, g_ref, o_ref):
    x = x_ref[...]
    ms = jnp.mean(x * x, axis=-1, keepdims=True)
    o_ref[...] = (x * lax.rsqrt(ms + NORM_EPS) * g_ref[...]).astype(o_ref.dtype)


def _rmsnorm(x, g, tm):
    m, d = x.shape
    return pl.pallas_call(
        _rmsnorm_kernel, grid=(m // tm,),
        in_specs=[pl.BlockSpec((tm, d), lambda i: (i, 0)), pl.BlockSpec((1, d), lambda i: (0, 0))],
        out_specs=pl.BlockSpec((tm, d), lambda i: (i, 0)),
        out_shape=jax.ShapeDtypeStruct((m, d), BF16),
        compiler_params=_cparams(1), name="rmsnorm")(x, g.reshape(1, d))


def _mm_kernel(x_ref, w_ref, o_ref):
    o_ref[...] = jnp.dot(x_ref[...], w_ref[...], preferred_element_type=F32)


def _matmul(x, w, tm, tn, name):
    m, k = x.shape
    n = w.shape[1]
    return pl.pallas_call(
        _mm_kernel, grid=(m // tm, n // tn),
        in_specs=[pl.BlockSpec((tm, k), lambda i, j: (i, 0)), pl.BlockSpec((k, tn), lambda i, j: (0, j))],
        out_specs=pl.BlockSpec((tm, tn), lambda i, j: (i, j)),
        out_shape=jax.ShapeDtypeStruct((m, n), F32),
        compiler_params=_cparams(2), name=name)(x, w)


def _rwkv_prep_kernel(z_ref, prev_ref, shift_ref, mu_ref, w0_ref, w2_ref, a0_ref, a2_ref, kk_ref, ka_ref,
                      r_o, w_o, k_o, v_o, kk_o, a_o, g_o, *, seq_len, tile):
    z = z_ref[...]
    rows = lax.broadcasted_iota(jnp.int32, z.shape, 0)
    rolled = pltpu.roll(z, 1, 0)
    if seq_len >= tile:
        first = jnp.where(pl.program_id(1) == 0, shift_ref[...], prev_ref[SUBLANES - 1:SUBLANES, :])
        prev = jnp.where(rows == 0, first, rolled)
    else:
        prev = jnp.where(lax.rem(rows, seq_len) == 0, shift_ref[...], rolled)
    zs = z + (prev - z) * mu_ref[...]
    d = D_MODEL
    r = zs[:, 0:d]
    k = zs[:, d:2 * d]
    v = zs[:, 2 * d:3 * d]
    g = zs[:, 3 * d:4 * d]
    wd = zs[:, 4 * d:4 * d + LORA]
    ad = zs[:, 4 * d + LORA:4 * d + 2 * LORA]
    wl = w0_ref[...] + jnp.dot(jnp.tanh(wd).astype(BF16), w2_ref[...], preferred_element_type=F32)
    decay = jnp.exp(-_sigmoid(wl) * math.exp(-0.5))
    a = _sigmoid(a0_ref[...] + jnp.dot(ad.astype(BF16), a2_ref[...], preferred_element_type=F32))
    r_o[...] = r
    w_o[...] = decay
    k_o[...] = k * (1.0 + (a - 1.0) * ka_ref[...])
    v_o[...] = v
    kk_o[...] = k * kk_ref[...]
    a_o[...] = a
    g_o[...] = g * _sigmoid(g)


def _rwkv_prep(zr, shift, mu, w0, w2, a0, a2, k_k, k_a, *, batch, seq_len, tile):
    m, c = zr.shape
    d = D_MODEL
    row = lambda x: x.reshape(1, -1)
    consts = [row(mu), row(w0), w2.astype(BF16), row(a0), a2.astype(BF16), row(k_k), row(k_a)]
    const_specs = [pl.BlockSpec(x.shape, lambda *_: (0, 0)) for x in consts]
    if seq_len >= tile:
        nt = seq_len // tile
        grid = (batch, nt)
        zmap = lambda b, t: (b * nt + t, 0)
        pmap = lambda b, t: (jnp.maximum((b * seq_len + t * tile) // SUBLANES - 1, 0), 0)
        shift_spec = pl.BlockSpec((None, 1, c), lambda b, t: (b, 0, 0))
    else:
        grid = (1, m // tile)
        zmap = lambda b, t: (t, 0)
        pmap = lambda b, t: (0, 0)
        shift_spec = pl.BlockSpec((tile, c), zmap)
    out_spec = pl.BlockSpec((tile, d), zmap)
    kern = functools.partial(_rwkv_prep_kernel, seq_len=seq_len, tile=tile)
    return pl.pallas_call(
        kern, grid=grid,
        in_specs=[pl.BlockSpec((tile, c), zmap), pl.BlockSpec((SUBLANES, c), pmap), shift_spec] + const_specs,
        out_specs=[out_spec] * 7,
        out_shape=[jax.ShapeDtypeStruct((m, d), F32)] * 7,
        compiler_params=_cparams(2), name="rwkv_prep")(zr, zr, shift, *consts)


def _rwkv_scan_kernel(r_ref, w_ref, k_ref, v_ref, kk_ref, a_ref, s0_ref, lg_ref, lb_ref, rk_ref,
                      o_ref, s_ref, vec_ref, *, steps):
    nb = HEAD // SUBLANES

    @pl.when(pl.program_id(1) == 0)
    def _():
        s_ref[...] = s0_ref[...]

    def bcast_row(ref, lead, j):
        return jnp.broadcast_to(ref[lead, pl.ds(j, 1), :], (SUBLANES, LANES))

    def step(t, carry):
        kkraw = kk_ref[t]
        n2 = jnp.sum(kkraw * kkraw, axis=0, keepdims=True)
        kk = kkraw / jnp.maximum(jnp.sqrt(n2), 1e-12)
        vec_ref[0] = -kk
        vec_ref[1] = kk * a_ref[t]
        sa = [jnp.zeros((SUBLANES, LANES), F32) for _ in range(nb)]
        for j in range(HEAD):
            nk = bcast_row(vec_ref, 0, j)
            for ib in range(nb):
                sa[ib] = sa[ib] + s_ref[j, ib * SUBLANES:(ib + 1) * SUBLANES, :] * nk
        vt = [v_ref[t, ib * SUBLANES:(ib + 1) * SUBLANES, :] for ib in range(nb)]
        out = [jnp.zeros((SUBLANES, LANES), F32) for _ in range(nb)]
        for j in range(HEAD):
            wj = bcast_row(w_ref, t, j)
            kaj = bcast_row(vec_ref, 1, j)
            kj = bcast_row(k_ref, t, j)
            rj = bcast_row(r_ref, t, j)
            for ib in range(nb):
                sl = slice(ib * SUBLANES, (ib + 1) * SUBLANES)
                sn = s_ref[j, sl, :] * wj + sa[ib] * kaj + vt[ib] * kj
                s_ref[j, sl, :] = sn
                out[ib] = out[ib] + sn * rj
        o = jnp.concatenate(out, axis=0)
        mean = jnp.mean(o, axis=0, keepdims=True)
        dev = o - mean
        var = jnp.mean(dev * dev, axis=0, keepdims=True)
        y = dev * lax.rsqrt(var + LNX_EPS) * lg_ref[...] + lb_ref[...]
        bonus = jnp.sum(r_ref[t] * k_ref[t] * rk_ref[...], axis=0, keepdims=True) * v_ref[t]
        o_ref[t] = y + bonus
        return carry

    lax.fori_loop(0, steps, step, 0)


def _rwkv_scan(seqs, s0, lg_t, lb_t, rk_t, *, steps):
    t_len, n, p = seqs[0].shape
    grid = (p // LANES, t_len // steps)
    seq_spec = pl.BlockSpec((steps, n, LANES), lambda g, t: (t, 0, g))
    st_spec = pl.BlockSpec((n, n, LANES), lambda g, t: (0, 0, g))
    c_spec = pl.BlockSpec((n, LANES), lambda g, t: (0, g))
    kern = functools.partial(_rwkv_scan_kernel, steps=steps)
    return pl.pallas_call(
        kern, grid=grid,
        in_specs=[seq_spec] * 6 + [st_spec] + [c_spec] * 3,
        out_specs=[seq_spec, st_spec],
        out_shape=[jax.ShapeDtypeStruct((t_len, n, p), F32), jax.ShapeDtypeStruct((n, n, p), F32)],
        scratch_shapes=[pltpu.VMEM((2, n, LANES), F32)],
        compiler_params=_cparams(2), name="rwkv_scan")(*seqs, s0, lg_t, lb_t, rk_t)


def _to_pairs(x, batch, seq_len):
    x = x.reshape(batch, seq_len, RWKV_HEADS, HEAD)
    return jnp.transpose(x, (1, 3, 0, 2)).reshape(seq_len, HEAD, batch * RWKV_HEADS)


def _from_pairs(x, batch, seq_len):
    x = x.reshape(seq_len, HEAD, batch, RWKV_HEADS)
    return jnp.transpose(x, (2, 0, 3, 1)).reshape(batch * seq_len, RWKV_HEADS * HEAD)


def _head_const_pairs(x, batch):
    x = x.reshape(RWKV_HEADS, HEAD).T
    return jnp.tile(x, (1, batch))


def _count_ge(score, thr):
    return jnp.sum(jnp.where(score >= thr, 1.0, 0.0), axis=-1, keepdims=True)


def _select_topk(score, kpos, k_sel, bisect_steps):
    kf = float(k_sel)
    vis = score > _NEG_INF
    nvis = jnp.sum(jnp.where(vis, 1.0, 0.0), axis=-1, keepdims=True)
    need_sel = nvis > kf
    rowmax = jnp.where(need_sel, jnp.max(score, axis=-1, keepdims=True), 0.0)
    rowmin = jnp.where(need_sel, jnp.min(jnp.where(vis, score, _POS_INF), axis=-1, keepdims=True), 0.0)

    def bis(_, c):
        lo, hi, chi = c
        hfin = jnp.where(hi == _POS_INF, rowmax, hi)
        piv = 0.5 * lo + 0.5 * hfin
        cnt = _count_ge(score, piv)
        ge = cnt >= kf
        return jnp.where(ge, piv, lo), jnp.where(ge, hi, piv), jnp.where(ge, chi, cnt)

    lo, hi, chi = lax.fori_loop(0, bisect_steps, bis,
                                (rowmin, jnp.full_like(rowmin, _POS_INF), jnp.zeros_like(rowmin)))

    def walk_cond(c):
        return jnp.min(c[4]) < 0.5

    def walk(c):
        hi, chi, tau, cgt, done, ceq = c
        pending = done < 0.5
        bmax = jnp.max(jnp.where(score < hi, score, _NEG_INF), axis=-1, keepdims=True)
        cnt = _count_ge(score, bmax)
        fin = jnp.logical_and(cnt >= kf, pending)
        tau = jnp.where(fin, bmax, tau)
        cgt = jnp.where(fin, chi, cgt)
        ceq = jnp.where(fin, cnt - chi, ceq)
        adv = jnp.logical_and(cnt < kf, pending)
        hi = jnp.where(adv, bmax, hi)
        chi = jnp.where(adv, cnt, chi)
        return hi, chi, tau, cgt, jnp.where(fin, 1.0, done), ceq

    done0 = jnp.where(need_sel, 0.0, 1.0)
    neg = jnp.full_like(rowmin, _NEG_INF)
    zero = jnp.zeros_like(rowmin)
    _, _, tau, cgt, _, ceq = lax.while_loop(walk_cond, walk, (hi, chi, neg, zero, done0, zero))

    need = kf - cgt
    excess = jnp.logical_and(need_sel, ceq > need)
    eq = score == tau
    n_keys = score.shape[-1]

    def tie_break(_):
        def body(_, c):
            plo, phi = c
            mid = lax.shift_right_arithmetic(plo + phi, 1)
            cnt = jnp.sum(jnp.where(jnp.logical_and(eq, kpos <= mid), 1.0, 0.0), axis=-1, keepdims=True)
            ge = cnt >= need
            return jnp.where(ge, plo, mid), jnp.where(ge, mid, phi)
        plo0 = jnp.full(tau.shape, -1, jnp.int32)
        phi0 = jnp.full(tau.shape, n_keys - 1, jnp.int32)
        _, phi = lax.fori_loop(0, n_keys.bit_length() + 1, body, (plo0, phi0))
        return jnp.where(excess, phi, n_keys)

    any_excess = jnp.max(jnp.where(excess, 1.0, 0.0)) > 0.0
    pcut = lax.cond(any_excess, tie_break, lambda _: jnp.full(tau.shape, n_keys, jnp.int32), 0)
    pcut = jnp.where(need_sel, pcut, -1)
    return jnp.logical_or(score > tau, jnp.logical_and(eq, kpos <= pcut))


def _knorm_kernel(kv_ref, g_ref, o_ref):
    for h in range(ATT_KV_HEADS):
        x = kv_ref[:, h * HEAD:(h + 1) * HEAD]
        ms = jnp.mean(x * x, axis=-1, keepdims=True)
        o_ref[:, h * HEAD:(h + 1) * HEAD] = x * lax.rsqrt(ms + NORM_EPS) * g_ref[...]


def _knorm(za, kn_g, tm):
    m = za.shape[0]
    w = ATT_KV_HEADS * HEAD
    return pl.pallas_call(
        _knorm_kernel, grid=(m // tm,),
        in_specs=[pl.BlockSpec((tm, 2 * w), lambda i: (i, ZA_KV // (2 * w))),
                  pl.BlockSpec((1, HEAD), lambda i: (0, 0))],
        out_specs=pl.BlockSpec((tm, w), lambda i: (i, 0)),
        out_shape=jax.ShapeDtypeStruct((m, w), F32),
        compiler_params=_cparams(1), name="knorm")(za, kn_g.reshape(1, HEAD))


def _bucket_table(n):
    max_exact = REL_BUCKETS // 2
    d = jnp.arange(n, dtype=jnp.int32)
    df = jnp.maximum(d, 1).astype(F32)
    large = max_exact + (jnp.log(df / max_exact) / math.log(REL_MAX_DIST / max_exact)
                         * (REL_BUCKETS - max_exact)).astype(jnp.int32)
    large = jnp.minimum(large, REL_BUCKETS - 1)
    return jnp.where(d < max_exact, d, large)


def _attn_prompt_kernel(cb_ref, q_ref, qi_ref, kiwq_ref, g_ref, kn_ref, kv_ref, kiwk_ref, tbd_ref, qg_ref,
                        o_ref, s_scr, m_scr, *, tq, k_sel, bisect_steps):
    qt = pl.program_id(1)
    q0 = pl.multiple_of(qt * tq, tq)
    n_keys = kn_ref.shape[0]
    shape = (tq, n_keys)

    kidx = kiwk_ref[:, 0:HEAD].astype(BF16)
    acc = jnp.zeros(shape, F32)
    for h in range(IDX_HEADS):
        qh = (qi_ref[:, h * HEAD:(h + 1) * HEAD] * (HEAD ** -0.5)).astype(BF16)
        wh = kiwq_ref[:, HEAD + h:HEAD + h + 1] * (IDX_HEADS ** -0.5)
        acc = acc + jnp.maximum(_dot_nt(qh, kidx), 0.0) * wh
    kpos = lax.broadcasted_iota(jnp.int32, shape, 1)
    qpos = q0 + lax.broadcasted_iota(jnp.int32, shape, 0)
    score = jnp.where(kpos <= qpos, acc, _NEG_INF)
    sel = _select_topk(score, kpos, k_sel, bisect_steps)
    m_scr[...] = jnp.where(sel, 0.0, _NEG_INF)

    kn = kn_ref[...].astype(BF16)
    vv = kv_ref[:, ATT_KV_HEADS * HEAD:2 * ATT_KV_HEADS * HEAD].astype(BF16)
    qg = qg_ref[...]
    for h in range(ATT_HEADS):
        kvh = h // ATT_GROUP
        hs = slice(h * HEAD, (h + 1) * HEAD)
        ks = slice(kvh * HEAD, (kvh + 1) * HEAD)
        qh = q_ref[:, hs]
        qh = qh * lax.rsqrt(jnp.mean(qh * qh, axis=-1, keepdims=True) + NORM_EPS) * qg
        s_scr[...] = _dot_nt(qh.astype(BF16), kn[:, ks]) * (HEAD ** -0.5) + cb_ref[h] + m_scr[...]
        s_scr[:, pl.ds(q0, tq)] += tbd_ref[h, 0]

        @pl.when(qt > 0)
        def _():
            s_scr[:, pl.ds(pl.multiple_of(q0 - tq, tq), tq)] += tbd_ref[h, 1]

        s = s_scr[...]
        p = jnp.exp(s - jnp.max(s, axis=-1, keepdims=True))
        l = jnp.sum(p, axis=-1, keepdims=True)
        o = jnp.dot(p.astype(BF16), vv[:, ks], preferred_element_type=F32) / l
        gh = g_ref[:, hs]
        o_ref[:, hs] = o * (gh * _sigmoid(gh))


def _attn_prompt(za, kn, rel_bias, qn_g, *, batch, seq_len, tq, k_sel):
    m = za.shape[0]
    nq = seq_len // tq
    w = ATT_KV_HEADS * HEAD
    brow = jnp.take(rel_bias, _bucket_table(2 * tq), axis=0)
    cb = rel_bias[REL_BUCKETS - 1]
    rr = np.arange(tq)[:, None] - np.arange(tq)[None, :]
    tb = jnp.stack([brow[np.maximum(rr, 0)], brow[tq + rr]], axis=0)
    tbd = jnp.transpose(tb, (3, 0, 1, 2)) - cb[:, None, None, None]
    assert REL_MAX_DIST <= tq
    row_map = lambda blk: (lambda b, t, blk=blk: (b * nq + t, blk))
    key_map = lambda blk: (lambda b, t, blk=blk: (b, blk))
    kern = functools.partial(_attn_prompt_kernel, tq=tq, k_sel=k_sel, bisect_steps=14)
    return pl.pallas_call(
        kern, grid=(batch, nq),
        in_specs=[
            pl.BlockSpec(memory_space=pltpu.SMEM),
            pl.BlockSpec((tq, D_MODEL), row_map(ZA_Q // D_MODEL)),
            pl.BlockSpec((tq, IDX_HEADS * HEAD), row_map(ZA_QI // (IDX_HEADS * HEAD))),
            pl.BlockSpec((tq, LANES), row_map(ZA_KIW // LANES)),
            pl.BlockSpec((tq, D_MODEL), row_map(ZA_GATT // D_MODEL)),
            pl.BlockSpec((seq_len, w), key_map(0)),
            pl.BlockSpec((seq_len, 2 * w), key_map(ZA_KV // (2 * w))),
            pl.BlockSpec((seq_len, LANES), key_map(ZA_KIW // LANES)),
            pl.BlockSpec((ATT_HEADS, 2, tq, tq), lambda b, t: (0, 0, 0, 0)),
            pl.BlockSpec((1, HEAD), lambda b, t: (0, 0)),
        ],
        out_specs=pl.BlockSpec((tq, D_MODEL), row_map(0)),
        out_shape=jax.ShapeDtypeStruct((m, D_MODEL), F32),
        scratch_shapes=[pltpu.VMEM((tq, seq_len), F32), pltpu.VMEM((tq, seq_len), F32)],
        compiler_params=_cparams(2), name="attn_prompt")(
            cb, za, za, za, za, kn, za, za, tbd, qn_g.reshape(1, HEAD))


def _sample_score_kernel(pt_ref, q_ref, w_ref, kiw_new_ref, *rest, n_pages, page, dec_seq):
    page_refs = rest[:n_pages]
    o_ref = rest[n_pages]
    past = n_pages * page
    kidx = jnp.concatenate([r[...] for r in page_refs], axis=0).astype(BF16)
    new = kiw_new_ref[:, 0:HEAD]
    new = jnp.concatenate([new, jnp.zeros((LANES - dec_seq, HEAD), F32)], axis=0).astype(BF16)
    q = (q_ref[...] * (HEAD ** -0.5)).astype(BF16)
    lg = jnp.concatenate([_dot_nt(q, kidx), _dot_nt(q, new)], axis=1)
    wr = jnp.maximum(lg, 0.0) * (w_ref[...] * (IDX_HEADS ** -0.5))
    n_keys = past + LANES
    sc = jnp.sum(wr.reshape(dec_seq, IDX_HEADS, n_keys), axis=1)
    kpos = lax.broadcasted_iota(jnp.int32, (dec_seq, n_keys), 1)
    tpos = lax.broadcasted_iota(jnp.int32, (dec_seq, n_keys), 0)
    o_ref[...] = jnp.where(kpos <= past + tpos, sc, _NEG_INF)


def _sample_select_kernel(s_ref, o_ref, *, k_sel, bisect_steps):
    score = s_ref[...]
    kpos = lax.broadcasted_iota(jnp.int32, score.shape, 1)
    sel = _select_topk(score, kpos, k_sel, bisect_steps)
    o_ref[...] = jnp.where(sel, 0.0, _NEG_INF)


def _sample_attn_kernel(pt_ref, q_ref, g_ref, kn_new_ref, kv_new_ref, mask_ref, bias_ref, qg_ref, *rest,
                        n_pages, page, dec_seq):
    k_pages = rest[:n_pages]
    v_pages = rest[n_pages:2 * n_pages]
    o_ref = rest[2 * n_pages]
    w = ATT_KV_HEADS * HEAD
    pad = jnp.zeros((LANES - dec_seq, w), F32)
    k_all = jnp.concatenate([r[...] for r in k_pages] + [kn_new_ref[...], pad], axis=0).astype(BF16)
    v_all = jnp.concatenate([r[...] for r in v_pages] + [kv_new_ref[:, w:2 * w], pad], axis=0).astype(BF16)
    n_keys = k_all.shape[0]
    rows = dec_seq * ATT_HEADS
    q = q_ref[...]
    q = q * lax.rsqrt(jnp.mean(q * q, axis=-1, keepdims=True) + NORM_EPS) * qg_ref[...]
    qb = q.astype(BF16)
    head = lax.rem(lax.broadcasted_iota(jnp.int32, (rows, 1), 0), ATT_HEADS)
    first = head < ATT_GROUP
    lg = jnp.where(first, _dot_nt(qb, k_all[:, 0:HEAD]), _dot_nt(qb, k_all[:, HEAD:2 * HEAD]))
    mask = jnp.broadcast_to(mask_ref[...][:, None, :], (dec_seq, ATT_HEADS, n_keys)).reshape(rows, n_keys)
    s = lg * (HEAD ** -0.5) + bias_ref[...] + mask
    p = jnp.exp(s - jnp.max(s, axis=-1, keepdims=True))
    l = jnp.sum(p, axis=-1, keepdims=True)
    pb = p.astype(BF16)
    o = jnp.where(first, jnp.dot(pb, v_all[:, 0:HEAD], preferred_element_type=F32),
                  jnp.dot(pb, v_all[:, HEAD:2 * HEAD], preferred_element_type=F32)) / l
    g = g_ref[...]
    o_ref[...] = o * (g * _sigmoid(g))


def _attn_sample(za, kn, cache_k, cache_v, cache_kidx, page_table, rel_bias, qn_g, *, dec_seq, k_sel):
    nb, n_pages = page_table.shape
    n_phys, page = cache_k.shape[0], cache_k.shape[1]
    past = n_pages * page
    n_keys = past + LANES
    w = ATT_KV_HEADS * HEAD
    ck = cache_k.reshape(n_phys, page, w)
    cv = cache_v.reshape(n_phys, page, w)
    ci = cache_kidx.reshape(n_phys, page, HEAD)
    za3 = za.reshape(nb, dec_seq, ZA_COLS)
    kn3 = kn.reshape(nb, dec_seq, w)
    qi = za[:, ZA_QI:ZA_QI + IDX_HEADS * HEAD].reshape(nb, dec_seq * IDX_HEADS, HEAD)
    wi = za[:, ZA_KIW + HEAD:ZA_KIW + HEAD + IDX_HEADS].reshape(nb, dec_seq * IDX_HEADS, 1)
    qa = za[:, ZA_Q:ZA_Q + D_MODEL].reshape(nb, dec_seq * ATT_HEADS, HEAD)
    ga = za[:, ZA_GATT:ZA_GATT + D_MODEL].reshape(nb, dec_seq * ATT_HEADS, HEAD)

    def page_specs(width):
        return [pl.BlockSpec((None, page, width), lambda b, pt, j=j: (pt[b, j], 0, 0)) for j in range(n_pages)]

    per_b = lambda r, c: pl.BlockSpec((None, r, c), lambda b, pt: (b, 0, 0))
    kiw_new = pl.BlockSpec((None, dec_seq, LANES), lambda b, pt: (b, 0, ZA_KIW // LANES))

    scores = pl.pallas_call(
        functools.partial(_sample_score_kernel, n_pages=n_pages, page=page, dec_seq=dec_seq),
        grid_spec=pltpu.PrefetchScalarGridSpec(
            num_scalar_prefetch=1, grid=(nb,),
            in_specs=[per_b(dec_seq * IDX_HEADS, HEAD), per_b(dec_seq * IDX_HEADS, 1), kiw_new] + page_specs(HEAD),
            out_specs=per_b(dec_seq, n_keys)),
        out_shape=jax.ShapeDtypeStruct((nb, dec_seq, n_keys), F32),
        compiler_params=_cparams(1), name="sample_scores")(page_table, qi, wi, za3, *([ci] * n_pages))

    rows = nb * dec_seq
    tr = min(rows, 128)
    mask = pl.pallas_call(
        functools.partial(_sample_select_kernel, k_sel=k_sel, bisect_steps=14),
        grid=(rows // tr,),
        in_specs=[pl.BlockSpec((tr, n_keys), lambda i: (i, 0))],
        out_specs=pl.BlockSpec((tr, n_keys), lambda i: (i, 0)),
        out_shape=jax.ShapeDtypeStruct((rows, n_keys), F32),
        compiler_params=_cparams(1), name="sample_select")(scores.reshape(rows, n_keys))

    t_idx = np.repeat(np.arange(dec_seq), ATT_HEADS)[:, None]
    h_idx = np.tile(np.arange(ATT_HEADS), dec_seq)[:, None]
    dist = np.maximum(past + t_idx - np.arange(n_keys)[None, :], 0)
    bias = rel_bias[_bucket_table(past + dec_seq)[dist], h_idx]

    const = lambda shape: pl.BlockSpec(shape, lambda b, pt: (0,) * len(shape))
    kv_new = pl.BlockSpec((None, dec_seq, 2 * w), lambda b, pt: (b, 0, ZA_KV // (2 * w)))
    out = pl.pallas_call(
        functools.partial(_sample_attn_kernel, n_pages=n_pages, page=page, dec_seq=dec_seq),
        grid_spec=pltpu.PrefetchScalarGridSpec(
            num_scalar_prefetch=1, grid=(nb,),
            in_specs=[per_b(dec_seq * ATT_HEADS, HEAD), per_b(dec_seq * ATT_HEADS, HEAD), per_b(dec_seq, w),
                      kv_new, per_b(dec_seq, n_keys), const((dec_seq * ATT_HEADS, n_keys)), const((1, HEAD))]
            + page_specs(w) + page_specs(w),
            out_specs=per_b(dec_seq * ATT_HEADS, HEAD)),
        out_shape=jax.ShapeDtypeStruct((nb, dec_seq * ATT_HEADS, HEAD), F32),
        compiler_params=_cparams(1), name="sample_attn")(
            page_table, qa, ga, kn3, za3, mask.reshape(nb, dec_seq, n_keys), bias, qn_g.reshape(1, HEAD),
            *([ck] * n_pages), *([cv] * n_pages))
    return out.reshape(rows, D_MODEL)


def _merge_kernel(x_ref, oa_ref, sg_ref, ob_ref, ga_ref, gb_ref, wpa_ref, wpb_ref, wo_ref, y_ref):
    oa = (oa_ref[...] * sg_ref[...]).astype(BF16)
    pa = jnp.dot(oa, wpa_ref[...], preferred_element_type=F32)
    pb = jnp.dot(ob_ref[...].astype(BF16), wpb_ref[...], preferred_element_type=F32)
    merged = _sigmoid(ga_ref[...]) * pa + _sigmoid(gb_ref[...]) * pb
    y_ref[...] = x_ref[...] + jnp.dot(merged.astype(BF16), wo_ref[...], preferred_element_type=F32)


def _merge(x, oa, sg, ob, za, w_pa, w_pb, w_out, tm):
    m, d = x.shape
    row = pl.BlockSpec((tm, d), lambda i: (i, 0))
    wsp = pl.BlockSpec((d, d), lambda i: (0, 0))
    return pl.pallas_call(
        _merge_kernel, grid=(m // tm,),
        in_specs=[row, row, row, row,
                  pl.BlockSpec((tm, d), lambda i: (i, ZA_GA // d)), pl.BlockSpec((tm, d), lambda i: (i, ZA_GB // d)),
                  wsp, wsp, wsp],
        out_specs=row,
        out_shape=jax.ShapeDtypeStruct((m, d), F32),
        compiler_params=_cparams(1), name="merge")(x, oa, sg, ob, za, za, w_pa, w_pb, w_out)


def _layer(x, shift, s0, params, *, batch, seq_len, attend):
    (norm_g, w_r, w_a, mu, w0, w2, a0, a2, k_k, k_a, r_k, lnx_g, lnx_b, kn_g, w_pa, w_pb, w_out) = params
    m = batch * seq_len
    tm = min(m, 512)
    xn = _rmsnorm(x, norm_g, tm)
    zr = _matmul(xn, w_r, tm, RWKV_COLS // 3, "inproj_rwkv")
    za = _matmul(xn, w_a, tm, ZA_COLS // 4, "inproj_attn")

    tile = min(m, 256)
    r, w, k, v, kk, a, sg = _rwkv_prep(zr, shift, mu, w0, w2, a0, a2, k_k, k_a,
                                       batch=batch, seq_len=seq_len, tile=tile)
    seqs = [_to_pairs(t, batch, seq_len) for t in (r, w, k, v, kk, a)]
    consts = [_head_const_pairs(t, batch) for t in (lnx_g, lnx_b, r_k.reshape(-1))]
    o_t, s_t = _rwkv_scan(seqs, s0, *consts, steps=min(seq_len, 32))
    oa = _from_pairs(o_t, batch, seq_len)

    kn = _knorm(za, kn_g, tm)
    ob = attend(za, kn)

    y = _merge(x, oa, sg, ob, za, w_pa, w_pb, w_out, min(m, 256))
    return y, zr, za, kn, s_t


def _state_to_pairs(s):
    b, h, n, _ = s.shape
    return jnp.transpose(s, (3, 2, 0, 1)).reshape(n, n, b * h)


def _state_from_pairs(s, batch):
    n = s.shape[0]
    return jnp.transpose(s.reshape(n, n, batch, RWKV_HEADS), (2, 3, 1, 0))


def kernel(x_prompt, x_sample, cache_k, cache_v, cache_kidx, state_wkv, state_shift, page_table, norm_g, w_in,
           shift_mu, w0, w2, a0, a2, k_k, k_a, r_k, lnx_g, lnx_b, q_norm_g, k_norm_g, rel_bias, w_pa, w_pb, w_out):
    bsz, seq, d = x_prompt.shape
    dec_bsz, dec_seq, _ = x_sample.shape
    depth = w_in.shape[0]
    assert depth == 1 and d == D_MODEL
    past_len = page_table.shape[1] * cache_k.shape[2]
    topk_p = min(TOPK_MAX, seq // 4)
    topk_s = min(TOPK_MAX, (past_len + dec_seq) // 4)
    l = 0

    wl = w_in[l]
    c0 = RWKV_COLS
    q_w, kv_w, qi_w = wl[:, c0:c0 + 1024], wl[:, c0 + 1024:c0 + 1280], wl[:, c0 + 1280:c0 + 1792]
    kiw_w = wl[:, c0 + 1792:c0 + 1864]
    rest_w = wl[:, c0 + 1864:]
    zpad = lambda n: jnp.zeros((d, n), wl.dtype)
    w_a = jnp.concatenate([q_w, qi_w, kv_w, kiw_w, zpad(LANES - kiw_w.shape[1]), zpad(LANES), rest_w],
                          axis=1).astype(BF16)
    assert w_a.shape[1] == ZA_COLS
    w_r = wl[:, :c0].astype(BF16)
    params = (norm_g[l], w_r, w_a, shift_mu[l], w0[l], w2[l], a0[l], a2[l], k_k[l], k_a[l], r_k[l],
              lnx_g[l], lnx_b[l], k_norm_g[l], w_pa[l].astype(BF16), w_pb[l].astype(BF16), w_out[l].astype(BF16))

    xp = x_prompt.reshape(bsz * seq, d)
    attend_p = functools.partial(_attn_prompt, rel_bias=rel_bias, qn_g=q_norm_g[l], batch=bsz, seq_len=seq,
                                 tq=128, k_sel=topk_p)
    yp, zr_p, za_p, kn_p, st_p = _layer(
        xp, jnp.zeros((bsz, 1, RWKV_COLS), F32), jnp.zeros((HEAD, HEAD, bsz * RWKV_HEADS), F32), params,
        batch=bsz, seq_len=seq, attend=attend_p)

    xs = x_sample.reshape(dec_bsz * dec_seq, d)
    attend_s = functools.partial(_attn_sample, cache_k=cache_k[l], cache_v=cache_v[l], cache_kidx=cache_kidx[l],
                                 page_table=page_table, rel_bias=rel_bias, qn_g=q_norm_g[l], dec_seq=dec_seq,
                                 k_sel=topk_s)
    shift_rows = jnp.repeat(state_shift[l], dec_seq, axis=0)
    ys, zr_s, za_s, kn_s, st_s = _layer(
        xs, shift_rows, _state_to_pairs(state_wkv[l]), params, batch=dec_bsz, seq_len=dec_seq, attend=attend_s)

    w = ATT_KV_HEADS * HEAD

    def pack(y, zr, za, kn, st, b, t):
        v = za[:, ZA_KV + w:ZA_KV + 2 * w]
        kidx = za[:, ZA_KIW:ZA_KIW + HEAD]
        return (y.reshape(b, t, d),
                kn.reshape(1, b, t, ATT_KV_HEADS, HEAD), v.reshape(1, b, t, ATT_KV_HEADS, HEAD),
                kidx.reshape(1, b, t, HEAD), _state_from_pairs(st, b)[None],
                zr.reshape(b, t, RWKV_COLS)[:, -1][None])

    p = pack(yp, zr_p, za_p, kn_p, st_p, bsz, seq)
    s = pack(ys, zr_s, za_s, kn_s, st_s, dec_bsz, dec_seq)
    return (p[0], s[0]) + p[1:] + s[1:]
```

```python
import functools
import math

import numpy as np
import jax
import jax.numpy as jnp
from jax import lax
from jax.experimental import pallas as pl
from jax.experimental.pallas import tpu as pltpu

F32 = jnp.float32
BF16 = jnp.bfloat16

D_MODEL = 1024
HEAD = 64
RWKV_HEADS = D_MODEL // HEAD
LORA = 64
LNX_EPS = 64e-5
ATT_HEADS = D_MODEL // HEAD
ATT_KV_HEADS = 2
ATT_GROUP = ATT_HEADS // ATT_KV_HEADS
IDX_HEADS = 8
TOPK_MAX = 256
REL_BUCKETS = 32
REL_MAX_DIST = 128
NORM_EPS = 1e-6
RWKV_COLS = 4 * D_MODEL + 2 * LORA

LANES = 128
SUBLANES = 8
VMEM_LIMIT_BYTES = 48 * 1024 * 1024

ZA_Q = 0
ZA_QI = 1024
ZA_KV = 1536
ZA_KIW = 1792
ZA_GATT = 2048
ZA_GA = 3072
ZA_GB = 4096
ZA_COLS = 5120

_NEG_INF = float("-inf")
_POS_INF = float("inf")


def _cparams(n_axes):
    return pltpu.CompilerParams(dimension_semantics=("arbitrary",) * n_axes,
                                vmem_limit_bytes=VMEM_LIMIT_BYTES)


def _sigmoid(x):
    return 1.0 / (1.0 + jnp.exp(-x))


def _dot_nt(a, b):
    return lax.dot_general(a, b, (((1,), (1,)), ((), ())), preferred_element_type=F32)


def _rmsnorm_kernel(x_ref, g_ref, o_ref):
    x = x_ref[...]
    ms = jnp.mean(x * x, axis=-1, keepdims=True)
    o_ref[...] = (x * lax.rsqrt(ms + NORM_EPS) * g_ref[...]).astype(o_ref.dtype)


def _rmsnorm(x, g, tm):
    m, d = x.shape
    return pl.pallas_call(
        _rmsnorm_kernel, grid=(m // tm,),
        in_specs=[pl.BlockSpec((tm, d), lambda i: (i, 0)), pl.BlockSpec((1, d), lambda i: (0, 0))],
        out_specs=pl.BlockSpec((tm, d), lambda i: (i, 0)),
        out_shape=jax.ShapeDtypeStruct((m, d), BF16),
        compiler_params=_cparams(1), name="rmsnorm")(x, g.reshape(1, d))


def _mm_kernel(x_ref, w_ref, o_ref):
    o_ref[...] = jnp.dot(x_ref[...], w_ref[...], preferred_element_type=F32)


def _matmul(x, w, tm, tn, name):
    m, k = x.shape
    n = w.shape[1]
    return pl.pallas_call(
        _mm_kernel, grid=(m // tm, n // tn),
        in_specs=[pl.BlockSpec((tm, k), lambda i, j: (i, 0)), pl.BlockSpec((k, tn), lambda i, j: (0, j))],
        out_specs=pl.BlockSpec((tm, tn), lambda i, j: (i, j)),
        out_shape=jax.ShapeDtypeStruct((m, n), F32),
        compiler_params=_cparams(2), name=name)(x, w)


def _rwkv_prep_kernel(z_ref, prev_ref, shift_ref, mu_ref, w0_ref, w2_ref, a0_ref, a2_ref, kk_ref, ka_ref,
                      r_o, w_o, k_o, v_o, kk_o, a_o, g_o, *, seq_len, tile):
    z = z_ref[...]
    rows = lax.broadcasted_iota(jnp.int32, z.shape, 0)
    rolled = pltpu.roll(z, 1, 0)
    if seq_len >= tile:
        first = jnp.where(pl.program_id(1) == 0, shift_ref[...], prev_ref[SUBLANES - 1:SUBLANES, :])
        prev = jnp.where(rows == 0, first, rolled)
    else:
        prev = jnp.where(lax.rem(rows, seq_len) == 0, shift_ref[...], rolled)
    zs = z + (prev - z) * mu_ref[...]
    d = D_MODEL
    r = zs[:, 0:d]
    k = zs[:, d:2 * d]
    v = zs[:, 2 * d:3 * d]
    g = zs[:, 3 * d:4 * d]
    wd = zs[:, 4 * d:4 * d + LORA]
    ad = zs[:, 4 * d + LORA:4 * d + 2 * LORA]
    wl = w0_ref[...] + jnp.dot(jnp.tanh(wd).astype(BF16), w2_ref[...], preferred_element_type=F32)
    decay = jnp.exp(-_sigmoid(wl) * math.exp(-0.5))
    a = _sigmoid(a0_ref[...] + jnp.dot(ad.astype(BF16), a2_ref[...], preferred_element_type=F32))
    r_o[...] = r
    w_o[...] = decay
    k_o[...] = k * (1.0 + (a - 1.0) * ka_ref[...])
    v_o[...] = v
    kk_o[...] = k * kk_ref[...]
    a_o[...] = a
    g_o[...] = g * _sigmoid(g)


def _rwkv_prep(zr, shift, mu, w0, w2, a0, a2, k_k, k_a, *, batch, seq_len, tile):
    m, c = zr.shape
    d = D_MODEL
    row = lambda x: x.reshape(1, -1)
    consts = [row(mu), row(w0), w2.astype(BF16), row(a0), a2.astype(BF16), row(k_k), row(k_a)]
    const_specs = [pl.BlockSpec(x.shape, lambda *_: (0, 0)) for x in consts]
    if seq_len >= tile:
        nt = seq_len // tile
        grid = (batch, nt)
        zmap = lambda b, t: (b * nt + t, 0)
        pmap = lambda b, t: (jnp.maximum((b * seq_len + t * tile) // SUBLANES - 1, 0), 0)
        shift_spec = pl.BlockSpec((None, 1, c), lambda b, t: (b, 0, 0))
    else:
        grid = (1, m // tile)
        zmap = lambda b, t: (t, 0)
        pmap = lambda b, t: (0, 0)
        shift_spec = pl.BlockSpec((tile, c), zmap)
    out_spec = pl.BlockSpec((tile, d), zmap)
    kern = functools.partial(_rwkv_prep_kernel, seq_len=seq_len, tile=tile)
    return pl.pallas_call(
        kern, grid=grid,
        in_specs=[pl.BlockSpec((tile, c), zmap), pl.BlockSpec((SUBLANES, c), pmap), shift_spec] + const_specs,
        out_specs=[out_spec] * 7,
        out_shape=[jax.ShapeDtypeStruct((m, d), F32)] * 7,
        compiler_params=_cparams(2), name="rwkv_prep")(zr, zr, shift, *consts)


def _rwkv_scan_kernel(r_ref, w_ref, k_ref, v_ref, kk_ref, a_ref, s0_ref, lg_ref, lb_ref, rk_ref,
                      o_ref, s_ref, vec_ref, *, steps):
    nb = HEAD // SUBLANES

    @pl.when(pl.program_id(1) == 0)
    def _():
        s_ref[...] = s0_ref[...]

    def bcast_row(ref, lead, j):
        return jnp.broadcast_to(ref[lead, pl.ds(j, 1), :], (SUBLANES, LANES))

    def step(t, carry):
        kkraw = kk_ref[t]
        n2 = jnp.sum(kkraw * kkraw, axis=0, keepdims=True)
        kk = kkraw / jnp.maximum(jnp.sqrt(n2), 1e-12)
        vec_ref[0] = -kk
        vec_ref[1] = kk * a_ref[t]
        sa = [jnp.zeros((SUBLANES, LANES), F32) for _ in range(nb)]
        for j in range(HEAD):
            nk = bcast_row(vec_ref, 0, j)
            for ib in range(nb):
                sa[ib] = sa[ib] + s_ref[j, ib * SUBLANES:(ib + 1) * SUBLANES, :] * nk
        vt = [v_ref[t, ib * SUBLANES:(ib + 1) * SUBLANES, :] for ib in range(nb)]
        out = [jnp.zeros((SUBLANES, LANES), F32) for _ in range(nb)]
        for j in range(HEAD):
            wj = bcast_row(w_ref, t, j)
            kaj = bcast_row(vec_ref, 1, j)
            kj = bcast_row(k_ref, t, j)
            rj = bcast_row(r_ref, t, j)
            for ib in range(nb):
                sl = slice(ib * SUBLANES, (ib + 1) * SUBLANES)
                sn = s_ref[j, sl, :] * wj + sa[ib] * kaj + vt[ib] * kj
                s_ref[j, sl, :] = sn
                out[ib] = out[ib] + sn * rj
        o = jnp.concatenate(out, axis=0)
        mean = jnp.mean(o, axis=0, keepdims=True)
        dev = o - mean
        var = jnp.mean(dev * dev, axis=0, keepdims=True)
        y = dev * lax.rsqrt(var + LNX_EPS) * lg_ref[...] + lb_ref[...]
        bonus = jnp.sum(r_ref[t] * k_ref[t] * rk_ref[...], axis=0, keepdims=True) * v_ref[t]
        o_ref[t] = y + bonus
        return carry

    lax.fori_loop(0, steps, step, 0)


def _rwkv_scan(seqs, s0, lg_t, lb_t, rk_t, *, steps):
    t_len, n, p = seqs[0].shape
    grid = (p // LANES, t_len // steps)
    seq_spec = pl.BlockSpec((steps, n, LANES), lambda g, t: (t, 0, g))
    st_spec = pl.BlockSpec((n, n, LANES), lambda g, t: (0, 0, g))
    c_spec = pl.BlockSpec((n, LANES), lambda g, t: (0, g))
    kern = functools.partial(_rwkv_scan_kernel, steps=steps)
    return pl.pallas_call(
        kern, grid=grid,
        in_specs=[seq_spec] * 6 + [st_spec] + [c_spec] * 3,
        out_specs=[seq_spec, st_spec],
        out_shape=[jax.ShapeDtypeStruct((t_len, n, p), F32), jax.ShapeDtypeStruct((n, n, p), F32)],
        scratch_shapes=[pltpu.VMEM((2, n, LANES), F32)],
        compiler_params=_cparams(2), name="rwkv_scan")(*seqs, s0, lg_t, lb_t, rk_t)


def _to_pairs(x, batch, seq_len):
    x = x.reshape(batch, seq_len, RWKV_HEADS, HEAD)
    return jnp.transpose(x, (1, 3, 0, 2)).reshape(seq_len, HEAD, batch * RWKV_HEADS)


def _from_pairs(x, batch, seq_len):
    x = x.reshape(seq_len, HEAD, batch, RWKV_HEADS)
    return jnp.transpose(x, (2, 0, 3, 1)).reshape(batch * seq_len, RWKV_HEADS * HEAD)


def _head_const_pairs(x, batch):
    x = x.reshape(RWKV_HEADS, HEAD).T
    return jnp.tile(x, (1, batch))


def _count_ge(score, thr):
    return jnp.sum(jnp.where(score >= thr, 1.0, 0.0), axis=-1, keepdims=True)


def _select_topk(score, kpos, k_sel, bisect_steps):
    kf = float(k_sel)
    vis = score > _NEG_INF
    nvis = jnp.sum(jnp.where(vis, 1.0, 0.0), axis=-1, keepdims=True)
    need_sel = nvis > kf
    rowmax = jnp.where(need_sel, jnp.max(score, axis=-1, keepdims=True), 0.0)
    rowmin = jnp.where(need_sel, jnp.min(jnp.where(vis, score, _POS_INF), axis=-1, keepdims=True), 0.0)

    def bis(_, c):
        lo, hi, chi = c
        hfin = jnp.where(hi == _POS_INF, rowmax, hi)
        piv = 0.5 * lo + 0.5 * hfin
        cnt = _count_ge(score, piv)
        ge = cnt >= kf
        return jnp.where(ge, piv, lo), jnp.where(ge, hi, piv), jnp.where(ge, chi, cnt)

    lo, hi, chi = lax.fori_loop(0, bisect_steps, bis,
                                (rowmin, jnp.full_like(rowmin, _POS_INF), jnp.zeros_like(rowmin)))

    def walk_cond(c):
        return jnp.min(c[4]) < 0.5

    def walk(c):
        hi, chi, tau, cgt, done, ceq = c
        pending = done < 0.5
        bmax = jnp.max(jnp.where(score < hi, score, _NEG_INF), axis=-1, keepdims=True)
        cnt = _count_ge(score, bmax)
        fin = jnp.logical_and(cnt >= kf, pending)
        tau = jnp.where(fin, bmax, tau)
        cgt = jnp.where(fin, chi, cgt)
        ceq = jnp.where(fin, cnt - chi, ceq)
        adv = jnp.logical_and(cnt < kf, pending)
        hi = jnp.where(adv, bmax, hi)
        chi = jnp.where(adv, cnt, chi)
        return hi, chi, tau, cgt, jnp.where(fin, 1.0, done), ceq

    done0 = jnp.where(need_sel, 0.0, 1.0)
    neg = jnp.full_like(rowmin, _NEG_INF)
    zero = jnp.zeros_like(rowmin)
    _, _, tau, cgt, _, ceq = lax.while_loop(walk_cond, walk, (hi, chi, neg, zero, done0, zero))

    need = kf - cgt
    excess = jnp.logical_and(need_sel, ceq > need)
    eq = score == tau
    n_keys = score.shape[-1]

    def tie_break(_):
        def body(_, c):
            plo, phi = c
            mid = lax.shift_right_arithmetic(plo + phi, 1)
            cnt = jnp.sum(jnp.where(jnp.logical_and(eq, kpos <= mid), 1.0, 0.0), axis=-1, keepdims=True)
            ge = cnt >= need
            return jnp.where(ge, plo, mid), jnp.where(ge, mid, phi)
        plo0 = jnp.full(tau.shape, -1, jnp.int32)
        phi0 = jnp.full(tau.shape, n_keys - 1, jnp.int32)
        _, phi = lax.fori_loop(0, n_keys.bit_length() + 1, body, (plo0, phi0))
        return jnp.where(excess, phi, n_keys)

    any_excess = jnp.max(jnp.where(excess, 1.0, 0.0)) > 0.0
    pcut = lax.cond(any_excess, tie_break, lambda _: jnp.full(tau.shape, n_keys, jnp.int32), 0)
    pcut = jnp.where(need_sel, pcut, -1)
    return jnp.logical_or(score > tau, jnp.logical_and(eq, kpos <= pcut))


def _knorm_kernel(kv_ref, g_ref, o_ref):
    for h in range(ATT_KV_HEADS):
        x = kv_ref[:, h * HEAD:(h + 1) * HEAD]
        ms = jnp.mean(x * x, axis=-1, keepdims=True)
        o_ref[:, h * HEAD:(h + 1) * HEAD] = x * lax.rsqrt(ms + NORM_EPS) * g_ref[...]


def _knorm(za, kn_g, tm):
    m = za.shape[0]
    w = ATT_KV_HEADS * HEAD
    return pl.pallas_call(
        _knorm_kernel, grid=(m // tm,),
        in_specs=[pl.BlockSpec((tm, 2 * w), lambda i: (i, ZA_KV // (2 * w))),
                  pl.BlockSpec((1, HEAD), lambda i: (0, 0))],
        out_specs=pl.BlockSpec((tm, w), lambda i: (i, 0)),
        out_shape=jax.ShapeDtypeStruct((m, w), F32),
        compiler_params=_cparams(1), name="knorm")(za, kn_g.reshape(1, HEAD))


def _bucket_edges():
    max_exact = REL_BUCKETS // 2
    d = np.arange(REL_MAX_DIST + 1)
    df = np.maximum(d, 1).astype(np.float32)
    large = max_exact + (np.log(df / max_exact) / math.log(REL_MAX_DIST / max_exact)
                         * (REL_BUCKETS - max_exact)).astype(np.int32)
    bucket = np.where(d < max_exact, d, np.minimum(large, REL_BUCKETS - 1))
    return [int(np.argmax(bucket >= b)) for b in range(REL_BUCKETS)]


_BUCKET_EDGES = _bucket_edges()


def _rel_bias_lookup(dist, value_of_bucket):
    bias = value_of_bucket(REL_BUCKETS - 1)
    for b in range(REL_BUCKETS - 2, -1, -1):
        bias = jnp.where(dist < _BUCKET_EDGES[b + 1], value_of_bucket(b), bias)
    return bias


def _attn_prompt_kernel(rb_ref, q_ref, qi_ref, kiwq_ref, g_ref, kn_ref, kv_ref, kiwk_ref, qg_ref,
                        o_ref, s_scr, m_scr, tbd_ref, *, tq, k_sel, bisect_steps):
    qt = pl.program_id(1)
    q0 = pl.multiple_of(qt * tq, tq)
    n_keys = kn_ref.shape[0]
    shape = (tq, n_keys)

    @pl.when(jnp.logical_and(pl.program_id(0) == 0, qt == 0))
    def _():
        rr = lax.broadcasted_iota(jnp.int32, (tq, tq), 0) - lax.broadcasted_iota(jnp.int32, (tq, tq), 1)
        for h in range(ATT_HEADS):
            far = rb_ref[REL_BUCKETS - 1, h]
            tbd_ref[h, 0] = _rel_bias_lookup(rr, lambda b: rb_ref[b, h]) - far
            tbd_ref[h, 1] = _rel_bias_lookup(rr + tq, lambda b: rb_ref[b, h]) - far

    kidx = kiwk_ref[:, 0:HEAD].astype(BF16)
    acc = jnp.zeros(shape, F32)
    for h in range(IDX_HEADS):
        qh = (qi_ref[:, h * HEAD:(h + 1) * HEAD] * (HEAD ** -0.5)).astype(BF16)
        wh = kiwq_ref[:, HEAD + h:HEAD + h + 1] * (IDX_HEADS ** -0.5)
        acc = acc + jnp.maximum(_dot_nt(qh, kidx), 0.0) * wh
    kpos = lax.broadcasted_iota(jnp.int32, shape, 1)
    qpos = q0 + lax.broadcasted_iota(jnp.int32, shape, 0)
    score = jnp.where(kpos <= qpos, acc, _NEG_INF)
    sel = _select_topk(score, kpos, k_sel, bisect_steps)
    m_scr[...] = jnp.where(sel, 0.0, _NEG_INF)

    kn = kn_ref[...].astype(BF16)
    vv = kv_ref[:, ATT_KV_HEADS * HEAD:2 * ATT_KV_HEADS * HEAD].astype(BF16)
    qg = qg_ref[...]
    for h in range(ATT_HEADS):
        kvh = h // ATT_GROUP
        hs = slice(h * HEAD, (h + 1) * HEAD)
        ks = slice(kvh * HEAD, (kvh + 1) * HEAD)
        qh = q_ref[:, hs]
        qh = qh * lax.rsqrt(jnp.mean(qh * qh, axis=-1, keepdims=True) + NORM_EPS) * qg
        s_scr[...] = (_dot_nt(qh.astype(BF16), kn[:, ks]) * (HEAD ** -0.5) + rb_ref[REL_BUCKETS - 1, h]
                      + m_scr[...])
        s_scr[:, pl.ds(q0, tq)] += tbd_ref[h, 0]

        @pl.when(qt > 0)
        def _():
            s_scr[:, pl.ds(pl.multiple_of(q0 - tq, tq), tq)] += tbd_ref[h, 1]

        s = s_scr[...]
        p = jnp.exp(s - jnp.max(s, axis=-1, keepdims=True))
        l = jnp.sum(p, axis=-1, keepdims=True)
        o = jnp.dot(p.astype(BF16), vv[:, ks], preferred_element_type=F32) / l
        gh = g_ref[:, hs]
        o_ref[:, hs] = o * (gh * _sigmoid(gh))


def _attn_prompt(za, kn, rel_bias, qn_g, *, batch, seq_len, tq, k_sel):
    m = za.shape[0]
    nq = seq_len // tq
    w = ATT_KV_HEADS * HEAD
    assert REL_MAX_DIST <= tq
    row_map = lambda blk: (lambda b, t, blk=blk: (b * nq + t, blk))
    key_map = lambda blk: (lambda b, t, blk=blk: (b, blk))
    kern = functools.partial(_attn_prompt_kernel, tq=tq, k_sel=k_sel, bisect_steps=14)
    return pl.pallas_call(
        kern, grid=(batch, nq),
        in_specs=[
            pl.BlockSpec(memory_space=pltpu.SMEM),
            pl.BlockSpec((tq, D_MODEL), row_map(ZA_Q // D_MODEL)),
            pl.BlockSpec((tq, IDX_HEADS * HEAD), row_map(ZA_QI // (IDX_HEADS * HEAD))),
            pl.BlockSpec((tq, LANES), row_map(ZA_KIW // LANES)),
            pl.BlockSpec((tq, D_MODEL), row_map(ZA_GATT // D_MODEL)),
            pl.BlockSpec((seq_len, w), key_map(0)),
            pl.BlockSpec((seq_len, 2 * w), key_map(ZA_KV // (2 * w))),
            pl.BlockSpec((seq_len, LANES), key_map(ZA_KIW // LANES)),
            pl.BlockSpec((1, HEAD), lambda b, t: (0, 0)),
        ],
        out_specs=pl.BlockSpec((tq, D_MODEL), row_map(0)),
        out_shape=jax.ShapeDtypeStruct((m, D_MODEL), F32),
        scratch_shapes=[pltpu.VMEM((tq, seq_len), F32), pltpu.VMEM((tq, seq_len), F32),
                        pltpu.VMEM((ATT_HEADS, 2, tq, tq), F32)],
        compiler_params=_cparams(2), name="attn_prompt")(
            rel_bias, za, za, za, za, kn, za, za, qn_g.reshape(1, HEAD))


def _sample_score_kernel(pt_ref, q_ref, w_ref, kiw_new_ref, *rest, n_pages, page, dec_seq):
    page_refs = rest[:n_pages]
    o_ref = rest[n_pages]
    past = n_pages * page
    kidx = jnp.concatenate([r[...] for r in page_refs], axis=0).astype(BF16)
    new = kiw_new_ref[:, 0:HEAD]
    new = jnp.concatenate([new, jnp.zeros((LANES - dec_seq, HEAD), F32)], axis=0).astype(BF16)
    q = (q_ref[...] * (HEAD ** -0.5)).astype(BF16)
    lg = jnp.concatenate([_dot_nt(q, kidx), _dot_nt(q, new)], axis=1)
    wr = jnp.maximum(lg, 0.0) * (w_ref[...] * (IDX_HEADS ** -0.5))
    n_keys = past + LANES
    sc = jnp.sum(wr.reshape(dec_seq, IDX_HEADS, n_keys), axis=1)
    kpos = lax.broadcasted_iota(jnp.int32, (dec_seq, n_keys), 1)
    tpos = lax.broadcasted_iota(jnp.int32, (dec_seq, n_keys), 0)
    o_ref[...] = jnp.where(kpos <= past + tpos, sc, _NEG_INF)


def _sample_select_kernel(s_ref, o_ref, *, k_sel, bisect_steps):
    score = s_ref[...]
    kpos = lax.broadcasted_iota(jnp.int32, score.shape, 1)
    sel = _select_topk(score, kpos, k_sel, bisect_steps)
    o_ref[...] = jnp.where(sel, 0.0, _NEG_INF)


def _sample_attn_kernel(pt_ref, q_ref, g_ref, kn_new_ref, kv_new_ref, mask_ref, rb_rows_ref, qg_ref, *rest,
                        n_pages, page, dec_seq):
    k_pages = rest[:n_pages]
    v_pages = rest[n_pages:2 * n_pages]
    o_ref = rest[2 * n_pages]
    bias_ref = rest[2 * n_pages + 1]
    w = ATT_KV_HEADS * HEAD
    rows = dec_seq * ATT_HEADS
    n_keys = n_pages * page + LANES

    @pl.when(pl.program_id(0) == 0)
    def _():
        t_row = lax.div(lax.broadcasted_iota(jnp.int32, (rows, n_keys), 0), ATT_HEADS)
        dist = n_pages * page + t_row - lax.broadcasted_iota(jnp.int32, (rows, n_keys), 1)
        bias_ref[...] = _rel_bias_lookup(dist, lambda b: rb_rows_ref[:, b:b + 1])

    pad = jnp.zeros((LANES - dec_seq, w), F32)
    k_all = jnp.concatenate([r[...] for r in k_pages] + [kn_new_ref[...], pad], axis=0).astype(BF16)
    v_all = jnp.concatenate([r[...] for r in v_pages] + [kv_new_ref[:, w:2 * w], pad], axis=0).astype(BF16)
    q = q_ref[...]
    q = q * lax.rsqrt(jnp.mean(q * q, axis=-1, keepdims=True) + NORM_EPS) * qg_ref[...]
    qb = q.astype(BF16)
    head = lax.rem(lax.broadcasted_iota(jnp.int32, (rows, 1), 0), ATT_HEADS)
    first = head < ATT_GROUP
    lg = jnp.where(first, _dot_nt(qb, k_all[:, 0:HEAD]), _dot_nt(qb, k_all[:, HEAD:2 * HEAD]))
    mask = jnp.broadcast_to(mask_ref[...][:, None, :], (dec_seq, ATT_HEADS, n_keys)).reshape(rows, n_keys)
    s = lg * (HEAD ** -0.5) + bias_ref[...] + mask
    p = jnp.exp(s - jnp.max(s, axis=-1, keepdims=True))
    l = jnp.sum(p, axis=-1, keepdims=True)
    pb = p.astype(BF16)
    o = jnp.where(first, jnp.dot(pb, v_all[:, 0:HEAD], preferred_element_type=F32),
                  jnp.dot(pb, v_all[:, HEAD:2 * HEAD], preferred_element_type=F32)) / l
    g = g_ref[...]
    o_ref[...] = o * (g * _sigmoid(g))


def _attn_sample(za, kn, cache_k, cache_v, cache_kidx, page_table, rel_bias, qn_g, *, dec_seq, k_sel):
    nb, n_pages = page_table.shape
    n_phys, page = cache_k.shape[0], cache_k.shape[1]
    past = n_pages * page
    n_keys = past + LANES
    w = ATT_KV_HEADS * HEAD
    ck = cache_k.reshape(n_phys, page, w)
    cv = cache_v.reshape(n_phys, page, w)
    ci = cache_kidx.reshape(n_phys, page, HEAD)
    za3 = za.reshape(nb, dec_seq, ZA_COLS)
    kn3 = kn.reshape(nb, dec_seq, w)
    qi = za[:, ZA_QI:ZA_QI + IDX_HEADS * HEAD].reshape(nb, dec_seq * IDX_HEADS, HEAD)
    wi = za[:, ZA_KIW + HEAD:ZA_KIW + HEAD + IDX_HEADS].reshape(nb, dec_seq * IDX_HEADS, 1)
    qa = za[:, ZA_Q:ZA_Q + D_MODEL].reshape(nb, dec_seq * ATT_HEADS, HEAD)
    ga = za[:, ZA_GATT:ZA_GATT + D_MODEL].reshape(nb, dec_seq * ATT_HEADS, HEAD)

    def page_specs(width):
        return [pl.BlockSpec((None, page, width), lambda b, pt, j=j: (pt[b, j], 0, 0)) for j in range(n_pages)]

    per_b = lambda r, c: pl.BlockSpec((None, r, c), lambda b, pt: (b, 0, 0))
    kiw_new = pl.BlockSpec((None, dec_seq, LANES), lambda b, pt: (b, 0, ZA_KIW // LANES))

    scores = pl.pallas_call(
        functools.partial(_sample_score_kernel, n_pages=n_pages, page=page, dec_seq=dec_seq),
        grid_spec=pltpu.PrefetchScalarGridSpec(
            num_scalar_prefetch=1, grid=(nb,),
            in_specs=[per_b(dec_seq * IDX_HEADS, HEAD), per_b(dec_seq * IDX_HEADS, 1), kiw_new] + page_specs(HEAD),
            out_specs=per_b(dec_seq, n_keys)),
        out_shape=jax.ShapeDtypeStruct((nb, dec_seq, n_keys), F32),
        compiler_params=_cparams(1), name="sample_scores")(page_table, qi, wi, za3, *([ci] * n_pages))

    rows = nb * dec_seq
    tr = min(rows, 128)
    mask = pl.pallas_call(
        functools.partial(_sample_select_kernel, k_sel=k_sel, bisect_steps=14),
        grid=(rows // tr,),
        in_specs=[pl.BlockSpec((tr, n_keys), lambda i: (i, 0))],
        out_specs=pl.BlockSpec((tr, n_keys), lambda i: (i, 0)),
        out_shape=jax.ShapeDtypeStruct((rows, n_keys), F32),
        compiler_params=_cparams(1), name="sample_select")(scores.reshape(rows, n_keys))

    rb_rows = jnp.tile(rel_bias.T, (dec_seq, 1))

    const = lambda shape: pl.BlockSpec(shape, lambda b, pt: (0,) * len(shape))
    kv_new = pl.BlockSpec((None, dec_seq, 2 * w), lambda b, pt: (b, 0, ZA_KV // (2 * w)))
    out = pl.pallas_call(
        functools.partial(_sample_attn_kernel, n_pages=n_pages, page=page, dec_seq=dec_seq),
        grid_spec=pltpu.PrefetchScalarGridSpec(
            num_scalar_prefetch=1, grid=(nb,),
            in_specs=[per_b(dec_seq * ATT_HEADS, HEAD), per_b(dec_seq * ATT_HEADS, HEAD), per_b(dec_seq, w),
                      kv_new, per_b(dec_seq, n_keys), const((dec_seq * ATT_HEADS, REL_BUCKETS)), const((1, HEAD))]
            + page_specs(w) + page_specs(w),
            out_specs=per_b(dec_seq * ATT_HEADS, HEAD),
            scratch_shapes=[pltpu.VMEM((dec_seq * ATT_HEADS, n_keys), F32)]),
        out_shape=jax.ShapeDtypeStruct((nb, dec_seq * ATT_HEADS, HEAD), F32),
        compiler_params=_cparams(1), name="sample_attn")(
            page_table, qa, ga, kn3, za3, mask.reshape(nb, dec_seq, n_keys), rb_rows, qn_g.reshape(1, HEAD),
            *([ck] * n_pages), *([cv] * n_pages))
    return out.reshape(rows, D_MODEL)


def _merge_kernel(x_ref, oa_ref, sg_ref, ob_ref, ga_ref, gb_ref, wpa_ref, wpb_ref, wo_ref, y_ref):
    oa = (oa_ref[...] * sg_ref[...]).astype(BF16)
    pa = jnp.dot(oa, wpa_ref[...], preferred_element_type=F32)
    pb = jnp.dot(ob_ref[...].astype(BF16), wpb_ref[...], preferred_element_type=F32)
    merged = _sigmoid(ga_ref[...]) * pa + _sigmoid(gb_ref[...]) * pb
    y_ref[...] = x_ref[...] + jnp.dot(merged.astype(BF16), wo_ref[...], preferred_element_type=F32)


def _merge(x, oa, sg, ob, za, w_pa, w_pb, w_out, tm):
    m, d = x.shape
    row = pl.BlockSpec((tm, d), lambda i: (i, 0))
    wsp = pl.BlockSpec((d, d), lambda i: (0, 0))
    return pl.pallas_call(
        _merge_kernel, grid=(m // tm,),
        in_specs=[row, row, row, row,
                  pl.BlockSpec((tm, d), lambda i: (i, ZA_GA // d)), pl.BlockSpec((tm, d), lambda i: (i, ZA_GB // d)),
                  wsp, wsp, wsp],
        out_specs=row,
        out_shape=jax.ShapeDtypeStruct((m, d), F32),
        compiler_params=_cparams(1), name="merge")(x, oa, sg, ob, za, za, w_pa, w_pb, w_out)


def _layer(x, shift, s0, params, *, batch, seq_len, attend):
    (norm_g, w_r, w_a, mu, w0, w2, a0, a2, k_k, k_a, r_k, lnx_g, lnx_b, kn_g, w_pa, w_pb, w_out) = params
    m = batch * seq_len
    tm = min(m, 512)
    xn = _rmsnorm(x, norm_g, tm)
    zr = _matmul(xn, w_r, tm, RWKV_COLS // 3, "inproj_rwkv")
    za = _matmul(xn, w_a, tm, ZA_COLS // 4, "inproj_attn")

    tile = min(m, 256)
    r, w, k, v, kk, a, sg = _rwkv_prep(zr, shift, mu, w0, w2, a0, a2, k_k, k_a,
                                       batch=batch, seq_len=seq_len, tile=tile)
    seqs = [_to_pairs(t, batch, seq_len) for t in (r, w, k, v, kk, a)]
    consts = [_head_const_pairs(t, batch) for t in (lnx_g, lnx_b, r_k.reshape(-1))]
    o_t, s_t = _rwkv_scan(seqs, s0, *consts, steps=min(seq_len, 32))
    oa = _from_pairs(o_t, batch, seq_len)

    kn = _knorm(za, kn_g, tm)
    ob = attend(za, kn)

    y = _merge(x, oa, sg, ob, za, w_pa, w_pb, w_out, min(m, 256))
    return y, zr, za, kn, s_t


def _state_to_pairs(s):
    b, h, n, _ = s.shape
    return jnp.transpose(s, (3, 2, 0, 1)).reshape(n, n, b * h)


def _state_from_pairs(s, batch):
    n = s.shape[0]
    return jnp.transpose(s.reshape(n, n, batch, RWKV_HEADS), (2, 3, 1, 0))


def kernel(x_prompt, x_sample, cache_k, cache_v, cache_kidx, state_wkv, state_shift, page_table, norm_g, w_in,
           shift_mu, w0, w2, a0, a2, k_k, k_a, r_k, lnx_g, lnx_b, q_norm_g, k_norm_g, rel_bias, w_pa, w_pb, w_out):
    bsz, seq, d = x_prompt.shape
    dec_bsz, dec_seq, _ = x_sample.shape
    depth = w_in.shape[0]
    assert depth == 1 and d == D_MODEL
    past_len = page_table.shape[1] * cache_k.shape[2]
    topk_p = min(TOPK_MAX, seq // 4)
    topk_s = min(TOPK_MAX, (past_len + dec_seq) // 4)
    l = 0

    wl = w_in[l]
    c0 = RWKV_COLS
    q_w, kv_w, qi_w = wl[:, c0:c0 + 1024], wl[:, c0 + 1024:c0 + 1280], wl[:, c0 + 1280:c0 + 1792]
    kiw_w = wl[:, c0 + 1792:c0 + 1864]
    rest_w = wl[:, c0 + 1864:]
    zpad = lambda n: jnp.zeros((d, n), wl.dtype)
    w_a = jnp.concatenate([q_w, qi_w, kv_w, kiw_w, zpad(LANES - kiw_w.shape[1]), zpad(LANES), rest_w],
                          axis=1).astype(BF16)
    assert w_a.shape[1] == ZA_COLS
    w_r = wl[:, :c0].astype(BF16)
    params = (norm_g[l], w_r, w_a, shift_mu[l], w0[l], w2[l], a0[l], a2[l], k_k[l], k_a[l], r_k[l],
              lnx_g[l], lnx_b[l], k_norm_g[l], w_pa[l].astype(BF16), w_pb[l].astype(BF16), w_out[l].astype(BF16))

    xp = x_prompt.reshape(bsz * seq, d)
    attend_p = functools.partial(_attn_prompt, rel_bias=rel_bias, qn_g=q_norm_g[l], batch=bsz, seq_len=seq,
                                 tq=128, k_sel=topk_p)
    yp, zr_p, za_p, kn_p, st_p = _layer(
        xp, jnp.zeros((bsz, 1, RWKV_COLS), F32), jnp.zeros((HEAD, HEAD, bsz * RWKV_HEADS), F32), params,
        batch=bsz, seq_len=seq, attend=attend_p)

    xs = x_sample.reshape(dec_bsz * dec_seq, d)
    attend_s = functools.partial(_attn_sample, cache_k=cache_k[l], cache_v=cache_v[l], cache_kidx=cache_kidx[l],
                                 page_table=page_table, rel_bias=rel_bias, qn_g=q_norm_g[l], dec_seq=dec_seq,
                                 k_sel=topk_s)
    shift_rows = jnp.repeat(state_shift[l], dec_seq, axis=0)
    ys, zr_s, za_s, kn_s, st_s = _layer(
        xs, shift_rows, _state_to_pairs(state_wkv[l]), params, batch=dec_bsz, seq_len=dec_seq, attend=attend_s)

    w = ATT_KV_HEADS * HEAD

    def pack(y, zr, za, kn, st, b, t):
        v = za[:, ZA_KV + w:ZA_KV + 2 * w]
        kidx = za[:, ZA_KIW:ZA_KIW + HEAD]
        return (y.reshape(b, t, d),
                kn.reshape(1, b, t, ATT_KV_HEADS, HEAD), v.reshape(1, b, t, ATT_KV_HEADS, HEAD),
                kidx.reshape(1, b, t, HEAD), _state_from_pairs(st, b)[None],
                zr.reshape(b, t, RWKV_COLS)[:, -1][None])

    p = pack(yp, zr_p, za_p, kn_p, st_p, bsz, seq)
    s = pack(ys, zr_s, za_s, kn_s, st_s, dec_bsz, dec_seq)
    return (p[0], s[0]) + p[1:] + s[1:]
```

```python
import functools
import math

import numpy as np
import jax
import jax.numpy as jnp
from jax import lax
from jax.experimental import pallas as pl
from jax.experimental.pallas import tpu as pltpu

F32 = jnp.float32
BF16 = jnp.bfloat16

D_MODEL = 1024
HEAD = 64
RWKV_HEADS = D_MODEL // HEAD
LORA = 64
LNX_EPS = 64e-5
ATT_HEADS = D_MODEL // HEAD
ATT_KV_HEADS = 2
ATT_GROUP = ATT_HEADS // ATT_KV_HEADS
IDX_HEADS = 8
TOPK_MAX = 256
REL_BUCKETS = 32
REL_MAX_DIST = 128
NORM_EPS = 1e-6
RWKV_COLS = 4 * D_MODEL + 2 * LORA

LANES = 128
SUBLANES = 8
VMEM_LIMIT_BYTES = 48 * 1024 * 1024

ZA_Q = 0
ZA_QI = 1024
ZA_KV = 1536
ZA_KIW = 1792
ZA_GATT = 2048
ZA_GA = 3072
ZA_GB = 4096
ZA_COLS = 5120

_NEG_INF = float("-inf")
_POS_INF = float("inf")


def _cparams(n_axes):
    return pltpu.CompilerParams(dimension_semantics=("arbitrary",) * n_axes,
                                vmem_limit_bytes=VMEM_LIMIT_BYTES)


def _sigmoid(x):
    return 1.0 / (1.0 + jnp.exp(-x))


def _dot_nt(a, b):
    return lax.dot_general(a, b, (((1,), (1,)), ((), ())), preferred_element_type=F32)


def _rmsnorm_kernel(x_ref, g_ref, o_ref):
    x = x_ref[...]
    ms = jnp.mean(x * x, axis=-1, keepdims=True)
    o_ref[...] = (x * lax.rsqrt(ms + NORM_EPS) * g_ref[...]).astype(o_ref.dtype)


def _rmsnorm(x, g, tm):
    m, d = x.shape
    return pl.pallas_call(
        _rmsnorm_kernel, grid=(m // tm,),
        in_specs=[pl.BlockSpec((tm, d), lambda i: (i, 0)), pl.BlockSpec((1, d), lambda i: (0, 0))],
        out_specs=pl.BlockSpec((tm, d), lambda i: (i, 0)),
        out_shape=jax.ShapeDtypeStruct((m, d), BF16),
        compiler_params=_cparams(1), name="rmsnorm")(x, g.reshape(1, d))


def _mm_kernel(x_ref, w_ref, o_ref):
    o_ref[...] = jnp.dot(x_ref[...], w_ref[...], preferred_element_type=F32)


def _matmul(x, w, tm, tn, name):
    m, k = x.shape
    n = w.shape[1]
    return pl.pallas_call(
        _mm_kernel, grid=(m // tm, n // tn),
        in_specs=[pl.BlockSpec((tm, k), lambda i, j: (i, 0)), pl.BlockSpec((k, tn), lambda i, j: (0, j))],
        out_specs=pl.BlockSpec((tm, tn), lambda i, j: (i, j)),
        out_shape=jax.ShapeDtypeStruct((m, n), F32),
        compiler_params=_cparams(2), name=name)(x, w)


def _rwkv_prep_kernel(z_ref, prev_ref, shift_ref, mu_ref, w0_ref, w2_ref, a0_ref, a2_ref, kk_ref, ka_ref,
                      r_o, w_o, k_o, v_o, kk_o, a_o, g_o, *, seq_len, tile):
    z = z_ref[...]
    rows = lax.broadcasted_iota(jnp.int32, z.shape, 0)
    rolled = pltpu.roll(z, 1, 0)
    if seq_len >= tile:
        first = jnp.where(pl.program_id(1) == 0, shift_ref[...], prev_ref[SUBLANES - 1:SUBLANES, :])
        prev = jnp.where(rows == 0, first, rolled)
    else:
        prev = jnp.where(lax.rem(rows, seq_len) == 0, shift_ref[...], rolled)
    zs = z + (prev - z) * mu_ref[...]
    d = D_MODEL
    r = zs[:, 0:d]
    k = zs[:, d:2 * d]
    v = zs[:, 2 * d:3 * d]
    g = zs[:, 3 * d:4 * d]
    wd = zs[:, 4 * d:4 * d + LORA]
    ad = zs[:, 4 * d + LORA:4 * d + 2 * LORA]
    wl = w0_ref[...] + jnp.dot(jnp.tanh(wd).astype(BF16), w2_ref[...], preferred_element_type=F32)
    decay = jnp.exp(-_sigmoid(wl) * math.exp(-0.5))
    a = _sigmoid(a0_ref[...] + jnp.dot(ad.astype(BF16), a2_ref[...], preferred_element_type=F32))
    r_o[...] = r
    w_o[...] = decay
    k_o[...] = k * (1.0 + (a - 1.0) * ka_ref[...])
    v_o[...] = v
    kk_o[...] = k * kk_ref[...]
    a_o[...] = a
    g_o[...] = g * _sigmoid(g)


def _rwkv_prep(zr, shift, mu, w0, w2, a0, a2, k_k, k_a, *, batch, seq_len, tile):
    m, c = zr.shape
    d = D_MODEL
    row = lambda x: x.reshape(1, -1)
    consts = [row(mu), row(w0), w2.astype(BF16), row(a0), a2.astype(BF16), row(k_k), row(k_a)]
    const_specs = [pl.BlockSpec(x.shape, lambda *_: (0, 0)) for x in consts]
    if seq_len >= tile:
        nt = seq_len // tile
        grid = (batch, nt)
        zmap = lambda b, t: (b * nt + t, 0)
        pmap = lambda b, t: (jnp.maximum((b * seq_len + t * tile) // SUBLANES - 1, 0), 0)
        shift_spec = pl.BlockSpec((None, 1, c), lambda b, t: (b, 0, 0))
    else:
        grid = (1, m // tile)
        zmap = lambda b, t: (t, 0)
        pmap = lambda b, t: (0, 0)
        shift_spec = pl.BlockSpec((tile, c), zmap)
    out_spec = pl.BlockSpec((tile, d), zmap)
    kern = functools.partial(_rwkv_prep_kernel, seq_len=seq_len, tile=tile)
    return pl.pallas_call(
        kern, grid=grid,
        in_specs=[pl.BlockSpec((tile, c), zmap), pl.BlockSpec((SUBLANES, c), pmap), shift_spec] + const_specs,
        out_specs=[out_spec] * 7,
        out_shape=[jax.ShapeDtypeStruct((m, d), F32)] * 7,
        compiler_params=_cparams(2), name="rwkv_prep")(zr, zr, shift, *consts)


def _rwkv_scan_kernel(r_ref, w_ref, k_ref, v_ref, kk_ref, a_ref, s0_ref, lg_ref, lb_ref, rk_ref,
                      o_ref, s_ref, vec_ref, *, steps):
    nb = HEAD // SUBLANES

    @pl.when(pl.program_id(1) == 0)
    def _():
        s_ref[...] = s0_ref[...]

    def bcast_row(ref, lead, j):
        return jnp.broadcast_to(ref[lead, pl.ds(j, 1), :], (SUBLANES, LANES))

    def step(t, carry):
        kkraw = kk_ref[t]
        n2 = jnp.sum(kkraw * kkraw, axis=0, keepdims=True)
        kk = kkraw / jnp.maximum(jnp.sqrt(n2), 1e-12)
        vec_ref[0] = -kk
        vec_ref[1] = kk * a_ref[t]
        sa = [jnp.zeros((SUBLANES, LANES), F32) for _ in range(nb)]
        for j in range(HEAD):
            nk = bcast_row(vec_ref, 0, j)
            for ib in range(nb):
                sa[ib] = sa[ib] + s_ref[j, ib * SUBLANES:(ib + 1) * SUBLANES, :] * nk
        vt = [v_ref[t, ib * SUBLANES:(ib + 1) * SUBLANES, :] for ib in range(nb)]
        out = [jnp.zeros((SUBLANES, LANES), F32) for _ in range(nb)]
        for j in range(HEAD):
            wj = bcast_row(w_ref, t, j)
            kaj = bcast_row(vec_ref, 1, j)
            kj = bcast_row(k_ref, t, j)
            rj = bcast_row(r_ref, t, j)
            for ib in range(nb):
                sl = slice(ib * SUBLANES, (ib + 1) * SUBLANES)
                sn = s_ref[j, sl, :] * wj + sa[ib] * kaj + vt[ib] * kj
                s_ref[j, sl, :] = sn
                out[ib] = out[ib] + sn * rj
        o = jnp.concatenate(out, axis=0)
        mean = jnp.mean(o, axis=0, keepdims=True)
        dev = o - mean
        var = jnp.mean(dev * dev, axis=0, keepdims=True)
        y = dev * lax.rsqrt(var + LNX_EPS) * lg_ref[...] + lb_ref[...]
        bonus = jnp.sum(r_ref[t] * k_ref[t] * rk_ref[...], axis=0, keepdims=True) * v_ref[t]
        o_ref[t] = y + bonus
        return carry

    lax.fori_loop(0, steps, step, 0)


def _rwkv_scan(seqs, s0, lg_t, lb_t, rk_t, *, steps):
    t_len, n, p = seqs[0].shape
    grid = (p // LANES, t_len // steps)
    seq_spec = pl.BlockSpec((steps, n, LANES), lambda g, t: (t, 0, g))
    st_spec = pl.BlockSpec((n, n, LANES), lambda g, t: (0, 0, g))
    c_spec = pl.BlockSpec((n, LANES), lambda g, t: (0, g))
    kern = functools.partial(_rwkv_scan_kernel, steps=steps)
    return pl.pallas_call(
        kern, grid=grid,
        in_specs=[seq_spec] * 6 + [st_spec] + [c_spec] * 3,
        out_specs=[seq_spec, st_spec],
        out_shape=[jax.ShapeDtypeStruct((t_len, n, p), F32), jax.ShapeDtypeStruct((n, n, p), F32)],
        scratch_shapes=[pltpu.VMEM((2, n, LANES), F32)],
        compiler_params=_cparams(2), name="rwkv_scan")(*seqs, s0, lg_t, lb_t, rk_t)


def _to_pairs(x, batch, seq_len):
    x = x.reshape(batch, seq_len, RWKV_HEADS, HEAD)
    return jnp.transpose(x, (1, 3, 0, 2)).reshape(seq_len, HEAD, batch * RWKV_HEADS)


def _from_pairs(x, batch, seq_len):
    x = x.reshape(seq_len, HEAD, batch, RWKV_HEADS)
    return jnp.transpose(x, (2, 0, 3, 1)).reshape(batch * seq_len, RWKV_HEADS * HEAD)


def _head_const_pairs(x, batch):
    x = x.reshape(RWKV_HEADS, HEAD).T
    return jnp.tile(x, (1, batch))


def _count_ge(score, thr):
    return jnp.sum(jnp.where(score >= thr, 1.0, 0.0), axis=-1, keepdims=True)


def _select_topk(score, kpos, k_sel, bisect_steps):
    kf = float(k_sel)
    vis = score > _NEG_INF
    nvis = jnp.sum(jnp.where(vis, 1.0, 0.0), axis=-1, keepdims=True)
    need_sel = nvis > kf
    rowmax = jnp.where(need_sel, jnp.max(score, axis=-1, keepdims=True), 0.0)
    rowmin = jnp.where(need_sel, jnp.min(jnp.where(vis, score, _POS_INF), axis=-1, keepdims=True), 0.0)

    def bis(_, c):
        lo, hi, chi = c
        hfin = jnp.where(hi == _POS_INF, rowmax, hi)
        piv = 0.5 * lo + 0.5 * hfin
        cnt = _count_ge(score, piv)
        ge = cnt >= kf
        return jnp.where(ge, piv, lo), jnp.where(ge, hi, piv), jnp.where(ge, chi, cnt)

    lo, hi, chi = lax.fori_loop(0, bisect_steps, bis,
                                (rowmin, jnp.full_like(rowmin, _POS_INF), jnp.zeros_like(rowmin)))

    def walk_cond(c):
        return jnp.min(c[4]) < 0.5

    def walk(c):
        hi, chi, tau, cgt, done, ceq = c
        pending = done < 0.5
        bmax = jnp.max(jnp.where(score < hi, score, _NEG_INF), axis=-1, keepdims=True)
        cnt = _count_ge(score, bmax)
        fin = jnp.logical_and(cnt >= kf, pending)
        tau = jnp.where(fin, bmax, tau)
        cgt = jnp.where(fin, chi, cgt)
        ceq = jnp.where(fin, cnt - chi, ceq)
        adv = jnp.logical_and(cnt < kf, pending)
        hi = jnp.where(adv, bmax, hi)
        chi = jnp.where(adv, cnt, chi)
        return hi, chi, tau, cgt, jnp.where(fin, 1.0, done), ceq

    done0 = jnp.where(need_sel, 0.0, 1.0)
    neg = jnp.full_like(rowmin, _NEG_INF)
    zero = jnp.zeros_like(rowmin)
    _, _, tau, cgt, _, ceq = lax.while_loop(walk_cond, walk, (hi, chi, neg, zero, done0, zero))

    need = kf - cgt
    excess = jnp.logical_and(need_sel, ceq > need)
    eq = score == tau
    n_keys = score.shape[-1]

    def tie_break(_):
        def body(_, c):
            plo, phi = c
            mid = lax.shift_right_arithmetic(plo + phi, 1)
            cnt = jnp.sum(jnp.where(jnp.logical_and(eq, kpos <= mid), 1.0, 0.0), axis=-1, keepdims=True)
            ge = cnt >= need
            return jnp.where(ge, plo, mid), jnp.where(ge, mid, phi)
        plo0 = jnp.full(tau.shape, -1, jnp.int32)
        phi0 = jnp.full(tau.shape, n_keys - 1, jnp.int32)
        _, phi = lax.fori_loop(0, n_keys.bit_length() + 1, body, (plo0, phi0))
        return jnp.where(excess, phi, n_keys)

    any_excess = jnp.max(jnp.where(excess, 1.0, 0.0)) > 0.0
    pcut = lax.cond(any_excess, tie_break, lambda _: jnp.full(tau.shape, n_keys, jnp.int32), 0)
    pcut = jnp.where(need_sel, pcut, -1)
    return jnp.logical_or(score > tau, jnp.logical_and(eq, kpos <= pcut))


_MASKED = -1e30


def _select_topk_chunked(s_ref, nch, ck, k_sel, nvis, bisect_steps):
    rows = s_ref.shape[0]
    kf = float(k_sel)
    nfold = ck // LANES

    def chunk(c):
        return s_ref[:, pl.ds(pl.multiple_of(c * ck, ck), ck)]

    def fold(x, op):
        out = x[:, 0:LANES]
        for i in range(1, nfold):
            out = op(out, x[:, i * LANES:(i + 1) * LANES])
        return out

    def count(pred):
        def body(c, acc):
            return acc + fold(jnp.where(pred(chunk(c), c), 1.0, 0.0), jnp.add)
        acc = lax.fori_loop(0, nch, body, jnp.zeros((rows, LANES), F32))
        return jnp.sum(acc, axis=-1, keepdims=True)

    def row_max(val):
        def body(c, acc):
            return jnp.maximum(acc, fold(val(chunk(c)), jnp.maximum))
        acc = lax.fori_loop(0, nch, body, jnp.full((rows, LANES), _NEG_INF, F32))
        return jnp.max(acc, axis=-1, keepdims=True)

    need_sel = nvis > kf
    rowmax = jnp.where(need_sel, row_max(lambda x: x), 0.0)
    rowmin = jnp.where(need_sel, -row_max(lambda x: jnp.where(x > _NEG_INF, -x, _NEG_INF)), 0.0)

    def bis(_, c):
        lo, hi, chi = c
        hfin = jnp.where(hi == _POS_INF, rowmax, hi)
        piv = 0.5 * lo + 0.5 * hfin
        cnt = count(lambda x, _c: x >= piv)
        ge = cnt >= kf
        return jnp.where(ge, piv, lo), jnp.where(ge, hi, piv), jnp.where(ge, chi, cnt)

    lo, hi, chi = lax.fori_loop(0, bisect_steps, bis,
                                (rowmin, jnp.full_like(rowmin, _POS_INF), jnp.zeros_like(rowmin)))

    def walk_cond(c):
        return jnp.min(c[4]) < 0.5

    def walk(c):
        hi, chi, tau, cgt, done, ceq = c
        pending = done < 0.5
        bmax = row_max(lambda x: jnp.where(x < hi, x, _NEG_INF))
        cnt = count(lambda x, _c: x >= bmax)
        fin = jnp.logical_and(cnt >= kf, pending)
        tau = jnp.where(fin, bmax, tau)
        cgt = jnp.where(fin, chi, cgt)
        ceq = jnp.where(fin, cnt - chi, ceq)
        adv = jnp.logical_and(cnt < kf, pending)
        hi = jnp.where(adv, bmax, hi)
        chi = jnp.where(adv, cnt, chi)
        return hi, chi, tau, cgt, jnp.where(fin, 1.0, done), ceq

    neg = jnp.full_like(rowmin, _NEG_INF)
    zero = jnp.zeros_like(rowmin)
    _, _, tau, cgt, _, ceq = lax.while_loop(
        walk_cond, walk, (hi, chi, neg, zero, jnp.where(need_sel, 0.0, 1.0), zero))

    need = kf - cgt
    excess = jnp.logical_and(need_sel, ceq > need)
    n_keys = s_ref.shape[1]

    def kpos(c):
        return c * ck + lax.broadcasted_iota(jnp.int32, (rows, ck), 1)

    def tie_break(_):
        def body(_, c):
            plo, phi = c
            mid = lax.shift_right_arithmetic(plo + phi, 1)
            cnt = count(lambda x, cc: jnp.logical_and(x == tau, kpos(cc) <= mid))
            ge = cnt >= need
            return jnp.where(ge, plo, mid), jnp.where(ge, mid, phi)
        plo0 = jnp.full(tau.shape, -1, jnp.int32)
        phi0 = jnp.full(tau.shape, n_keys - 1, jnp.int32)
        _, phi = lax.fori_loop(0, n_keys.bit_length() + 1, body, (plo0, phi0))
        return jnp.where(excess, phi, n_keys)

    any_excess = jnp.max(jnp.where(excess, 1.0, 0.0)) > 0.0
    pcut = lax.cond(any_excess, tie_break, lambda _: jnp.full(tau.shape, n_keys, jnp.int32), 0)
    pcut = jnp.where(need_sel, pcut, -1)

    def write(c, carry):
        x = chunk(c)
        sel = jnp.logical_or(x > tau, jnp.logical_and(x == tau, kpos(c) <= pcut))
        s_ref[:, pl.ds(pl.multiple_of(c * ck, ck), ck)] = jnp.where(sel, 0.0, _MASKED)
        return carry

    lax.fori_loop(0, nch, write, 0)


def _attn_prep_kernel(q_ref, qi_ref, kv_ref, kiw_ref, qg_ref, kg_ref, kn_o, qb_o, qib_o, kvb_o, kib_o):
    w = ATT_KV_HEADS * HEAD
    for h in range(ATT_KV_HEADS):
        hs = slice(h * HEAD, (h + 1) * HEAD)
        x = kv_ref[:, hs]
        kn = x * lax.rsqrt(jnp.mean(x * x, axis=-1, keepdims=True) + NORM_EPS) * kg_ref[...]
        kn_o[:, hs] = kn
        kvb_o[:, hs] = kn.astype(BF16)
    kvb_o[:, w:2 * w] = kv_ref[:, w:2 * w].astype(BF16)
    for h in range(ATT_HEADS):
        hs = slice(h * HEAD, (h + 1) * HEAD)
        x = q_ref[:, hs]
        qn = x * lax.rsqrt(jnp.mean(x * x, axis=-1, keepdims=True) + NORM_EPS) * (qg_ref[...] * HEAD ** -0.5)
        qb_o[:, hs] = qn.astype(BF16)
    qib_o[...] = (qi_ref[...] * HEAD ** -0.5).astype(BF16)
    kib_o[...] = kiw_ref[:, 0:HEAD].astype(BF16)


def _attn_prep(za, qn_g, kn_g, tm):
    m = za.shape[0]
    w = ATT_KV_HEADS * HEAD
    wi = IDX_HEADS * HEAD
    row = lambda width, blk: pl.BlockSpec((tm, width), lambda i, blk=blk: (i, blk))
    gspec = pl.BlockSpec((1, HEAD), lambda i: (0, 0))
    return pl.pallas_call(
        _attn_prep_kernel, grid=(m // tm,),
        in_specs=[row(D_MODEL, ZA_Q // D_MODEL), row(wi, ZA_QI // wi), row(2 * w, ZA_KV // (2 * w)),
                  row(LANES, ZA_KIW // LANES), gspec, gspec],
        out_specs=[row(w, 0), row(D_MODEL, 0), row(wi, 0), row(2 * w, 0), row(HEAD, 0)],
        out_shape=[jax.ShapeDtypeStruct((m, w), F32), jax.ShapeDtypeStruct((m, D_MODEL), BF16),
                   jax.ShapeDtypeStruct((m, wi), BF16), jax.ShapeDtypeStruct((m, 2 * w), BF16),
                   jax.ShapeDtypeStruct((m, HEAD), BF16)],
        compiler_params=_cparams(1), name="attn_prep")(za, za, za, za, qn_g.reshape(1, HEAD), kn_g.reshape(1, HEAD))


def _bucket_edges():
    max_exact = REL_BUCKETS // 2
    d = np.arange(REL_MAX_DIST + 1)
    df = np.maximum(d, 1).astype(np.float32)
    large = max_exact + (np.log(df / max_exact) / math.log(REL_MAX_DIST / max_exact)
                         * (REL_BUCKETS - max_exact)).astype(np.int32)
    bucket = np.where(d < max_exact, d, np.minimum(large, REL_BUCKETS - 1))
    return [int(np.argmax(bucket >= b)) for b in range(REL_BUCKETS)]


_BUCKET_EDGES = _bucket_edges()


def _rel_bias_lookup(dist, value_of_bucket):
    bias = value_of_bucket(REL_BUCKETS - 1)
    for b in range(REL_BUCKETS - 2, -1, -1):
        bias = jnp.where(dist < _BUCKET_EDGES[b + 1], value_of_bucket(b), bias)
    return bias


def _attn_prompt_kernel(rb_ref, q_ref, qi_ref, kiwq_ref, g_ref, kvb_ref, kib_ref, o_ref, s_scr, tbd_ref,
                        *, tq, k_sel, bisect_steps):
    qt = pl.program_id(1)
    q0 = qt * tq
    ck = tq
    rg_rows = LANES
    n_rg = tq // rg_rows
    w = ATT_KV_HEADS * HEAD

    @pl.when(jnp.logical_and(pl.program_id(0) == 0, qt == 0))
    def _():
        rr = lax.broadcasted_iota(jnp.int32, (tq, tq), 0) - lax.broadcasted_iota(jnp.int32, (tq, tq), 1)
        for h in range(ATT_HEADS):
            far = rb_ref[REL_BUCKETS - 1, h]
            tbd_ref[h, 0] = _rel_bias_lookup(rr, lambda b: rb_ref[b, h]) - far
            tbd_ref[h, 1] = _rel_bias_lookup(rr + tq, lambda b: rb_ref[b, h]) - far

    def chunk_start(c):
        return pl.multiple_of(c * ck, ck)

    wcol = kiwq_ref[:, HEAD:HEAD + IDX_HEADS] * (IDX_HEADS ** -0.5)
    qi_h = [[qi_ref[rg * rg_rows:(rg + 1) * rg_rows, h * HEAD:(h + 1) * HEAD] for h in range(IDX_HEADS)]
            for rg in range(n_rg)]
    w_h = [[wcol[rg * rg_rows:(rg + 1) * rg_rows, h:h + 1] for h in range(IDX_HEADS)] for rg in range(n_rg)]

    def score_chunk(c, carry):
        k0 = chunk_start(c)
        kc = kib_ref[pl.ds(k0, ck), :]
        kpos = k0 + lax.broadcasted_iota(jnp.int32, (rg_rows, ck), 1)
        for rg in range(n_rg):
            acc = jnp.zeros((rg_rows, ck), F32)
            for h in range(IDX_HEADS):
                acc = acc + jnp.maximum(_dot_nt(qi_h[rg][h], kc), 0.0) * w_h[rg][h]
            qpos = q0 + rg * rg_rows + lax.broadcasted_iota(jnp.int32, (rg_rows, ck), 0)
            s_scr[rg * rg_rows:(rg + 1) * rg_rows, pl.ds(k0, ck)] = jnp.where(kpos <= qpos, acc, _NEG_INF)
        return carry

    nch = qt + 1
    lax.fori_loop(0, nch, score_chunk, 0)

    nvis = (q0 + 1 + lax.broadcasted_iota(jnp.int32, (tq, 1), 0)).astype(F32)
    _select_topk_chunked(s_scr, nch, ck, k_sel, nvis, bisect_steps)

    n_far = jnp.maximum(qt - 1, 0)
    for h in range(ATT_HEADS):
        kvh = h // ATT_GROUP
        hs = slice(h * HEAD, (h + 1) * HEAD)
        for rg in range(n_rg):
            rs = slice(rg * rg_rows, (rg + 1) * rg_rows)
            qh = q_ref[rs, hs]

            def body(c, carry, near, qh=qh, rs=rs, h=h, kvh=kvh):
                m, l, acc = carry
                k0 = chunk_start(c)
                kc = kvb_ref[pl.ds(k0, ck), kvh * HEAD:(kvh + 1) * HEAD]
                vc = kvb_ref[pl.ds(k0, ck), w + kvh * HEAD:w + (kvh + 1) * HEAD]
                s = _dot_nt(qh, kc) + s_scr[rs, pl.ds(k0, ck)]
                if near:
                    s = s + tbd_ref[h, qt - c, rs, :]
                mn = jnp.maximum(m, jnp.max(s, axis=-1, keepdims=True))
                alpha = jnp.exp(m - mn)
                p = jnp.exp(s - mn)
                l = alpha * l + jnp.sum(p, axis=-1, keepdims=True)
                acc = alpha * acc + jnp.dot(p.astype(BF16), vc, preferred_element_type=F32)
                return mn, l, acc

            init = (jnp.full((rg_rows, 1), _MASKED, F32), jnp.zeros((rg_rows, 1), F32),
                    jnp.zeros((rg_rows, HEAD), F32))
            carry = lax.fori_loop(0, n_far, functools.partial(body, near=False), init)
            _, l, acc = lax.fori_loop(n_far, nch, functools.partial(body, near=True), carry)
            gh = g_ref[rs, hs]
            o_ref[rs, hs] = (acc / l) * (gh * _sigmoid(gh))


def _attn_prompt(za, kn, qb, qib, kvb, kib, *, rel_bias, batch, seq_len, tq, k_sel):
    m = za.shape[0]
    nq = seq_len // tq
    w = ATT_KV_HEADS * HEAD
    wi = IDX_HEADS * HEAD
    assert REL_MAX_DIST <= tq and tq % LANES == 0
    row_map = lambda blk: (lambda b, t, blk=blk: (b * nq + t, blk))
    key_map = lambda b, t: (b, 0)
    kern = functools.partial(_attn_prompt_kernel, tq=tq, k_sel=k_sel, bisect_steps=14)
    return pl.pallas_call(
        kern, grid=(batch, nq),
        in_specs=[
            pl.BlockSpec(memory_space=pltpu.SMEM),
            pl.BlockSpec((tq, D_MODEL), row_map(0)),
            pl.BlockSpec((tq, wi), row_map(0)),
            pl.BlockSpec((tq, LANES), row_map(ZA_KIW // LANES)),
            pl.BlockSpec((tq, D_MODEL), row_map(ZA_GATT // D_MODEL)),
            pl.BlockSpec((seq_len, 2 * w), key_map),
            pl.BlockSpec((seq_len, HEAD), key_map),
        ],
        out_specs=pl.BlockSpec((tq, D_MODEL), row_map(0)),
        out_shape=jax.ShapeDtypeStruct((m, D_MODEL), F32),
        scratch_shapes=[pltpu.VMEM((tq, seq_len), F32), pltpu.VMEM((ATT_HEADS, 2, tq, tq), F32)],
        compiler_params=_cparams(2), name="attn_prompt")(rel_bias, qb, qib, za, za, kvb, kib)


def _sample_score_kernel(pt_ref, q_ref, w_ref, kiw_new_ref, *rest, n_pages, page, dec_seq):
    page_refs = rest[:n_pages]
    o_ref = rest[n_pages]
    past = n_pages * page
    kidx = jnp.concatenate([r[...] for r in page_refs], axis=0).astype(BF16)
    new = kiw_new_ref[:, 0:HEAD]
    new = jnp.concatenate([new, jnp.zeros((LANES - dec_seq, HEAD), F32)], axis=0).astype(BF16)
    q = (q_ref[...] * (HEAD ** -0.5)).astype(BF16)
    lg = jnp.concatenate([_dot_nt(q, kidx), _dot_nt(q, new)], axis=1)
    wr = jnp.maximum(lg, 0.0) * (w_ref[...] * (IDX_HEADS ** -0.5))
    n_keys = past + LANES
    sc = jnp.sum(wr.reshape(dec_seq, IDX_HEADS, n_keys), axis=1)
    kpos = lax.broadcasted_iota(jnp.int32, (dec_seq, n_keys), 1)
    tpos = lax.broadcasted_iota(jnp.int32, (dec_seq, n_keys), 0)
    o_ref[...] = jnp.where(kpos <= past + tpos, sc, _NEG_INF)


def _sample_select_kernel(s_ref, o_ref, *, k_sel, bisect_steps):
    score = s_ref[...]
    kpos = lax.broadcasted_iota(jnp.int32, score.shape, 1)
    sel = _select_topk(score, kpos, k_sel, bisect_steps)
    o_ref[...] = jnp.where(sel, 0.0, _NEG_INF)


def _sample_attn_kernel(pt_ref, q_ref, g_ref, kn_new_ref, kv_new_ref, mask_ref, rb_rows_ref, qg_ref, *rest,
                        n_pages, page, dec_seq):
    k_pages = rest[:n_pages]
    v_pages = rest[n_pages:2 * n_pages]
    o_ref = rest[2 * n_pages]
    bias_ref = rest[2 * n_pages + 1]
    w = ATT_KV_HEADS * HEAD
    rows = dec_seq * ATT_HEADS
    n_keys = n_pages * page + LANES

    @pl.when(pl.program_id(0) == 0)
    def _():
        t_row = lax.div(lax.broadcasted_iota(jnp.int32, (rows, n_keys), 0), ATT_HEADS)
        dist = n_pages * page + t_row - lax.broadcasted_iota(jnp.int32, (rows, n_keys), 1)
        bias_ref[...] = _rel_bias_lookup(dist, lambda b: rb_rows_ref[:, b:b + 1])

    pad = jnp.zeros((LANES - dec_seq, w), F32)
    k_all = jnp.concatenate([r[...] for r in k_pages] + [kn_new_ref[...], pad], axis=0).astype(BF16)
    v_all = jnp.concatenate([r[...] for r in v_pages] + [kv_new_ref[:, w:2 * w], pad], axis=0).astype(BF16)
    q = q_ref[...]
    q = q * lax.rsqrt(jnp.mean(q * q, axis=-1, keepdims=True) + NORM_EPS) * qg_ref[...]
    qb = q.astype(BF16)
    head = lax.rem(lax.broadcasted_iota(jnp.int32, (rows, 1), 0), ATT_HEADS)
    first = head < ATT_GROUP
    lg = jnp.where(first, _dot_nt(qb, k_all[:, 0:HEAD]), _dot_nt(qb, k_all[:, HEAD:2 * HEAD]))
    mask = jnp.broadcast_to(mask_ref[...][:, None, :], (dec_seq, ATT_HEADS, n_keys)).reshape(rows, n_keys)
    s = lg * (HEAD ** -0.5) + bias_ref[...] + mask
    p = jnp.exp(s - jnp.max(s, axis=-1, keepdims=True))
    l = jnp.sum(p, axis=-1, keepdims=True)
    pb = p.astype(BF16)
    o = jnp.where(first, jnp.dot(pb, v_all[:, 0:HEAD], preferred_element_type=F32),
                  jnp.dot(pb, v_all[:, HEAD:2 * HEAD], preferred_element_type=F32)) / l
    g = g_ref[...]
    o_ref[...] = o * (g * _sigmoid(g))


def _attn_sample(za, kn, qb, qib, kvb, kib, *, cache_k, cache_v, cache_kidx, page_table, rel_bias, qn_g,
                 dec_seq, k_sel):
    nb, n_pages = page_table.shape
    n_phys, page = cache_k.shape[0], cache_k.shape[1]
    past = n_pages * page
    n_keys = past + LANES
    w = ATT_KV_HEADS * HEAD
    ck = cache_k.reshape(n_phys, page, w)
    cv = cache_v.reshape(n_phys, page, w)
    ci = cache_kidx.reshape(n_phys, page, HEAD)
    za3 = za.reshape(nb, dec_seq, ZA_COLS)
    kn3 = kn.reshape(nb, dec_seq, w)
    qi = za[:, ZA_QI:ZA_QI + IDX_HEADS * HEAD].reshape(nb, dec_seq * IDX_HEADS, HEAD)
    wi = za[:, ZA_KIW + HEAD:ZA_KIW + HEAD + IDX_HEADS].reshape(nb, dec_seq * IDX_HEADS, 1)
    qa = za[:, ZA_Q:ZA_Q + D_MODEL].reshape(nb, dec_seq * ATT_HEADS, HEAD)
    ga = za[:, ZA_GATT:ZA_GATT + D_MODEL].reshape(nb, dec_seq * ATT_HEADS, HEAD)

    def page_specs(width):
        return [pl.BlockSpec((None, page, width), lambda b, pt, j=j: (pt[b, j], 0, 0)) for j in range(n_pages)]

    per_b = lambda r, c: pl.BlockSpec((None, r, c), lambda b, pt: (b, 0, 0))
    kiw_new = pl.BlockSpec((None, dec_seq, LANES), lambda b, pt: (b, 0, ZA_KIW // LANES))

    scores = pl.pallas_call(
        functools.partial(_sample_score_kernel, n_pages=n_pages, page=page, dec_seq=dec_seq),
        grid_spec=pltpu.PrefetchScalarGridSpec(
            num_scalar_prefetch=1, grid=(nb,),
            in_specs=[per_b(dec_seq * IDX_HEADS, HEAD), per_b(dec_seq * IDX_HEADS, 1), kiw_new] + page_specs(HEAD),
            out_specs=per_b(dec_seq, n_keys)),
        out_shape=jax.ShapeDtypeStruct((nb, dec_seq, n_keys), F32),
        compiler_params=_cparams(1), name="sample_scores")(page_table, qi, wi, za3, *([ci] * n_pages))

    rows = nb * dec_seq
    tr = min(rows, 128)
    mask = pl.pallas_call(
        functools.partial(_sample_select_kernel, k_sel=k_sel, bisect_steps=14),
        grid=(rows // tr,),
        in_specs=[pl.BlockSpec((tr, n_keys), lambda i: (i, 0))],
        out_specs=pl.BlockSpec((tr, n_keys), lambda i: (i, 0)),
        out_shape=jax.ShapeDtypeStruct((rows, n_keys), F32),
        compiler_params=_cparams(1), name="sample_select")(scores.reshape(rows, n_keys))

    rb_rows = jnp.tile(rel_bias.T, (dec_seq, 1))

    const = lambda shape: pl.BlockSpec(shape, lambda b, pt: (0,) * len(shape))
    kv_new = pl.BlockSpec((None, dec_seq, 2 * w), lambda b, pt: (b, 0, ZA_KV // (2 * w)))
    out = pl.pallas_call(
        functools.partial(_sample_attn_kernel, n_pages=n_pages, page=page, dec_seq=dec_seq),
        grid_spec=pltpu.PrefetchScalarGridSpec(
            num_scalar_prefetch=1, grid=(nb,),
            in_specs=[per_b(dec_seq * ATT_HEADS, HEAD), per_b(dec_seq * ATT_HEADS, HEAD), per_b(dec_seq, w),
                      kv_new, per_b(dec_seq, n_keys), const((dec_seq * ATT_HEADS, REL_BUCKETS)), const((1, HEAD))]
            + page_specs(w) + page_specs(w),
            out_specs=per_b(dec_seq * ATT_HEADS, HEAD),
            scratch_shapes=[pltpu.VMEM((dec_seq * ATT_HEADS, n_keys), F32)]),
        out_shape=jax.ShapeDtypeStruct((nb, dec_seq * ATT_HEADS, HEAD), F32),
        compiler_params=_cparams(1), name="sample_attn")(
            page_table, qa, ga, kn3, za3, mask.reshape(nb, dec_seq, n_keys), rb_rows, qn_g.reshape(1, HEAD),
            *([ck] * n_pages), *([cv] * n_pages))
    return out.reshape(rows, D_MODEL)


def _merge_kernel(x_ref, oa_ref, sg_ref, ob_ref, ga_ref, gb_ref, wpa_ref, wpb_ref, wo_ref, y_ref):
    oa = (oa_ref[...] * sg_ref[...]).astype(BF16)
    pa = jnp.dot(oa, wpa_ref[...], preferred_element_type=F32)
    pb = jnp.dot(ob_ref[...].astype(BF16), wpb_ref[...], preferred_element_type=F32)
    merged = _sigmoid(ga_ref[...]) * pa + _sigmoid(gb_ref[...]) * pb
    y_ref[...] = x_ref[...] + jnp.dot(merged.astype(BF16), wo_ref[...], preferred_element_type=F32)


def _merge(x, oa, sg, ob, za, w_pa, w_pb, w_out, tm):
    m, d = x.shape
    row = pl.BlockSpec((tm, d), lambda i: (i, 0))
    wsp = pl.BlockSpec((d, d), lambda i: (0, 0))
    return pl.pallas_call(
        _merge_kernel, grid=(m // tm,),
        in_specs=[row, row, row, row,
                  pl.BlockSpec((tm, d), lambda i: (i, ZA_GA // d)), pl.BlockSpec((tm, d), lambda i: (i, ZA_GB // d)),
                  wsp, wsp, wsp],
        out_specs=row,
        out_shape=jax.ShapeDtypeStruct((m, d), F32),
        compiler_params=_cparams(1), name="merge")(x, oa, sg, ob, za, za, w_pa, w_pb, w_out)


def _layer(x, shift, s0, params, *, batch, seq_len, attend):
    (norm_g, w_r, w_a, mu, w0, w2, a0, a2, k_k, k_a, r_k, lnx_g, lnx_b, qn_g, kn_g, w_pa, w_pb, w_out) = params
    m = batch * seq_len
    tm = min(m, 512)
    xn = _rmsnorm(x, norm_g, tm)
    zr = _matmul(xn, w_r, tm, RWKV_COLS // 3, "inproj_rwkv")
    za = _matmul(xn, w_a, tm, ZA_COLS // 4, "inproj_attn")

    tile = min(m, 256)
    r, w, k, v, kk, a, sg = _rwkv_prep(zr, shift, mu, w0, w2, a0, a2, k_k, k_a,
                                       batch=batch, seq_len=seq_len, tile=tile)
    seqs = [_to_pairs(t, batch, seq_len) for t in (r, w, k, v, kk, a)]
    consts = [_head_const_pairs(t, batch) for t in (lnx_g, lnx_b, r_k.reshape(-1))]
    o_t, s_t = _rwkv_scan(seqs, s0, *consts, steps=min(seq_len, 32))
    oa = _from_pairs(o_t, batch, seq_len)

    kn, qb, qib, kvb, kib = _attn_prep(za, qn_g, kn_g, tm)
    ob = attend(za, kn, qb, qib, kvb, kib)

    y = _merge(x, oa, sg, ob, za, w_pa, w_pb, w_out, min(m, 256))
    return y, zr, za, kn, s_t


def _state_to_pairs(s):
    b, h, n, _ = s.shape
    return jnp.transpose(s, (3, 2, 0, 1)).reshape(n, n, b * h)


def _state_from_pairs(s, batch):
    n = s.shape[0]
    return jnp.transpose(s.reshape(n, n, batch, RWKV_HEADS), (2, 3, 1, 0))


def kernel(x_prompt, x_sample, cache_k, cache_v, cache_kidx, state_wkv, state_shift, page_table, norm_g, w_in,
           shift_mu, w0, w2, a0, a2, k_k, k_a, r_k, lnx_g, lnx_b, q_norm_g, k_norm_g, rel_bias, w_pa, w_pb, w_out):
    bsz, seq, d = x_prompt.shape
    dec_bsz, dec_seq, _ = x_sample.shape
    depth = w_in.shape[0]
    assert depth == 1 and d == D_MODEL
    past_len = page_table.shape[1] * cache_k.shape[2]
    topk_p = min(TOPK_MAX, seq // 4)
    topk_s = min(TOPK_MAX, (past_len + dec_seq) // 4)
    l = 0

    wl = w_in[l]
    c0 = RWKV_COLS
    q_w, kv_w, qi_w = wl[:, c0:c0 + 1024], wl[:, c0 + 1024:c0 + 1280], wl[:, c0 + 1280:c0 + 1792]
    kiw_w = wl[:, c0 + 1792:c0 + 1864]
    rest_w = wl[:, c0 + 1864:]
    zpad = lambda n: jnp.zeros((d, n), wl.dtype)
    w_a = jnp.concatenate([q_w, qi_w, kv_w, kiw_w, zpad(LANES - kiw_w.shape[1]), zpad(LANES), rest_w],
                          axis=1).astype(BF16)
    assert w_a.shape[1] == ZA_COLS
    w_r = wl[:, :c0].astype(BF16)
    params = (norm_g[l], w_r, w_a, shift_mu[l], w0[l], w2[l], a0[l], a2[l], k_k[l], k_a[l], r_k[l],
              lnx_g[l], lnx_b[l], q_norm_g[l], k_norm_g[l],
              w_pa[l].astype(BF16), w_pb[l].astype(BF16), w_out[l].astype(BF16))

    xp = x_prompt.reshape(bsz * seq, d)
    attend_p = functools.partial(_attn_prompt, rel_bias=rel_bias, batch=bsz, seq_len=seq,
                                 tq=min(seq, 256), k_sel=topk_p)
    yp, zr_p, za_p, kn_p, st_p = _layer(
        xp, jnp.zeros((bsz, 1, RWKV_COLS), F32), jnp.zeros((HEAD, HEAD, bsz * RWKV_HEADS), F32), params,
        batch=bsz, seq_len=seq, attend=attend_p)

    xs = x_sample.reshape(dec_bsz * dec_seq, d)
    attend_s = functools.partial(_attn_sample, cache_k=cache_k[l], cache_v=cache_v[l], cache_kidx=cache_kidx[l],
                                 page_table=page_table, rel_bias=rel_bias, qn_g=q_norm_g[l], dec_seq=dec_seq,
                                 k_sel=topk_s)
    shift_rows = jnp.repeat(state_shift[l], dec_seq, axis=0)
    ys, zr_s, za_s, kn_s, st_s = _layer(
        xs, shift_rows, _state_to_pairs(state_wkv[l]), params, batch=dec_bsz, seq_len=dec_seq, attend=attend_s)

    w = ATT_KV_HEADS * HEAD

    def pack(y, zr, za, kn, st, b, t):
        v = za[:, ZA_KV + w:ZA_KV + 2 * w]
        kidx = za[:, ZA_KIW:ZA_KIW + HEAD]
        return (y.reshape(b, t, d),
                kn.reshape(1, b, t, ATT_KV_HEADS, HEAD), v.reshape(1, b, t, ATT_KV_HEADS, HEAD),
                kidx.reshape(1, b, t, HEAD), _state_from_pairs(st, b)[None],
                zr.reshape(b, t, RWKV_COLS)[:, -1][None])

    p = pack(yp, zr_p, za_p, kn_p, st_p, bsz, seq)
    s = pack(ys, zr_s, za_s, kn_s, st_s, dec_bsz, dec_seq)
    return (p[0], s[0]) + p[1:] + s[1:]
```

```python
import functools
import math

import numpy as np
import jax
import jax.numpy as jnp
from jax import lax
from jax.experimental import pallas as pl
from jax.experimental.pallas import tpu as pltpu

F32 = jnp.float32
BF16 = jnp.bfloat16

D_MODEL = 1024
HEAD = 64
RWKV_HEADS = D_MODEL // HEAD
LORA = 64
LNX_EPS = 64e-5
ATT_HEADS = D_MODEL // HEAD
ATT_KV_HEADS = 2
ATT_GROUP = ATT_HEADS // ATT_KV_HEADS
IDX_HEADS = 8
TOPK_MAX = 256
REL_BUCKETS = 32
REL_MAX_DIST = 128
NORM_EPS = 1e-6
RWKV_COLS = 4 * D_MODEL + 2 * LORA

LANES = 128
SUBLANES = 8
VMEM_LIMIT_BYTES = 48 * 1024 * 1024

ZA_Q = 0
ZA_QI = 1024
ZA_KV = 1536
ZA_KIW = 1792
ZA_GATT = 2048
ZA_GA = 3072
ZA_GB = 4096
ZA_COLS = 5120

_NEG_INF = float("-inf")
_POS_INF = float("inf")


def _cparams(n_axes):
    return pltpu.CompilerParams(dimension_semantics=("arbitrary",) * n_axes,
                                vmem_limit_bytes=VMEM_LIMIT_BYTES)


def _sigmoid(x):
    return 1.0 / (1.0 + jnp.exp(-x))


def _dot_nt(a, b):
    return lax.dot_general(a, b, (((1,), (1,)), ((), ())), preferred_element_type=F32)


def _rmsnorm_kernel(x_ref, g_ref, o_ref):
    x = x_ref[...]
    ms = jnp.mean(x * x, axis=-1, keepdims=True)
    o_ref[...] = (x * lax.rsqrt(ms + NORM_EPS) * g_ref[...]).astype(o_ref.dtype)


def _rmsnorm(x, g, tm):
    m, d = x.shape
    return pl.pallas_call(
        _rmsnorm_kernel, grid=(m // tm,),
        in_specs=[pl.BlockSpec((tm, d), lambda i: (i, 0)), pl.BlockSpec((1, d), lambda i: (0, 0))],
        out_specs=pl.BlockSpec((tm, d), lambda i: (i, 0)),
        out_shape=jax.ShapeDtypeStruct((m, d), BF16),
        compiler_params=_cparams(1), name="rmsnorm")(x, g.reshape(1, d))


def _mm_kernel(x_ref, w_ref, o_ref):
    o_ref[...] = jnp.dot(x_ref[...], w_ref[...], preferred_element_type=F32)


def _matmul(x, w, tm, tn, name):
    m, k = x.shape
    n = w.shape[1]
    return pl.pallas_call(
        _mm_kernel, grid=(m // tm, n // tn),
        in_specs=[pl.BlockSpec((tm, k), lambda i, j: (i, 0)), pl.BlockSpec((k, tn), lambda i, j: (0, j))],
        out_specs=pl.BlockSpec((tm, tn), lambda i, j: (i, j)),
        out_shape=jax.ShapeDtypeStruct((m, n), F32),
        compiler_params=_cparams(2), name=name)(x, w)


def _rwkv_prep_kernel(z_ref, prev_ref, shift_ref, mu_ref, w0_ref, w2_ref, a0_ref, a2_ref, kk_ref, ka_ref,
                      r_o, w_o, k_o, v_o, kk_o, a_o, g_o, *, seq_len, tile):
    z = z_ref[...]
    rows = lax.broadcasted_iota(jnp.int32, z.shape, 0)
    rolled = pltpu.roll(z, 1, 0)
    if seq_len >= tile:
        first = jnp.where(pl.program_id(1) == 0, shift_ref[...], prev_ref[SUBLANES - 1:SUBLANES, :])
        prev = jnp.where(rows == 0, first, rolled)
    else:
        prev = jnp.where(lax.rem(rows, seq_len) == 0, shift_ref[...], rolled)
    zs = z + (prev - z) * mu_ref[...]
    d = D_MODEL
    r = zs[:, 0:d]
    k = zs[:, d:2 * d]
    v = zs[:, 2 * d:3 * d]
    g = zs[:, 3 * d:4 * d]
    wd = zs[:, 4 * d:4 * d + LORA]
    ad = zs[:, 4 * d + LORA:4 * d + 2 * LORA]
    wl = w0_ref[...] + jnp.dot(jnp.tanh(wd).astype(BF16), w2_ref[...], preferred_element_type=F32)
    decay = jnp.exp(-_sigmoid(wl) * math.exp(-0.5))
    a = _sigmoid(a0_ref[...] + jnp.dot(ad.astype(BF16), a2_ref[...], preferred_element_type=F32))
    r_o[...] = r
    w_o[...] = decay
    k_o[...] = k * (1.0 + (a - 1.0) * ka_ref[...])
    v_o[...] = v
    kk_o[...] = k * kk_ref[...]
    a_o[...] = a
    g_o[...] = g * _sigmoid(g)


def _rwkv_prep(zr, shift, mu, w0, w2, a0, a2, k_k, k_a, *, batch, seq_len, tile):
    m, c = zr.shape
    d = D_MODEL
    row = lambda x: x.reshape(1, -1)
    consts = [row(mu), row(w0), w2.astype(BF16), row(a0), a2.astype(BF16), row(k_k), row(k_a)]
    const_specs = [pl.BlockSpec(x.shape, lambda *_: (0, 0)) for x in consts]
    if seq_len >= tile:
        nt = seq_len // tile
        grid = (batch, nt)
        zmap = lambda b, t: (b * nt + t, 0)
        pmap = lambda b, t: (jnp.maximum((b * seq_len + t * tile) // SUBLANES - 1, 0), 0)
        shift_spec = pl.BlockSpec((None, 1, c), lambda b, t: (b, 0, 0))
    else:
        grid = (1, m // tile)
        zmap = lambda b, t: (t, 0)
        pmap = lambda b, t: (0, 0)
        shift_spec = pl.BlockSpec((tile, c), zmap)
    out_spec = pl.BlockSpec((tile, d), zmap)
    kern = functools.partial(_rwkv_prep_kernel, seq_len=seq_len, tile=tile)
    return pl.pallas_call(
        kern, grid=grid,
        in_specs=[pl.BlockSpec((tile, c), zmap), pl.BlockSpec((SUBLANES, c), pmap), shift_spec] + const_specs,
        out_specs=[out_spec] * 7,
        out_shape=[jax.ShapeDtypeStruct((m, d), F32)] * 7,
        compiler_params=_cparams(2), name="rwkv_prep")(zr, zr, shift, *consts)


def _rwkv_scan_kernel(r_ref, w_ref, k_ref, v_ref, kk_ref, a_ref, s0_ref, lg_ref, lb_ref, rk_ref,
                      o_ref, s_ref, vec_ref, *, steps):
    nb = HEAD // SUBLANES

    @pl.when(pl.program_id(1) == 0)
    def _():
        s_ref[...] = s0_ref[...]

    def bcast_row(ref, lead, j):
        return jnp.broadcast_to(ref[lead, pl.ds(j, 1), :], (SUBLANES, LANES))

    def step(t, carry):
        kkraw = kk_ref[t]
        n2 = jnp.sum(kkraw * kkraw, axis=0, keepdims=True)
        kk = kkraw / jnp.maximum(jnp.sqrt(n2), 1e-12)
        vec_ref[0] = -kk
        vec_ref[1] = kk * a_ref[t]
        sa = [jnp.zeros((SUBLANES, LANES), F32) for _ in range(nb)]
        for j in range(HEAD):
            nk = bcast_row(vec_ref, 0, j)
            for ib in range(nb):
                sa[ib] = sa[ib] + s_ref[j, ib * SUBLANES:(ib + 1) * SUBLANES, :] * nk
        vt = [v_ref[t, ib * SUBLANES:(ib + 1) * SUBLANES, :] for ib in range(nb)]
        out = [jnp.zeros((SUBLANES, LANES), F32) for _ in range(nb)]
        for j in range(HEAD):
            wj = bcast_row(w_ref, t, j)
            kaj = bcast_row(vec_ref, 1, j)
            kj = bcast_row(k_ref, t, j)
            rj = bcast_row(r_ref, t, j)
            for ib in range(nb):
                sl = slice(ib * SUBLANES, (ib + 1) * SUBLANES)
                sn = s_ref[j, sl, :] * wj + sa[ib] * kaj + vt[ib] * kj
                s_ref[j, sl, :] = sn
                out[ib] = out[ib] + sn * rj
        o = jnp.concatenate(out, axis=0)
        mean = jnp.mean(o, axis=0, keepdims=True)
        dev = o - mean
        var = jnp.mean(dev * dev, axis=0, keepdims=True)
        y = dev * lax.rsqrt(var + LNX_EPS) * lg_ref[...] + lb_ref[...]
        bonus = jnp.sum(r_ref[t] * k_ref[t] * rk_ref[...], axis=0, keepdims=True) * v_ref[t]
        o_ref[t] = y + bonus
        return carry

    lax.fori_loop(0, steps, step, 0)


def _rwkv_scan(seqs, s0, lg_t, lb_t, rk_t, *, steps):
    t_len, n, p = seqs[0].shape
    grid = (p // LANES, t_len // steps)
    seq_spec = pl.BlockSpec((steps, n, LANES), lambda g, t: (t, 0, g))
    st_spec = pl.BlockSpec((n, n, LANES), lambda g, t: (0, 0, g))
    c_spec = pl.BlockSpec((n, LANES), lambda g, t: (0, g))
    kern = functools.partial(_rwkv_scan_kernel, steps=steps)
    return pl.pallas_call(
        kern, grid=grid,
        in_specs=[seq_spec] * 6 + [st_spec] + [c_spec] * 3,
        out_specs=[seq_spec, st_spec],
        out_shape=[jax.ShapeDtypeStruct((t_len, n, p), F32), jax.ShapeDtypeStruct((n, n, p), F32)],
        scratch_shapes=[pltpu.VMEM((2, n, LANES), F32)],
        compiler_params=_cparams(2), name="rwkv_scan")(*seqs, s0, lg_t, lb_t, rk_t)


def _to_pairs(x, batch, seq_len):
    x = x.reshape(batch, seq_len, RWKV_HEADS, HEAD)
    return jnp.transpose(x, (1, 3, 0, 2)).reshape(seq_len, HEAD, batch * RWKV_HEADS)


def _from_pairs(x, batch, seq_len):
    x = x.reshape(seq_len, HEAD, batch, RWKV_HEADS)
    return jnp.transpose(x, (2, 0, 3, 1)).reshape(batch * seq_len, RWKV_HEADS * HEAD)


def _head_const_pairs(x, batch):
    x = x.reshape(RWKV_HEADS, HEAD).T
    return jnp.tile(x, (1, batch))


def _count_ge(score, thr):
    return jnp.sum(jnp.where(score >= thr, 1.0, 0.0), axis=-1, keepdims=True)


def _select_topk(score, kpos, k_sel, bisect_steps):
    kf = float(k_sel)
    vis = score > _NEG_INF
    nvis = jnp.sum(jnp.where(vis, 1.0, 0.0), axis=-1, keepdims=True)
    need_sel = nvis > kf
    rowmax = jnp.where(need_sel, jnp.max(score, axis=-1, keepdims=True), 0.0)
    rowmin = jnp.where(need_sel, jnp.min(jnp.where(vis, score, _POS_INF), axis=-1, keepdims=True), 0.0)

    def bis(_, c):
        lo, hi, chi = c
        hfin = jnp.where(hi == _POS_INF, rowmax, hi)
        piv = 0.5 * lo + 0.5 * hfin
        cnt = _count_ge(score, piv)
        ge = cnt >= kf
        return jnp.where(ge, piv, lo), jnp.where(ge, hi, piv), jnp.where(ge, chi, cnt)

    lo, hi, chi = lax.fori_loop(0, bisect_steps, bis,
                                (rowmin, jnp.full_like(rowmin, _POS_INF), jnp.zeros_like(rowmin)))

    def walk_cond(c):
        return jnp.min(c[4]) < 0.5

    def walk(c):
        hi, chi, tau, cgt, done, ceq = c
        pending = done < 0.5
        bmax = jnp.max(jnp.where(score < hi, score, _NEG_INF), axis=-1, keepdims=True)
        cnt = _count_ge(score, bmax)
        fin = jnp.logical_and(cnt >= kf, pending)
        tau = jnp.where(fin, bmax, tau)
        cgt = jnp.where(fin, chi, cgt)
        ceq = jnp.where(fin, cnt - chi, ceq)
        adv = jnp.logical_and(cnt < kf, pending)
        hi = jnp.where(adv, bmax, hi)
        chi = jnp.where(adv, cnt, chi)
        return hi, chi, tau, cgt, jnp.where(fin, 1.0, done), ceq

    done0 = jnp.where(need_sel, 0.0, 1.0)
    neg = jnp.full_like(rowmin, _NEG_INF)
    zero = jnp.zeros_like(rowmin)
    _, _, tau, cgt, _, ceq = lax.while_loop(walk_cond, walk, (hi, chi, neg, zero, done0, zero))

    need = kf - cgt
    excess = jnp.logical_and(need_sel, ceq > need)
    eq = score == tau
    n_keys = score.shape[-1]

    def tie_break(_):
        def body(_, c):
            plo, phi = c
            mid = lax.shift_right_arithmetic(plo + phi, 1)
            cnt = jnp.sum(jnp.where(jnp.logical_and(eq, kpos <= mid), 1.0, 0.0), axis=-1, keepdims=True)
            ge = cnt >= need
            return jnp.where(ge, plo, mid), jnp.where(ge, mid, phi)
        plo0 = jnp.full(tau.shape, -1, jnp.int32)
        phi0 = jnp.full(tau.shape, n_keys - 1, jnp.int32)
        _, phi = lax.fori_loop(0, n_keys.bit_length() + 1, body, (plo0, phi0))
        return jnp.where(excess, phi, n_keys)

    any_excess = jnp.max(jnp.where(excess, 1.0, 0.0)) > 0.0
    pcut = lax.cond(any_excess, tie_break, lambda _: jnp.full(tau.shape, n_keys, jnp.int32), 0)
    pcut = jnp.where(need_sel, pcut, -1)
    return jnp.logical_or(score > tau, jnp.logical_and(eq, kpos <= pcut))


_MASKED = -1e30


def _select_topk_chunked(s_ref, nch, ck, k_sel, nvis, bisect_steps):
    rows = s_ref.shape[0]
    kf = float(k_sel)
    nfold = ck // LANES

    def chunk(c):
        return s_ref[:, pl.ds(pl.multiple_of(c * ck, ck), ck)]

    def fold(x, op):
        out = x[:, 0:LANES]
        for i in range(1, nfold):
            out = op(out, x[:, i * LANES:(i + 1) * LANES])
        return out

    def count(pred):
        def body(c, acc):
            return acc + fold(jnp.where(pred(chunk(c), c), 1.0, 0.0), jnp.add)
        acc = lax.fori_loop(0, nch, body, jnp.zeros((rows, LANES), F32))
        return jnp.sum(acc, axis=-1, keepdims=True)

    def row_max(val):
        def body(c, acc):
            return jnp.maximum(acc, fold(val(chunk(c)), jnp.maximum))
        acc = lax.fori_loop(0, nch, body, jnp.full((rows, LANES), _NEG_INF, F32))
        return jnp.max(acc, axis=-1, keepdims=True)

    need_sel = nvis > kf
    rowmax = jnp.where(need_sel, row_max(lambda x: x), 0.0)
    rowmin = jnp.where(need_sel, -row_max(lambda x: jnp.where(x > _NEG_INF, -x, _NEG_INF)), 0.0)

    def bis(_, c):
        lo, hi, chi = c
        hfin = jnp.where(hi == _POS_INF, rowmax, hi)
        piv = 0.5 * lo + 0.5 * hfin
        cnt = count(lambda x, _c: x >= piv)
        ge = cnt >= kf
        return jnp.where(ge, piv, lo), jnp.where(ge, hi, piv), jnp.where(ge, chi, cnt)

    lo, hi, chi = lax.fori_loop(0, bisect_steps, bis,
                                (rowmin, jnp.full_like(rowmin, _POS_INF), jnp.zeros_like(rowmin)))

    def walk_cond(c):
        return jnp.min(c[4]) < 0.5

    def walk(c):
        hi, chi, tau, cgt, done, ceq = c
        pending = done < 0.5
        bmax = row_max(lambda x: jnp.where(x < hi, x, _NEG_INF))
        cnt = count(lambda x, _c: x >= bmax)
        fin = jnp.logical_and(cnt >= kf, pending)
        tau = jnp.where(fin, bmax, tau)
        cgt = jnp.where(fin, chi, cgt)
        ceq = jnp.where(fin, cnt - chi, ceq)
        adv = jnp.logical_and(cnt < kf, pending)
        hi = jnp.where(adv, bmax, hi)
        chi = jnp.where(adv, cnt, chi)
        return hi, chi, tau, cgt, jnp.where(fin, 1.0, done), ceq

    neg = jnp.full_like(rowmin, _NEG_INF)
    zero = jnp.zeros_like(rowmin)
    _, _, tau, cgt, _, ceq = lax.while_loop(
        walk_cond, walk, (hi, chi, neg, zero, jnp.where(need_sel, 0.0, 1.0), zero))

    need = kf - cgt
    excess = jnp.logical_and(need_sel, ceq > need)
    n_keys = s_ref.shape[1]

    def kpos(c):
        return c * ck + lax.broadcasted_iota(jnp.int32, (rows, ck), 1)

    def tie_break(_):
        def body(_, c):
            plo, phi = c
            mid = lax.shift_right_arithmetic(plo + phi, 1)
            cnt = count(lambda x, cc: jnp.logical_and(x == tau, kpos(cc) <= mid))
            ge = cnt >= need
            return jnp.where(ge, plo, mid), jnp.where(ge, mid, phi)
        plo0 = jnp.full(tau.shape, -1, jnp.int32)
        phi0 = jnp.full(tau.shape, n_keys - 1, jnp.int32)
        _, phi = lax.fori_loop(0, n_keys.bit_length() + 1, body, (plo0, phi0))
        return jnp.where(excess, phi, n_keys)

    any_excess = jnp.max(jnp.where(excess, 1.0, 0.0)) > 0.0
    pcut = lax.cond(any_excess, tie_break, lambda _: jnp.full(tau.shape, n_keys, jnp.int32), 0)
    pcut = jnp.where(need_sel, pcut, -1)

    def write(c, carry):
        x = chunk(c)
        sel = jnp.logical_or(x > tau, jnp.logical_and(x == tau, kpos(c) <= pcut))
        s_ref[:, pl.ds(pl.multiple_of(c * ck, ck), ck)] = jnp.where(sel, 0.0, _MASKED)
        return carry

    lax.fori_loop(0, nch, write, 0)


def _attn_prep_kernel(q_ref, qi_ref, kv_ref, kiw_ref, qg_ref, kg_ref, kn_o, qb_o, qib_o, kvb_o, kib_o):
    w = ATT_KV_HEADS * HEAD
    for h in range(ATT_KV_HEADS):
        hs = slice(h * HEAD, (h + 1) * HEAD)
        x = kv_ref[:, hs]
        kn = x * lax.rsqrt(jnp.mean(x * x, axis=-1, keepdims=True) + NORM_EPS) * kg_ref[...]
        kn_o[:, hs] = kn
        kvb_o[:, hs] = kn.astype(BF16)
        v = kv_ref[:, w + h * HEAD:w + (h + 1) * HEAD].astype(BF16)
        kvb_o[:, w + h * LANES:w + (h + 1) * LANES] = jnp.concatenate([v, jnp.ones_like(v)], axis=1)
    for h in range(ATT_HEADS):
        x = q_ref[:, h * HEAD:(h + 1) * HEAD]
        qn = x * lax.rsqrt(jnp.mean(x * x, axis=-1, keepdims=True) + NORM_EPS) * (qg_ref[...] * HEAD ** -0.5)
        qb_o[h] = qn.astype(BF16)
    qib_o[...] = (qi_ref[...] * HEAD ** -0.5).astype(BF16)
    kib_o[...] = kiw_ref[:, 0:HEAD].astype(BF16)


def _attn_prep(za, qn_g, kn_g, tm):
    m = za.shape[0]
    w = ATT_KV_HEADS * HEAD
    wi = IDX_HEADS * HEAD
    row = lambda width, blk: pl.BlockSpec((tm, width), lambda i, blk=blk: (i, blk))
    gspec = pl.BlockSpec((1, HEAD), lambda i: (0, 0))
    return pl.pallas_call(
        _attn_prep_kernel, grid=(m // tm,),
        in_specs=[row(D_MODEL, ZA_Q // D_MODEL), row(wi, ZA_QI // wi), row(2 * w, ZA_KV // (2 * w)),
                  row(LANES, ZA_KIW // LANES), gspec, gspec],
        out_specs=[row(w, 0), pl.BlockSpec((ATT_HEADS, tm, HEAD), lambda i: (0, i, 0)), row(wi, 0),
                   row(w + ATT_KV_HEADS * LANES, 0), row(HEAD, 0)],
        out_shape=[jax.ShapeDtypeStruct((m, w), F32), jax.ShapeDtypeStruct((ATT_HEADS, m, HEAD), BF16),
                   jax.ShapeDtypeStruct((m, wi), BF16), jax.ShapeDtypeStruct((m, w + ATT_KV_HEADS * LANES), BF16),
                   jax.ShapeDtypeStruct((m, HEAD), BF16)],
        compiler_params=_cparams(1), name="attn_prep")(za, za, za, za, qn_g.reshape(1, HEAD), kn_g.reshape(1, HEAD))


def _bucket_edges():
    max_exact = REL_BUCKETS // 2
    d = np.arange(REL_MAX_DIST + 1)
    df = np.maximum(d, 1).astype(np.float32)
    large = max_exact + (np.log(df / max_exact) / math.log(REL_MAX_DIST / max_exact)
                         * (REL_BUCKETS - max_exact)).astype(np.int32)
    bucket = np.where(d < max_exact, d, np.minimum(large, REL_BUCKETS - 1))
    return [int(np.argmax(bucket >= b)) for b in range(REL_BUCKETS)]


_BUCKET_EDGES = _bucket_edges()


def _rel_bias_lookup(dist, value_of_bucket):
    bias = value_of_bucket(REL_BUCKETS - 1)
    for b in range(REL_BUCKETS - 2, -1, -1):
        bias = jnp.where(dist < _BUCKET_EDGES[b + 1], value_of_bucket(b), bias)
    return bias


def _attn_prompt_kernel(rb_ref, q_ref, qi_ref, kiwq_ref, g_ref, kvb_ref, kib_ref, o_ref, s_scr, tbd_ref,
                        *, tq, k_sel, bisect_steps):
    qt = pl.program_id(1)
    q0 = qt * tq
    ck = tq
    rg_rows = LANES
    n_rg = tq // rg_rows
    w = ATT_KV_HEADS * HEAD

    @pl.when(jnp.logical_and(pl.program_id(0) == 0, qt == 0))
    def _():
        rr = lax.broadcasted_iota(jnp.int32, (tq, tq), 0) - lax.broadcasted_iota(jnp.int32, (tq, tq), 1)
        for h in range(ATT_HEADS):
            far = rb_ref[REL_BUCKETS - 1, h]
            tbd_ref[h, 0] = _rel_bias_lookup(rr, lambda b: rb_ref[b, h]) - far
            tbd_ref[h, 1] = _rel_bias_lookup(rr + tq, lambda b: rb_ref[b, h]) - far

    def chunk_start(c):
        return pl.multiple_of(c * ck, ck)

    wcol = kiwq_ref[:, HEAD:HEAD + IDX_HEADS] * (IDX_HEADS ** -0.5)
    qi_h = [[qi_ref[rg * rg_rows:(rg + 1) * rg_rows, h * HEAD:(h + 1) * HEAD] for h in range(IDX_HEADS)]
            for rg in range(n_rg)]
    w_h = [[wcol[rg * rg_rows:(rg + 1) * rg_rows, h:h + 1] for h in range(IDX_HEADS)] for rg in range(n_rg)]

    def score_chunk(c, carry):
        k0 = chunk_start(c)
        kc = kib_ref[pl.ds(k0, ck), :]
        kpos = k0 + lax.broadcasted_iota(jnp.int32, (rg_rows, ck), 1)
        for rg in range(n_rg):
            acc = jnp.zeros((rg_rows, ck), F32)
            for h in range(IDX_HEADS):
                acc = acc + jnp.maximum(_dot_nt(qi_h[rg][h], kc), 0.0) * w_h[rg][h]
            qpos = q0 + rg * rg_rows + lax.broadcasted_iota(jnp.int32, (rg_rows, ck), 0)
            s_scr[rg * rg_rows:(rg + 1) * rg_rows, pl.ds(k0, ck)] = jnp.where(kpos <= qpos, acc, _NEG_INF)
        return carry

    nch = qt + 1
    lax.fori_loop(0, nch, score_chunk, 0)

    nvis = (q0 + 1 + lax.broadcasted_iota(jnp.int32, (tq, 1), 0)).astype(F32)
    _select_topk_chunked(s_scr, nch, ck, k_sel, nvis, bisect_steps)

    def attend(n_vis_chunks):
        lq = n_vis_chunks * ck

        def near_bias(s, h, rs):
            parts = [s[:, lq - ck:] + tbd_ref[h, 0, rs, :]]
            if n_vis_chunks > 1:
                parts = [s[:, lq - 2 * ck:lq - ck] + tbd_ref[h, 1, rs, :]] + parts
            if n_vis_chunks > 2:
                parts = [s[:, :lq - 2 * ck]] + parts
            return jnp.concatenate(parts, axis=1) if len(parts) > 1 else parts[0]

        for kvh in range(ATT_KV_HEADS):
            kk = kvb_ref[0:lq, kvh * HEAD:(kvh + 1) * HEAD]
            vx = kvb_ref[0:lq, w + kvh * LANES:w + (kvh + 1) * LANES]

            def head_quad(i, carry, kvh=kvh, kk=kk, vx=vx):
                for pp in range(2):
                    pair = kvh * (ATT_GROUP // 2) + 2 * i + pp
                    cols = pl.ds(pl.multiple_of(pair * LANES, LANES), LANES)
                    for rg in range(n_rg):
                        rs = slice(rg * rg_rows, (rg + 1) * rg_rows)
                        outs = []
                        for hh in range(2):
                            h = 2 * pair + hh
                            s = near_bias(_dot_nt(q_ref[h, rs, :], kk) + s_scr[rs, 0:lq], h, rs)
                            p = jnp.exp(s - jnp.max(s, axis=-1, keepdims=True)).astype(BF16)
                            acc = jnp.dot(p, vx, preferred_element_type=F32)
                            outs.append(acc[:, 0:HEAD] / acc[:, HEAD:HEAD + 1])
                        gh = g_ref[rs, cols]
                        o_ref[rs, cols] = jnp.concatenate(outs, axis=1) * (gh * _sigmoid(gh))
                return carry

            lax.fori_loop(0, ATT_GROUP // 4, head_quad, 0)

    for n_vis_chunks in range(1, s_scr.shape[1] // ck + 1):
        pl.when(nch == n_vis_chunks)(functools.partial(attend, n_vis_chunks))


def _attn_prompt(za, kn, qb, qib, kvb, kib, *, rel_bias, batch, seq_len, tq, k_sel):
    m = za.shape[0]
    nq = seq_len // tq
    w = ATT_KV_HEADS * HEAD
    wi = IDX_HEADS * HEAD
    assert REL_MAX_DIST <= tq and tq % LANES == 0
    row_map = lambda blk: (lambda b, t, blk=blk: (b * nq + t, blk))
    key_map = lambda b, t: (b, 0)
    kern = functools.partial(_attn_prompt_kernel, tq=tq, k_sel=k_sel, bisect_steps=14)
    return pl.pallas_call(
        kern, grid=(batch, nq),
        in_specs=[
            pl.BlockSpec(memory_space=pltpu.SMEM),
            pl.BlockSpec((ATT_HEADS, tq, HEAD), lambda b, t: (0, b * nq + t, 0)),
            pl.BlockSpec((tq, wi), row_map(0)),
            pl.BlockSpec((tq, LANES), row_map(ZA_KIW // LANES)),
            pl.BlockSpec((tq, D_MODEL), row_map(ZA_GATT // D_MODEL)),
            pl.BlockSpec((seq_len, w + ATT_KV_HEADS * LANES), key_map),
            pl.BlockSpec((seq_len, HEAD), key_map),
        ],
        out_specs=pl.BlockSpec((tq, D_MODEL), row_map(0)),
        out_shape=jax.ShapeDtypeStruct((m, D_MODEL), F32),
        scratch_shapes=[pltpu.VMEM((tq, seq_len), F32), pltpu.VMEM((ATT_HEADS, 2, tq, tq), F32)],
        compiler_params=_cparams(2), name="attn_prompt")(rel_bias, qb, qib, za, za, kvb, kib)


def _sample_score_kernel(pt_ref, q_ref, w_ref, kiw_new_ref, *rest, n_pages, page, dec_seq):
    page_refs = rest[:n_pages]
    o_ref = rest[n_pages]
    past = n_pages * page
    kidx = jnp.concatenate([r[...] for r in page_refs], axis=0).astype(BF16)
    new = kiw_new_ref[:, 0:HEAD]
    new = jnp.concatenate([new, jnp.zeros((LANES - dec_seq, HEAD), F32)], axis=0).astype(BF16)
    q = (q_ref[...] * (HEAD ** -0.5)).astype(BF16)
    lg = jnp.concatenate([_dot_nt(q, kidx), _dot_nt(q, new)], axis=1)
    wr = jnp.maximum(lg, 0.0) * (w_ref[...] * (IDX_HEADS ** -0.5))
    n_keys = past + LANES
    sc = jnp.sum(wr.reshape(dec_seq, IDX_HEADS, n_keys), axis=1)
    kpos = lax.broadcasted_iota(jnp.int32, (dec_seq, n_keys), 1)
    tpos = lax.broadcasted_iota(jnp.int32, (dec_seq, n_keys), 0)
    o_ref[...] = jnp.where(kpos <= past + tpos, sc, _NEG_INF)


def _sample_select_kernel(s_ref, o_ref, *, k_sel, bisect_steps):
    score = s_ref[...]
    kpos = lax.broadcasted_iota(jnp.int32, score.shape, 1)
    sel = _select_topk(score, kpos, k_sel, bisect_steps)
    o_ref[...] = jnp.where(sel, 0.0, _NEG_INF)


def _sample_attn_kernel(pt_ref, q_ref, g_ref, kn_new_ref, kv_new_ref, mask_ref, rb_rows_ref, qg_ref, *rest,
                        n_pages, page, dec_seq):
    k_pages = rest[:n_pages]
    v_pages = rest[n_pages:2 * n_pages]
    o_ref = rest[2 * n_pages]
    bias_ref = rest[2 * n_pages + 1]
    w = ATT_KV_HEADS * HEAD
    rows = dec_seq * ATT_HEADS
    n_keys = n_pages * page + LANES

    @pl.when(pl.program_id(0) == 0)
    def _():
        t_row = lax.div(lax.broadcasted_iota(jnp.int32, (rows, n_keys), 0), ATT_HEADS)
        dist = n_pages * page + t_row - lax.broadcasted_iota(jnp.int32, (rows, n_keys), 1)
        bias_ref[...] = _rel_bias_lookup(dist, lambda b: rb_rows_ref[:, b:b + 1])

    pad = jnp.zeros((LANES - dec_seq, w), F32)
    k_all = jnp.concatenate([r[...] for r in k_pages] + [kn_new_ref[...], pad], axis=0).astype(BF16)
    v_all = jnp.concatenate([r[...] for r in v_pages] + [kv_new_ref[:, w:2 * w], pad], axis=0).astype(BF16)
    q = q_ref[...]
    q = q * lax.rsqrt(jnp.mean(q * q, axis=-1, keepdims=True) + NORM_EPS) * qg_ref[...]
    qb = q.astype(BF16)
    head = lax.rem(lax.broadcasted_iota(jnp.int32, (rows, 1), 0), ATT_HEADS)
    first = head < ATT_GROUP
    lg = jnp.where(first, _dot_nt(qb, k_all[:, 0:HEAD]), _dot_nt(qb, k_all[:, HEAD:2 * HEAD]))
    mask = jnp.broadcast_to(mask_ref[...][:, None, :], (dec_seq, ATT_HEADS, n_keys)).reshape(rows, n_keys)
    s = lg * (HEAD ** -0.5) + bias_ref[...] + mask
    p = jnp.exp(s - jnp.max(s, axis=-1, keepdims=True))
    l = jnp.sum(p, axis=-1, keepdims=True)
    pb = p.astype(BF16)
    o = jnp.where(first, jnp.dot(pb, v_all[:, 0:HEAD], preferred_element_type=F32),
                  jnp.dot(pb, v_all[:, HEAD:2 * HEAD], preferred_element_type=F32)) / l
    g = g_ref[...]
    o_ref[...] = o * (g * _sigmoid(g))


def _attn_sample(za, kn, qb, qib, kvb, kib, *, cache_k, cache_v, cache_kidx, page_table, rel_bias, qn_g,
                 dec_seq, k_sel):
    nb, n_pages = page_table.shape
    n_phys, page = cache_k.shape[0], cache_k.shape[1]
    past = n_pages * page
    n_keys = past + LANES
    w = ATT_KV_HEADS * HEAD
    ck = cache_k.reshape(n_phys, page, w)
    cv = cache_v.reshape(n_phys, page, w)
    ci = cache_kidx.reshape(n_phys, page, HEAD)
    za3 = za.reshape(nb, dec_seq, ZA_COLS)
    kn3 = kn.reshape(nb, dec_seq, w)
    qi = za[:, ZA_QI:ZA_QI + IDX_HEADS * HEAD].reshape(nb, dec_seq * IDX_HEADS, HEAD)
    wi = za[:, ZA_KIW + HEAD:ZA_KIW + HEAD + IDX_HEADS].reshape(nb, dec_seq * IDX_HEADS, 1)
    qa = za[:, ZA_Q:ZA_Q + D_MODEL].reshape(nb, dec_seq * ATT_HEADS, HEAD)
    ga = za[:, ZA_GATT:ZA_GATT + D_MODEL].reshape(nb, dec_seq * ATT_HEADS, HEAD)

    def page_specs(width):
        return [pl.BlockSpec((None, page, width), lambda b, pt, j=j: (pt[b, j], 0, 0)) for j in range(n_pages)]

    per_b = lambda r, c: pl.BlockSpec((None, r, c), lambda b, pt: (b, 0, 0))
    kiw_new = pl.BlockSpec((None, dec_seq, LANES), lambda b, pt: (b, 0, ZA_KIW // LANES))

    scores = pl.pallas_call(
        functools.partial(_sample_score_kernel, n_pages=n_pages, page=page, dec_seq=dec_seq),
        grid_spec=pltpu.PrefetchScalarGridSpec(
            num_scalar_prefetch=1, grid=(nb,),
            in_specs=[per_b(dec_seq * IDX_HEADS, HEAD), per_b(dec_seq * IDX_HEADS, 1), kiw_new] + page_specs(HEAD),
            out_specs=per_b(dec_seq, n_keys)),
        out_shape=jax.ShapeDtypeStruct((nb, dec_seq, n_keys), F32),
        compiler_params=_cparams(1), name="sample_scores")(page_table, qi, wi, za3, *([ci] * n_pages))

    rows = nb * dec_seq
    tr = min(rows, 128)
    mask = pl.pallas_call(
        functools.partial(_sample_select_kernel, k_sel=k_sel, bisect_steps=14),
        grid=(rows // tr,),
        in_specs=[pl.BlockSpec((tr, n_keys), lambda i: (i, 0))],
        out_specs=pl.BlockSpec((tr, n_keys), lambda i: (i, 0)),
        out_shape=jax.ShapeDtypeStruct((rows, n_keys), F32),
        compiler_params=_cparams(1), name="sample_select")(scores.reshape(rows, n_keys))

    rb_rows = jnp.tile(rel_bias.T, (dec_seq, 1))

    const = lambda shape: pl.BlockSpec(shape, lambda b, pt: (0,) * len(shape))
    kv_new = pl.BlockSpec((None, dec_seq, 2 * w), lambda b, pt: (b, 0, ZA_KV // (2 * w)))
    out = pl.pallas_call(
        functools.partial(_sample_attn_kernel, n_pages=n_pages, page=page, dec_seq=dec_seq),
        grid_spec=pltpu.PrefetchScalarGridSpec(
            num_scalar_prefetch=1, grid=(nb,),
            in_specs=[per_b(dec_seq * ATT_HEADS, HEAD), per_b(dec_seq * ATT_HEADS, HEAD), per_b(dec_seq, w),
                      kv_new, per_b(dec_seq, n_keys), const((dec_seq * ATT_HEADS, REL_BUCKETS)), const((1, HEAD))]
            + page_specs(w) + page_specs(w),
            out_specs=per_b(dec_seq * ATT_HEADS, HEAD),
            scratch_shapes=[pltpu.VMEM((dec_seq * ATT_HEADS, n_keys), F32)]),
        out_shape=jax.ShapeDtypeStruct((nb, dec_seq * ATT_HEADS, HEAD), F32),
        compiler_params=_cparams(1), name="sample_attn")(
            page_table, qa, ga, kn3, za3, mask.reshape(nb, dec_seq, n_keys), rb_rows, qn_g.reshape(1, HEAD),
            *([ck] * n_pages), *([cv] * n_pages))
    return out.reshape(rows, D_MODEL)


def _merge_kernel(x_ref, oa_ref, sg_ref, ob_ref, ga_ref, gb_ref, wpa_ref, wpb_ref, wo_ref, y_ref):
    oa = (oa_ref[...] * sg_ref[...]).astype(BF16)
    pa = jnp.dot(oa, wpa_ref[...], preferred_element_type=F32)
    pb = jnp.dot(ob_ref[...].astype(BF16), wpb_ref[...], preferred_element_type=F32)
    merged = _sigmoid(ga_ref[...]) * pa + _sigmoid(gb_ref[...]) * pb
    y_ref[...] = x_ref[...] + jnp.dot(merged.astype(BF16), wo_ref[...], preferred_element_type=F32)


def _merge(x, oa, sg, ob, za, w_pa, w_pb, w_out, tm):
    m, d = x.shape
    row = pl.BlockSpec((tm, d), lambda i: (i, 0))
    wsp = pl.BlockSpec((d, d), lambda i: (0, 0))
    return pl.pallas_call(
        _merge_kernel, grid=(m // tm,),
        in_specs=[row, row, row, row,
                  pl.BlockSpec((tm, d), lambda i: (i, ZA_GA // d)), pl.BlockSpec((tm, d), lambda i: (i, ZA_GB // d)),
                  wsp, wsp, wsp],
        out_specs=row,
        out_shape=jax.ShapeDtypeStruct((m, d), F32),
        compiler_params=_cparams(1), name="merge")(x, oa, sg, ob, za, za, w_pa, w_pb, w_out)


def _layer(x, shift, s0, params, *, batch, seq_len, attend):
    (norm_g, w_r, w_a, mu, w0, w2, a0, a2, k_k, k_a, r_k, lnx_g, lnx_b, qn_g, kn_g, w_pa, w_pb, w_out) = params
    m = batch * seq_len
    tm = min(m, 512)
    xn = _rmsnorm(x, norm_g, tm)
    zr = _matmul(xn, w_r, tm, RWKV_COLS // 3, "inproj_rwkv")
    za = _matmul(xn, w_a, tm, ZA_COLS // 4, "inproj_attn")

    tile = min(m, 256)
    r, w, k, v, kk, a, sg = _rwkv_prep(zr, shift, mu, w0, w2, a0, a2, k_k, k_a,
                                       batch=batch, seq_len=seq_len, tile=tile)
    seqs = [_to_pairs(t, batch, seq_len) for t in (r, w, k, v, kk, a)]
    consts = [_head_const_pairs(t, batch) for t in (lnx_g, lnx_b, r_k.reshape(-1))]
    o_t, s_t = _rwkv_scan(seqs, s0, *consts, steps=min(seq_len, 32))
    oa = _from_pairs(o_t, batch, seq_len)

    kn, qb, qib, kvb, kib = _attn_prep(za, qn_g, kn_g, tm)
    ob = attend(za, kn, qb, qib, kvb, kib)

    y = _merge(x, oa, sg, ob, za, w_pa, w_pb, w_out, min(m, 256))
    return y, zr, za, kn, s_t


def _state_to_pairs(s):
    b, h, n, _ = s.shape
    return jnp.transpose(s, (3, 2, 0, 1)).reshape(n, n, b * h)


def _state_from_pairs(s, batch):
    n = s.shape[0]
    return jnp.transpose(s.reshape(n, n, batch, RWKV_HEADS), (2, 3, 1, 0))


def kernel(x_prompt, x_sample, cache_k, cache_v, cache_kidx, state_wkv, state_shift, page_table, norm_g, w_in,
           shift_mu, w0, w2, a0, a2, k_k, k_a, r_k, lnx_g, lnx_b, q_norm_g, k_norm_g, rel_bias, w_pa, w_pb, w_out):
    bsz, seq, d = x_prompt.shape
    dec_bsz, dec_seq, _ = x_sample.shape
    depth = w_in.shape[0]
    assert depth == 1 and d == D_MODEL
    past_len = page_table.shape[1] * cache_k.shape[2]
    topk_p = min(TOPK_MAX, seq // 4)
    topk_s = min(TOPK_MAX, (past_len + dec_seq) // 4)
    l = 0

    wl = w_in[l]
    c0 = RWKV_COLS
    q_w, kv_w, qi_w = wl[:, c0:c0 + 1024], wl[:, c0 + 1024:c0 + 1280], wl[:, c0 + 1280:c0 + 1792]
    kiw_w = wl[:, c0 + 1792:c0 + 1864]
    rest_w = wl[:, c0 + 1864:]
    zpad = lambda n: jnp.zeros((d, n), wl.dtype)
    w_a = jnp.concatenate([q_w, qi_w, kv_w, kiw_w, zpad(LANES - kiw_w.shape[1]), zpad(LANES), rest_w],
                          axis=1).astype(BF16)
    assert w_a.shape[1] == ZA_COLS
    w_r = wl[:, :c0].astype(BF16)
    params = (norm_g[l], w_r, w_a, shift_mu[l], w0[l], w2[l], a0[l], a2[l], k_k[l], k_a[l], r_k[l],
              lnx_g[l], lnx_b[l], q_norm_g[l], k_norm_g[l],
              w_pa[l].astype(BF16), w_pb[l].astype(BF16), w_out[l].astype(BF16))

    xp = x_prompt.reshape(bsz * seq, d)
    attend_p = functools.partial(_attn_prompt, rel_bias=rel_bias, batch=bsz, seq_len=seq,
                                 tq=min(seq, 256), k_sel=topk_p)
    yp, zr_p, za_p, kn_p, st_p = _layer(
        xp, jnp.zeros((bsz, 1, RWKV_COLS), F32), jnp.zeros((HEAD, HEAD, bsz * RWKV_HEADS), F32), params,
        batch=bsz, seq_len=seq, attend=attend_p)

    xs = x_sample.reshape(dec_bsz * dec_seq, d)
    attend_s = functools.partial(_attn_sample, cache_k=cache_k[l], cache_v=cache_v[l], cache_kidx=cache_kidx[l],
                                 page_table=page_table, rel_bias=rel_bias, qn_g=q_norm_g[l], dec_seq=dec_seq,
                                 k_sel=topk_s)
    shift_rows = jnp.repeat(state_shift[l], dec_seq, axis=0)
    ys, zr_s, za_s, kn_s, st_s = _layer(
        xs, shift_rows, _state_to_pairs(state_wkv[l]), params, batch=dec_bsz, seq_len=dec_seq, attend=attend_s)

    w = ATT_KV_HEADS * HEAD

    def pack(y, zr, za, kn, st, b, t):
        v = za[:, ZA_KV + w:ZA_KV + 2 * w]
        kidx = za[:, ZA_KIW:ZA_KIW + HEAD]
        return (y.reshape(b, t, d),
                kn.reshape(1, b, t, ATT_KV_HEADS, HEAD), v.reshape(1, b, t, ATT_KV_HEADS, HEAD),
                kidx.reshape(1, b, t, HEAD), _state_from_pairs(st, b)[None],
                zr.reshape(b, t, RWKV_COLS)[:, -1][None])

    p = pack(yp, zr_p, za_p, kn_p, st_p, bsz, seq)
    s = pack(ys, zr_s, za_s, kn_s, st_s, dec_bsz, dec_seq)
    return (p[0], s[0]) + p[1:] + s[1:]
```

```python
import functools
import math

import numpy as np
import jax
import jax.numpy as jnp
from jax import lax
from jax.experimental import pallas as pl
from jax.experimental.pallas import tpu as pltpu

F32 = jnp.float32
BF16 = jnp.bfloat16

D_MODEL = 1024
HEAD = 64
RWKV_HEADS = D_MODEL // HEAD
LORA = 64
LNX_EPS = 64e-5
ATT_HEADS = D_MODEL // HEAD
ATT_KV_HEADS = 2
ATT_GROUP = ATT_HEADS // ATT_KV_HEADS
IDX_HEADS = 8
TOPK_MAX = 256
REL_BUCKETS = 32
REL_MAX_DIST = 128
NORM_EPS = 1e-6
RWKV_COLS = 4 * D_MODEL + 2 * LORA

LANES = 128
SUBLANES = 8
VMEM_LIMIT_BYTES = 48 * 1024 * 1024

ZA_Q = 0
ZA_QI = 1024
ZA_KV = 1536
ZA_KIW = 1792
ZA_GATT = 2048
ZA_GA = 3072
ZA_GB = 4096
ZA_COLS = 5120

_NEG_INF = float("-inf")
_POS_INF = float("inf")


def _cparams(n_axes):
    return pltpu.CompilerParams(dimension_semantics=("arbitrary",) * n_axes,
                                vmem_limit_bytes=VMEM_LIMIT_BYTES)


def _sigmoid(x):
    return 1.0 / (1.0 + jnp.exp(-x))


def _dot_nt(a, b):
    return lax.dot_general(a, b, (((1,), (1,)), ((), ())), preferred_element_type=F32)


def _rmsnorm_kernel(x_ref, g_ref, o_ref):
    x = x_ref[...]
    ms = jnp.mean(x * x, axis=-1, keepdims=True)
    o_ref[...] = (x * lax.rsqrt(ms + NORM_EPS) * g_ref[...]).astype(o_ref.dtype)


def _rmsnorm(x, g, tm):
    m, d = x.shape
    return pl.pallas_call(
        _rmsnorm_kernel, grid=(m // tm,),
        in_specs=[pl.BlockSpec((tm, d), lambda i: (i, 0)), pl.BlockSpec((1, d), lambda i: (0, 0))],
        out_specs=pl.BlockSpec((tm, d), lambda i: (i, 0)),
        out_shape=jax.ShapeDtypeStruct((m, d), BF16),
        compiler_params=_cparams(1), name="rmsnorm")(x, g.reshape(1, d))


def _mm_kernel(x_ref, w_ref, o_ref):
    o_ref[...] = jnp.dot(x_ref[...], w_ref[...], preferred_element_type=F32)


def _matmul(x, w, tm, tn, name):
    m, k = x.shape
    n = w.shape[1]
    return pl.pallas_call(
        _mm_kernel, grid=(m // tm, n // tn),
        in_specs=[pl.BlockSpec((tm, k), lambda i, j: (i, 0)), pl.BlockSpec((k, tn), lambda i, j: (0, j))],
        out_specs=pl.BlockSpec((tm, tn), lambda i, j: (i, j)),
        out_shape=jax.ShapeDtypeStruct((m, n), F32),
        compiler_params=_cparams(2), name=name)(x, w)


def _rwkv_prep_kernel(z_ref, prev_ref, shift_ref, mu_ref, w0_ref, w2_ref, a0_ref, a2_ref, kk_ref, ka_ref,
                      r_o, w_o, k_o, v_o, kk_o, a_o, g_o, *, seq_len, tile):
    z = z_ref[...]
    rows = lax.broadcasted_iota(jnp.int32, z.shape, 0)
    rolled = pltpu.roll(z, 1, 0)
    if seq_len >= tile:
        first = jnp.where(pl.program_id(1) == 0, shift_ref[...], prev_ref[SUBLANES - 1:SUBLANES, :])
        prev = jnp.where(rows == 0, first, rolled)
    else:
        prev = jnp.where(lax.rem(rows, seq_len) == 0, shift_ref[...], rolled)
    zs = z + (prev - z) * mu_ref[...]
    d = D_MODEL
    r = zs[:, 0:d]
    k = zs[:, d:2 * d]
    v = zs[:, 2 * d:3 * d]
    g = zs[:, 3 * d:4 * d]
    wd = zs[:, 4 * d:4 * d + LORA]
    ad = zs[:, 4 * d + LORA:4 * d + 2 * LORA]
    wl = w0_ref[...] + jnp.dot(jnp.tanh(wd).astype(BF16), w2_ref[...], preferred_element_type=F32)
    decay = jnp.exp(-_sigmoid(wl) * math.exp(-0.5))
    a = _sigmoid(a0_ref[...] + jnp.dot(ad.astype(BF16), a2_ref[...], preferred_element_type=F32))
    r_o[...] = r
    w_o[...] = decay
    k_o[...] = k * (1.0 + (a - 1.0) * ka_ref[...])
    v_o[...] = v
    kk_o[...] = k * kk_ref[...]
    a_o[...] = a
    g_o[...] = g * _sigmoid(g)


def _rwkv_prep(zr, shift, mu, w0, w2, a0, a2, k_k, k_a, *, batch, seq_len, tile):
    m, c = zr.shape
    d = D_MODEL
    row = lambda x: x.reshape(1, -1)
    consts = [row(mu), row(w0), w2.astype(BF16), row(a0), a2.astype(BF16), row(k_k), row(k_a)]
    const_specs = [pl.BlockSpec(x.shape, lambda *_: (0, 0)) for x in consts]
    if seq_len >= tile:
        nt = seq_len // tile
        grid = (batch, nt)
        zmap = lambda b, t: (b * nt + t, 0)
        pmap = lambda b, t: (jnp.maximum((b * seq_len + t * tile) // SUBLANES - 1, 0), 0)
        shift_spec = pl.BlockSpec((None, 1, c), lambda b, t: (b, 0, 0))
    else:
        grid = (1, m // tile)
        zmap = lambda b, t: (t, 0)
        pmap = lambda b, t: (0, 0)
        shift_spec = pl.BlockSpec((tile, c), zmap)
    out_spec = pl.BlockSpec((tile, d), zmap)
    kern = functools.partial(_rwkv_prep_kernel, seq_len=seq_len, tile=tile)
    return pl.pallas_call(
        kern, grid=grid,
        in_specs=[pl.BlockSpec((tile, c), zmap), pl.BlockSpec((SUBLANES, c), pmap), shift_spec] + const_specs,
        out_specs=[out_spec] * 7,
        out_shape=[jax.ShapeDtypeStruct((m, d), F32)] * 7,
        compiler_params=_cparams(2), name="rwkv_prep")(zr, zr, shift, *consts)


def _rwkv_scan_kernel(r_ref, w_ref, k_ref, v_ref, kk_ref, a_ref, s0_ref, lg_ref, lb_ref, rk_ref,
                      o_ref, s_ref, vec_ref, *, steps):
    nb = HEAD // SUBLANES

    @pl.when(pl.program_id(1) == 0)
    def _():
        s_ref[...] = s0_ref[...]

    def bcast_row(ref, lead, j):
        return jnp.broadcast_to(ref[lead, pl.ds(j, 1), :], (SUBLANES, LANES))

    def step(t, carry):
        kkraw = kk_ref[t]
        n2 = jnp.sum(kkraw * kkraw, axis=0, keepdims=True)
        kk = kkraw / jnp.maximum(jnp.sqrt(n2), 1e-12)
        vec_ref[0] = -kk
        vec_ref[1] = kk * a_ref[t]
        sa = [jnp.zeros((SUBLANES, LANES), F32) for _ in range(nb)]
        for j in range(HEAD):
            nk = bcast_row(vec_ref, 0, j)
            for ib in range(nb):
                sa[ib] = sa[ib] + s_ref[j, ib * SUBLANES:(ib + 1) * SUBLANES, :] * nk
        vt = [v_ref[t, ib * SUBLANES:(ib + 1) * SUBLANES, :] for ib in range(nb)]
        out = [jnp.zeros((SUBLANES, LANES), F32) for _ in range(nb)]
        for j in range(HEAD):
            wj = bcast_row(w_ref, t, j)
            kaj = bcast_row(vec_ref, 1, j)
            kj = bcast_row(k_ref, t, j)
            rj = bcast_row(r_ref, t, j)
            for ib in range(nb):
                sl = slice(ib * SUBLANES, (ib + 1) * SUBLANES)
                sn = s_ref[j, sl, :] * wj + sa[ib] * kaj + vt[ib] * kj
                s_ref[j, sl, :] = sn
                out[ib] = out[ib] + sn * rj
        o = jnp.concatenate(out, axis=0)
        mean = jnp.mean(o, axis=0, keepdims=True)
        dev = o - mean
        var = jnp.mean(dev * dev, axis=0, keepdims=True)
        y = dev * lax.rsqrt(var + LNX_EPS) * lg_ref[...] + lb_ref[...]
        bonus = jnp.sum(r_ref[t] * k_ref[t] * rk_ref[...], axis=0, keepdims=True) * v_ref[t]
        o_ref[t] = y + bonus
        return carry

    lax.fori_loop(0, steps, step, 0)


def _rwkv_scan(seqs, s0, lg_t, lb_t, rk_t, *, steps):
    t_len, n, p = seqs[0].shape
    grid = (p // LANES, t_len // steps)
    seq_spec = pl.BlockSpec((steps, n, LANES), lambda g, t: (t, 0, g))
    st_spec = pl.BlockSpec((n, n, LANES), lambda g, t: (0, 0, g))
    c_spec = pl.BlockSpec((n, LANES), lambda g, t: (0, g))
    kern = functools.partial(_rwkv_scan_kernel, steps=steps)
    return pl.pallas_call(
        kern, grid=grid,
        in_specs=[seq_spec] * 6 + [st_spec] + [c_spec] * 3,
        out_specs=[seq_spec, st_spec],
        out_shape=[jax.ShapeDtypeStruct((t_len, n, p), F32), jax.ShapeDtypeStruct((n, n, p), F32)],
        scratch_shapes=[pltpu.VMEM((2, n, LANES), F32)],
        compiler_params=_cparams(2), name="rwkv_scan")(*seqs, s0, lg_t, lb_t, rk_t)


def _to_pairs(x, batch, seq_len):
    x = x.reshape(batch, seq_len, RWKV_HEADS, HEAD)
    return jnp.transpose(x, (1, 3, 0, 2)).reshape(seq_len, HEAD, batch * RWKV_HEADS)


def _from_pairs(x, batch, seq_len):
    x = x.reshape(seq_len, HEAD, batch, RWKV_HEADS)
    return jnp.transpose(x, (2, 0, 3, 1)).reshape(batch * seq_len, RWKV_HEADS * HEAD)


def _head_const_pairs(x, batch):
    x = x.reshape(RWKV_HEADS, HEAD).T
    return jnp.tile(x, (1, batch))


def _count_ge(score, thr):
    return jnp.sum(jnp.where(score >= thr, 1.0, 0.0), axis=-1, keepdims=True)


def _select_topk(score, kpos, k_sel, bisect_steps):
    kf = float(k_sel)
    vis = score > _NEG_INF
    nvis = jnp.sum(jnp.where(vis, 1.0, 0.0), axis=-1, keepdims=True)
    need_sel = nvis > kf
    rowmax = jnp.where(need_sel, jnp.max(score, axis=-1, keepdims=True), 0.0)
    rowmin = jnp.where(need_sel, jnp.min(jnp.where(vis, score, _POS_INF), axis=-1, keepdims=True), 0.0)

    def bis(_, c):
        lo, hi, chi = c
        hfin = jnp.where(hi == _POS_INF, rowmax, hi)
        piv = 0.5 * lo + 0.5 * hfin
        cnt = _count_ge(score, piv)
        ge = cnt >= kf
        return jnp.where(ge, piv, lo), jnp.where(ge, hi, piv), jnp.where(ge, chi, cnt)

    lo, hi, chi = lax.fori_loop(0, bisect_steps, bis,
                                (rowmin, jnp.full_like(rowmin, _POS_INF), jnp.zeros_like(rowmin)))

    def walk_cond(c):
        return jnp.min(c[4]) < 0.5

    def walk(c):
        hi, chi, tau, cgt, done, ceq = c
        pending = done < 0.5
        bmax = jnp.max(jnp.where(score < hi, score, _NEG_INF), axis=-1, keepdims=True)
        cnt = _count_ge(score, bmax)
        fin = jnp.logical_and(cnt >= kf, pending)
        tau = jnp.where(fin, bmax, tau)
        cgt = jnp.where(fin, chi, cgt)
        ceq = jnp.where(fin, cnt - chi, ceq)
        adv = jnp.logical_and(cnt < kf, pending)
        hi = jnp.where(adv, bmax, hi)
        chi = jnp.where(adv, cnt, chi)
        return hi, chi, tau, cgt, jnp.where(fin, 1.0, done), ceq

    done0 = jnp.where(need_sel, 0.0, 1.0)
    neg = jnp.full_like(rowmin, _NEG_INF)
    zero = jnp.zeros_like(rowmin)
    _, _, tau, cgt, _, ceq = lax.while_loop(walk_cond, walk, (hi, chi, neg, zero, done0, zero))

    need = kf - cgt
    excess = jnp.logical_and(need_sel, ceq > need)
    eq = score == tau
    n_keys = score.shape[-1]

    def tie_break(_):
        def body(_, c):
            plo, phi = c
            mid = lax.shift_right_arithmetic(plo + phi, 1)
            cnt = jnp.sum(jnp.where(jnp.logical_and(eq, kpos <= mid), 1.0, 0.0), axis=-1, keepdims=True)
            ge = cnt >= need
            return jnp.where(ge, plo, mid), jnp.where(ge, mid, phi)
        plo0 = jnp.full(tau.shape, -1, jnp.int32)
        phi0 = jnp.full(tau.shape, n_keys - 1, jnp.int32)
        _, phi = lax.fori_loop(0, n_keys.bit_length() + 1, body, (plo0, phi0))
        return jnp.where(excess, phi, n_keys)

    any_excess = jnp.max(jnp.where(excess, 1.0, 0.0)) > 0.0
    pcut = lax.cond(any_excess, tie_break, lambda _: jnp.full(tau.shape, n_keys, jnp.int32), 0)
    pcut = jnp.where(need_sel, pcut, -1)
    return jnp.logical_or(score > tau, jnp.logical_and(eq, kpos <= pcut))


_MASKED = -1e30


def _select_topk_chunked(s_ref, nch, ck, k_sel, nvis, bisect_steps):
    rows = s_ref.shape[0]
    kf = float(k_sel)
    nfold = ck // LANES

    def chunk(c):
        return s_ref[:, pl.ds(pl.multiple_of(c * ck, ck), ck)]

    def fold(x, op):
        out = x[:, 0:LANES]
        for i in range(1, nfold):
            out = op(out, x[:, i * LANES:(i + 1) * LANES])
        return out

    def count(pred):
        def body(c, acc):
            return acc + fold(jnp.where(pred(chunk(c), c), 1.0, 0.0), jnp.add)
        acc = lax.fori_loop(0, nch, body, jnp.zeros((rows, LANES), F32))
        return jnp.sum(acc, axis=-1, keepdims=True)

    def row_max(val):
        def body(c, acc):
            return jnp.maximum(acc, fold(val(chunk(c)), jnp.maximum))
        acc = lax.fori_loop(0, nch, body, jnp.full((rows, LANES), _NEG_INF, F32))
        return jnp.max(acc, axis=-1, keepdims=True)

    need_sel = nvis > kf
    rowmax = jnp.where(need_sel, row_max(lambda x: x), 0.0)
    rowmin = jnp.where(need_sel, -row_max(lambda x: jnp.where(x > _NEG_INF, -x, _NEG_INF)), 0.0)

    def bis(_, c):
        lo, hi, chi = c
        hfin = jnp.where(hi == _POS_INF, rowmax, hi)
        piv = 0.5 * lo + 0.5 * hfin
        cnt = count(lambda x, _c: x >= piv)
        ge = cnt >= kf
        return jnp.where(ge, piv, lo), jnp.where(ge, hi, piv), jnp.where(ge, chi, cnt)

    lo, hi, chi = lax.fori_loop(0, bisect_steps, bis,
                                (rowmin, jnp.full_like(rowmin, _POS_INF), jnp.zeros_like(rowmin)))

    def walk_cond(c):
        return jnp.min(c[4]) < 0.5

    def walk(c):
        hi, chi, tau, cgt, done, ceq = c
        pending = done < 0.5
        bmax = row_max(lambda x: jnp.where(x < hi, x, _NEG_INF))
        cnt = count(lambda x, _c: x >= bmax)
        fin = jnp.logical_and(cnt >= kf, pending)
        tau = jnp.where(fin, bmax, tau)
        cgt = jnp.where(fin, chi, cgt)
        ceq = jnp.where(fin, cnt - chi, ceq)
        adv = jnp.logical_and(cnt < kf, pending)
        hi = jnp.where(adv, bmax, hi)
        chi = jnp.where(adv, cnt, chi)
        return hi, chi, tau, cgt, jnp.where(fin, 1.0, done), ceq

    neg = jnp.full_like(rowmin, _NEG_INF)
    zero = jnp.zeros_like(rowmin)
    _, _, tau, cgt, _, ceq = lax.while_loop(
        walk_cond, walk, (hi, chi, neg, zero, jnp.where(need_sel, 0.0, 1.0), zero))

    need = kf - cgt
    excess = jnp.logical_and(need_sel, ceq > need)
    n_keys = s_ref.shape[1]

    def kpos(c):
        return c * ck + lax.broadcasted_iota(jnp.int32, (rows, ck), 1)

    def tie_break(_):
        def body(_, c):
            plo, phi = c
            mid = lax.shift_right_arithmetic(plo + phi, 1)
            cnt = count(lambda x, cc: jnp.logical_and(x == tau, kpos(cc) <= mid))
            ge = cnt >= need
            return jnp.where(ge, plo, mid), jnp.where(ge, mid, phi)
        plo0 = jnp.full(tau.shape, -1, jnp.int32)
        phi0 = jnp.full(tau.shape, n_keys - 1, jnp.int32)
        _, phi = lax.fori_loop(0, n_keys.bit_length() + 1, body, (plo0, phi0))
        return jnp.where(excess, phi, n_keys)

    any_excess = jnp.max(jnp.where(excess, 1.0, 0.0)) > 0.0
    pcut = lax.cond(any_excess, tie_break, lambda _: jnp.full(tau.shape, n_keys, jnp.int32), 0)
    pcut = jnp.where(need_sel, pcut, -1)

    def write(c, carry):
        x = chunk(c)
        sel = jnp.logical_or(x > tau, jnp.logical_and(x == tau, kpos(c) <= pcut))
        s_ref[:, pl.ds(pl.multiple_of(c * ck, ck), ck)] = jnp.where(sel, 0.0, _MASKED)
        return carry

    lax.fori_loop(0, nch, write, 0)


def _attn_prep_kernel(q_ref, qi_ref, kv_ref, kiw_ref, qg_ref, kg_ref, kn_o, qb_o, qib_o, kvb_o, kib_o):
    w = ATT_KV_HEADS * HEAD
    for h in range(ATT_KV_HEADS):
        hs = slice(h * HEAD, (h + 1) * HEAD)
        x = kv_ref[:, hs]
        kn = x * lax.rsqrt(jnp.mean(x * x, axis=-1, keepdims=True) + NORM_EPS) * kg_ref[...]
        kn_o[:, hs] = kn
        kvb_o[:, hs] = kn.astype(BF16)
        v = kv_ref[:, w + h * HEAD:w + (h + 1) * HEAD].astype(BF16)
        kvb_o[:, w + h * LANES:w + (h + 1) * LANES] = jnp.concatenate([v, jnp.ones_like(v)], axis=1)
    for h in range(ATT_HEADS):
        x = q_ref[:, h * HEAD:(h + 1) * HEAD]
        qn = x * lax.rsqrt(jnp.mean(x * x, axis=-1, keepdims=True) + NORM_EPS) * (qg_ref[...] * HEAD ** -0.5)
        qb_o[h] = qn.astype(BF16)
    qib_o[...] = (qi_ref[...] * HEAD ** -0.5).astype(BF16)
    kib_o[...] = kiw_ref[:, 0:HEAD].astype(BF16)


def _attn_prep(za, qn_g, kn_g, tm):
    m = za.shape[0]
    w = ATT_KV_HEADS * HEAD
    wi = IDX_HEADS * HEAD
    row = lambda width, blk: pl.BlockSpec((tm, width), lambda i, blk=blk: (i, blk))
    gspec = pl.BlockSpec((1, HEAD), lambda i: (0, 0))
    return pl.pallas_call(
        _attn_prep_kernel, grid=(m // tm,),
        in_specs=[row(D_MODEL, ZA_Q // D_MODEL), row(wi, ZA_QI // wi), row(2 * w, ZA_KV // (2 * w)),
                  row(LANES, ZA_KIW // LANES), gspec, gspec],
        out_specs=[row(w, 0), pl.BlockSpec((ATT_HEADS, tm, HEAD), lambda i: (0, i, 0)), row(wi, 0),
                   row(w + ATT_KV_HEADS * LANES, 0), row(HEAD, 0)],
        out_shape=[jax.ShapeDtypeStruct((m, w), F32), jax.ShapeDtypeStruct((ATT_HEADS, m, HEAD), BF16),
                   jax.ShapeDtypeStruct((m, wi), BF16), jax.ShapeDtypeStruct((m, w + ATT_KV_HEADS * LANES), BF16),
                   jax.ShapeDtypeStruct((m, HEAD), BF16)],
        compiler_params=_cparams(1), name="attn_prep")(za, za, za, za, qn_g.reshape(1, HEAD), kn_g.reshape(1, HEAD))


def _bucket_edges():
    max_exact = REL_BUCKETS // 2
    d = np.arange(REL_MAX_DIST + 1)
    df = np.maximum(d, 1).astype(np.float32)
    large = max_exact + (np.log(df / max_exact) / math.log(REL_MAX_DIST / max_exact)
                         * (REL_BUCKETS - max_exact)).astype(np.int32)
    bucket = np.where(d < max_exact, d, np.minimum(large, REL_BUCKETS - 1))
    return [int(np.argmax(bucket >= b)) for b in range(REL_BUCKETS)]


_BUCKET_EDGES = _bucket_edges()


def _rel_bias_lookup(dist, value_of_bucket):
    bias = value_of_bucket(REL_BUCKETS - 1)
    for b in range(REL_BUCKETS - 2, -1, -1):
        bias = jnp.where(dist < _BUCKET_EDGES[b + 1], value_of_bucket(b), bias)
    return bias


def _attn_prompt_kernel(rb_ref, q_ref, qi_ref, kiwq_ref, g_ref, kvb_ref, kib_ref, o_ref, s_scr, tbd_ref,
                        *, tq, k_sel, bisect_steps):
    qt = pl.program_id(0)
    q0 = qt * tq
    ck = tq
    rg_rows = LANES
    n_rg = tq // rg_rows
    w = ATT_KV_HEADS * HEAD

    @pl.when(jnp.logical_and(pl.program_id(1) == 0, qt == 0))
    def _():
        rr = lax.broadcasted_iota(jnp.int32, (tq, tq), 0) - lax.broadcasted_iota(jnp.int32, (tq, tq), 1)
        for h in range(ATT_HEADS):
            far = rb_ref[REL_BUCKETS - 1, h]
            tbd_ref[h, 0] = _rel_bias_lookup(rr, lambda b: rb_ref[b, h]) - far
            tbd_ref[h, 1] = _rel_bias_lookup(rr + tq, lambda b: rb_ref[b, h]) - far

    def chunk_start(c):
        return pl.multiple_of(c * ck, ck)

    wcol = kiwq_ref[:, HEAD:HEAD + IDX_HEADS] * (IDX_HEADS ** -0.5)
    qi_h = [[qi_ref[rg * rg_rows:(rg + 1) * rg_rows, h * HEAD:(h + 1) * HEAD] for h in range(IDX_HEADS)]
            for rg in range(n_rg)]
    w_h = [[wcol[rg * rg_rows:(rg + 1) * rg_rows, h:h + 1] for h in range(IDX_HEADS)] for rg in range(n_rg)]

    def score_chunk(c, carry):
        k0 = chunk_start(c)
        kc = kib_ref[pl.ds(k0, ck), :]
        kpos = k0 + lax.broadcasted_iota(jnp.int32, (rg_rows, ck), 1)
        for rg in range(n_rg):
            acc = jnp.zeros((rg_rows, ck), F32)
            for h in range(IDX_HEADS):
                acc = acc + jnp.maximum(_dot_nt(qi_h[rg][h], kc), 0.0) * w_h[rg][h]
            qpos = q0 + rg * rg_rows + lax.broadcasted_iota(jnp.int32, (rg_rows, ck), 0)
            s_scr[rg * rg_rows:(rg + 1) * rg_rows, pl.ds(k0, ck)] = jnp.where(kpos <= qpos, acc, _NEG_INF)
        return carry

    nch = qt + 1
    lax.fori_loop(0, nch, score_chunk, 0)

    nvis = (q0 + 1 + lax.broadcasted_iota(jnp.int32, (tq, 1), 0)).astype(F32)
    _select_topk_chunked(s_scr, nch, ck, k_sel, nvis, bisect_steps)

    def attend(n_vis_chunks):
        lq = n_vis_chunks * ck

        def near_bias(s, h, rs):
            parts = [s[:, lq - ck:] + tbd_ref[h, 0, rs, :]]
            if n_vis_chunks > 1:
                parts = [s[:, lq - 2 * ck:lq - ck] + tbd_ref[h, 1, rs, :]] + parts
            if n_vis_chunks > 2:
                parts = [s[:, :lq - 2 * ck]] + parts
            return jnp.concatenate(parts, axis=1) if len(parts) > 1 else parts[0]

        for kvh in range(ATT_KV_HEADS):
            kk = kvb_ref[0:lq, kvh * HEAD:(kvh + 1) * HEAD]
            vx = kvb_ref[0:lq, w + kvh * LANES:w + (kvh + 1) * LANES]

            def head_quad(i, carry, kvh=kvh, kk=kk, vx=vx):
                for pp in range(2):
                    pair = kvh * (ATT_GROUP // 2) + 2 * i + pp
                    cols = pl.ds(pl.multiple_of(pair * LANES, LANES), LANES)
                    for rg in range(n_rg):
                        rs = slice(rg * rg_rows, (rg + 1) * rg_rows)
                        outs = []
                        for hh in range(2):
                            h = 2 * pair + hh
                            s = near_bias(_dot_nt(q_ref[h, rs, :], kk) + s_scr[rs, 0:lq], h, rs)
                            p = jnp.exp(s - jnp.max(s, axis=-1, keepdims=True)).astype(BF16)
                            acc = jnp.dot(p, vx, preferred_element_type=F32)
                            outs.append(acc[:, 0:HEAD] / acc[:, HEAD:HEAD + 1])
                        gh = g_ref[rs, cols]
                        o_ref[rs, cols] = jnp.concatenate(outs, axis=1) * (gh * _sigmoid(gh))
                return carry

            lax.fori_loop(0, ATT_GROUP // 4, head_quad, 0)

    for n_vis_chunks in range(1, s_scr.shape[1] // ck + 1):
        pl.when(nch == n_vis_chunks)(functools.partial(attend, n_vis_chunks))


def _attn_prompt(za, kn, qb, qib, kvb, kib, *, rel_bias, batch, seq_len, tq, k_sel):
    m = za.shape[0]
    nq = seq_len // tq
    w = ATT_KV_HEADS * HEAD
    wi = IDX_HEADS * HEAD
    assert REL_MAX_DIST <= tq and tq % LANES == 0
    row_map = lambda blk: (lambda t, b, blk=blk: (b * nq + t, blk))
    key_map = lambda t, b: (b, 0)
    kern = functools.partial(_attn_prompt_kernel, tq=tq, k_sel=k_sel, bisect_steps=14)
    return pl.pallas_call(
        kern, grid=(nq, batch),
        in_specs=[
            pl.BlockSpec(memory_space=pltpu.SMEM),
            pl.BlockSpec((ATT_HEADS, tq, HEAD), lambda t, b: (0, b * nq + t, 0)),
            pl.BlockSpec((tq, wi), row_map(0)),
            pl.BlockSpec((tq, LANES), row_map(ZA_KIW // LANES)),
            pl.BlockSpec((tq, D_MODEL), row_map(ZA_GATT // D_MODEL)),
            pl.BlockSpec((seq_len, w + ATT_KV_HEADS * LANES), key_map),
            pl.BlockSpec((seq_len, HEAD), key_map),
        ],
        out_specs=pl.BlockSpec((tq, D_MODEL), row_map(0)),
        out_shape=jax.ShapeDtypeStruct((m, D_MODEL), F32),
        scratch_shapes=[pltpu.VMEM((tq, seq_len), F32), pltpu.VMEM((ATT_HEADS, 2, tq, tq), F32)],
        compiler_params=_cparams(2), name="attn_prompt")(rel_bias, qb, qib, za, za, kvb, kib)


def _sample_score_kernel(pt_ref, q_ref, w_ref, kiw_new_ref, *rest, n_pages, page, dec_seq):
    page_refs = rest[:n_pages]
    o_ref = rest[n_pages]
    past = n_pages * page
    kidx = jnp.concatenate([r[...] for r in page_refs], axis=0).astype(BF16)
    new = kiw_new_ref[:, 0:HEAD]
    new = jnp.concatenate([new, jnp.zeros((LANES - dec_seq, HEAD), F32)], axis=0).astype(BF16)
    q = (q_ref[...] * (HEAD ** -0.5)).astype(BF16)
    lg = jnp.concatenate([_dot_nt(q, kidx), _dot_nt(q, new)], axis=1)
    wr = jnp.maximum(lg, 0.0) * (w_ref[...] * (IDX_HEADS ** -0.5))
    n_keys = past + LANES
    sc = jnp.sum(wr.reshape(dec_seq, IDX_HEADS, n_keys), axis=1)
    kpos = lax.broadcasted_iota(jnp.int32, (dec_seq, n_keys), 1)
    tpos = lax.broadcasted_iota(jnp.int32, (dec_seq, n_keys), 0)
    o_ref[...] = jnp.where(kpos <= past + tpos, sc, _NEG_INF)


def _sample_select_kernel(s_ref, o_ref, *, k_sel, bisect_steps):
    score = s_ref[...]
    kpos = lax.broadcasted_iota(jnp.int32, score.shape, 1)
    sel = _select_topk(score, kpos, k_sel, bisect_steps)
    o_ref[...] = jnp.where(sel, 0.0, _NEG_INF)


def _sample_attn_kernel(pt_ref, q_ref, g_ref, kn_new_ref, kv_new_ref, mask_ref, rb_rows_ref, qg_ref, *rest,
                        n_pages, page, dec_seq):
    k_pages = rest[:n_pages]
    v_pages = rest[n_pages:2 * n_pages]
    o_ref = rest[2 * n_pages]
    bias_ref = rest[2 * n_pages + 1]
    w = ATT_KV_HEADS * HEAD
    rows = dec_seq * ATT_HEADS
    n_keys = n_pages * page + LANES

    @pl.when(pl.program_id(0) == 0)
    def _():
        t_row = lax.div(lax.broadcasted_iota(jnp.int32, (rows, n_keys), 0), ATT_HEADS)
        dist = n_pages * page + t_row - lax.broadcasted_iota(jnp.int32, (rows, n_keys), 1)
        bias_ref[...] = _rel_bias_lookup(dist, lambda b: rb_rows_ref[:, b:b + 1])

    pad = jnp.zeros((LANES - dec_seq, w), F32)
    k_all = jnp.concatenate([r[...] for r in k_pages] + [kn_new_ref[...], pad], axis=0).astype(BF16)
    v_all = jnp.concatenate([r[...] for r in v_pages] + [kv_new_ref[:, w:2 * w], pad], axis=0).astype(BF16)
    q = q_ref[...]
    q = q * lax.rsqrt(jnp.mean(q * q, axis=-1, keepdims=True) + NORM_EPS) * qg_ref[...]
    qb = q.astype(BF16)
    head = lax.rem(lax.broadcasted_iota(jnp.int32, (rows, 1), 0), ATT_HEADS)
    first = head < ATT_GROUP
    lg = jnp.where(first, _dot_nt(qb, k_all[:, 0:HEAD]), _dot_nt(qb, k_all[:, HEAD:2 * HEAD]))
    mask = jnp.broadcast_to(mask_ref[...][:, None, :], (dec_seq, ATT_HEADS, n_keys)).reshape(rows, n_keys)
    s = lg * (HEAD ** -0.5) + bias_ref[...] + mask
    p = jnp.exp(s - jnp.max(s, axis=-1, keepdims=True))
    l = jnp.sum(p, axis=-1, keepdims=True)
    pb = p.astype(BF16)
    o = jnp.where(first, jnp.dot(pb, v_all[:, 0:HEAD], preferred_element_type=F32),
                  jnp.dot(pb, v_all[:, HEAD:2 * HEAD], preferred_element_type=F32)) / l
    g = g_ref[...]
    o_ref[...] = o * (g * _sigmoid(g))


def _attn_sample(za, kn, qb, qib, kvb, kib, *, cache_k, cache_v, cache_kidx, page_table, rel_bias, qn_g,
                 dec_seq, k_sel):
    nb, n_pages = page_table.shape
    n_phys, page = cache_k.shape[0], cache_k.shape[1]
    past = n_pages * page
    n_keys = past + LANES
    w = ATT_KV_HEADS * HEAD
    ck = cache_k.reshape(n_phys, page, w)
    cv = cache_v.reshape(n_phys, page, w)
    ci = cache_kidx.reshape(n_phys, page, HEAD)
    za3 = za.reshape(nb, dec_seq, ZA_COLS)
    kn3 = kn.reshape(nb, dec_seq, w)
    qi = za[:, ZA_QI:ZA_QI + IDX_HEADS * HEAD].reshape(nb, dec_seq * IDX_HEADS, HEAD)
    wi = za[:, ZA_KIW + HEAD:ZA_KIW + HEAD + IDX_HEADS].reshape(nb, dec_seq * IDX_HEADS, 1)
    qa = za[:, ZA_Q:ZA_Q + D_MODEL].reshape(nb, dec_seq * ATT_HEADS, HEAD)
    ga = za[:, ZA_GATT:ZA_GATT + D_MODEL].reshape(nb, dec_seq * ATT_HEADS, HEAD)

    def page_specs(width):
        return [pl.BlockSpec((None, page, width), lambda b, pt, j=j: (pt[b, j], 0, 0)) for j in range(n_pages)]

    per_b = lambda r, c: pl.BlockSpec((None, r, c), lambda b, pt: (b, 0, 0))
    kiw_new = pl.BlockSpec((None, dec_seq, LANES), lambda b, pt: (b, 0, ZA_KIW // LANES))

    scores = pl.pallas_call(
        functools.partial(_sample_score_kernel, n_pages=n_pages, page=page, dec_seq=dec_seq),
        grid_spec=pltpu.PrefetchScalarGridSpec(
            num_scalar_prefetch=1, grid=(nb,),
            in_specs=[per_b(dec_seq * IDX_HEADS, HEAD), per_b(dec_seq * IDX_HEADS, 1), kiw_new] + page_specs(HEAD),
            out_specs=per_b(dec_seq, n_keys)),
        out_shape=jax.ShapeDtypeStruct((nb, dec_seq, n_keys), F32),
        compiler_params=_cparams(1), name="sample_scores")(page_table, qi, wi, za3, *([ci] * n_pages))

    rows = nb * dec_seq
    tr = min(rows, 128)
    mask = pl.pallas_call(
        functools.partial(_sample_select_kernel, k_sel=k_sel, bisect_steps=14),
        grid=(rows // tr,),
        in_specs=[pl.BlockSpec((tr, n_keys), lambda i: (i, 0))],
        out_specs=pl.BlockSpec((tr, n_keys), lambda i: (i, 0)),
        out_shape=jax.ShapeDtypeStruct((rows, n_keys), F32),
        compiler_params=_cparams(1), name="sample_select")(scores.reshape(rows, n_keys))

    rb_rows = jnp.tile(rel_bias.T, (dec_seq, 1))

    const = lambda shape: pl.BlockSpec(shape, lambda b, pt: (0,) * len(shape))
    kv_new = pl.BlockSpec((None, dec_seq, 2 * w), lambda b, pt: (b, 0, ZA_KV // (2 * w)))
    out = pl.pallas_call(
        functools.partial(_sample_attn_kernel, n_pages=n_pages, page=page, dec_seq=dec_seq),
        grid_spec=pltpu.PrefetchScalarGridSpec(
            num_scalar_prefetch=1, grid=(nb,),
            in_specs=[per_b(dec_seq * ATT_HEADS, HEAD), per_b(dec_seq * ATT_HEADS, HEAD), per_b(dec_seq, w),
                      kv_new, per_b(dec_seq, n_keys), const((dec_seq * ATT_HEADS, REL_BUCKETS)), const((1, HEAD))]
            + page_specs(w) + page_specs(w),
            out_specs=per_b(dec_seq * ATT_HEADS, HEAD),
            scratch_shapes=[pltpu.VMEM((dec_seq * ATT_HEADS, n_keys), F32)]),
        out_shape=jax.ShapeDtypeStruct((nb, dec_seq * ATT_HEADS, HEAD), F32),
        compiler_params=_cparams(1), name="sample_attn")(
            page_table, qa, ga, kn3, za3, mask.reshape(nb, dec_seq, n_keys), rb_rows, qn_g.reshape(1, HEAD),
            *([ck] * n_pages), *([cv] * n_pages))
    return out.reshape(rows, D_MODEL)


def _merge_kernel(x_ref, oa_ref, sg_ref, ob_ref, ga_ref, gb_ref, wpa_ref, wpb_ref, wo_ref, y_ref):
    oa = (oa_ref[...] * sg_ref[...]).astype(BF16)
    pa = jnp.dot(oa, wpa_ref[...], preferred_element_type=F32)
    pb = jnp.dot(ob_ref[...].astype(BF16), wpb_ref[...], preferred_element_type=F32)
    merged = _sigmoid(ga_ref[...]) * pa + _sigmoid(gb_ref[...]) * pb
    y_ref[...] = x_ref[...] + jnp.dot(merged.astype(BF16), wo_ref[...], preferred_element_type=F32)


def _merge(x, oa, sg, ob, za, w_pa, w_pb, w_out, tm):
    m, d = x.shape
    row = pl.BlockSpec((tm, d), lambda i: (i, 0))
    wsp = pl.BlockSpec((d, d), lambda i: (0, 0))
    return pl.pallas_call(
        _merge_kernel, grid=(m // tm,),
        in_specs=[row, row, row, row,
                  pl.BlockSpec((tm, d), lambda i: (i, ZA_GA // d)), pl.BlockSpec((tm, d), lambda i: (i, ZA_GB // d)),
                  wsp, wsp, wsp],
        out_specs=row,
        out_shape=jax.ShapeDtypeStruct((m, d), F32),
        compiler_params=_cparams(1), name="merge")(x, oa, sg, ob, za, za, w_pa, w_pb, w_out)


def _layer(x, shift, s0, params, *, batch, seq_len, attend):
    (norm_g, w_r, w_a, mu, w0, w2, a0, a2, k_k, k_a, r_k, lnx_g, lnx_b, qn_g, kn_g, w_pa, w_pb, w_out) = params
    m = batch * seq_len
    tm = min(m, 512)
    xn = _rmsnorm(x, norm_g, tm)
    zr = _matmul(xn, w_r, tm, RWKV_COLS // 3, "inproj_rwkv")
    za = _matmul(xn, w_a, tm, ZA_COLS // 4, "inproj_attn")

    tile = min(m, 256)
    r, w, k, v, kk, a, sg = _rwkv_prep(zr, shift, mu, w0, w2, a0, a2, k_k, k_a,
                                       batch=batch, seq_len=seq_len, tile=tile)
    seqs = [_to_pairs(t, batch, seq_len) for t in (r, w, k, v, kk, a)]
    consts = [_head_const_pairs(t, batch) for t in (lnx_g, lnx_b, r_k.reshape(-1))]
    o_t, s_t = _rwkv_scan(seqs, s0, *consts, steps=min(seq_len, 32))
    oa = _from_pairs(o_t, batch, seq_len)

    kn, qb, qib, kvb, kib = _attn_prep(za, qn_g, kn_g, tm)
    ob = attend(za, kn, qb, qib, kvb, kib)

    y = _merge(x, oa, sg, ob, za, w_pa, w_pb, w_out, min(m, 256))
    return y, zr, za, kn, s_t


def _state_to_pairs(s):
    b, h, n, _ = s.shape
    return jnp.transpose(s, (3, 2, 0, 1)).reshape(n, n, b * h)


def _state_from_pairs(s, batch):
    n = s.shape[0]
    return jnp.transpose(s.reshape(n, n, batch, RWKV_HEADS), (2, 3, 1, 0))


def kernel(x_prompt, x_sample, cache_k, cache_v, cache_kidx, state_wkv, state_shift, page_table, norm_g, w_in,
           shift_mu, w0, w2, a0, a2, k_k, k_a, r_k, lnx_g, lnx_b, q_norm_g, k_norm_g, rel_bias, w_pa, w_pb, w_out):
    bsz, seq, d = x_prompt.shape
    dec_bsz, dec_seq, _ = x_sample.shape
    depth = w_in.shape[0]
    assert depth == 1 and d == D_MODEL
    past_len = page_table.shape[1] * cache_k.shape[2]
    topk_p = min(TOPK_MAX, seq // 4)
    topk_s = min(TOPK_MAX, (past_len + dec_seq) // 4)
    l = 0

    wl = w_in[l]
    c0 = RWKV_COLS
    q_w, kv_w, qi_w = wl[:, c0:c0 + 1024], wl[:, c0 + 1024:c0 + 1280], wl[:, c0 + 1280:c0 + 1792]
    kiw_w = wl[:, c0 + 1792:c0 + 1864]
    rest_w = wl[:, c0 + 1864:]
    zpad = lambda n: jnp.zeros((d, n), wl.dtype)
    w_a = jnp.concatenate([q_w, qi_w, kv_w, kiw_w, zpad(LANES - kiw_w.shape[1]), zpad(LANES), rest_w],
                          axis=1).astype(BF16)
    assert w_a.shape[1] == ZA_COLS
    w_r = wl[:, :c0].astype(BF16)
    params = (norm_g[l], w_r, w_a, shift_mu[l], w0[l], w2[l], a0[l], a2[l], k_k[l], k_a[l], r_k[l],
              lnx_g[l], lnx_b[l], q_norm_g[l], k_norm_g[l],
              w_pa[l].astype(BF16), w_pb[l].astype(BF16), w_out[l].astype(BF16))

    xp = x_prompt.reshape(bsz * seq, d)
    attend_p = functools.partial(_attn_prompt, rel_bias=rel_bias, batch=bsz, seq_len=seq,
                                 tq=min(seq, 256), k_sel=topk_p)
    yp, zr_p, za_p, kn_p, st_p = _layer(
        xp, jnp.zeros((bsz, 1, RWKV_COLS), F32), jnp.zeros((HEAD, HEAD, bsz * RWKV_HEADS), F32), params,
        batch=bsz, seq_len=seq, attend=attend_p)

    xs = x_sample.reshape(dec_bsz * dec_seq, d)
    attend_s = functools.partial(_attn_sample, cache_k=cache_k[l], cache_v=cache_v[l], cache_kidx=cache_kidx[l],
                                 page_table=page_table, rel_bias=rel_bias, qn_g=q_norm_g[l], dec_seq=dec_seq,
                                 k_sel=topk_s)
    shift_rows = jnp.repeat(state_shift[l], dec_seq, axis=0)
    ys, zr_s, za_s, kn_s, st_s = _layer(
        xs, shift_rows, _state_to_pairs(state_wkv[l]), params, batch=dec_bsz, seq_len=dec_seq, attend=attend_s)

    w = ATT_KV_HEADS * HEAD

    def pack(y, zr, za, kn, st, b, t):
        v = za[:, ZA_KV + w:ZA_KV + 2 * w]
        kidx = za[:, ZA_KIW:ZA_KIW + HEAD]
        return (y.reshape(b, t, d),
                kn.reshape(1, b, t, ATT_KV_HEADS, HEAD), v.reshape(1, b, t, ATT_KV_HEADS, HEAD),
                kidx.reshape(1, b, t, HEAD), _state_from_pairs(st, b)[None],
                zr.reshape(b, t, RWKV_COLS)[:, -1][None])

    p = pack(yp, zr_p, za_p, kn_p, st_p, bsz, seq)
    s = pack(ys, zr_s, za_s, kn_s, st_s, dec_bsz, dec_seq)
    return (p[0], s[0]) + p[1:] + s[1:]
```

```python
import functools
import math

import numpy as np
import jax
import jax.numpy as jnp
from jax import lax
from jax.experimental import pallas as pl
from jax.experimental.pallas import tpu as pltpu

F32 = jnp.float32
BF16 = jnp.bfloat16

D_MODEL = 1024
HEAD = 64
RWKV_HEADS = D_MODEL // HEAD
LORA = 64
LNX_EPS = 64e-5
ATT_HEADS = D_MODEL // HEAD
ATT_KV_HEADS = 2
ATT_GROUP = ATT_HEADS // ATT_KV_HEADS
IDX_HEADS = 8
TOPK_MAX = 256
REL_BUCKETS = 32
REL_MAX_DIST = 128
NORM_EPS = 1e-6
RWKV_COLS = 4 * D_MODEL + 2 * LORA

LANES = 128
SUBLANES = 8
VMEM_LIMIT_BYTES = 56 * 1024 * 1024

ZA_Q = 0
ZA_QI = 1024
ZA_KV = 1536
ZA_KIW = 1792
ZA_GATT = 2048
ZA_GA = 3072
ZA_GB = 4096
ZA_COLS = 5120

_NEG_INF = float("-inf")
_POS_INF = float("inf")


def _cparams(n_axes):
    return pltpu.CompilerParams(dimension_semantics=("arbitrary",) * n_axes,
                                vmem_limit_bytes=VMEM_LIMIT_BYTES)


def _sigmoid(x):
    return 1.0 / (1.0 + jnp.exp(-x))


def _dot_nt(a, b):
    return lax.dot_general(a, b, (((1,), (1,)), ((), ())), preferred_element_type=F32)


def _rmsnorm_kernel(x_ref, g_ref, o_ref):
    x = x_ref[...]
    ms = jnp.mean(x * x, axis=-1, keepdims=True)
    o_ref[...] = (x * lax.rsqrt(ms + NORM_EPS) * g_ref[...]).astype(o_ref.dtype)


def _rmsnorm(x, g, tm):
    m, d = x.shape
    return pl.pallas_call(
        _rmsnorm_kernel, grid=(m // tm,),
        in_specs=[pl.BlockSpec((tm, d), lambda i: (i, 0)), pl.BlockSpec((1, d), lambda i: (0, 0))],
        out_specs=pl.BlockSpec((tm, d), lambda i: (i, 0)),
        out_shape=jax.ShapeDtypeStruct((m, d), BF16),
        compiler_params=_cparams(1), name="rmsnorm")(x, g.reshape(1, d))


def _mm_kernel(x_ref, w_ref, o_ref):
    o_ref[...] = jnp.dot(x_ref[...], w_ref[...], preferred_element_type=F32)


def _matmul(x, w, tm, tn, name):
    m, k = x.shape
    n = w.shape[1]
    return pl.pallas_call(
        _mm_kernel, grid=(m // tm, n // tn),
        in_specs=[pl.BlockSpec((tm, k), lambda i, j: (i, 0)), pl.BlockSpec((k, tn), lambda i, j: (0, j))],
        out_specs=pl.BlockSpec((tm, tn), lambda i, j: (i, j)),
        out_shape=jax.ShapeDtypeStruct((m, n), F32),
        compiler_params=_cparams(2), name=name)(x, w)


def _rwkv_prep_kernel(z_ref, prev_ref, shift_ref, mu_ref, w0_ref, w2_ref, a0_ref, a2_ref, kk_ref, ka_ref,
                      r_o, w_o, k_o, v_o, kk_o, a_o, g_o, *, seq_len, tile):
    z = z_ref[...]
    rows = lax.broadcasted_iota(jnp.int32, z.shape, 0)
    rolled = pltpu.roll(z, 1, 0)
    if seq_len >= tile:
        first = jnp.where(pl.program_id(1) == 0, shift_ref[...], prev_ref[SUBLANES - 1:SUBLANES, :])
        prev = jnp.where(rows == 0, first, rolled)
    else:
        prev = jnp.where(lax.rem(rows, seq_len) == 0, shift_ref[...], rolled)
    zs = z + (prev - z) * mu_ref[...]
    d = D_MODEL
    r = zs[:, 0:d]
    k = zs[:, d:2 * d]
    v = zs[:, 2 * d:3 * d]
    g = zs[:, 3 * d:4 * d]
    wd = zs[:, 4 * d:4 * d + LORA]
    ad = zs[:, 4 * d + LORA:4 * d + 2 * LORA]
    wl = w0_ref[...] + jnp.dot(jnp.tanh(wd).astype(BF16), w2_ref[...], preferred_element_type=F32)
    decay = jnp.exp(-_sigmoid(wl) * math.exp(-0.5))
    a = _sigmoid(a0_ref[...] + jnp.dot(ad.astype(BF16), a2_ref[...], preferred_element_type=F32))
    r_o[...] = r
    w_o[...] = decay
    k_o[...] = k * (1.0 + (a - 1.0) * ka_ref[...])
    v_o[...] = v
    kk_o[...] = k * kk_ref[...]
    a_o[...] = a
    g_o[...] = g * _sigmoid(g)


def _rwkv_prep(zr, shift, mu, w0, w2, a0, a2, k_k, k_a, *, batch, seq_len, tile):
    m, c = zr.shape
    d = D_MODEL
    row = lambda x: x.reshape(1, -1)
    consts = [row(mu), row(w0), w2.astype(BF16), row(a0), a2.astype(BF16), row(k_k), row(k_a)]
    const_specs = [pl.BlockSpec(x.shape, lambda *_: (0, 0)) for x in consts]
    if seq_len >= tile:
        nt = seq_len // tile
        grid = (batch, nt)
        zmap = lambda b, t: (b * nt + t, 0)
        pmap = lambda b, t: (jnp.maximum((b * seq_len + t * tile) // SUBLANES - 1, 0), 0)
        shift_spec = pl.BlockSpec((None, 1, c), lambda b, t: (b, 0, 0))
    else:
        grid = (1, m // tile)
        zmap = lambda b, t: (t, 0)
        pmap = lambda b, t: (0, 0)
        shift_spec = pl.BlockSpec((tile, c), zmap)
    out_spec = pl.BlockSpec((tile, d), zmap)
    kern = functools.partial(_rwkv_prep_kernel, seq_len=seq_len, tile=tile)
    return pl.pallas_call(
        kern, grid=grid,
        in_specs=[pl.BlockSpec((tile, c), zmap), pl.BlockSpec((SUBLANES, c), pmap), shift_spec] + const_specs,
        out_specs=[out_spec] * 7,
        out_shape=[jax.ShapeDtypeStruct((m, d), F32)] * 7,
        compiler_params=_cparams(2), name="rwkv_prep")(zr, zr, shift, *consts)


def _rwkv_scan_kernel(r_ref, w_ref, k_ref, v_ref, kk_ref, a_ref, s0_ref, lg_ref, lb_ref, rk_ref,
                      o_ref, s_ref, vec_ref, *, steps):
    nb = HEAD // SUBLANES

    @pl.when(pl.program_id(1) == 0)
    def _():
        s_ref[...] = s0_ref[...]

    def bcast_row(ref, lead, j):
        return jnp.broadcast_to(ref[lead, pl.ds(j, 1), :], (SUBLANES, LANES))

    def step(t, carry):
        kkraw = kk_ref[t]
        n2 = jnp.sum(kkraw * kkraw, axis=0, keepdims=True)
        kk = kkraw / jnp.maximum(jnp.sqrt(n2), 1e-12)
        vec_ref[0] = -kk
        vec_ref[1] = kk * a_ref[t]
        sa = [jnp.zeros((SUBLANES, LANES), F32) for _ in range(nb)]
        for j in range(HEAD):
            nk = bcast_row(vec_ref, 0, j)
            for ib in range(nb):
                sa[ib] = sa[ib] + s_ref[j, ib * SUBLANES:(ib + 1) * SUBLANES, :] * nk
        vt = [v_ref[t, ib * SUBLANES:(ib + 1) * SUBLANES, :] for ib in range(nb)]
        out = [jnp.zeros((SUBLANES, LANES), F32) for _ in range(nb)]
        for j in range(HEAD):
            wj = bcast_row(w_ref, t, j)
            kaj = bcast_row(vec_ref, 1, j)
            kj = bcast_row(k_ref, t, j)
            rj = bcast_row(r_ref, t, j)
            for ib in range(nb):
                sl = slice(ib * SUBLANES, (ib + 1) * SUBLANES)
                sn = s_ref[j, sl, :] * wj + sa[ib] * kaj + vt[ib] * kj
                s_ref[j, sl, :] = sn
                out[ib] = out[ib] + sn * rj
        o = jnp.concatenate(out, axis=0)
        mean = jnp.mean(o, axis=0, keepdims=True)
        dev = o - mean
        var = jnp.mean(dev * dev, axis=0, keepdims=True)
        y = dev * lax.rsqrt(var + LNX_EPS) * lg_ref[...] + lb_ref[...]
        bonus = jnp.sum(r_ref[t] * k_ref[t] * rk_ref[...], axis=0, keepdims=True) * v_ref[t]
        o_ref[t] = y + bonus
        return carry

    lax.fori_loop(0, steps, step, 0)


def _rwkv_scan(seqs, s0, lg_t, lb_t, rk_t, *, steps):
    t_len, n, p = seqs[0].shape
    grid = (p // LANES, t_len // steps)
    seq_spec = pl.BlockSpec((steps, n, LANES), lambda g, t: (t, 0, g))
    st_spec = pl.BlockSpec((n, n, LANES), lambda g, t: (0, 0, g))
    c_spec = pl.BlockSpec((n, LANES), lambda g, t: (0, g))
    kern = functools.partial(_rwkv_scan_kernel, steps=steps)
    return pl.pallas_call(
        kern, grid=grid,
        in_specs=[seq_spec] * 6 + [st_spec] + [c_spec] * 3,
        out_specs=[seq_spec, st_spec],
        out_shape=[jax.ShapeDtypeStruct((t_len, n, p), F32), jax.ShapeDtypeStruct((n, n, p), F32)],
        scratch_shapes=[pltpu.VMEM((2, n, LANES), F32)],
        compiler_params=_cparams(2), name="rwkv_scan")(*seqs, s0, lg_t, lb_t, rk_t)


def _to_pairs(x, batch, seq_len):
    x = x.reshape(batch, seq_len, RWKV_HEADS, HEAD)
    return jnp.transpose(x, (1, 3, 0, 2)).reshape(seq_len, HEAD, batch * RWKV_HEADS)


def _from_pairs(x, batch, seq_len):
    x = x.reshape(seq_len, HEAD, batch, RWKV_HEADS)
    return jnp.transpose(x, (2, 0, 3, 1)).reshape(batch * seq_len, RWKV_HEADS * HEAD)


def _head_const_pairs(x, batch):
    x = x.reshape(RWKV_HEADS, HEAD).T
    return jnp.tile(x, (1, batch))


def _count_ge(score, thr):
    return jnp.sum(jnp.where(score >= thr, 1.0, 0.0), axis=-1, keepdims=True)


def _select_topk(score, kpos, k_sel, bisect_steps):
    kf = float(k_sel)
    vis = score > _NEG_INF
    nvis = jnp.sum(jnp.where(vis, 1.0, 0.0), axis=-1, keepdims=True)
    need_sel = nvis > kf
    rowmax = jnp.where(need_sel, jnp.max(score, axis=-1, keepdims=True), 0.0)
    rowmin = jnp.where(need_sel, jnp.min(jnp.where(vis, score, _POS_INF), axis=-1, keepdims=True), 0.0)

    def bis(_, c):
        lo, hi, chi = c
        hfin = jnp.where(hi == _POS_INF, rowmax, hi)
        piv = 0.5 * lo + 0.5 * hfin
        cnt = _count_ge(score, piv)
        ge = cnt >= kf
        return jnp.where(ge, piv, lo), jnp.where(ge, hi, piv), jnp.where(ge, chi, cnt)

    lo, hi, chi = lax.fori_loop(0, bisect_steps, bis,
                                (rowmin, jnp.full_like(rowmin, _POS_INF), jnp.zeros_like(rowmin)))

    def walk_cond(c):
        return jnp.min(c[4]) < 0.5

    def walk(c):
        hi, chi, tau, cgt, done, ceq = c
        pending = done < 0.5
        bmax = jnp.max(jnp.where(score < hi, score, _NEG_INF), axis=-1, keepdims=True)
        cnt = _count_ge(score, bmax)
        fin = jnp.logical_and(cnt >= kf, pending)
        tau = jnp.where(fin, bmax, tau)
        cgt = jnp.where(fin, chi, cgt)
        ceq = jnp.where(fin, cnt - chi, ceq)
        adv = jnp.logical_and(cnt < kf, pending)
        hi = jnp.where(adv, bmax, hi)
        chi = jnp.where(adv, cnt, chi)
        return hi, chi, tau, cgt, jnp.where(fin, 1.0, done), ceq

    done0 = jnp.where(need_sel, 0.0, 1.0)
    neg = jnp.full_like(rowmin, _NEG_INF)
    zero = jnp.zeros_like(rowmin)
    _, _, tau, cgt, _, ceq = lax.while_loop(walk_cond, walk, (hi, chi, neg, zero, done0, zero))

    need = kf - cgt
    excess = jnp.logical_and(need_sel, ceq > need)
    eq = score == tau
    n_keys = score.shape[-1]

    def tie_break(_):
        def body(_, c):
            plo, phi = c
            mid = lax.shift_right_arithmetic(plo + phi, 1)
            cnt = jnp.sum(jnp.where(jnp.logical_and(eq, kpos <= mid), 1.0, 0.0), axis=-1, keepdims=True)
            ge = cnt >= need
            return jnp.where(ge, plo, mid), jnp.where(ge, mid, phi)
        plo0 = jnp.full(tau.shape, -1, jnp.int32)
        phi0 = jnp.full(tau.shape, n_keys - 1, jnp.int32)
        _, phi = lax.fori_loop(0, n_keys.bit_length() + 1, body, (plo0, phi0))
        return jnp.where(excess, phi, n_keys)

    any_excess = jnp.max(jnp.where(excess, 1.0, 0.0)) > 0.0
    pcut = lax.cond(any_excess, tie_break, lambda _: jnp.full(tau.shape, n_keys, jnp.int32), 0)
    pcut = jnp.where(need_sel, pcut, -1)
    return jnp.logical_or(score > tau, jnp.logical_and(eq, kpos <= pcut))


_MASKED = -1e30


def _select_topk_chunked(s_ref, nch, ck, k_sel, nvis, bisect_steps):
    rows = s_ref.shape[0]
    kf = float(k_sel)
    nfold = ck // LANES

    def chunk(c):
        return s_ref[:, pl.ds(pl.multiple_of(c * ck, ck), ck)]

    def fold(x, op):
        out = x[:, 0:LANES]
        for i in range(1, nfold):
            out = op(out, x[:, i * LANES:(i + 1) * LANES])
        return out

    def count(pred):
        def body(c, acc):
            return acc + fold(jnp.where(pred(chunk(c), c), 1.0, 0.0), jnp.add)
        acc = lax.fori_loop(0, nch, body, jnp.zeros((rows, LANES), F32))
        return jnp.sum(acc, axis=-1, keepdims=True)

    def row_max(val):
        def body(c, acc):
            return jnp.maximum(acc, fold(val(chunk(c)), jnp.maximum))
        acc = lax.fori_loop(0, nch, body, jnp.full((rows, LANES), _NEG_INF, F32))
        return jnp.max(acc, axis=-1, keepdims=True)

    need_sel = nvis > kf
    rowmax = jnp.where(need_sel, row_max(lambda x: x), 0.0)
    rowmin = jnp.where(need_sel, -row_max(lambda x: jnp.where(x > _NEG_INF, -x, _NEG_INF)), 0.0)

    def bis(_, c):
        lo, hi, chi = c
        hfin = jnp.where(hi == _POS_INF, rowmax, hi)
        piv = 0.5 * lo + 0.5 * hfin
        cnt = count(lambda x, _c: x >= piv)
        ge = cnt >= kf
        return jnp.where(ge, piv, lo), jnp.where(ge, hi, piv), jnp.where(ge, chi, cnt)

    lo, hi, chi = lax.fori_loop(0, bisect_steps, bis,
                                (rowmin, jnp.full_like(rowmin, _POS_INF), jnp.zeros_like(rowmin)))

    def walk_cond(c):
        return jnp.min(c[4]) < 0.5

    def walk(c):
        hi, chi, tau, cgt, done, ceq = c
        pending = done < 0.5
        bmax = row_max(lambda x: jnp.where(x < hi, x, _NEG_INF))
        cnt = count(lambda x, _c: x >= bmax)
        fin = jnp.logical_and(cnt >= kf, pending)
        tau = jnp.where(fin, bmax, tau)
        cgt = jnp.where(fin, chi, cgt)
        ceq = jnp.where(fin, cnt - chi, ceq)
        adv = jnp.logical_and(cnt < kf, pending)
        hi = jnp.where(adv, bmax, hi)
        chi = jnp.where(adv, cnt, chi)
        return hi, chi, tau, cgt, jnp.where(fin, 1.0, done), ceq

    neg = jnp.full_like(rowmin, _NEG_INF)
    zero = jnp.zeros_like(rowmin)
    _, _, tau, cgt, _, ceq = lax.while_loop(
        walk_cond, walk, (hi, chi, neg, zero, jnp.where(need_sel, 0.0, 1.0), zero))

    need = kf - cgt
    excess = jnp.logical_and(need_sel, ceq > need)
    n_keys = s_ref.shape[1]

    def kpos(c):
        return c * ck + lax.broadcasted_iota(jnp.int32, (rows, ck), 1)

    def tie_break(_):
        def body(_, c):
            plo, phi = c
            mid = lax.shift_right_arithmetic(plo + phi, 1)
            cnt = count(lambda x, cc: jnp.logical_and(x == tau, kpos(cc) <= mid))
            ge = cnt >= need
            return jnp.where(ge, plo, mid), jnp.where(ge, mid, phi)
        plo0 = jnp.full(tau.shape, -1, jnp.int32)
        phi0 = jnp.full(tau.shape, n_keys - 1, jnp.int32)
        _, phi = lax.fori_loop(0, n_keys.bit_length() + 1, body, (plo0, phi0))
        return jnp.where(excess, phi, n_keys)

    any_excess = jnp.max(jnp.where(excess, 1.0, 0.0)) > 0.0
    pcut = lax.cond(any_excess, tie_break, lambda _: jnp.full(tau.shape, n_keys, jnp.int32), 0)
    pcut = jnp.where(need_sel, pcut, -1)

    def write(c, carry):
        x = chunk(c)
        sel = jnp.logical_or(x > tau, jnp.logical_and(x == tau, kpos(c) <= pcut))
        s_ref[:, pl.ds(pl.multiple_of(c * ck, ck), ck)] = jnp.where(sel, 0.0, _MASKED)
        return carry

    lax.fori_loop(0, nch, write, 0)


def _attn_prep_kernel(q_ref, qi_ref, kv_ref, kiw_ref, qg_ref, kg_ref, kn_o, qb_o, qib_o, kvb_o, kib_o):
    w = ATT_KV_HEADS * HEAD
    for h in range(ATT_KV_HEADS):
        hs = slice(h * HEAD, (h + 1) * HEAD)
        x = kv_ref[:, hs]
        kn = x * lax.rsqrt(jnp.mean(x * x, axis=-1, keepdims=True) + NORM_EPS) * kg_ref[...]
        kn_o[:, hs] = kn
        kvb_o[:, hs] = kn.astype(BF16)
        v = kv_ref[:, w + h * HEAD:w + (h + 1) * HEAD].astype(BF16)
        kvb_o[:, w + h * LANES:w + (h + 1) * LANES] = jnp.concatenate([v, jnp.ones_like(v)], axis=1)
    for h in range(ATT_HEADS):
        x = q_ref[:, h * HEAD:(h + 1) * HEAD]
        qn = x * lax.rsqrt(jnp.mean(x * x, axis=-1, keepdims=True) + NORM_EPS) * (qg_ref[...] * HEAD ** -0.5)
        qb_o[h] = qn.astype(BF16)
    qib_o[...] = (qi_ref[...] * HEAD ** -0.5).astype(BF16)
    kib_o[...] = kiw_ref[:, 0:HEAD].astype(BF16)


def _attn_prep(za, qn_g, kn_g, tm):
    m = za.shape[0]
    w = ATT_KV_HEADS * HEAD
    wi = IDX_HEADS * HEAD
    row = lambda width, blk: pl.BlockSpec((tm, width), lambda i, blk=blk: (i, blk))
    gspec = pl.BlockSpec((1, HEAD), lambda i: (0, 0))
    return pl.pallas_call(
        _attn_prep_kernel, grid=(m // tm,),
        in_specs=[row(D_MODEL, ZA_Q // D_MODEL), row(wi, ZA_QI // wi), row(2 * w, ZA_KV // (2 * w)),
                  row(LANES, ZA_KIW // LANES), gspec, gspec],
        out_specs=[row(w, 0), pl.BlockSpec((ATT_HEADS, tm, HEAD), lambda i: (0, i, 0)), row(wi, 0),
                   row(w + ATT_KV_HEADS * LANES, 0), row(HEAD, 0)],
        out_shape=[jax.ShapeDtypeStruct((m, w), F32), jax.ShapeDtypeStruct((ATT_HEADS, m, HEAD), BF16),
                   jax.ShapeDtypeStruct((m, wi), BF16), jax.ShapeDtypeStruct((m, w + ATT_KV_HEADS * LANES), BF16),
                   jax.ShapeDtypeStruct((m, HEAD), BF16)],
        compiler_params=_cparams(1), name="attn_prep")(za, za, za, za, qn_g.reshape(1, HEAD), kn_g.reshape(1, HEAD))


def _bucket_edges():
    max_exact = REL_BUCKETS // 2
    d = np.arange(REL_MAX_DIST + 1)
    df = np.maximum(d, 1).astype(np.float32)
    large = max_exact + (np.log(df / max_exact) / math.log(REL_MAX_DIST / max_exact)
                         * (REL_BUCKETS - max_exact)).astype(np.int32)
    bucket = np.where(d < max_exact, d, np.minimum(large, REL_BUCKETS - 1))
    return [int(np.argmax(bucket >= b)) for b in range(REL_BUCKETS)]


_BUCKET_EDGES = _bucket_edges()


def _rel_bias_lookup(dist, value_of_bucket):
    bias = value_of_bucket(REL_BUCKETS - 1)
    for b in range(REL_BUCKETS - 2, -1, -1):
        bias = jnp.where(dist < _BUCKET_EDGES[b + 1], value_of_bucket(b), bias)
    return bias


def _attn_prompt_kernel(rb_ref, q_ref, qi_ref, kiwq_ref, g_ref, kvb_ref, kib_ref, o_ref, s_scr, tbd_ref,
                        l_scr, mx_scr, acc_scr, *, tq, k_sel, bisect_steps):
    qt = pl.program_id(0)
    q0 = qt * tq
    ck = tq
    rg_rows = LANES
    n_rg = tq // rg_rows
    w = ATT_KV_HEADS * HEAD

    @pl.when(jnp.logical_and(pl.program_id(1) == 0, qt == 0))
    def _():
        rr = lax.broadcasted_iota(jnp.int32, (tq, tq), 0) - lax.broadcasted_iota(jnp.int32, (tq, tq), 1)
        for h in range(ATT_HEADS):
            far = rb_ref[REL_BUCKETS - 1, h]
            tbd_ref[h, 0] = _rel_bias_lookup(rr, lambda b: rb_ref[b, h]) - far
            tbd_ref[h, 1] = _rel_bias_lookup(rr + tq, lambda b: rb_ref[b, h]) - far

    def chunk_start(c):
        return pl.multiple_of(c * ck, ck)

    wcol = kiwq_ref[:, HEAD:HEAD + IDX_HEADS] * (IDX_HEADS ** -0.5)
    qi_h = [[qi_ref[rg * rg_rows:(rg + 1) * rg_rows, h * HEAD:(h + 1) * HEAD] for h in range(IDX_HEADS)]
            for rg in range(n_rg)]
    w_h = [[wcol[rg * rg_rows:(rg + 1) * rg_rows, h:h + 1] for h in range(IDX_HEADS)] for rg in range(n_rg)]

    def score_chunk(c, carry):
        k0 = chunk_start(c)
        kc = kib_ref[pl.ds(k0, ck), :]
        kpos = k0 + lax.broadcasted_iota(jnp.int32, (rg_rows, ck), 1)
        for rg in range(n_rg):
            acc = jnp.zeros((rg_rows, ck), F32)
            for h in range(IDX_HEADS):
                acc = acc + jnp.maximum(_dot_nt(qi_h[rg][h], kc), 0.0) * w_h[rg][h]
            qpos = q0 + rg * rg_rows + lax.broadcasted_iota(jnp.int32, (rg_rows, ck), 0)
            s_scr[rg * rg_rows:(rg + 1) * rg_rows, pl.ds(k0, ck)] = jnp.where(kpos <= qpos, acc, _NEG_INF)
        return carry

    nch = qt + 1
    lax.fori_loop(0, nch, score_chunk, 0)

    nvis = (q0 + 1 + lax.broadcasted_iota(jnp.int32, (tq, 1), 0)).astype(F32)
    _select_topk_chunked(s_scr, nch, ck, k_sel, nvis, bisect_steps)

    g_rows = ATT_GROUP * tq
    n_far = jnp.maximum(qt - 1, 0)
    for kvh in range(ATT_KV_HEADS):
        gs = slice(kvh * ATT_GROUP, (kvh + 1) * ATT_GROUP)
        qg = q_ref[gs, :, :].reshape(g_rows, HEAD)
        mx_scr[...] = jnp.full(mx_scr.shape, _MASKED, F32)
        acc_scr[...] = jnp.zeros(acc_scr.shape, F32)

        def logits_chunk(c, carry, near, qg=qg, gs=gs, kvh=kvh):
            k0 = chunk_start(c)
            kc = kvb_ref[pl.ds(k0, ck), kvh * HEAD:(kvh + 1) * HEAD]
            s3 = _dot_nt(qg, kc).reshape(ATT_GROUP, tq, ck) + s_scr[:, pl.ds(k0, ck)][None]
            if near:
                s3 = s3 + tbd_ref[gs, qt - c, :, :]
            s = s3.reshape(g_rows, ck)
            l_scr[:, pl.ds(k0, ck)] = s
            m = mx_scr[...]
            for i in range(ck // LANES):
                m = jnp.maximum(m, s[:, i * LANES:(i + 1) * LANES])
            mx_scr[...] = m
            return carry

        lax.fori_loop(0, n_far, functools.partial(logits_chunk, near=False), 0)
        lax.fori_loop(n_far, nch, functools.partial(logits_chunk, near=True), 0)
        mx_scr[...] = jnp.broadcast_to(jnp.max(mx_scr[...], axis=-1, keepdims=True), mx_scr.shape)

        def pv_chunk(c, carry, kvh=kvh):
            k0 = chunk_start(c)
            vx = kvb_ref[pl.ds(k0, ck), w + kvh * LANES:w + (kvh + 1) * LANES]
            m = mx_scr[...]
            p = jnp.concatenate([jnp.exp(l_scr[:, pl.ds(k0 + i * LANES, LANES)] - m)
                                 for i in range(ck // LANES)], axis=1).astype(BF16)
            acc_scr[...] += jnp.dot(p, vx, preferred_element_type=F32)
            return carry

        lax.fori_loop(0, nch, pv_chunk, 0)

        acc = acc_scr[...]
        o = acc[:, 0:HEAD] / acc[:, HEAD:HEAD + 1]
        for pp in range(ATT_GROUP // 2):
            cols = slice((kvh * ATT_GROUP // 2 + pp) * LANES, (kvh * ATT_GROUP // 2 + pp + 1) * LANES)
            gh = g_ref[:, cols]
            pair = jnp.concatenate([o[2 * pp * tq:(2 * pp + 1) * tq], o[(2 * pp + 1) * tq:(2 * pp + 2) * tq]], axis=1)
            o_ref[:, cols] = pair * (gh * _sigmoid(gh))


def _attn_prompt(za, kn, qb, qib, kvb, kib, *, rel_bias, batch, seq_len, tq, k_sel):
    m = za.shape[0]
    nq = seq_len // tq
    w = ATT_KV_HEADS * HEAD
    wi = IDX_HEADS * HEAD
    assert REL_MAX_DIST <= tq and tq % LANES == 0
    row_map = lambda blk: (lambda t, b, blk=blk: (b * nq + t, blk))
    key_map = lambda t, b: (b, 0)
    kern = functools.partial(_attn_prompt_kernel, tq=tq, k_sel=k_sel, bisect_steps=14)
    return pl.pallas_call(
        kern, grid=(nq, batch),
        in_specs=[
            pl.BlockSpec(memory_space=pltpu.SMEM),
            pl.BlockSpec((ATT_HEADS, tq, HEAD), lambda t, b: (0, b * nq + t, 0)),
            pl.BlockSpec((tq, wi), row_map(0)),
            pl.BlockSpec((tq, LANES), row_map(ZA_KIW // LANES)),
            pl.BlockSpec((tq, D_MODEL), row_map(ZA_GATT // D_MODEL)),
            pl.BlockSpec((seq_len, w + ATT_KV_HEADS * LANES), key_map),
            pl.BlockSpec((seq_len, HEAD), key_map),
        ],
        out_specs=pl.BlockSpec((tq, D_MODEL), row_map(0)),
        out_shape=jax.ShapeDtypeStruct((m, D_MODEL), F32),
        scratch_shapes=[pltpu.VMEM((tq, seq_len), F32), pltpu.VMEM((ATT_HEADS, 2, tq, tq), F32),
                        pltpu.VMEM((ATT_GROUP * tq, seq_len), F32), pltpu.VMEM((ATT_GROUP * tq, LANES), F32),
                        pltpu.VMEM((ATT_GROUP * tq, LANES), F32)],
        compiler_params=_cparams(2), name="attn_prompt")(rel_bias, qb, qib, za, za, kvb, kib)


def _sample_score_kernel(pt_ref, q_ref, w_ref, kiw_new_ref, *rest, n_pages, page, dec_seq):
    page_refs = rest[:n_pages]
    o_ref = rest[n_pages]
    past = n_pages * page
    kidx = jnp.concatenate([r[...] for r in page_refs], axis=0).astype(BF16)
    new = kiw_new_ref[:, 0:HEAD]
    new = jnp.concatenate([new, jnp.zeros((LANES - dec_seq, HEAD), F32)], axis=0).astype(BF16)
    q = (q_ref[...] * (HEAD ** -0.5)).astype(BF16)
    lg = jnp.concatenate([_dot_nt(q, kidx), _dot_nt(q, new)], axis=1)
    wr = jnp.maximum(lg, 0.0) * (w_ref[...] * (IDX_HEADS ** -0.5))
    n_keys = past + LANES
    sc = jnp.sum(wr.reshape(dec_seq, IDX_HEADS, n_keys), axis=1)
    kpos = lax.broadcasted_iota(jnp.int32, (dec_seq, n_keys), 1)
    tpos = lax.broadcasted_iota(jnp.int32, (dec_seq, n_keys), 0)
    o_ref[...] = jnp.where(kpos <= past + tpos, sc, _NEG_INF)


def _sample_select_kernel(s_ref, o_ref, *, k_sel, bisect_steps):
    score = s_ref[...]
    kpos = lax.broadcasted_iota(jnp.int32, score.shape, 1)
    sel = _select_topk(score, kpos, k_sel, bisect_steps)
    o_ref[...] = jnp.where(sel, 0.0, _NEG_INF)


def _sample_attn_kernel(pt_ref, q_ref, g_ref, kn_new_ref, kv_new_ref, mask_ref, rb_rows_ref, qg_ref, *rest,
                        n_pages, page, dec_seq):
    k_pages = rest[:n_pages]
    v_pages = rest[n_pages:2 * n_pages]
    o_ref = rest[2 * n_pages]
    bias_ref = rest[2 * n_pages + 1]
    w = ATT_KV_HEADS * HEAD
    rows = dec_seq * ATT_HEADS
    n_keys = n_pages * page + LANES

    @pl.when(pl.program_id(0) == 0)
    def _():
        t_row = lax.div(lax.broadcasted_iota(jnp.int32, (rows, n_keys), 0), ATT_HEADS)
        dist = n_pages * page + t_row - lax.broadcasted_iota(jnp.int32, (rows, n_keys), 1)
        bias_ref[...] = _rel_bias_lookup(dist, lambda b: rb_rows_ref[:, b:b + 1])

    pad = jnp.zeros((LANES - dec_seq, w), F32)
    k_all = jnp.concatenate([r[...] for r in k_pages] + [kn_new_ref[...], pad], axis=0).astype(BF16)
    v_all = jnp.concatenate([r[...] for r in v_pages] + [kv_new_ref[:, w:2 * w], pad], axis=0).astype(BF16)
    q = q_ref[...]
    q = q * lax.rsqrt(jnp.mean(q * q, axis=-1, keepdims=True) + NORM_EPS) * qg_ref[...]
    qb = q.astype(BF16)
    head = lax.rem(lax.broadcasted_iota(jnp.int32, (rows, 1), 0), ATT_HEADS)
    first = head < ATT_GROUP
    lg = jnp.where(first, _dot_nt(qb, k_all[:, 0:HEAD]), _dot_nt(qb, k_all[:, HEAD:2 * HEAD]))
    mask = jnp.broadcast_to(mask_ref[...][:, None, :], (dec_seq, ATT_HEADS, n_keys)).reshape(rows, n_keys)
    s = lg * (HEAD ** -0.5) + bias_ref[...] + mask
    p = jnp.exp(s - jnp.max(s, axis=-1, keepdims=True))
    l = jnp.sum(p, axis=-1, keepdims=True)
    pb = p.astype(BF16)
    o = jnp.where(first, jnp.dot(pb, v_all[:, 0:HEAD], preferred_element_type=F32),
                  jnp.dot(pb, v_all[:, HEAD:2 * HEAD], preferred_element_type=F32)) / l
    g = g_ref[...]
    o_ref[...] = o * (g * _sigmoid(g))


def _attn_sample(za, kn, qb, qib, kvb, kib, *, cache_k, cache_v, cache_kidx, page_table, rel_bias, qn_g,
                 dec_seq, k_sel):
    nb, n_pages = page_table.shape
    n_phys, page = cache_k.shape[0], cache_k.shape[1]
    past = n_pages * page
    n_keys = past + LANES
    w = ATT_KV_HEADS * HEAD
    ck = cache_k.reshape(n_phys, page, w)
    cv = cache_v.reshape(n_phys, page, w)
    ci = cache_kidx.reshape(n_phys, page, HEAD)
    za3 = za.reshape(nb, dec_seq, ZA_COLS)
    kn3 = kn.reshape(nb, dec_seq, w)
    qi = za[:, ZA_QI:ZA_QI + IDX_HEADS * HEAD].reshape(nb, dec_seq * IDX_HEADS, HEAD)
    wi = za[:, ZA_KIW + HEAD:ZA_KIW + HEAD + IDX_HEADS].reshape(nb, dec_seq * IDX_HEADS, 1)
    qa = za[:, ZA_Q:ZA_Q + D_MODEL].reshape(nb, dec_seq * ATT_HEADS, HEAD)
    ga = za[:, ZA_GATT:ZA_GATT + D_MODEL].reshape(nb, dec_seq * ATT_HEADS, HEAD)

    def page_specs(width):
        return [pl.BlockSpec((None, page, width), lambda b, pt, j=j: (pt[b, j], 0, 0)) for j in range(n_pages)]

    per_b = lambda r, c: pl.BlockSpec((None, r, c), lambda b, pt: (b, 0, 0))
    kiw_new = pl.BlockSpec((None, dec_seq, LANES), lambda b, pt: (b, 0, ZA_KIW // LANES))

    scores = pl.pallas_call(
        functools.partial(_sample_score_kernel, n_pages=n_pages, page=page, dec_seq=dec_seq),
        grid_spec=pltpu.PrefetchScalarGridSpec(
            num_scalar_prefetch=1, grid=(nb,),
            in_specs=[per_b(dec_seq * IDX_HEADS, HEAD), per_b(dec_seq * IDX_HEADS, 1), kiw_new] + page_specs(HEAD),
            out_specs=per_b(dec_seq, n_keys)),
        out_shape=jax.ShapeDtypeStruct((nb, dec_seq, n_keys), F32),
        compiler_params=_cparams(1), name="sample_scores")(page_table, qi, wi, za3, *([ci] * n_pages))

    rows = nb * dec_seq
    tr = min(rows, 128)
    mask = pl.pallas_call(
        functools.partial(_sample_select_kernel, k_sel=k_sel, bisect_steps=14),
        grid=(rows // tr,),
        in_specs=[pl.BlockSpec((tr, n_keys), lambda i: (i, 0))],
        out_specs=pl.BlockSpec((tr, n_keys), lambda i: (i, 0)),
        out_shape=jax.ShapeDtypeStruct((rows, n_keys), F32),
        compiler_params=_cparams(1), name="sample_select")(scores.reshape(rows, n_keys))

    rb_rows = jnp.tile(rel_bias.T, (dec_seq, 1))

    const = lambda shape: pl.BlockSpec(shape, lambda b, pt: (0,) * len(shape))
    kv_new = pl.BlockSpec((None, dec_seq, 2 * w), lambda b, pt: (b, 0, ZA_KV // (2 * w)))
    out = pl.pallas_call(
        functools.partial(_sample_attn_kernel, n_pages=n_pages, page=page, dec_seq=dec_seq),
        grid_spec=pltpu.PrefetchScalarGridSpec(
            num_scalar_prefetch=1, grid=(nb,),
            in_specs=[per_b(dec_seq * ATT_HEADS, HEAD), per_b(dec_seq * ATT_HEADS, HEAD), per_b(dec_seq, w),
                      kv_new, per_b(dec_seq, n_keys), const((dec_seq * ATT_HEADS, REL_BUCKETS)), const((1, HEAD))]
            + page_specs(w) + page_specs(w),
            out_specs=per_b(dec_seq * ATT_HEADS, HEAD),
            scratch_shapes=[pltpu.VMEM((dec_seq * ATT_HEADS, n_keys), F32)]),
        out_shape=jax.ShapeDtypeStruct((nb, dec_seq * ATT_HEADS, HEAD), F32),
        compiler_params=_cparams(1), name="sample_attn")(
            page_table, qa, ga, kn3, za3, mask.reshape(nb, dec_seq, n_keys), rb_rows, qn_g.reshape(1, HEAD),
            *([ck] * n_pages), *([cv] * n_pages))
    return out.reshape(rows, D_MODEL)


def _merge_kernel(x_ref, oa_ref, sg_ref, ob_ref, ga_ref, gb_ref, wpa_ref, wpb_ref, wo_ref, y_ref):
    oa = (oa_ref[...] * sg_ref[...]).astype(BF16)
    pa = jnp.dot(oa, wpa_ref[...], preferred_element_type=F32)
    pb = jnp.dot(ob_ref[...].astype(BF16), wpb_ref[...], preferred_element_type=F32)
    merged = _sigmoid(ga_ref[...]) * pa + _sigmoid(gb_ref[...]) * pb
    y_ref[...] = x_ref[...] + jnp.dot(merged.astype(BF16), wo_ref[...], preferred_element_type=F32)


def _merge(x, oa, sg, ob, za, w_pa, w_pb, w_out, tm):
    m, d = x.shape
    row = pl.BlockSpec((tm, d), lambda i: (i, 0))
    wsp = pl.BlockSpec((d, d), lambda i: (0, 0))
    return pl.pallas_call(
        _merge_kernel, grid=(m // tm,),
        in_specs=[row, row, row, row,
                  pl.BlockSpec((tm, d), lambda i: (i, ZA_GA // d)), pl.BlockSpec((tm, d), lambda i: (i, ZA_GB // d)),
                  wsp, wsp, wsp],
        out_specs=row,
        out_shape=jax.ShapeDtypeStruct((m, d), F32),
        compiler_params=_cparams(1), name="merge")(x, oa, sg, ob, za, za, w_pa, w_pb, w_out)


def _layer(x, shift, s0, params, *, batch, seq_len, attend):
    (norm_g, w_r, w_a, mu, w0, w2, a0, a2, k_k, k_a, r_k, lnx_g, lnx_b, qn_g, kn_g, w_pa, w_pb, w_out) = params
    m = batch * seq_len
    tm = min(m, 512)
    xn = _rmsnorm(x, norm_g, tm)
    zr = _matmul(xn, w_r, tm, RWKV_COLS // 3, "inproj_rwkv")
    za = _matmul(xn, w_a, tm, ZA_COLS // 4, "inproj_attn")

    tile = min(m, 256)
    r, w, k, v, kk, a, sg = _rwkv_prep(zr, shift, mu, w0, w2, a0, a2, k_k, k_a,
                                       batch=batch, seq_len=seq_len, tile=tile)
    seqs = [_to_pairs(t, batch, seq_len) for t in (r, w, k, v, kk, a)]
    consts = [_head_const_pairs(t, batch) for t in (lnx_g, lnx_b, r_k.reshape(-1))]
    o_t, s_t = _rwkv_scan(seqs, s0, *consts, steps=min(seq_len, 32))
    oa = _from_pairs(o_t, batch, seq_len)

    kn, qb, qib, kvb, kib = _attn_prep(za, qn_g, kn_g, tm)
    ob = attend(za, kn, qb, qib, kvb, kib)

    y = _merge(x, oa, sg, ob, za, w_pa, w_pb, w_out, min(m, 256))
    return y, zr, za, kn, s_t


def _state_to_pairs(s):
    b, h, n, _ = s.shape
    return jnp.transpose(s, (3, 2, 0, 1)).reshape(n, n, b * h)


def _state_from_pairs(s, batch):
    n = s.shape[0]
    return jnp.transpose(s.reshape(n, n, batch, RWKV_HEADS), (2, 3, 1, 0))


def kernel(x_prompt, x_sample, cache_k, cache_v, cache_kidx, state_wkv, state_shift, page_table, norm_g, w_in,
           shift_mu, w0, w2, a0, a2, k_k, k_a, r_k, lnx_g, lnx_b, q_norm_g, k_norm_g, rel_bias, w_pa, w_pb, w_out):
    bsz, seq, d = x_prompt.shape
    dec_bsz, dec_seq, _ = x_sample.shape
    depth = w_in.shape[0]
    assert depth == 1 and d == D_MODEL
    past_len = page_table.shape[1] * cache_k.shape[2]
    topk_p = min(TOPK_MAX, seq // 4)
    topk_s = min(TOPK_MAX, (past_len + dec_seq) // 4)
    l = 0

    wl = w_in[l]
    c0 = RWKV_COLS
    q_w, kv_w, qi_w = wl[:, c0:c0 + 1024], wl[:, c0 + 1024:c0 + 1280], wl[:, c0 + 1280:c0 + 1792]
    kiw_w = wl[:, c0 + 1792:c0 + 1864]
    rest_w = wl[:, c0 + 1864:]
    zpad = lambda n: jnp.zeros((d, n), wl.dtype)
    w_a = jnp.concatenate([q_w, qi_w, kv_w, kiw_w, zpad(LANES - kiw_w.shape[1]), zpad(LANES), rest_w],
                          axis=1).astype(BF16)
    assert w_a.shape[1] == ZA_COLS
    w_r = wl[:, :c0].astype(BF16)
    params = (norm_g[l], w_r, w_a, shift_mu[l], w0[l], w2[l], a0[l], a2[l], k_k[l], k_a[l], r_k[l],
              lnx_g[l], lnx_b[l], q_norm_g[l], k_norm_g[l],
              w_pa[l].astype(BF16), w_pb[l].astype(BF16), w_out[l].astype(BF16))

    xp = x_prompt.reshape(bsz * seq, d)
    attend_p = functools.partial(_attn_prompt, rel_bias=rel_bias, batch=bsz, seq_len=seq,
                                 tq=min(seq, 256), k_sel=topk_p)
    yp, zr_p, za_p, kn_p, st_p = _layer(
        xp, jnp.zeros((bsz, 1, RWKV_COLS), F32), jnp.zeros((HEAD, HEAD, bsz * RWKV_HEADS), F32), params,
        batch=bsz, seq_len=seq, attend=attend_p)

    xs = x_sample.reshape(dec_bsz * dec_seq, d)
    attend_s = functools.partial(_attn_sample, cache_k=cache_k[l], cache_v=cache_v[l], cache_kidx=cache_kidx[l],
                                 page_table=page_table, rel_bias=rel_bias, qn_g=q_norm_g[l], dec_seq=dec_seq,
                                 k_sel=topk_s)
    shift_rows = jnp.repeat(state_shift[l], dec_seq, axis=0)
    ys, zr_s, za_s, kn_s, st_s = _layer(
        xs, shift_rows, _state_to_pairs(state_wkv[l]), params, batch=dec_bsz, seq_len=dec_seq, attend=attend_s)

    w = ATT_KV_HEADS * HEAD

    def pack(y, zr, za, kn, st, b, t):
        v = za[:, ZA_KV + w:ZA_KV + 2 * w]
        kidx = za[:, ZA_KIW:ZA_KIW + HEAD]
        return (y.reshape(b, t, d),
                kn.reshape(1, b, t, ATT_KV_HEADS, HEAD), v.reshape(1, b, t, ATT_KV_HEADS, HEAD),
                kidx.reshape(1, b, t, HEAD), _state_from_pairs(st, b)[None],
                zr.reshape(b, t, RWKV_COLS)[:, -1][None])

    p = pack(yp, zr_p, za_p, kn_p, st_p, bsz, seq)
    s = pack(ys, zr_s, za_s, kn_s, st_s, dec_bsz, dec_seq)
    return (p[0], s[0]) + p[1:] + s[1:]
```

```python
import functools
import math

import numpy as np
import jax
import jax.numpy as jnp
from jax import lax
from jax.experimental import pallas as pl
from jax.experimental.pallas import tpu as pltpu

F32 = jnp.float32
BF16 = jnp.bfloat16

D_MODEL = 1024
HEAD = 64
RWKV_HEADS = D_MODEL // HEAD
LORA = 64
LNX_EPS = 64e-5
ATT_HEADS = D_MODEL // HEAD
ATT_KV_HEADS = 2
ATT_GROUP = ATT_HEADS // ATT_KV_HEADS
IDX_HEADS = 8
TOPK_MAX = 256
REL_BUCKETS = 32
REL_MAX_DIST = 128
NORM_EPS = 1e-6
RWKV_COLS = 4 * D_MODEL + 2 * LORA

LANES = 128
SUBLANES = 8
VMEM_LIMIT_BYTES = 56 * 1024 * 1024

ZA_Q = 0
ZA_QI = 1024
ZA_KV = 1536
ZA_KIW = 1792
ZA_GATT = 2048
ZA_GA = 3072
ZA_GB = 4096
ZA_COLS = 5120

_NEG_INF = float("-inf")
_POS_INF = float("inf")


def _cparams(n_axes):
    return pltpu.CompilerParams(dimension_semantics=("arbitrary",) * n_axes,
                                vmem_limit_bytes=VMEM_LIMIT_BYTES)


def _sigmoid(x):
    return 1.0 / (1.0 + jnp.exp(-x))


def _dot_nt(a, b):
    return lax.dot_general(a, b, (((1,), (1,)), ((), ())), preferred_element_type=F32)


def _rmsnorm_kernel(x_ref, g_ref, o_ref):
    x = x_ref[...]
    ms = jnp.mean(x * x, axis=-1, keepdims=True)
    o_ref[...] = (x * lax.rsqrt(ms + NORM_EPS) * g_ref[...]).astype(o_ref.dtype)


def _rmsnorm(x, g, tm):
    m, d = x.shape
    return pl.pallas_call(
        _rmsnorm_kernel, grid=(m // tm,),
        in_specs=[pl.BlockSpec((tm, d), lambda i: (i, 0)), pl.BlockSpec((1, d), lambda i: (0, 0))],
        out_specs=pl.BlockSpec((tm, d), lambda i: (i, 0)),
        out_shape=jax.ShapeDtypeStruct((m, d), BF16),
        compiler_params=_cparams(1), name="rmsnorm")(x, g.reshape(1, d))


def _mm_kernel(x_ref, w_ref, o_ref):
    o_ref[...] = jnp.dot(x_ref[...], w_ref[...], preferred_element_type=F32)


def _matmul(x, w, tm, tn, name):
    m, k = x.shape
    n = w.shape[1]
    return pl.pallas_call(
        _mm_kernel, grid=(m // tm, n // tn),
        in_specs=[pl.BlockSpec((tm, k), lambda i, j: (i, 0)), pl.BlockSpec((k, tn), lambda i, j: (0, j))],
        out_specs=pl.BlockSpec((tm, tn), lambda i, j: (i, j)),
        out_shape=jax.ShapeDtypeStruct((m, n), F32),
        compiler_params=_cparams(2), name=name)(x, w)


def _rwkv_prep_kernel(z_ref, prev_ref, shift_ref, mu_ref, w0_ref, w2_ref, a0_ref, a2_ref,
                      r_o, w_o, k_o, v_o, a_o, g_o, *, seq_len, tile):
    z = z_ref[...]
    rows = lax.broadcasted_iota(jnp.int32, z.shape, 0)
    rolled = pltpu.roll(z, 1, 0)
    if seq_len >= tile:
        first = jnp.where(pl.program_id(1) == 0, shift_ref[...], prev_ref[SUBLANES - 1:SUBLANES, :])
        prev = jnp.where(rows == 0, first, rolled)
    else:
        prev = jnp.where(lax.rem(rows, seq_len) == 0, shift_ref[...], rolled)
    zs = z + (prev - z) * mu_ref[...]
    d = D_MODEL
    r = zs[:, 0:d]
    k = zs[:, d:2 * d]
    v = zs[:, 2 * d:3 * d]
    g = zs[:, 3 * d:4 * d]
    wd = zs[:, 4 * d:4 * d + LORA]
    ad = zs[:, 4 * d + LORA:4 * d + 2 * LORA]
    wl = w0_ref[...] + jnp.dot(jnp.tanh(wd).astype(BF16), w2_ref[...], preferred_element_type=F32)
    decay = jnp.exp(-_sigmoid(wl) * math.exp(-0.5))
    a = _sigmoid(a0_ref[...] + jnp.dot(ad.astype(BF16), a2_ref[...], preferred_element_type=F32))
    r_o[...] = r
    w_o[...] = decay
    k_o[...] = k
    v_o[...] = v
    a_o[...] = a
    g_o[...] = g * _sigmoid(g)


def _rwkv_prep(zr, shift, mu, w0, w2, a0, a2, *, batch, seq_len, tile):
    m, c = zr.shape
    d = D_MODEL
    row = lambda x: x.reshape(1, -1)
    consts = [row(mu), row(w0), w2.astype(BF16), row(a0), a2.astype(BF16)]
    const_specs = [pl.BlockSpec(x.shape, lambda *_: (0, 0)) for x in consts]
    if seq_len >= tile:
        nt = seq_len // tile
        grid = (batch, nt)
        zmap = lambda b, t: (b * nt + t, 0)
        pmap = lambda b, t: (jnp.maximum((b * seq_len + t * tile) // SUBLANES - 1, 0), 0)
        shift_spec = pl.BlockSpec((None, 1, c), lambda b, t: (b, 0, 0))
    else:
        grid = (1, m // tile)
        zmap = lambda b, t: (t, 0)
        pmap = lambda b, t: (0, 0)
        shift_spec = pl.BlockSpec((tile, c), zmap)
    out_spec = pl.BlockSpec((tile, d), zmap)
    kern = functools.partial(_rwkv_prep_kernel, seq_len=seq_len, tile=tile)
    return pl.pallas_call(
        kern, grid=grid,
        in_specs=[pl.BlockSpec((tile, c), zmap), pl.BlockSpec((SUBLANES, c), pmap), shift_spec] + const_specs,
        out_specs=[out_spec] * 6,
        out_shape=[jax.ShapeDtypeStruct((m, d), F32)] * 6,
        compiler_params=_cparams(2), name="rwkv_prep")(zr, zr, shift, *consts)


def _rwkv_scan_kernel(r_in, w_in, k_in, v_in, a_in, s0_ref, lg_ref, lb_ref, rk_ref, kkc_ref, kac_ref,
                      o_ref, s_ref, vec_ref, stage_ref, *, steps):
    nb = HEAD // SUBLANES

    @pl.when(pl.program_id(1) == 0)
    def _():
        s_ref[...] = s0_ref[...]

    for idx, ref in enumerate((r_in, w_in, k_in, v_in, a_in)):
        for t in range(steps):
            stage_ref[idx, t] = ref[:, t, :]
    r_ref, w_ref, k_ref, v_ref, a_ref = (stage_ref.at[i] for i in range(5))

    def bcast_row(ref, lead, j):
        return jnp.broadcast_to(ref[lead, pl.ds(j, 1), :], (SUBLANES, LANES))

    def step(t, carry):
        a = a_ref[t]
        kraw = k_ref[t]
        kkraw = kraw * kkc_ref[...]
        n2 = jnp.sum(kkraw * kkraw, axis=0, keepdims=True)
        kk = kkraw / jnp.maximum(jnp.sqrt(n2), 1e-12)
        vec_ref[0] = -kk
        vec_ref[1] = kk * a
        vec_ref[2] = kraw * (1.0 + (a - 1.0) * kac_ref[...])
        sa = [jnp.zeros((SUBLANES, LANES), F32) for _ in range(nb)]
        for j in range(HEAD):
            nk = bcast_row(vec_ref, 0, j)
            for ib in range(nb):
                sa[ib] = sa[ib] + s_ref[j, ib * SUBLANES:(ib + 1) * SUBLANES, :] * nk
        vt = [v_ref[t, ib * SUBLANES:(ib + 1) * SUBLANES, :] for ib in range(nb)]
        out = [jnp.zeros((SUBLANES, LANES), F32) for _ in range(nb)]
        for j in range(HEAD):
            wj = bcast_row(w_ref, t, j)
            kaj = bcast_row(vec_ref, 1, j)
            kj = bcast_row(vec_ref, 2, j)
            rj = bcast_row(r_ref, t, j)
            for ib in range(nb):
                sl = slice(ib * SUBLANES, (ib + 1) * SUBLANES)
                sn = s_ref[j, sl, :] * wj + sa[ib] * kaj + vt[ib] * kj
                s_ref[j, sl, :] = sn
                out[ib] = out[ib] + sn * rj
        o = jnp.concatenate(out, axis=0)
        mean = jnp.mean(o, axis=0, keepdims=True)
        dev = o - mean
        var = jnp.mean(dev * dev, axis=0, keepdims=True)
        y = dev * lax.rsqrt(var + LNX_EPS) * lg_ref[...] + lb_ref[...]
        bonus = jnp.sum(r_ref[t] * vec_ref[2] * rk_ref[...], axis=0, keepdims=True) * v_ref[t]
        o_ref[t] = y + bonus
        return carry

    lax.fori_loop(0, steps, step, 0)


def _rwkv_scan(seqs, s0, consts, *, steps):
    n, t_len, p = seqs[0].shape
    grid = (p // LANES, t_len // steps)
    in_spec = pl.BlockSpec((n, steps, LANES), lambda g, t: (0, t, g))
    out_spec = pl.BlockSpec((steps, n, LANES), lambda g, t: (t, 0, g))
    st_spec = pl.BlockSpec((n, n, LANES), lambda g, t: (0, 0, g))
    c_spec = pl.BlockSpec((n, LANES), lambda g, t: (0, g))
    kern = functools.partial(_rwkv_scan_kernel, steps=steps)
    return pl.pallas_call(
        kern, grid=grid,
        in_specs=[in_spec] * 5 + [st_spec] + [c_spec] * 5,
        out_specs=[out_spec, st_spec],
        out_shape=[jax.ShapeDtypeStruct((t_len, n, p), F32), jax.ShapeDtypeStruct((n, n, p), F32)],
        scratch_shapes=[pltpu.VMEM((3, n, LANES), F32), pltpu.VMEM((5, steps, n, LANES), F32)],
        compiler_params=_cparams(2), name="rwkv_scan")(*seqs, s0, *consts)


def _nh_order(x, axis):
    shape = x.shape
    x = x.reshape(shape[:axis] + (RWKV_HEADS, HEAD) + shape[axis + 1:])
    return jnp.swapaxes(x, axis, axis + 1).reshape(shape)


def _hn_order(x, axis):
    shape = x.shape
    x = x.reshape(shape[:axis] + (HEAD, RWKV_HEADS) + shape[axis + 1:])
    return jnp.swapaxes(x, axis, axis + 1).reshape(shape)


def _to_pairs_kernel(x_ref, o_ref, a_scr):
    nb = x_ref.shape[0]
    for b in range(nb):
        a_scr[b] = x_ref[b].T
    for n in range(HEAD):
        blk = a_scr[:, n * RWKV_HEADS:(n + 1) * RWKV_HEADS, :]
        o_ref[n] = blk.reshape(nb * RWKV_HEADS, blk.shape[2]).T


def _to_pairs(x, batch, seq_len, tt):
    assert batch * RWKV_HEADS == LANES
    x3 = x.reshape(batch, seq_len, D_MODEL)
    return pl.pallas_call(
        _to_pairs_kernel, grid=(seq_len // tt,),
        in_specs=[pl.BlockSpec((batch, tt, D_MODEL), lambda t: (0, t, 0))],
        out_specs=pl.BlockSpec((HEAD, tt, LANES), lambda t: (0, t, 0)),
        out_shape=jax.ShapeDtypeStruct((HEAD, seq_len, LANES), F32),
        scratch_shapes=[pltpu.VMEM((batch, D_MODEL, tt), F32)],
        compiler_params=_cparams(1), name="to_pairs")(x3)


def _from_pairs_kernel(o_ref, x_ref, a_scr):
    nb = x_ref.shape[0]
    for n in range(HEAD):
        blk = o_ref[:, n, :].T
        a_scr[:, n * RWKV_HEADS:(n + 1) * RWKV_HEADS, :] = blk.reshape(nb, RWKV_HEADS, blk.shape[1])
    for b in range(nb):
        x_ref[b] = a_scr[b].T


def _from_pairs(o, batch, seq_len, tt):
    assert batch * RWKV_HEADS == LANES
    out = pl.pallas_call(
        _from_pairs_kernel, grid=(seq_len // tt,),
        in_specs=[pl.BlockSpec((tt, HEAD, LANES), lambda t: (t, 0, 0))],
        out_specs=pl.BlockSpec((batch, tt, D_MODEL), lambda t: (0, t, 0)),
        out_shape=jax.ShapeDtypeStruct((batch, seq_len, D_MODEL), F32),
        scratch_shapes=[pltpu.VMEM((batch, D_MODEL, tt), F32)],
        compiler_params=_cparams(1), name="from_pairs")(o)
    return out.reshape(batch * seq_len, D_MODEL)


def _to_pairs_xla(x, batch, seq_len):
    x = x.reshape(batch, seq_len, HEAD, RWKV_HEADS)
    return jnp.transpose(x, (2, 1, 0, 3)).reshape(HEAD, seq_len, batch * RWKV_HEADS)


def _from_pairs_xla(o, batch, seq_len):
    o = o.reshape(seq_len, HEAD, batch, RWKV_HEADS)
    return jnp.transpose(o, (2, 0, 1, 3)).reshape(batch * seq_len, D_MODEL)


def _head_const_pairs(x, batch):
    x = x.reshape(RWKV_HEADS, HEAD).T
    return jnp.tile(x, (1, batch))


def _count_ge(score, thr):
    return jnp.sum(jnp.where(score >= thr, 1.0, 0.0), axis=-1, keepdims=True)


def _select_topk(score, kpos, k_sel, bisect_steps):
    kf = float(k_sel)
    vis = score > _NEG_INF
    nvis = jnp.sum(jnp.where(vis, 1.0, 0.0), axis=-1, keepdims=True)
    need_sel = nvis > kf
    rowmax = jnp.where(need_sel, jnp.max(score, axis=-1, keepdims=True), 0.0)
    rowmin = jnp.where(need_sel, jnp.min(jnp.where(vis, score, _POS_INF), axis=-1, keepdims=True), 0.0)

    def bis(_, c):
        lo, hi, chi = c
        hfin = jnp.where(hi == _POS_INF, rowmax, hi)
        piv = 0.5 * lo + 0.5 * hfin
        cnt = _count_ge(score, piv)
        ge = cnt >= kf
        return jnp.where(ge, piv, lo), jnp.where(ge, hi, piv), jnp.where(ge, chi, cnt)

    lo, hi, chi = lax.fori_loop(0, bisect_steps, bis,
                                (rowmin, jnp.full_like(rowmin, _POS_INF), jnp.zeros_like(rowmin)))

    def walk_cond(c):
        return jnp.min(c[4]) < 0.5

    def walk(c):
        hi, chi, tau, cgt, done, ceq = c
        pending = done < 0.5
        bmax = jnp.max(jnp.where(score < hi, score, _NEG_INF), axis=-1, keepdims=True)
        cnt = _count_ge(score, bmax)
        fin = jnp.logical_and(cnt >= kf, pending)
        tau = jnp.where(fin, bmax, tau)
        cgt = jnp.where(fin, chi, cgt)
        ceq = jnp.where(fin, cnt - chi, ceq)
        adv = jnp.logical_and(cnt < kf, pending)
        hi = jnp.where(adv, bmax, hi)
        chi = jnp.where(adv, cnt, chi)
        return hi, chi, tau, cgt, jnp.where(fin, 1.0, done), ceq

    done0 = jnp.where(need_sel, 0.0, 1.0)
    neg = jnp.full_like(rowmin, _NEG_INF)
    zero = jnp.zeros_like(rowmin)
    _, _, tau, cgt, _, ceq = lax.while_loop(walk_cond, walk, (hi, chi, neg, zero, done0, zero))

    need = kf - cgt
    excess = jnp.logical_and(need_sel, ceq > need)
    eq = score == tau
    n_keys = score.shape[-1]

    def tie_break(_):
        def body(_, c):
            plo, phi = c
            mid = lax.shift_right_arithmetic(plo + phi, 1)
            cnt = jnp.sum(jnp.where(jnp.logical_and(eq, kpos <= mid), 1.0, 0.0), axis=-1, keepdims=True)
            ge = cnt >= need
            return jnp.where(ge, plo, mid), jnp.where(ge, mid, phi)
        plo0 = jnp.full(tau.shape, -1, jnp.int32)
        phi0 = jnp.full(tau.shape, n_keys - 1, jnp.int32)
        _, phi = lax.fori_loop(0, n_keys.bit_length() + 1, body, (plo0, phi0))
        return jnp.where(excess, phi, n_keys)

    any_excess = jnp.max(jnp.where(excess, 1.0, 0.0)) > 0.0
    pcut = lax.cond(any_excess, tie_break, lambda _: jnp.full(tau.shape, n_keys, jnp.int32), 0)
    pcut = jnp.where(need_sel, pcut, -1)
    return jnp.logical_or(score > tau, jnp.logical_and(eq, kpos <= pcut))


_MASKED = -1e30


def _select_topk_chunked(s_ref, nch, ck, k_sel, nvis, bisect_steps):
    rows = s_ref.shape[0]
    kf = float(k_sel)
    nfold = ck // LANES

    def chunk(c):
        return s_ref[:, pl.ds(pl.multiple_of(c * ck, ck), ck)]

    def fold(x, op):
        out = x[:, 0:LANES]
        for i in range(1, nfold):
            out = op(out, x[:, i * LANES:(i + 1) * LANES])
        return out

    def count(pred):
        def body(c, acc):
            return acc + fold(jnp.where(pred(chunk(c), c), 1.0, 0.0), jnp.add)
        acc = lax.fori_loop(0, nch, body, jnp.zeros((rows, LANES), F32))
        return jnp.sum(acc, axis=-1, keepdims=True)

    def row_max(val):
        def body(c, acc):
            return jnp.maximum(acc, fold(val(chunk(c)), jnp.maximum))
        acc = lax.fori_loop(0, nch, body, jnp.full((rows, LANES), _NEG_INF, F32))
        return jnp.max(acc, axis=-1, keepdims=True)

    need_sel = nvis > kf
    rowmax = jnp.where(need_sel, row_max(lambda x: x), 0.0)
    rowmin = jnp.where(need_sel, -row_max(lambda x: jnp.where(x > _NEG_INF, -x, _NEG_INF)), 0.0)

    def bis(_, c):
        lo, hi, chi = c
        hfin = jnp.where(hi == _POS_INF, rowmax, hi)
        piv = 0.5 * lo + 0.5 * hfin
        cnt = count(lambda x, _c: x >= piv)
        ge = cnt >= kf
        return jnp.where(ge, piv, lo), jnp.where(ge, hi, piv), jnp.where(ge, chi, cnt)

    lo, hi, chi = lax.fori_loop(0, bisect_steps, bis,
                                (rowmin, jnp.full_like(rowmin, _POS_INF), jnp.zeros_like(rowmin)))

    def walk_cond(c):
        return jnp.min(c[4]) < 0.5

    def walk(c):
        hi, chi, tau, cgt, done, ceq = c
        pending = done < 0.5
        bmax = row_max(lambda x: jnp.where(x < hi, x, _NEG_INF))
        cnt = count(lambda x, _c: x >= bmax)
        fin = jnp.logical_and(cnt >= kf, pending)
        tau = jnp.where(fin, bmax, tau)
        cgt = jnp.where(fin, chi, cgt)
        ceq = jnp.where(fin, cnt - chi, ceq)
        adv = jnp.logical_and(cnt < kf, pending)
        hi = jnp.where(adv, bmax, hi)
        chi = jnp.where(adv, cnt, chi)
        return hi, chi, tau, cgt, jnp.where(fin, 1.0, done), ceq

    neg = jnp.full_like(rowmin, _NEG_INF)
    zero = jnp.zeros_like(rowmin)
    _, _, tau, cgt, _, ceq = lax.while_loop(
        walk_cond, walk, (hi, chi, neg, zero, jnp.where(need_sel, 0.0, 1.0), zero))

    need = kf - cgt
    excess = jnp.logical_and(need_sel, ceq > need)
    n_keys = s_ref.shape[1]

    def kpos(c):
        return c * ck + lax.broadcasted_iota(jnp.int32, (rows, ck), 1)

    def tie_break(_):
        def body(_, c):
            plo, phi = c
            mid = lax.shift_right_arithmetic(plo + phi, 1)
            cnt = count(lambda x, cc: jnp.logical_and(x == tau, kpos(cc) <= mid))
            ge = cnt >= need
            return jnp.where(ge, plo, mid), jnp.where(ge, mid, phi)
        plo0 = jnp.full(tau.shape, -1, jnp.int32)
        phi0 = jnp.full(tau.shape, n_keys - 1, jnp.int32)
        _, phi = lax.fori_loop(0, n_keys.bit_length() + 1, body, (plo0, phi0))
        return jnp.where(excess, phi, n_keys)

    any_excess = jnp.max(jnp.where(excess, 1.0, 0.0)) > 0.0
    pcut = lax.cond(any_excess, tie_break, lambda _: jnp.full(tau.shape, n_keys, jnp.int32), 0)
    pcut = jnp.where(need_sel, pcut, -1)

    def write(c, carry):
        x = chunk(c)
        sel = jnp.logical_or(x > tau, jnp.logical_and(x == tau, kpos(c) <= pcut))
        s_ref[:, pl.ds(pl.multiple_of(c * ck, ck), ck)] = jnp.where(sel, 0.0, _MASKED)
        return carry

    lax.fori_loop(0, nch, write, 0)


def _attn_prep_kernel(q_ref, qi_ref, kv_ref, kiw_ref, qg_ref, kg_ref, kn_o, qb_o, qib_o, kvb_o, kib_o):
    w = ATT_KV_HEADS * HEAD
    for h in range(ATT_KV_HEADS):
        hs = slice(h * HEAD, (h + 1) * HEAD)
        x = kv_ref[:, hs]
        kn = x * lax.rsqrt(jnp.mean(x * x, axis=-1, keepdims=True) + NORM_EPS) * kg_ref[...]
        kn_o[:, hs] = kn
        kvb_o[:, hs] = kn.astype(BF16)
        v = kv_ref[:, w + h * HEAD:w + (h + 1) * HEAD].astype(BF16)
        kvb_o[:, w + h * LANES:w + (h + 1) * LANES] = jnp.concatenate([v, jnp.ones_like(v)], axis=1)
    for h in range(ATT_HEADS):
        x = q_ref[:, h * HEAD:(h + 1) * HEAD]
        qn = x * lax.rsqrt(jnp.mean(x * x, axis=-1, keepdims=True) + NORM_EPS) * (qg_ref[...] * HEAD ** -0.5)
        qb_o[h] = qn.astype(BF16)
    qib_o[...] = (qi_ref[...] * HEAD ** -0.5).astype(BF16)
    kib_o[...] = kiw_ref[:, 0:HEAD].astype(BF16)


def _attn_prep(za, qn_g, kn_g, tm):
    m = za.shape[0]
    w = ATT_KV_HEADS * HEAD
    wi = IDX_HEADS * HEAD
    row = lambda width, blk: pl.BlockSpec((tm, width), lambda i, blk=blk: (i, blk))
    gspec = pl.BlockSpec((1, HEAD), lambda i: (0, 0))
    return pl.pallas_call(
        _attn_prep_kernel, grid=(m // tm,),
        in_specs=[row(D_MODEL, ZA_Q // D_MODEL), row(wi, ZA_QI // wi), row(2 * w, ZA_KV // (2 * w)),
                  row(LANES, ZA_KIW // LANES), gspec, gspec],
        out_specs=[row(w, 0), pl.BlockSpec((ATT_HEADS, tm, HEAD), lambda i: (0, i, 0)), row(wi, 0),
                   row(w + ATT_KV_HEADS * LANES, 0), row(HEAD, 0)],
        out_shape=[jax.ShapeDtypeStruct((m, w), F32), jax.ShapeDtypeStruct((ATT_HEADS, m, HEAD), BF16),
                   jax.ShapeDtypeStruct((m, wi), BF16), jax.ShapeDtypeStruct((m, w + ATT_KV_HEADS * LANES), BF16),
                   jax.ShapeDtypeStruct((m, HEAD), BF16)],
        compiler_params=_cparams(1), name="attn_prep")(za, za, za, za, qn_g.reshape(1, HEAD), kn_g.reshape(1, HEAD))


def _bucket_edges():
    max_exact = REL_BUCKETS // 2
    d = np.arange(REL_MAX_DIST + 1)
    df = np.maximum(d, 1).astype(np.float32)
    large = max_exact + (np.log(df / max_exact) / math.log(REL_MAX_DIST / max_exact)
                         * (REL_BUCKETS - max_exact)).astype(np.int32)
    bucket = np.where(d < max_exact, d, np.minimum(large, REL_BUCKETS - 1))
    return [int(np.argmax(bucket >= b)) for b in range(REL_BUCKETS)]


_BUCKET_EDGES = _bucket_edges()


def _rel_bias_lookup(dist, value_of_bucket):
    bias = value_of_bucket(REL_BUCKETS - 1)
    for b in range(REL_BUCKETS - 2, -1, -1):
        bias = jnp.where(dist < _BUCKET_EDGES[b + 1], value_of_bucket(b), bias)
    return bias


def _attn_prompt_kernel(rb_ref, q_ref, qi_ref, kiwq_ref, g_ref, kvb_ref, kib_ref, o_ref, s_scr, tbd_ref,
                        l_scr, mx_scr, acc_scr, *, tq, k_sel, bisect_steps):
    qt = pl.program_id(0)
    q0 = qt * tq
    ck = tq
    rg_rows = LANES
    n_rg = tq // rg_rows
    w = ATT_KV_HEADS * HEAD

    @pl.when(jnp.logical_and(pl.program_id(1) == 0, qt == 0))
    def _():
        rr = lax.broadcasted_iota(jnp.int32, (tq, tq), 0) - lax.broadcasted_iota(jnp.int32, (tq, tq), 1)
        for h in range(ATT_HEADS):
            far = rb_ref[REL_BUCKETS - 1, h]
            tbd_ref[h, 0] = _rel_bias_lookup(rr, lambda b: rb_ref[b, h]) - far
            tbd_ref[h, 1] = _rel_bias_lookup(rr + tq, lambda b: rb_ref[b, h]) - far

    def chunk_start(c):
        return pl.multiple_of(c * ck, ck)

    wcol = kiwq_ref[:, HEAD:HEAD + IDX_HEADS] * (IDX_HEADS ** -0.5)
    qi_h = [[qi_ref[rg * rg_rows:(rg + 1) * rg_rows, h * HEAD:(h + 1) * HEAD] for h in range(IDX_HEADS)]
            for rg in range(n_rg)]
    w_h = [[wcol[rg * rg_rows:(rg + 1) * rg_rows, h:h + 1] for h in range(IDX_HEADS)] for rg in range(n_rg)]

    def score_chunk(c, carry):
        k0 = chunk_start(c)
        kc = kib_ref[pl.ds(k0, ck), :]
        kpos = k0 + lax.broadcasted_iota(jnp.int32, (rg_rows, ck), 1)
        for rg in range(n_rg):
            acc = jnp.zeros((rg_rows, ck), F32)
            for h in range(IDX_HEADS):
                acc = acc + jnp.maximum(_dot_nt(qi_h[rg][h], kc), 0.0) * w_h[rg][h]
            qpos = q0 + rg * rg_rows + lax.broadcasted_iota(jnp.int32, (rg_rows, ck), 0)
            s_scr[rg * rg_rows:(rg + 1) * rg_rows, pl.ds(k0, ck)] = jnp.where(kpos <= qpos, acc, _NEG_INF)
        return carry

    nch = qt + 1
    lax.fori_loop(0, nch, score_chunk, 0)

    nvis = (q0 + 1 + lax.broadcasted_iota(jnp.int32, (tq, 1), 0)).astype(F32)
    _select_topk_chunked(s_scr, nch, ck, k_sel, nvis, bisect_steps)

    g_rows = ATT_GROUP * tq
    n_far = jnp.maximum(qt - 1, 0)
    for kvh in range(ATT_KV_HEADS):
        gs = slice(kvh * ATT_GROUP, (kvh + 1) * ATT_GROUP)
        qg = q_ref[gs, :, :].reshape(g_rows, HEAD)
        mx_scr[...] = jnp.full(mx_scr.shape, _MASKED, F32)
        acc_scr[...] = jnp.zeros(acc_scr.shape, F32)

        def logits_chunk(c, carry, near, qg=qg, gs=gs, kvh=kvh):
            k0 = chunk_start(c)
            kc = kvb_ref[pl.ds(k0, ck), kvh * HEAD:(kvh + 1) * HEAD]
            s3 = _dot_nt(qg, kc).reshape(ATT_GROUP, tq, ck) + s_scr[:, pl.ds(k0, ck)][None]
            if near:
                s3 = s3 + tbd_ref[gs, qt - c, :, :]
            s = s3.reshape(g_rows, ck)
            l_scr[:, pl.ds(k0, ck)] = s
            m = mx_scr[...]
            for i in range(ck // LANES):
                m = jnp.maximum(m, s[:, i * LANES:(i + 1) * LANES])
            mx_scr[...] = m
            return carry

        lax.fori_loop(0, n_far, functools.partial(logits_chunk, near=False), 0)
        lax.fori_loop(n_far, nch, functools.partial(logits_chunk, near=True), 0)
        mx_scr[...] = jnp.broadcast_to(jnp.max(mx_scr[...], axis=-1, keepdims=True), mx_scr.shape)

        def pv_chunk(c, carry, kvh=kvh):
            k0 = chunk_start(c)
            vx = kvb_ref[pl.ds(k0, ck), w + kvh * LANES:w + (kvh + 1) * LANES]
            m = mx_scr[...]
            p = jnp.concatenate([jnp.exp(l_scr[:, pl.ds(k0 + i * LANES, LANES)] - m)
                                 for i in range(ck // LANES)], axis=1).astype(BF16)
            acc_scr[...] += jnp.dot(p, vx, preferred_element_type=F32)
            return carry

        lax.fori_loop(0, nch, pv_chunk, 0)

        acc = acc_scr[...]
        o = acc[:, 0:HEAD] / acc[:, HEAD:HEAD + 1]
        for pp in range(ATT_GROUP // 2):
            cols = slice((kvh * ATT_GROUP // 2 + pp) * LANES, (kvh * ATT_GROUP // 2 + pp + 1) * LANES)
            gh = g_ref[:, cols]
            pair = jnp.concatenate([o[2 * pp * tq:(2 * pp + 1) * tq], o[(2 * pp + 1) * tq:(2 * pp + 2) * tq]], axis=1)
            o_ref[:, cols] = pair * (gh * _sigmoid(gh))


def _attn_prompt(za, kn, qb, qib, kvb, kib, *, rel_bias, batch, seq_len, tq, k_sel):
    m = za.shape[0]
    nq = seq_len // tq
    w = ATT_KV_HEADS * HEAD
    wi = IDX_HEADS * HEAD
    assert REL_MAX_DIST <= tq and tq % LANES == 0
    row_map = lambda blk: (lambda t, b, blk=blk: (b * nq + t, blk))
    key_map = lambda t, b: (b, 0)
    kern = functools.partial(_attn_prompt_kernel, tq=tq, k_sel=k_sel, bisect_steps=14)
    return pl.pallas_call(
        kern, grid=(nq, batch),
        in_specs=[
            pl.BlockSpec(memory_space=pltpu.SMEM),
            pl.BlockSpec((ATT_HEADS, tq, HEAD), lambda t, b: (0, b * nq + t, 0)),
            pl.BlockSpec((tq, wi), row_map(0)),
            pl.BlockSpec((tq, LANES), row_map(ZA_KIW // LANES)),
            pl.BlockSpec((tq, D_MODEL), row_map(ZA_GATT // D_MODEL)),
            pl.BlockSpec((seq_len, w + ATT_KV_HEADS * LANES), key_map),
            pl.BlockSpec((seq_len, HEAD), key_map),
        ],
        out_specs=pl.BlockSpec((tq, D_MODEL), row_map(0)),
        out_shape=jax.ShapeDtypeStruct((m, D_MODEL), F32),
        scratch_shapes=[pltpu.VMEM((tq, seq_len), F32), pltpu.VMEM((ATT_HEADS, 2, tq, tq), F32),
                        pltpu.VMEM((ATT_GROUP * tq, seq_len), F32), pltpu.VMEM((ATT_GROUP * tq, LANES), F32),
                        pltpu.VMEM((ATT_GROUP * tq, LANES), F32)],
        compiler_params=_cparams(2), name="attn_prompt")(rel_bias, qb, qib, za, za, kvb, kib)


def _sample_score_kernel(pt_ref, q_ref, w_ref, kiw_new_ref, *rest, n_pages, page, dec_seq):
    page_refs = rest[:n_pages]
    o_ref = rest[n_pages]
    past = n_pages * page
    kidx = jnp.concatenate([r[...] for r in page_refs], axis=0).astype(BF16)
    new = kiw_new_ref[:, 0:HEAD]
    new = jnp.concatenate([new, jnp.zeros((LANES - dec_seq, HEAD), F32)], axis=0).astype(BF16)
    q = (q_ref[...] * (HEAD ** -0.5)).astype(BF16)
    lg = jnp.concatenate([_dot_nt(q, kidx), _dot_nt(q, new)], axis=1)
    wr = jnp.maximum(lg, 0.0) * (w_ref[...] * (IDX_HEADS ** -0.5))
    n_keys = past + LANES
    sc = jnp.sum(wr.reshape(dec_seq, IDX_HEADS, n_keys), axis=1)
    kpos = lax.broadcasted_iota(jnp.int32, (dec_seq, n_keys), 1)
    tpos = lax.broadcasted_iota(jnp.int32, (dec_seq, n_keys), 0)
    o_ref[...] = jnp.where(kpos <= past + tpos, sc, _NEG_INF)


def _sample_select_kernel(s_ref, o_ref, *, k_sel, bisect_steps):
    score = s_ref[...]
    kpos = lax.broadcasted_iota(jnp.int32, score.shape, 1)
    sel = _select_topk(score, kpos, k_sel, bisect_steps)
    o_ref[...] = jnp.where(sel, 0.0, _NEG_INF)


def _sample_attn_kernel(pt_ref, q_ref, g_ref, kn_new_ref, kv_new_ref, mask_ref, rb_rows_ref, qg_ref, *rest,
                        n_pages, page, dec_seq):
    k_pages = rest[:n_pages]
    v_pages = rest[n_pages:2 * n_pages]
    o_ref = rest[2 * n_pages]
    bias_ref = rest[2 * n_pages + 1]
    w = ATT_KV_HEADS * HEAD
    rows = dec_seq * ATT_HEADS
    n_keys = n_pages * page + LANES

    @pl.when(pl.program_id(0) == 0)
    def _():
        t_row = lax.div(lax.broadcasted_iota(jnp.int32, (rows, n_keys), 0), ATT_HEADS)
        dist = n_pages * page + t_row - lax.broadcasted_iota(jnp.int32, (rows, n_keys), 1)
        bias_ref[...] = _rel_bias_lookup(dist, lambda b: rb_rows_ref[:, b:b + 1])

    pad = jnp.zeros((LANES - dec_seq, w), F32)
    k_all = jnp.concatenate([r[...] for r in k_pages] + [kn_new_ref[...], pad], axis=0).astype(BF16)
    v_all = jnp.concatenate([r[...] for r in v_pages] + [kv_new_ref[:, w:2 * w], pad], axis=0).astype(BF16)
    q = q_ref[...]
    q = q * lax.rsqrt(jnp.mean(q * q, axis=-1, keepdims=True) + NORM_EPS) * qg_ref[...]
    qb = q.astype(BF16)
    head = lax.rem(lax.broadcasted_iota(jnp.int32, (rows, 1), 0), ATT_HEADS)
    first = head < ATT_GROUP
    lg = jnp.where(first, _dot_nt(qb, k_all[:, 0:HEAD]), _dot_nt(qb, k_all[:, HEAD:2 * HEAD]))
    mask = jnp.broadcast_to(mask_ref[...][:, None, :], (dec_seq, ATT_HEADS, n_keys)).reshape(rows, n_keys)
    s = lg * (HEAD ** -0.5) + bias_ref[...] + mask
    p = jnp.exp(s - jnp.max(s, axis=-1, keepdims=True))
    l = jnp.sum(p, axis=-1, keepdims=True)
    pb = p.astype(BF16)
    o = jnp.where(first, jnp.dot(pb, v_all[:, 0:HEAD], preferred_element_type=F32),
                  jnp.dot(pb, v_all[:, HEAD:2 * HEAD], preferred_element_type=F32)) / l
    g = g_ref[...]
    o_ref[...] = o * (g * _sigmoid(g))


def _attn_sample(za, kn, qb, qib, kvb, kib, *, cache_k, cache_v, cache_kidx, page_table, rel_bias, qn_g,
                 dec_seq, k_sel):
    nb, n_pages = page_table.shape
    n_phys, page = cache_k.shape[0], cache_k.shape[1]
    past = n_pages * page
    n_keys = past + LANES
    w = ATT_KV_HEADS * HEAD
    ck = cache_k.reshape(n_phys, page, w)
    cv = cache_v.reshape(n_phys, page, w)
    ci = cache_kidx.reshape(n_phys, page, HEAD)
    za3 = za.reshape(nb, dec_seq, ZA_COLS)
    kn3 = kn.reshape(nb, dec_seq, w)
    qi = za[:, ZA_QI:ZA_QI + IDX_HEADS * HEAD].reshape(nb, dec_seq * IDX_HEADS, HEAD)
    wi = za[:, ZA_KIW + HEAD:ZA_KIW + HEAD + IDX_HEADS].reshape(nb, dec_seq * IDX_HEADS, 1)
    qa = za[:, ZA_Q:ZA_Q + D_MODEL].reshape(nb, dec_seq * ATT_HEADS, HEAD)
    ga = za[:, ZA_GATT:ZA_GATT + D_MODEL].reshape(nb, dec_seq * ATT_HEADS, HEAD)

    def page_specs(width):
        return [pl.BlockSpec((None, page, width), lambda b, pt, j=j: (pt[b, j], 0, 0)) for j in range(n_pages)]

    per_b = lambda r, c: pl.BlockSpec((None, r, c), lambda b, pt: (b, 0, 0))
    kiw_new = pl.BlockSpec((None, dec_seq, LANES), lambda b, pt: (b, 0, ZA_KIW // LANES))

    scores = pl.pallas_call(
        functools.partial(_sample_score_kernel, n_pages=n_pages, page=page, dec_seq=dec_seq),
        grid_spec=pltpu.PrefetchScalarGridSpec(
            num_scalar_prefetch=1, grid=(nb,),
            in_specs=[per_b(dec_seq * IDX_HEADS, HEAD), per_b(dec_seq * IDX_HEADS, 1), kiw_new] + page_specs(HEAD),
            out_specs=per_b(dec_seq, n_keys)),
        out_shape=jax.ShapeDtypeStruct((nb, dec_seq, n_keys), F32),
        compiler_params=_cparams(1), name="sample_scores")(page_table, qi, wi, za3, *([ci] * n_pages))

    rows = nb * dec_seq
    tr = min(rows, 128)
    mask = pl.pallas_call(
        functools.partial(_sample_select_kernel, k_sel=k_sel, bisect_steps=14),
        grid=(rows // tr,),
        in_specs=[pl.BlockSpec((tr, n_keys), lambda i: (i, 0))],
        out_specs=pl.BlockSpec((tr, n_keys), lambda i: (i, 0)),
        out_shape=jax.ShapeDtypeStruct((rows, n_keys), F32),
        compiler_params=_cparams(1), name="sample_select")(scores.reshape(rows, n_keys))

    rb_rows = jnp.tile(rel_bias.T, (dec_seq, 1))

    const = lambda shape: pl.BlockSpec(shape, lambda b, pt: (0,) * len(shape))
    kv_new = pl.BlockSpec((None, dec_seq, 2 * w), lambda b, pt: (b, 0, ZA_KV // (2 * w)))
    out = pl.pallas_call(
        functools.partial(_sample_attn_kernel, n_pages=n_pages, page=page, dec_seq=dec_seq),
        grid_spec=pltpu.PrefetchScalarGridSpec(
            num_scalar_prefetch=1, grid=(nb,),
            in_specs=[per_b(dec_seq * ATT_HEADS, HEAD), per_b(dec_seq * ATT_HEADS, HEAD), per_b(dec_seq, w),
                      kv_new, per_b(dec_seq, n_keys), const((dec_seq * ATT_HEADS, REL_BUCKETS)), const((1, HEAD))]
            + page_specs(w) + page_specs(w),
            out_specs=per_b(dec_seq * ATT_HEADS, HEAD),
            scratch_shapes=[pltpu.VMEM((dec_seq * ATT_HEADS, n_keys), F32)]),
        out_shape=jax.ShapeDtypeStruct((nb, dec_seq * ATT_HEADS, HEAD), F32),
        compiler_params=_cparams(1), name="sample_attn")(
            page_table, qa, ga, kn3, za3, mask.reshape(nb, dec_seq, n_keys), rb_rows, qn_g.reshape(1, HEAD),
            *([ck] * n_pages), *([cv] * n_pages))
    return out.reshape(rows, D_MODEL)


def _merge_kernel(x_ref, oa_ref, sg_ref, ob_ref, ga_ref, gb_ref, wpa_ref, wpb_ref, wo_ref, y_ref):
    oa = (oa_ref[...] * sg_ref[...]).astype(BF16)
    pa = jnp.dot(oa, wpa_ref[...], preferred_element_type=F32)
    pb = jnp.dot(ob_ref[...].astype(BF16), wpb_ref[...], preferred_element_type=F32)
    merged = _sigmoid(ga_ref[...]) * pa + _sigmoid(gb_ref[...]) * pb
    y_ref[...] = x_ref[...] + jnp.dot(merged.astype(BF16), wo_ref[...], preferred_element_type=F32)


def _merge(x, oa, sg, ob, za, w_pa, w_pb, w_out, tm):
    m, d = x.shape
    row = pl.BlockSpec((tm, d), lambda i: (i, 0))
    wsp = pl.BlockSpec((d, d), lambda i: (0, 0))
    return pl.pallas_call(
        _merge_kernel, grid=(m // tm,),
        in_specs=[row, row, row, row,
                  pl.BlockSpec((tm, d), lambda i: (i, ZA_GA // d)), pl.BlockSpec((tm, d), lambda i: (i, ZA_GB // d)),
                  wsp, wsp, wsp],
        out_specs=row,
        out_shape=jax.ShapeDtypeStruct((m, d), F32),
        compiler_params=_cparams(1), name="merge")(x, oa, sg, ob, za, za, w_pa, w_pb, w_out)


def _layer(x, shift, s0, params, *, batch, seq_len, attend):
    (norm_g, w_r, w_a, mu, w0, w2, a0, a2, k_k, k_a, r_k, lnx_g, lnx_b, qn_g, kn_g, w_pa, w_pb, w_out) = params
    m = batch * seq_len
    tm = min(m, 512)
    xn = _rmsnorm(x, norm_g, tm)
    zr = _matmul(xn, w_r, tm, RWKV_COLS // 3, "inproj_rwkv")
    za = _matmul(xn, w_a, tm, ZA_COLS // 4, "inproj_attn")

    tile = min(m, 256)
    r, w, k, v, a, sg = _rwkv_prep(zr, shift, mu, w0, w2, a0, a2, batch=batch, seq_len=seq_len, tile=tile)
    relayout_tile = 128
    in_kernel_relayout = seq_len % relayout_tile == 0 and batch * RWKV_HEADS == LANES
    if in_kernel_relayout:
        seqs = [_to_pairs(t, batch, seq_len, relayout_tile) for t in (r, w, k, v, a)]
    else:
        seqs = [_to_pairs_xla(t, batch, seq_len) for t in (r, w, k, v, a)]
    consts = [_head_const_pairs(t, batch) for t in (lnx_g, lnx_b, r_k.reshape(-1), k_k, k_a)]
    o_t, s_t = _rwkv_scan(seqs, s0, consts, steps=min(seq_len, 32))
    if in_kernel_relayout:
        oa = _from_pairs(o_t, batch, seq_len, relayout_tile)
    else:
        oa = _from_pairs_xla(o_t, batch, seq_len)

    kn, qb, qib, kvb, kib = _attn_prep(za, qn_g, kn_g, tm)
    ob = attend(za, kn, qb, qib, kvb, kib)

    y = _merge(x, oa, sg, ob, za, w_pa, w_pb, w_out, min(m, 256))
    return y, zr, za, kn, s_t


def _state_to_pairs(s):
    b, h, n, _ = s.shape
    return jnp.transpose(s, (3, 2, 0, 1)).reshape(n, n, b * h)


def _state_from_pairs(s, batch):
    n = s.shape[0]
    return jnp.transpose(s.reshape(n, n, batch, RWKV_HEADS), (2, 3, 1, 0))


def kernel(x_prompt, x_sample, cache_k, cache_v, cache_kidx, state_wkv, state_shift, page_table, norm_g, w_in,
           shift_mu, w0, w2, a0, a2, k_k, k_a, r_k, lnx_g, lnx_b, q_norm_g, k_norm_g, rel_bias, w_pa, w_pb, w_out):
    bsz, seq, d = x_prompt.shape
    dec_bsz, dec_seq, _ = x_sample.shape
    depth = w_in.shape[0]
    assert depth == 1 and d == D_MODEL
    past_len = page_table.shape[1] * cache_k.shape[2]
    topk_p = min(TOPK_MAX, seq // 4)
    topk_s = min(TOPK_MAX, (past_len + dec_seq) // 4)
    l = 0

    wl = w_in[l]
    c0 = RWKV_COLS
    q_w, kv_w, qi_w = wl[:, c0:c0 + 1024], wl[:, c0 + 1024:c0 + 1280], wl[:, c0 + 1280:c0 + 1792]
    kiw_w = wl[:, c0 + 1792:c0 + 1864]
    rest_w = wl[:, c0 + 1864:]
    zpad = lambda n: jnp.zeros((d, n), wl.dtype)
    w_a = jnp.concatenate([q_w, qi_w, kv_w, kiw_w, zpad(LANES - kiw_w.shape[1]), zpad(LANES), rest_w],
                          axis=1).astype(BF16)
    assert w_a.shape[1] == ZA_COLS
    def rwkv_cols(x, fn):
        lead = x.shape[:-1]
        main = fn(x[..., :4 * d].reshape(lead + (4, d)), len(lead) + 1).reshape(lead + (4 * d,))
        return jnp.concatenate([main, x[..., 4 * d:]], axis=-1)

    w_r = rwkv_cols(wl[:, :c0], _nh_order).astype(BF16)
    params = (norm_g[l], w_r, w_a, rwkv_cols(shift_mu[l], _nh_order), _nh_order(w0[l], 0), _nh_order(w2[l], 1),
              _nh_order(a0[l], 0), _nh_order(a2[l], 1), k_k[l], k_a[l], r_k[l],
              lnx_g[l], lnx_b[l], q_norm_g[l], k_norm_g[l],
              _nh_order(w_pa[l], 0).astype(BF16), w_pb[l].astype(BF16), w_out[l].astype(BF16))

    xp = x_prompt.reshape(bsz * seq, d)
    attend_p = functools.partial(_attn_prompt, rel_bias=rel_bias, batch=bsz, seq_len=seq,
                                 tq=min(seq, 256), k_sel=topk_p)
    yp, zr_p, za_p, kn_p, st_p = _layer(
        xp, jnp.zeros((bsz, 1, RWKV_COLS), F32), jnp.zeros((HEAD, HEAD, bsz * RWKV_HEADS), F32), params,
        batch=bsz, seq_len=seq, attend=attend_p)

    xs = x_sample.reshape(dec_bsz * dec_seq, d)
    attend_s = functools.partial(_attn_sample, cache_k=cache_k[l], cache_v=cache_v[l], cache_kidx=cache_kidx[l],
                                 page_table=page_table, rel_bias=rel_bias, qn_g=q_norm_g[l], dec_seq=dec_seq,
                                 k_sel=topk_s)
    shift_rows = jnp.repeat(rwkv_cols(state_shift[l], _nh_order), dec_seq, axis=0)
    ys, zr_s, za_s, kn_s, st_s = _layer(
        xs, shift_rows, _state_to_pairs(state_wkv[l]), params, batch=dec_bsz, seq_len=dec_seq, attend=attend_s)

    w = ATT_KV_HEADS * HEAD

    def pack(y, zr, za, kn, st, b, t):
        v = za[:, ZA_KV + w:ZA_KV + 2 * w]
        kidx = za[:, ZA_KIW:ZA_KIW + HEAD]
        return (y.reshape(b, t, d),
                kn.reshape(1, b, t, ATT_KV_HEADS, HEAD), v.reshape(1, b, t, ATT_KV_HEADS, HEAD),
                kidx.reshape(1, b, t, HEAD), _state_from_pairs(st, b)[None],
                rwkv_cols(zr.reshape(b, t, RWKV_COLS)[:, -1], _hn_order)[None])

    p = pack(yp, zr_p, za_p, kn_p, st_p, bsz, seq)
    s = pack(ys, zr_s, za_s, kn_s, st_s, dec_bsz, dec_seq)
    return (p[0], s[0]) + p[1:] + s[1:]
```

```python
import functools
import math

import numpy as np
import jax
import jax.numpy as jnp
from jax import lax
from jax.experimental import pallas as pl
from jax.experimental.pallas import tpu as pltpu

F32 = jnp.float32
BF16 = jnp.bfloat16

D_MODEL = 1024
HEAD = 64
RWKV_HEADS = D_MODEL // HEAD
LORA = 64
LNX_EPS = 64e-5
ATT_HEADS = D_MODEL // HEAD
ATT_KV_HEADS = 2
ATT_GROUP = ATT_HEADS // ATT_KV_HEADS
IDX_HEADS = 8
TOPK_MAX = 256
REL_BUCKETS = 32
REL_MAX_DIST = 128
NORM_EPS = 1e-6
RWKV_COLS = 4 * D_MODEL + 2 * LORA

LANES = 128
SUBLANES = 8
VMEM_LIMIT_BYTES = 56 * 1024 * 1024

ZA_Q = 0
ZA_QI = 1024
ZA_KV = 1536
ZA_KIW = 1792
ZA_GATT = 2048
ZA_GA = 3072
ZA_GB = 4096
ZA_COLS = 5120

_NEG_INF = float("-inf")
_POS_INF = float("inf")


def _cparams(n_axes):
    return pltpu.CompilerParams(dimension_semantics=("arbitrary",) * n_axes,
                                vmem_limit_bytes=VMEM_LIMIT_BYTES)


def _sigmoid(x):
    return 1.0 / (1.0 + jnp.exp(-x))


def _dot_nt(a, b):
    return lax.dot_general(a, b, (((1,), (1,)), ((), ())), preferred_element_type=F32)


def _rmsnorm_kernel(x_ref, g_ref, o_ref):
    x = x_ref[...]
    ms = jnp.mean(x * x, axis=-1, keepdims=True)
    o_ref[...] = (x * lax.rsqrt(ms + NORM_EPS) * g_ref[...]).astype(o_ref.dtype)


def _rmsnorm(x, g, tm):
    m, d = x.shape
    return pl.pallas_call(
        _rmsnorm_kernel, grid=(m // tm,),
        in_specs=[pl.BlockSpec((tm, d), lambda i: (i, 0)), pl.BlockSpec((1, d), lambda i: (0, 0))],
        out_specs=pl.BlockSpec((tm, d), lambda i: (i, 0)),
        out_shape=jax.ShapeDtypeStruct((m, d), BF16),
        compiler_params=_cparams(1), name="rmsnorm")(x, g.reshape(1, d))


def _mm_kernel(x_ref, w_ref, o_ref):
    o_ref[...] = jnp.dot(x_ref[...], w_ref[...], preferred_element_type=F32)


def _matmul(x, w, tm, tn, name):
    m, k = x.shape
    n = w.shape[1]
    return pl.pallas_call(
        _mm_kernel, grid=(m // tm, n // tn),
        in_specs=[pl.BlockSpec((tm, k), lambda i, j: (i, 0)), pl.BlockSpec((k, tn), lambda i, j: (0, j))],
        out_specs=pl.BlockSpec((tm, tn), lambda i, j: (i, j)),
        out_shape=jax.ShapeDtypeStruct((m, n), F32),
        compiler_params=_cparams(2), name=name)(x, w)


def _rwkv_prep_kernel(z_ref, prev_ref, shift_ref, mu_ref, w0_ref, w2_ref, a0_ref, a2_ref,
                      r_o, w_o, k_o, v_o, a_o, g_o, *, seq_len, tile):
    z = z_ref[...]
    rows = lax.broadcasted_iota(jnp.int32, z.shape, 0)
    rolled = pltpu.roll(z, 1, 0)
    if seq_len >= tile:
        first = jnp.where(pl.program_id(1) == 0, shift_ref[...], prev_ref[SUBLANES - 1:SUBLANES, :])
        prev = jnp.where(rows == 0, first, rolled)
    else:
        prev = jnp.where(lax.rem(rows, seq_len) == 0, shift_ref[...], rolled)
    zs = z + (prev - z) * mu_ref[...]
    d = D_MODEL
    r = zs[:, 0:d]
    k = zs[:, d:2 * d]
    v = zs[:, 2 * d:3 * d]
    g = zs[:, 3 * d:4 * d]
    wd = zs[:, 4 * d:4 * d + LORA]
    ad = zs[:, 4 * d + LORA:4 * d + 2 * LORA]
    wl = w0_ref[...] + jnp.dot(jnp.tanh(wd).astype(BF16), w2_ref[...], preferred_element_type=F32)
    decay = jnp.exp(-_sigmoid(wl) * math.exp(-0.5))
    a = _sigmoid(a0_ref[...] + jnp.dot(ad.astype(BF16), a2_ref[...], preferred_element_type=F32))
    r_o[...] = r
    w_o[...] = decay
    k_o[...] = k
    v_o[...] = v
    a_o[...] = a
    g_o[...] = g * _sigmoid(g)


def _rwkv_prep(zr, shift, mu, w0, w2, a0, a2, *, batch, seq_len, tile):
    m, c = zr.shape
    d = D_MODEL
    row = lambda x: x.reshape(1, -1)
    consts = [row(mu), row(w0), w2.astype(BF16), row(a0), a2.astype(BF16)]
    const_specs = [pl.BlockSpec(x.shape, lambda *_: (0, 0)) for x in consts]
    if seq_len >= tile:
        nt = seq_len // tile
        grid = (batch, nt)
        zmap = lambda b, t: (b * nt + t, 0)
        pmap = lambda b, t: (jnp.maximum((b * seq_len + t * tile) // SUBLANES - 1, 0), 0)
        shift_spec = pl.BlockSpec((None, 1, c), lambda b, t: (b, 0, 0))
    else:
        grid = (1, m // tile)
        zmap = lambda b, t: (t, 0)
        pmap = lambda b, t: (0, 0)
        shift_spec = pl.BlockSpec((tile, c), zmap)
    out_spec = pl.BlockSpec((tile, d), zmap)
    kern = functools.partial(_rwkv_prep_kernel, seq_len=seq_len, tile=tile)
    return pl.pallas_call(
        kern, grid=grid,
        in_specs=[pl.BlockSpec((tile, c), zmap), pl.BlockSpec((SUBLANES, c), pmap), shift_spec] + const_specs,
        out_specs=[out_spec] * 6,
        out_shape=[jax.ShapeDtypeStruct((m, d), F32)] * 6,
        compiler_params=_cparams(2), name="rwkv_prep")(zr, zr, shift, *consts)


def _rwkv_scan_kernel(r_in, w_in, k_in, v_in, a_in, s0_ref, lg_ref, lb_ref, rk_ref, kkc_ref, kac_ref,
                      o_ref, s_ref, vec_ref, bf_ref, ge_ref, *, groups, live):
    nb = HEAD // SUBLANES

    @pl.when(pl.program_id(1) == 0)
    def _():
        s_ref[...] = s0_ref[...]

    shape4 = (groups, HEAD, SUBLANES, LANES)
    a = a_in[...].reshape(shape4)
    kraw = k_in[...].reshape(shape4)
    r_all = r_in[...].reshape(shape4)
    kkraw = kraw * kkc_ref[...][None]
    n2 = jnp.sum(kkraw * kkraw, axis=1, keepdims=True)
    kk = kkraw / jnp.maximum(jnp.sqrt(n2), 1e-12)
    kmod = kraw * (1.0 + (a - 1.0) * kac_ref[...][None])
    bf_ref[...] = jnp.sum(r_all * kmod * rk_ref[...][None], axis=1)
    row = lax.broadcasted_iota(jnp.int32, (HEAD, SUBLANES, LANES), 1)
    g_end = jnp.ones((HEAD, SUBLANES, LANES), F32)
    for q in range(groups):
        g = w_in[q].reshape(HEAD, SUBLANES, LANES)
        for sh in (1, 2, 4):
            g = g * jnp.where(row >= sh, pltpu.roll(g, sh, 1), 1.0)
        g = g * g_end
        g_prev = jnp.where(row >= 1, pltpu.roll(g, 1, 1), g_end)
        g_inv = 1.0 / g
        flat = (HEAD * SUBLANES, LANES)
        vec_ref[0, q] = (-kk[q] * g_prev).reshape(flat)
        vec_ref[1, q] = (kk[q] * a[q] * g_inv).reshape(flat)
        vec_ref[2, q] = (kmod[q] * g_inv).reshape(flat)
        vec_ref[3, q] = (r_all[q] * g).reshape(flat)
        last = live - 1 if q == groups - 1 else SUBLANES - 1
        g_end = jnp.broadcast_to(g[:, last:last + 1, :], (HEAD, SUBLANES, LANES))
    ge_ref[...] = g_end

    def group(q, carry):
        for r in range(live):
            def row_of(idx, j):
                return jnp.broadcast_to(vec_ref[idx, q, pl.ds(j * SUBLANES + r, 1), :], (SUBLANES, LANES))

            sa = [jnp.zeros((SUBLANES, LANES), F32) for _ in range(nb)]
            for j in range(HEAD):
                nk = row_of(0, j)
                for ib in range(nb):
                    sa[ib] = sa[ib] + s_ref[j, ib * SUBLANES:(ib + 1) * SUBLANES, :] * nk
            vt = [v_in[q, pl.ds(ib * SUBLANES * SUBLANES + r, SUBLANES, stride=SUBLANES), :] for ib in range(nb)]
            out = [jnp.zeros((SUBLANES, LANES), F32) for _ in range(nb)]
            for j in range(HEAD):
                kaj = row_of(1, j)
                kj = row_of(2, j)
                rj = row_of(3, j)
                for ib in range(nb):
                    sl = slice(ib * SUBLANES, (ib + 1) * SUBLANES)
                    sn = s_ref[j, sl, :] + sa[ib] * kaj + vt[ib] * kj
                    s_ref[j, sl, :] = sn
                    out[ib] = out[ib] + sn * rj
            o = jnp.concatenate(out, axis=0)
            mean = jnp.mean(o, axis=0, keepdims=True)
            dev = o - mean
            var = jnp.mean(dev * dev, axis=0, keepdims=True)
            y = dev * lax.rsqrt(var + LNX_EPS) * lg_ref[...] + lb_ref[...]
            o_ref[q * live + r] = y + bf_ref[q, pl.ds(r, 1), :] * jnp.concatenate(vt, axis=0)
        return carry

    lax.fori_loop(0, groups, group, 0)

    for j in range(HEAD):
        s_ref[j] = s_ref[j] * jnp.concatenate([ge_ref[j]] * nb, axis=0)


def _rwkv_scan(seqs, s0, row_consts, step_consts, *, seq_len, steps):
    t8, _, p = seqs[0].shape
    n = HEAD
    if seq_len >= SUBLANES:
        groups, live = steps // SUBLANES, SUBLANES
    else:
        groups, live = 1, seq_len
    grid = (p // LANES, t8 // groups)
    in_spec = pl.BlockSpec((groups, n * SUBLANES, LANES), lambda g, t: (t, 0, g))
    out_spec = pl.BlockSpec((groups * live, n, LANES), lambda g, t: (t, 0, g))
    st_spec = pl.BlockSpec((n, n, LANES), lambda g, t: (0, 0, g))
    rc_spec = pl.BlockSpec((n, LANES), lambda g, t: (0, g))
    sc_spec = pl.BlockSpec((n, SUBLANES, LANES), lambda g, t: (0, 0, g))
    kern = functools.partial(_rwkv_scan_kernel, groups=groups, live=live)
    return pl.pallas_call(
        kern, grid=grid,
        in_specs=[in_spec] * 5 + [st_spec] + [rc_spec] * 2 + [sc_spec] * 3,
        out_specs=[out_spec, st_spec],
        out_shape=[jax.ShapeDtypeStruct((seq_len, n, p), F32), jax.ShapeDtypeStruct((n, n, p), F32)],
        scratch_shapes=[pltpu.VMEM((4, groups, n * SUBLANES, LANES), F32), pltpu.VMEM((groups, SUBLANES, LANES), F32),
                        pltpu.VMEM((n, SUBLANES, LANES), F32)],
        compiler_params=_cparams(2), name="rwkv_scan")(*seqs, s0, *row_consts, *step_consts)


def _nh_order(x, axis):
    shape = x.shape
    x = x.reshape(shape[:axis] + (RWKV_HEADS, HEAD) + shape[axis + 1:])
    return jnp.swapaxes(x, axis, axis + 1).reshape(shape)


def _hn_order(x, axis):
    shape = x.shape
    x = x.reshape(shape[:axis] + (HEAD, RWKV_HEADS) + shape[axis + 1:])
    return jnp.swapaxes(x, axis, axis + 1).reshape(shape)


def _to_pairs_kernel(x_ref, o_ref, a_scr):
    nb = x_ref.shape[0]
    for b in range(nb):
        a_scr[b] = x_ref[b].T
    for n in range(HEAD):
        blk = a_scr[:, n * RWKV_HEADS:(n + 1) * RWKV_HEADS, :]
        y = blk.reshape(nb * RWKV_HEADS, blk.shape[2]).T
        o_ref[:, n * SUBLANES:(n + 1) * SUBLANES, :] = y.reshape(y.shape[0] // SUBLANES, SUBLANES, LANES)


def _to_pairs(x, batch, seq_len, tt):
    assert batch * RWKV_HEADS == LANES
    x3 = x.reshape(batch, seq_len, D_MODEL)
    return pl.pallas_call(
        _to_pairs_kernel, grid=(seq_len // tt,),
        in_specs=[pl.BlockSpec((batch, tt, D_MODEL), lambda t: (0, t, 0))],
        out_specs=pl.BlockSpec((tt // SUBLANES, HEAD * SUBLANES, LANES), lambda t: (t, 0, 0)),
        out_shape=jax.ShapeDtypeStruct((seq_len // SUBLANES, HEAD * SUBLANES, LANES), F32),
        scratch_shapes=[pltpu.VMEM((batch, D_MODEL, tt), F32)],
        compiler_params=_cparams(1), name="to_pairs")(x3)


def _from_pairs_kernel(o_ref, x_ref, a_scr):
    nb = x_ref.shape[0]
    for n in range(HEAD):
        blk = o_ref[:, n, :].T
        a_scr[:, n * RWKV_HEADS:(n + 1) * RWKV_HEADS, :] = blk.reshape(nb, RWKV_HEADS, blk.shape[1])
    for b in range(nb):
        x_ref[b] = a_scr[b].T


def _from_pairs(o, batch, seq_len, tt):
    assert batch * RWKV_HEADS == LANES
    out = pl.pallas_call(
        _from_pairs_kernel, grid=(seq_len // tt,),
        in_specs=[pl.BlockSpec((tt, HEAD, LANES), lambda t: (t, 0, 0))],
        out_specs=pl.BlockSpec((batch, tt, D_MODEL), lambda t: (0, t, 0)),
        out_shape=jax.ShapeDtypeStruct((batch, seq_len, D_MODEL), F32),
        scratch_shapes=[pltpu.VMEM((batch, D_MODEL, tt), F32)],
        compiler_params=_cparams(1), name="from_pairs")(o)
    return out.reshape(batch * seq_len, D_MODEL)


def _to_pairs_xla(x, batch, seq_len):
    assert seq_len < SUBLANES
    x = jnp.transpose(x.reshape(batch, seq_len, HEAD, RWKV_HEADS), (2, 1, 0, 3))
    x = x.reshape(HEAD, seq_len, batch * RWKV_HEADS)
    x = jnp.pad(x, ((0, 0), (0, SUBLANES - seq_len), (0, 0)), constant_values=1.0)
    return x.reshape(1, HEAD * SUBLANES, batch * RWKV_HEADS)


def _from_pairs_xla(o, batch, seq_len):
    o = o.reshape(seq_len, HEAD, batch, RWKV_HEADS)
    return jnp.transpose(o, (2, 0, 1, 3)).reshape(batch * seq_len, D_MODEL)


def _head_const_pairs(x, batch):
    x = x.reshape(RWKV_HEADS, HEAD).T
    return jnp.tile(x, (1, batch))


def _count_ge(score, thr):
    return jnp.sum(jnp.where(score >= thr, 1.0, 0.0), axis=-1, keepdims=True)


def _select_topk(score, kpos, k_sel, bisect_steps):
    kf = float(k_sel)
    vis = score > _NEG_INF
    nvis = jnp.sum(jnp.where(vis, 1.0, 0.0), axis=-1, keepdims=True)
    need_sel = nvis > kf
    rowmax = jnp.where(need_sel, jnp.max(score, axis=-1, keepdims=True), 0.0)
    rowmin = jnp.where(need_sel, jnp.min(jnp.where(vis, score, _POS_INF), axis=-1, keepdims=True), 0.0)

    def bis(_, c):
        lo, hi, chi = c
        hfin = jnp.where(hi == _POS_INF, rowmax, hi)
        piv = 0.5 * lo + 0.5 * hfin
        cnt = _count_ge(score, piv)
        ge = cnt >= kf
        return jnp.where(ge, piv, lo), jnp.where(ge, hi, piv), jnp.where(ge, chi, cnt)

    lo, hi, chi = lax.fori_loop(0, bisect_steps, bis,
                                (rowmin, jnp.full_like(rowmin, _POS_INF), jnp.zeros_like(rowmin)))

    def walk_cond(c):
        return jnp.min(c[4]) < 0.5

    def walk(c):
        hi, chi, tau, cgt, done, ceq = c
        pending = done < 0.5
        bmax = jnp.max(jnp.where(score < hi, score, _NEG_INF), axis=-1, keepdims=True)
        cnt = _count_ge(score, bmax)
        fin = jnp.logical_and(cnt >= kf, pending)
        tau = jnp.where(fin, bmax, tau)
        cgt = jnp.where(fin, chi, cgt)
        ceq = jnp.where(fin, cnt - chi, ceq)
        adv = jnp.logical_and(cnt < kf, pending)
        hi = jnp.where(adv, bmax, hi)
        chi = jnp.where(adv, cnt, chi)
        return hi, chi, tau, cgt, jnp.where(fin, 1.0, done), ceq

    done0 = jnp.where(need_sel, 0.0, 1.0)
    neg = jnp.full_like(rowmin, _NEG_INF)
    zero = jnp.zeros_like(rowmin)
    _, _, tau, cgt, _, ceq = lax.while_loop(walk_cond, walk, (hi, chi, neg, zero, done0, zero))

    need = kf - cgt
    excess = jnp.logical_and(need_sel, ceq > need)
    eq = score == tau
    n_keys = score.shape[-1]

    def tie_break(_):
        def body(_, c):
            plo, phi = c
            mid = lax.shift_right_arithmetic(plo + phi, 1)
            cnt = jnp.sum(jnp.where(jnp.logical_and(eq, kpos <= mid), 1.0, 0.0), axis=-1, keepdims=True)
            ge = cnt >= need
            return jnp.where(ge, plo, mid), jnp.where(ge, mid, phi)
        plo0 = jnp.full(tau.shape, -1, jnp.int32)
        phi0 = jnp.full(tau.shape, n_keys - 1, jnp.int32)
        _, phi = lax.fori_loop(0, n_keys.bit_length() + 1, body, (plo0, phi0))
        return jnp.where(excess, phi, n_keys)

    any_excess = jnp.max(jnp.where(excess, 1.0, 0.0)) > 0.0
    pcut = lax.cond(any_excess, tie_break, lambda _: jnp.full(tau.shape, n_keys, jnp.int32), 0)
    pcut = jnp.where(need_sel, pcut, -1)
    return jnp.logical_or(score > tau, jnp.logical_and(eq, kpos <= pcut))


_MASKED = -1e30


def _select_topk_chunked(s_ref, nch, ck, k_sel, nvis, bisect_steps):
    rows = s_ref.shape[0]
    kf = float(k_sel)
    nfold = ck // LANES

    def chunk(c):
        return s_ref[:, pl.ds(pl.multiple_of(c * ck, ck), ck)]

    def fold(x, op):
        out = x[:, 0:LANES]
        for i in range(1, nfold):
            out = op(out, x[:, i * LANES:(i + 1) * LANES])
        return out

    def count(pred):
        def body(c, acc):
            return acc + fold(jnp.where(pred(chunk(c), c), 1.0, 0.0), jnp.add)
        acc = lax.fori_loop(0, nch, body, jnp.zeros((rows, LANES), F32))
        return jnp.sum(acc, axis=-1, keepdims=True)

    def row_max(val):
        def body(c, acc):
            return jnp.maximum(acc, fold(val(chunk(c)), jnp.maximum))
        acc = lax.fori_loop(0, nch, body, jnp.full((rows, LANES), _NEG_INF, F32))
        return jnp.max(acc, axis=-1, keepdims=True)

    need_sel = nvis > kf
    rowmax = jnp.where(need_sel, row_max(lambda x: x), 0.0)
    rowmin = jnp.where(need_sel, -row_max(lambda x: jnp.where(x > _NEG_INF, -x, _NEG_INF)), 0.0)

    def bis(_, c):
        lo, hi, chi = c
        hfin = jnp.where(hi == _POS_INF, rowmax, hi)
        piv = 0.5 * lo + 0.5 * hfin
        cnt = count(lambda x, _c: x >= piv)
        ge = cnt >= kf
        return jnp.where(ge, piv, lo), jnp.where(ge, hi, piv), jnp.where(ge, chi, cnt)

    lo, hi, chi = lax.fori_loop(0, bisect_steps, bis,
                                (rowmin, jnp.full_like(rowmin, _POS_INF), jnp.zeros_like(rowmin)))

    def walk_cond(c):
        return jnp.min(c[4]) < 0.5

    def walk(c):
        hi, chi, tau, cgt, done, ceq = c
        pending = done < 0.5
        bmax = row_max(lambda x: jnp.where(x < hi, x, _NEG_INF))
        cnt = count(lambda x, _c: x >= bmax)
        fin = jnp.logical_and(cnt >= kf, pending)
        tau = jnp.where(fin, bmax, tau)
        cgt = jnp.where(fin, chi, cgt)
        ceq = jnp.where(fin, cnt - chi, ceq)
        adv = jnp.logical_and(cnt < kf, pending)
        hi = jnp.where(adv, bmax, hi)
        chi = jnp.where(adv, cnt, chi)
        return hi, chi, tau, cgt, jnp.where(fin, 1.0, done), ceq

    neg = jnp.full_like(rowmin, _NEG_INF)
    zero = jnp.zeros_like(rowmin)
    _, _, tau, cgt, _, ceq = lax.while_loop(
        walk_cond, walk, (hi, chi, neg, zero, jnp.where(need_sel, 0.0, 1.0), zero))

    need = kf - cgt
    excess = jnp.logical_and(need_sel, ceq > need)
    n_keys = s_ref.shape[1]

    def kpos(c):
        return c * ck + lax.broadcasted_iota(jnp.int32, (rows, ck), 1)

    def tie_break(_):
        def body(_, c):
            plo, phi = c
            mid = lax.shift_right_arithmetic(plo + phi, 1)
            cnt = count(lambda x, cc: jnp.logical_and(x == tau, kpos(cc) <= mid))
            ge = cnt >= need
            return jnp.where(ge, plo, mid), jnp.where(ge, mid, phi)
        plo0 = jnp.full(tau.shape, -1, jnp.int32)
        phi0 = jnp.full(tau.shape, n_keys - 1, jnp.int32)
        _, phi = lax.fori_loop(0, n_keys.bit_length() + 1, body, (plo0, phi0))
        return jnp.where(excess, phi, n_keys)

    any_excess = jnp.max(jnp.where(excess, 1.0, 0.0)) > 0.0
    pcut = lax.cond(any_excess, tie_break, lambda _: jnp.full(tau.shape, n_keys, jnp.int32), 0)
    pcut = jnp.where(need_sel, pcut, -1)

    def write(c, carry):
        x = chunk(c)
        sel = jnp.logical_or(x > tau, jnp.logical_and(x == tau, kpos(c) <= pcut))
        s_ref[:, pl.ds(pl.multiple_of(c * ck, ck), ck)] = jnp.where(sel, 0.0, _MASKED)
        return carry

    lax.fori_loop(0, nch, write, 0)


def _attn_prep_kernel(q_ref, qi_ref, kv_ref, kiw_ref, qg_ref, kg_ref, kn_o, qb_o, qib_o, kvb_o, kib_o):
    w = ATT_KV_HEADS * HEAD
    for h in range(ATT_KV_HEADS):
        hs = slice(h * HEAD, (h + 1) * HEAD)
        x = kv_ref[:, hs]
        kn = x * lax.rsqrt(jnp.mean(x * x, axis=-1, keepdims=True) + NORM_EPS) * kg_ref[...]
        kn_o[:, hs] = kn
        kvb_o[:, hs] = kn.astype(BF16)
        v = kv_ref[:, w + h * HEAD:w + (h + 1) * HEAD].astype(BF16)
        kvb_o[:, w + h * LANES:w + (h + 1) * LANES] = jnp.concatenate([v, jnp.ones_like(v)], axis=1)
    for h in range(ATT_HEADS):
        x = q_ref[:, h * HEAD:(h + 1) * HEAD]
        qn = x * lax.rsqrt(jnp.mean(x * x, axis=-1, keepdims=True) + NORM_EPS) * (qg_ref[...] * HEAD ** -0.5)
        qb_o[h] = qn.astype(BF16)
    qib_o[...] = (qi_ref[...] * HEAD ** -0.5).astype(BF16)
    kib_o[...] = kiw_ref[:, 0:HEAD].astype(BF16)


def _attn_prep(za, qn_g, kn_g, tm):
    m = za.shape[0]
    w = ATT_KV_HEADS * HEAD
    wi = IDX_HEADS * HEAD
    row = lambda width, blk: pl.BlockSpec((tm, width), lambda i, blk=blk: (i, blk))
    gspec = pl.BlockSpec((1, HEAD), lambda i: (0, 0))
    return pl.pallas_call(
        _attn_prep_kernel, grid=(m // tm,),
        in_specs=[row(D_MODEL, ZA_Q // D_MODEL), row(wi, ZA_QI // wi), row(2 * w, ZA_KV // (2 * w)),
                  row(LANES, ZA_KIW // LANES), gspec, gspec],
        out_specs=[row(w, 0), pl.BlockSpec((ATT_HEADS, tm, HEAD), lambda i: (0, i, 0)), row(wi, 0),
                   row(w + ATT_KV_HEADS * LANES, 0), row(HEAD, 0)],
        out_shape=[jax.ShapeDtypeStruct((m, w), F32), jax.ShapeDtypeStruct((ATT_HEADS, m, HEAD), BF16),
                   jax.ShapeDtypeStruct((m, wi), BF16), jax.ShapeDtypeStruct((m, w + ATT_KV_HEADS * LANES), BF16),
                   jax.ShapeDtypeStruct((m, HEAD), BF16)],
        compiler_params=_cparams(1), name="attn_prep")(za, za, za, za, qn_g.reshape(1, HEAD), kn_g.reshape(1, HEAD))


def _bucket_edges():
    max_exact = REL_BUCKETS // 2
    d = np.arange(REL_MAX_DIST + 1)
    df = np.maximum(d, 1).astype(np.float32)
    large = max_exact + (np.log(df / max_exact) / math.log(REL_MAX_DIST / max_exact)
                         * (REL_BUCKETS - max_exact)).astype(np.int32)
    bucket = np.where(d < max_exact, d, np.minimum(large, REL_BUCKETS - 1))
    return [int(np.argmax(bucket >= b)) for b in range(REL_BUCKETS)]


_BUCKET_EDGES = _bucket_edges()


def _rel_bias_lookup(dist, value_of_bucket):
    bias = value_of_bucket(REL_BUCKETS - 1)
    for b in range(REL_BUCKETS - 2, -1, -1):
        bias = jnp.where(dist < _BUCKET_EDGES[b + 1], value_of_bucket(b), bias)
    return bias


def _attn_prompt_kernel(rb_ref, q_ref, qi_ref, kiwq_ref, g_ref, kvb_ref, kib_ref, o_ref, s_scr, tbd_ref,
                        l_scr, mx_scr, acc_scr, *, tq, k_sel, bisect_steps):
    qt = pl.program_id(0)
    q0 = qt * tq
    ck = tq
    rg_rows = LANES
    n_rg = tq // rg_rows
    w = ATT_KV_HEADS * HEAD

    @pl.when(jnp.logical_and(pl.program_id(1) == 0, qt == 0))
    def _():
        rr = lax.broadcasted_iota(jnp.int32, (tq, tq), 0) - lax.broadcasted_iota(jnp.int32, (tq, tq), 1)
        for h in range(ATT_HEADS):
            far = rb_ref[REL_BUCKETS - 1, h]
            tbd_ref[h, 0] = _rel_bias_lookup(rr, lambda b: rb_ref[b, h]) - far
            tbd_ref[h, 1] = _rel_bias_lookup(rr + tq, lambda b: rb_ref[b, h]) - far

    def chunk_start(c):
        return pl.multiple_of(c * ck, ck)

    wcol = kiwq_ref[:, HEAD:HEAD + IDX_HEADS] * (IDX_HEADS ** -0.5)
    qi_h = [[qi_ref[rg * rg_rows:(rg + 1) * rg_rows, h * HEAD:(h + 1) * HEAD] for h in range(IDX_HEADS)]
            for rg in range(n_rg)]
    w_h = [[wcol[rg * rg_rows:(rg + 1) * rg_rows, h:h + 1] for h in range(IDX_HEADS)] for rg in range(n_rg)]

    def score_chunk(c, carry):
        k0 = chunk_start(c)
        kc = kib_ref[pl.ds(k0, ck), :]
        kpos = k0 + lax.broadcasted_iota(jnp.int32, (rg_rows, ck), 1)
        for rg in range(n_rg):
            acc = jnp.zeros((rg_rows, ck), F32)
            for h in range(IDX_HEADS):
                acc = acc + jnp.maximum(_dot_nt(qi_h[rg][h], kc), 0.0) * w_h[rg][h]
            qpos = q0 + rg * rg_rows + lax.broadcasted_iota(jnp.int32, (rg_rows, ck), 0)
            s_scr[rg * rg_rows:(rg + 1) * rg_rows, pl.ds(k0, ck)] = jnp.where(kpos <= qpos, acc, _NEG_INF)
        return carry

    nch = qt + 1
    lax.fori_loop(0, nch, score_chunk, 0)

    nvis = (q0 + 1 + lax.broadcasted_iota(jnp.int32, (tq, 1), 0)).astype(F32)
    _select_topk_chunked(s_scr, nch, ck, k_sel, nvis, bisect_steps)

    g_rows = ATT_GROUP * tq
    n_far = jnp.maximum(qt - 1, 0)
    for kvh in range(ATT_KV_HEADS):
        gs = slice(kvh * ATT_GROUP, (kvh + 1) * ATT_GROUP)
        qg = q_ref[gs, :, :].reshape(g_rows, HEAD)
        mx_scr[...] = jnp.full(mx_scr.shape, _MASKED, F32)
        acc_scr[...] = jnp.zeros(acc_scr.shape, F32)

        def logits_chunk(c, carry, near, qg=qg, gs=gs, kvh=kvh):
            k0 = chunk_start(c)
            kc = kvb_ref[pl.ds(k0, ck), kvh * HEAD:(kvh + 1) * HEAD]
            s3 = _dot_nt(qg, kc).reshape(ATT_GROUP, tq, ck) + s_scr[:, pl.ds(k0, ck)][None]
            if near:
                s3 = s3 + tbd_ref[gs, qt - c, :, :]
            s = s3.reshape(g_rows, ck)
            l_scr[:, pl.ds(k0, ck)] = s
            m = mx_scr[...]
            for i in range(ck // LANES):
                m = jnp.maximum(m, s[:, i * LANES:(i + 1) * LANES])
            mx_scr[...] = m
            return carry

        lax.fori_loop(0, n_far, functools.partial(logits_chunk, near=False), 0)
        lax.fori_loop(n_far, nch, functools.partial(logits_chunk, near=True), 0)
        mx_scr[...] = jnp.broadcast_to(jnp.max(mx_scr[...], axis=-1, keepdims=True), mx_scr.shape)

        def pv_chunk(c, carry, kvh=kvh):
            k0 = chunk_start(c)
            vx = kvb_ref[pl.ds(k0, ck), w + kvh * LANES:w + (kvh + 1) * LANES]
            m = mx_scr[...]
            p = jnp.concatenate([jnp.exp(l_scr[:, pl.ds(k0 + i * LANES, LANES)] - m)
                                 for i in range(ck // LANES)], axis=1).astype(BF16)
            acc_scr[...] += jnp.dot(p, vx, preferred_element_type=F32)
            return carry

        lax.fori_loop(0, nch, pv_chunk, 0)

        acc = acc_scr[...]
        o = acc[:, 0:HEAD] / acc[:, HEAD:HEAD + 1]
        for pp in range(ATT_GROUP // 2):
            cols = slice((kvh * ATT_GROUP // 2 + pp) * LANES, (kvh * ATT_GROUP // 2 + pp + 1) * LANES)
            gh = g_ref[:, cols]
            pair = jnp.concatenate([o[2 * pp * tq:(2 * pp + 1) * tq], o[(2 * pp + 1) * tq:(2 * pp + 2) * tq]], axis=1)
            o_ref[:, cols] = pair * (gh * _sigmoid(gh))


def _attn_prompt(za, kn, qb, qib, kvb, kib, *, rel_bias, batch, seq_len, tq, k_sel):
    m = za.shape[0]
    nq = seq_len // tq
    w = ATT_KV_HEADS * HEAD
    wi = IDX_HEADS * HEAD
    assert REL_MAX_DIST <= tq and tq % LANES == 0
    row_map = lambda blk: (lambda t, b, blk=blk: (b * nq + t, blk))
    key_map = lambda t, b: (b, 0)
    kern = functools.partial(_attn_prompt_kernel, tq=tq, k_sel=k_sel, bisect_steps=14)
    return pl.pallas_call(
        kern, grid=(nq, batch),
        in_specs=[
            pl.BlockSpec(memory_space=pltpu.SMEM),
            pl.BlockSpec((ATT_HEADS, tq, HEAD), lambda t, b: (0, b * nq + t, 0)),
            pl.BlockSpec((tq, wi), row_map(0)),
            pl.BlockSpec((tq, LANES), row_map(ZA_KIW // LANES)),
            pl.BlockSpec((tq, D_MODEL), row_map(ZA_GATT // D_MODEL)),
            pl.BlockSpec((seq_len, w + ATT_KV_HEADS * LANES), key_map),
            pl.BlockSpec((seq_len, HEAD), key_map),
        ],
        out_specs=pl.BlockSpec((tq, D_MODEL), row_map(0)),
        out_shape=jax.ShapeDtypeStruct((m, D_MODEL), F32),
        scratch_shapes=[pltpu.VMEM((tq, seq_len), F32), pltpu.VMEM((ATT_HEADS, 2, tq, tq), F32),
                        pltpu.VMEM((ATT_GROUP * tq, seq_len), F32), pltpu.VMEM((ATT_GROUP * tq, LANES), F32),
                        pltpu.VMEM((ATT_GROUP * tq, LANES), F32)],
        compiler_params=_cparams(2), name="attn_prompt")(rel_bias, qb, qib, za, za, kvb, kib)


def _sample_score_kernel(pt_ref, q_ref, w_ref, kiw_new_ref, *rest, n_pages, page, dec_seq):
    page_refs = rest[:n_pages]
    o_ref = rest[n_pages]
    past = n_pages * page
    kidx = jnp.concatenate([r[...] for r in page_refs], axis=0).astype(BF16)
    new = kiw_new_ref[:, 0:HEAD]
    new = jnp.concatenate([new, jnp.zeros((LANES - dec_seq, HEAD), F32)], axis=0).astype(BF16)
    q = (q_ref[...] * (HEAD ** -0.5)).astype(BF16)
    lg = jnp.concatenate([_dot_nt(q, kidx), _dot_nt(q, new)], axis=1)
    wr = jnp.maximum(lg, 0.0) * (w_ref[...] * (IDX_HEADS ** -0.5))
    n_keys = past + LANES
    sc = jnp.sum(wr.reshape(dec_seq, IDX_HEADS, n_keys), axis=1)
    kpos = lax.broadcasted_iota(jnp.int32, (dec_seq, n_keys), 1)
    tpos = lax.broadcasted_iota(jnp.int32, (dec_seq, n_keys), 0)
    o_ref[...] = jnp.where(kpos <= past + tpos, sc, _NEG_INF)


def _sample_select_kernel(s_ref, o_ref, *, k_sel, bisect_steps):
    score = s_ref[...]
    kpos = lax.broadcasted_iota(jnp.int32, score.shape, 1)
    sel = _select_topk(score, kpos, k_sel, bisect_steps)
    o_ref[...] = jnp.where(sel, 0.0, _NEG_INF)


def _sample_attn_kernel(pt_ref, q_ref, g_ref, kn_new_ref, kv_new_ref, mask_ref, rb_rows_ref, qg_ref, *rest,
                        n_pages, page, dec_seq):
    k_pages = rest[:n_pages]
    v_pages = rest[n_pages:2 * n_pages]
    o_ref = rest[2 * n_pages]
    bias_ref = rest[2 * n_pages + 1]
    w = ATT_KV_HEADS * HEAD
    rows = dec_seq * ATT_HEADS
    n_keys = n_pages * page + LANES

    @pl.when(pl.program_id(0) == 0)
    def _():
        t_row = lax.div(lax.broadcasted_iota(jnp.int32, (rows, n_keys), 0), ATT_HEADS)
        dist = n_pages * page + t_row - lax.broadcasted_iota(jnp.int32, (rows, n_keys), 1)
        bias_ref[...] = _rel_bias_lookup(dist, lambda b: rb_rows_ref[:, b:b + 1])

    pad = jnp.zeros((LANES - dec_seq, w), F32)
    k_all = jnp.concatenate([r[...] for r in k_pages] + [kn_new_ref[...], pad], axis=0).astype(BF16)
    v_all = jnp.concatenate([r[...] for r in v_pages] + [kv_new_ref[:, w:2 * w], pad], axis=0).astype(BF16)
    q = q_ref[...]
    q = q * lax.rsqrt(jnp.mean(q * q, axis=-1, keepdims=True) + NORM_EPS) * qg_ref[...]
    qb = q.astype(BF16)
    head = lax.rem(lax.broadcasted_iota(jnp.int32, (rows, 1), 0), ATT_HEADS)
    first = head < ATT_GROUP
    lg = jnp.where(first, _dot_nt(qb, k_all[:, 0:HEAD]), _dot_nt(qb, k_all[:, HEAD:2 * HEAD]))
    mask = jnp.broadcast_to(mask_ref[...][:, None, :], (dec_seq, ATT_HEADS, n_keys)).reshape(rows, n_keys)
    s = lg * (HEAD ** -0.5) + bias_ref[...] + mask
    p = jnp.exp(s - jnp.max(s, axis=-1, keepdims=True))
    l = jnp.sum(p, axis=-1, keepdims=True)
    pb = p.astype(BF16)
    o = jnp.where(first, jnp.dot(pb, v_all[:, 0:HEAD], preferred_element_type=F32),
                  jnp.dot(pb, v_all[:, HEAD:2 * HEAD], preferred_element_type=F32)) / l
    g = g_ref[...]
    o_ref[...] = o * (g * _sigmoid(g))


def _attn_sample(za, kn, qb, qib, kvb, kib, *, cache_k, cache_v, cache_kidx, page_table, rel_bias, qn_g,
                 dec_seq, k_sel):
    nb, n_pages = page_table.shape
    n_phys, page = cache_k.shape[0], cache_k.shape[1]
    past = n_pages * page
    n_keys = past + LANES
    w = ATT_KV_HEADS * HEAD
    ck = cache_k.reshape(n_phys, page, w)
    cv = cache_v.reshape(n_phys, page, w)
    ci = cache_kidx.reshape(n_phys, page, HEAD)
    za3 = za.reshape(nb, dec_seq, ZA_COLS)
    kn3 = kn.reshape(nb, dec_seq, w)
    qi = za[:, ZA_QI:ZA_QI + IDX_HEADS * HEAD].reshape(nb, dec_seq * IDX_HEADS, HEAD)
    wi = za[:, ZA_KIW + HEAD:ZA_KIW + HEAD + IDX_HEADS].reshape(nb, dec_seq * IDX_HEADS, 1)
    qa = za[:, ZA_Q:ZA_Q + D_MODEL].reshape(nb, dec_seq * ATT_HEADS, HEAD)
    ga = za[:, ZA_GATT:ZA_GATT + D_MODEL].reshape(nb, dec_seq * ATT_HEADS, HEAD)

    def page_specs(width):
        return [pl.BlockSpec((None, page, width), lambda b, pt, j=j: (pt[b, j], 0, 0)) for j in range(n_pages)]

    per_b = lambda r, c: pl.BlockSpec((None, r, c), lambda b, pt: (b, 0, 0))
    kiw_new = pl.BlockSpec((None, dec_seq, LANES), lambda b, pt: (b, 0, ZA_KIW // LANES))

    scores = pl.pallas_call(
        functools.partial(_sample_score_kernel, n_pages=n_pages, page=page, dec_seq=dec_seq),
        grid_spec=pltpu.PrefetchScalarGridSpec(
            num_scalar_prefetch=1, grid=(nb,),
            in_specs=[per_b(dec_seq * IDX_HEADS, HEAD), per_b(dec_seq * IDX_HEADS, 1), kiw_new] + page_specs(HEAD),
            out_specs=per_b(dec_seq, n_keys)),
        out_shape=jax.ShapeDtypeStruct((nb, dec_seq, n_keys), F32),
        compiler_params=_cparams(1), name="sample_scores")(page_table, qi, wi, za3, *([ci] * n_pages))

    rows = nb * dec_seq
    tr = min(rows, 128)
    mask = pl.pallas_call(
        functools.partial(_sample_select_kernel, k_sel=k_sel, bisect_steps=14),
        grid=(rows // tr,),
        in_specs=[pl.BlockSpec((tr, n_keys), lambda i: (i, 0))],
        out_specs=pl.BlockSpec((tr, n_keys), lambda i: (i, 0)),
        out_shape=jax.ShapeDtypeStruct((rows, n_keys), F32),
        compiler_params=_cparams(1), name="sample_select")(scores.reshape(rows, n_keys))

    rb_rows = jnp.tile(rel_bias.T, (dec_seq, 1))

    const = lambda shape: pl.BlockSpec(shape, lambda b, pt: (0,) * len(shape))
    kv_new = pl.BlockSpec((None, dec_seq, 2 * w), lambda b, pt: (b, 0, ZA_KV // (2 * w)))
    out = pl.pallas_call(
        functools.partial(_sample_attn_kernel, n_pages=n_pages, page=page, dec_seq=dec_seq),
        grid_spec=pltpu.PrefetchScalarGridSpec(
            num_scalar_prefetch=1, grid=(nb,),
            in_specs=[per_b(dec_seq * ATT_HEADS, HEAD), per_b(dec_seq * ATT_HEADS, HEAD), per_b(dec_seq, w),
                      kv_new, per_b(dec_seq, n_keys), const((dec_seq * ATT_HEADS, REL_BUCKETS)), const((1, HEAD))]
            + page_specs(w) + page_specs(w),
            out_specs=per_b(dec_seq * ATT_HEADS, HEAD),
            scratch_shapes=[pltpu.VMEM((dec_seq * ATT_HEADS, n_keys), F32)]),
        out_shape=jax.ShapeDtypeStruct((nb, dec_seq * ATT_HEADS, HEAD), F32),
        compiler_params=_cparams(1), name="sample_attn")(
            page_table, qa, ga, kn3, za3, mask.reshape(nb, dec_seq, n_keys), rb_rows, qn_g.reshape(1, HEAD),
            *([ck] * n_pages), *([cv] * n_pages))
    return out.reshape(rows, D_MODEL)


def _merge_kernel(x_ref, oa_ref, sg_ref, ob_ref, ga_ref, gb_ref, wpa_ref, wpb_ref, wo_ref, y_ref):
    oa = (oa_ref[...] * sg_ref[...]).astype(BF16)
    pa = jnp.dot(oa, wpa_ref[...], preferred_element_type=F32)
    pb = jnp.dot(ob_ref[...].astype(BF16), wpb_ref[...], preferred_element_type=F32)
    merged = _sigmoid(ga_ref[...]) * pa + _sigmoid(gb_ref[...]) * pb
    y_ref[...] = x_ref[...] + jnp.dot(merged.astype(BF16), wo_ref[...], preferred_element_type=F32)


def _merge(x, oa, sg, ob, za, w_pa, w_pb, w_out, tm):
    m, d = x.shape
    row = pl.BlockSpec((tm, d), lambda i: (i, 0))
    wsp = pl.BlockSpec((d, d), lambda i: (0, 0))
    return pl.pallas_call(
        _merge_kernel, grid=(m // tm,),
        in_specs=[row, row, row, row,
                  pl.BlockSpec((tm, d), lambda i: (i, ZA_GA // d)), pl.BlockSpec((tm, d), lambda i: (i, ZA_GB // d)),
                  wsp, wsp, wsp],
        out_specs=row,
        out_shape=jax.ShapeDtypeStruct((m, d), F32),
        compiler_params=_cparams(1), name="merge")(x, oa, sg, ob, za, za, w_pa, w_pb, w_out)


def _layer(x, shift, s0, params, *, batch, seq_len, attend):
    (norm_g, w_r, w_a, mu, w0, w2, a0, a2, k_k, k_a, r_k, lnx_g, lnx_b, qn_g, kn_g, w_pa, w_pb, w_out) = params
    m = batch * seq_len
    tm = min(m, 512)
    xn = _rmsnorm(x, norm_g, tm)
    zr = _matmul(xn, w_r, tm, RWKV_COLS // 3, "inproj_rwkv")
    za = _matmul(xn, w_a, tm, ZA_COLS // 4, "inproj_attn")

    tile = min(m, 256)
    r, w, k, v, a, sg = _rwkv_prep(zr, shift, mu, w0, w2, a0, a2, batch=batch, seq_len=seq_len, tile=tile)
    relayout_tile = 128
    in_kernel_relayout = seq_len % relayout_tile == 0 and batch * RWKV_HEADS == LANES
    if in_kernel_relayout:
        seqs = [_to_pairs(t, batch, seq_len, relayout_tile) for t in (r, w, k, v, a)]
    else:
        seqs = [_to_pairs_xla(t, batch, seq_len) for t in (r, w, k, v, a)]
    row_consts = [_head_const_pairs(t, batch) for t in (lnx_g, lnx_b)]
    step_consts = [jnp.broadcast_to(_head_const_pairs(t, batch)[:, None, :], (HEAD, SUBLANES, batch * RWKV_HEADS))
                   for t in (r_k.reshape(-1), k_k, k_a)]
    o_t, s_t = _rwkv_scan(seqs, s0, row_consts, step_consts, seq_len=seq_len, steps=min(seq_len, 32))
    if in_kernel_relayout:
        oa = _from_pairs(o_t, batch, seq_len, relayout_tile)
    else:
        oa = _from_pairs_xla(o_t, batch, seq_len)

    kn, qb, qib, kvb, kib = _attn_prep(za, qn_g, kn_g, tm)
    ob = attend(za, kn, qb, qib, kvb, kib)

    y = _merge(x, oa, sg, ob, za, w_pa, w_pb, w_out, min(m, 256))
    return y, zr, za, kn, s_t


def _state_to_pairs(s):
    b, h, n, _ = s.shape
    return jnp.transpose(s, (3, 2, 0, 1)).reshape(n, n, b * h)


def _state_from_pairs(s, batch):
    n = s.shape[0]
    return jnp.transpose(s.reshape(n, n, batch, RWKV_HEADS), (2, 3, 1, 0))


def kernel(x_prompt, x_sample, cache_k, cache_v, cache_kidx, state_wkv, state_shift, page_table, norm_g, w_in,
           shift_mu, w0, w2, a0, a2, k_k, k_a, r_k, lnx_g, lnx_b, q_norm_g, k_norm_g, rel_bias, w_pa, w_pb, w_out):
    bsz, seq, d = x_prompt.shape
    dec_bsz, dec_seq, _ = x_sample.shape
    depth = w_in.shape[0]
    assert depth == 1 and d == D_MODEL
    past_len = page_table.shape[1] * cache_k.shape[2]
    topk_p = min(TOPK_MAX, seq // 4)
    topk_s = min(TOPK_MAX, (past_len + dec_seq) // 4)
    l = 0

    wl = w_in[l]
    c0 = RWKV_COLS
    q_w, kv_w, qi_w = wl[:, c0:c0 + 1024], wl[:, c0 + 1024:c0 + 1280], wl[:, c0 + 1280:c0 + 1792]
    kiw_w = wl[:, c0 + 1792:c0 + 1864]
    rest_w = wl[:, c0 + 1864:]
    zpad = lambda n: jnp.zeros((d, n), wl.dtype)
    w_a = jnp.concatenate([q_w, qi_w, kv_w, kiw_w, zpad(LANES - kiw_w.shape[1]), zpad(LANES), rest_w],
                          axis=1).astype(BF16)
    assert w_a.shape[1] == ZA_COLS
    def rwkv_cols(x, fn):
        lead = x.shape[:-1]
        main = fn(x[..., :4 * d].reshape(lead + (4, d)), len(lead) + 1).reshape(lead + (4 * d,))
        return jnp.concatenate([main, x[..., 4 * d:]], axis=-1)

    w_r = rwkv_cols(wl[:, :c0], _nh_order).astype(BF16)
    params = (norm_g[l], w_r, w_a, rwkv_cols(shift_mu[l], _nh_order), _nh_order(w0[l], 0), _nh_order(w2[l], 1),
              _nh_order(a0[l], 0), _nh_order(a2[l], 1), k_k[l], k_a[l], r_k[l],
              lnx_g[l], lnx_b[l], q_norm_g[l], k_norm_g[l],
              _nh_order(w_pa[l], 0).astype(BF16), w_pb[l].astype(BF16), w_out[l].astype(BF16))

    xp = x_prompt.reshape(bsz * seq, d)
    attend_p = functools.partial(_attn_prompt, rel_bias=rel_bias, batch=bsz, seq_len=seq,
                                 tq=min(seq, 256), k_sel=topk_p)
    yp, zr_p, za_p, kn_p, st_p = _layer(
        xp, jnp.zeros((bsz, 1, RWKV_COLS), F32), jnp.zeros((HEAD, HEAD, bsz * RWKV_HEADS), F32), params,
        batch=bsz, seq_len=seq, attend=attend_p)

    xs = x_sample.reshape(dec_bsz * dec_seq, d)
    attend_s = functools.partial(_attn_sample, cache_k=cache_k[l], cache_v=cache_v[l], cache_kidx=cache_kidx[l],
                                 page_table=page_table, rel_bias=rel_bias, qn_g=q_norm_g[l], dec_seq=dec_seq,
                                 k_sel=topk_s)
    shift_rows = jnp.repeat(rwkv_cols(state_shift[l], _nh_order), dec_seq, axis=0)
    ys, zr_s, za_s, kn_s, st_s = _layer(
        xs, shift_rows, _state_to_pairs(state_wkv[l]), params, batch=dec_bsz, seq_len=dec_seq, attend=attend_s)

    w = ATT_KV_HEADS * HEAD

    def pack(y, zr, za, kn, st, b, t):
        v = za[:, ZA_KV + w:ZA_KV + 2 * w]
        kidx = za[:, ZA_KIW:ZA_KIW + HEAD]
        return (y.reshape(b, t, d),
                kn.reshape(1, b, t, ATT_KV_HEADS, HEAD), v.reshape(1, b, t, ATT_KV_HEADS, HEAD),
                kidx.reshape(1, b, t, HEAD), _state_from_pairs(st, b)[None],
                rwkv_cols(zr.reshape(b, t, RWKV_COLS)[:, -1], _hn_order)[None])

    p = pack(yp, zr_p, za_p, kn_p, st_p, bsz, seq)
    s = pack(ys, zr_s, za_s, kn_s, st_s, dec_bsz, dec_seq)
    return (p[0], s[0]) + p[1:] + s[1:]
```

```python
import functools
import math

import numpy as np
import jax
import jax.numpy as jnp
from jax import lax
from jax.experimental import pallas as pl
from jax.experimental.pallas import tpu as pltpu

F32 = jnp.float32
BF16 = jnp.bfloat16

D_MODEL = 1024
HEAD = 64
RWKV_HEADS = D_MODEL // HEAD
LORA = 64
LNX_EPS = 64e-5
ATT_HEADS = D_MODEL // HEAD
ATT_KV_HEADS = 2
ATT_GROUP = ATT_HEADS // ATT_KV_HEADS
IDX_HEADS = 8
TOPK_MAX = 256
REL_BUCKETS = 32
REL_MAX_DIST = 128
NORM_EPS = 1e-6
RWKV_COLS = 4 * D_MODEL + 2 * LORA

LANES = 128
SUBLANES = 8
VMEM_LIMIT_BYTES = 56 * 1024 * 1024

ZA_Q = 0
ZA_QI = 1024
ZA_KV = 1536
ZA_KIW = 1792
ZA_GATT = 2048
ZA_GA = 3072
ZA_GB = 4096
ZA_COLS = 5120

_NEG_INF = float("-inf")
_POS_INF = float("inf")


def _cparams(n_axes):
    return pltpu.CompilerParams(dimension_semantics=("arbitrary",) * n_axes,
                                vmem_limit_bytes=VMEM_LIMIT_BYTES)


def _sigmoid(x):
    return 1.0 / (1.0 + jnp.exp(-x))


def _dot_nt(a, b):
    return lax.dot_general(a, b, (((1,), (1,)), ((), ())), preferred_element_type=F32)


def _rmsnorm_kernel(x_ref, g_ref, o_ref):
    x = x_ref[...]
    ms = jnp.mean(x * x, axis=-1, keepdims=True)
    o_ref[...] = (x * lax.rsqrt(ms + NORM_EPS) * g_ref[...]).astype(o_ref.dtype)


def _rmsnorm(x, g, tm):
    m, d = x.shape
    return pl.pallas_call(
        _rmsnorm_kernel, grid=(m // tm,),
        in_specs=[pl.BlockSpec((tm, d), lambda i: (i, 0)), pl.BlockSpec((1, d), lambda i: (0, 0))],
        out_specs=pl.BlockSpec((tm, d), lambda i: (i, 0)),
        out_shape=jax.ShapeDtypeStruct((m, d), BF16),
        compiler_params=_cparams(1), name="rmsnorm")(x, g.reshape(1, d))


def _mm_kernel(x_ref, w_ref, o_ref):
    o_ref[...] = jnp.dot(x_ref[...], w_ref[...], preferred_element_type=F32)


def _matmul(x, w, tm, tn, name):
    m, k = x.shape
    n = w.shape[1]
    return pl.pallas_call(
        _mm_kernel, grid=(m // tm, n // tn),
        in_specs=[pl.BlockSpec((tm, k), lambda i, j: (i, 0)), pl.BlockSpec((k, tn), lambda i, j: (0, j))],
        out_specs=pl.BlockSpec((tm, tn), lambda i, j: (i, j)),
        out_shape=jax.ShapeDtypeStruct((m, n), F32),
        compiler_params=_cparams(2), name=name)(x, w)


def _rwkv_prep_kernel(z_ref, prev_ref, shift_ref, mu_ref, w0_ref, w2_ref, a0_ref, a2_ref,
                      r_o, w_o, k_o, v_o, a_o, g_o, *, seq_len, tile):
    z = z_ref[...]
    rows = lax.broadcasted_iota(jnp.int32, z.shape, 0)
    rolled = pltpu.roll(z, 1, 0)
    if seq_len >= tile:
        first = jnp.where(pl.program_id(1) == 0, shift_ref[...], prev_ref[SUBLANES - 1:SUBLANES, :])
        prev = jnp.where(rows == 0, first, rolled)
    else:
        prev = jnp.where(lax.rem(rows, seq_len) == 0, shift_ref[...], rolled)
    zs = z + (prev - z) * mu_ref[...]
    d = D_MODEL
    r = zs[:, 0:d]
    k = zs[:, d:2 * d]
    v = zs[:, 2 * d:3 * d]
    g = zs[:, 3 * d:4 * d]
    wd = zs[:, 4 * d:4 * d + LORA]
    ad = zs[:, 4 * d + LORA:4 * d + 2 * LORA]
    wl = w0_ref[...] + jnp.dot(jnp.tanh(wd).astype(BF16), w2_ref[...], preferred_element_type=F32)
    decay = jnp.exp(-_sigmoid(wl) * math.exp(-0.5))
    a = _sigmoid(a0_ref[...] + jnp.dot(ad.astype(BF16), a2_ref[...], preferred_element_type=F32))
    r_o[...] = r
    w_o[...] = decay
    k_o[...] = k
    v_o[...] = v
    a_o[...] = a
    g_o[...] = g * _sigmoid(g)


def _rwkv_prep(zr, shift, mu, w0, w2, a0, a2, *, batch, seq_len, tile):
    m, c = zr.shape
    d = D_MODEL
    row = lambda x: x.reshape(1, -1)
    consts = [row(mu), row(w0), w2.astype(BF16), row(a0), a2.astype(BF16)]
    const_specs = [pl.BlockSpec(x.shape, lambda *_: (0, 0)) for x in consts]
    if seq_len >= tile:
        nt = seq_len // tile
        grid = (batch, nt)
        zmap = lambda b, t: (b * nt + t, 0)
        pmap = lambda b, t: (jnp.maximum((b * seq_len + t * tile) // SUBLANES - 1, 0), 0)
        shift_spec = pl.BlockSpec((None, 1, c), lambda b, t: (b, 0, 0))
    else:
        grid = (1, m // tile)
        zmap = lambda b, t: (t, 0)
        pmap = lambda b, t: (0, 0)
        shift_spec = pl.BlockSpec((tile, c), zmap)
    out_spec = pl.BlockSpec((tile, d), zmap)
    kern = functools.partial(_rwkv_prep_kernel, seq_len=seq_len, tile=tile)
    return pl.pallas_call(
        kern, grid=grid,
        in_specs=[pl.BlockSpec((tile, c), zmap), pl.BlockSpec((SUBLANES, c), pmap), shift_spec] + const_specs,
        out_specs=[out_spec] * 6,
        out_shape=[jax.ShapeDtypeStruct((m, d), F32)] * 6,
        compiler_params=_cparams(2), name="rwkv_prep")(zr, zr, shift, *consts)


def _rwkv_scan_kernel(r_in, w_in, k_in, v_in, a_in, s0_ref, lg_ref, lb_ref, rk_ref, kkc_ref, kac_ref,
                      o_ref, s_ref, vec_ref, bf_ref, ge_ref, *, groups, live):
    nb = HEAD // SUBLANES

    @pl.when(pl.program_id(1) == 0)
    def _():
        s_ref[...] = s0_ref[...]

    shape4 = (groups, HEAD, SUBLANES, LANES)
    a = a_in[...].reshape(shape4)
    kraw = k_in[...].reshape(shape4)
    r_all = r_in[...].reshape(shape4)
    kkraw = kraw * kkc_ref[...][None]
    n2 = jnp.sum(kkraw * kkraw, axis=1, keepdims=True)
    kk = kkraw / jnp.maximum(jnp.sqrt(n2), 1e-12)
    kmod = kraw * (1.0 + (a - 1.0) * kac_ref[...][None])
    bf_ref[...] = jnp.sum(r_all * kmod * rk_ref[...][None], axis=1)
    row = lax.broadcasted_iota(jnp.int32, (HEAD, SUBLANES, LANES), 1)
    g_end = jnp.ones((HEAD, SUBLANES, LANES), F32)
    for q in range(groups):
        g = w_in[q].reshape(HEAD, SUBLANES, LANES)
        for sh in (1, 2, 4):
            g = g * jnp.where(row >= sh, pltpu.roll(g, sh, 1), 1.0)
        g = g * g_end
        g_prev = jnp.where(row >= 1, pltpu.roll(g, 1, 1), g_end)
        g_inv = 1.0 / g
        flat = (HEAD * SUBLANES, LANES)
        vec_ref[0, q] = (-kk[q] * g_prev).reshape(flat)
        vec_ref[1, q] = (kk[q] * a[q] * g_inv).reshape(flat)
        vec_ref[2, q] = (kmod[q] * g_inv).reshape(flat)
        vec_ref[3, q] = (r_all[q] * g).reshape(flat)
        last = live - 1 if q == groups - 1 else SUBLANES - 1
        g_end = jnp.broadcast_to(g[:, last:last + 1, :], (HEAD, SUBLANES, LANES))
    ge_ref[...] = g_end

    def group(q, carry):
        for r in range(live):
            def row_of(idx, j):
                return jnp.broadcast_to(vec_ref[idx, q, pl.ds(j * SUBLANES + r, 1), :], (SUBLANES, LANES))

            sa = [jnp.zeros((SUBLANES, LANES), F32) for _ in range(nb)]
            for j in range(HEAD):
                nk = row_of(0, j)
                for ib in range(nb):
                    sa[ib] = sa[ib] + s_ref[j, ib * SUBLANES:(ib + 1) * SUBLANES, :] * nk
            vt = [v_in[q, pl.ds(ib * SUBLANES * SUBLANES + r, SUBLANES, stride=SUBLANES), :] for ib in range(nb)]
            out = [jnp.zeros((SUBLANES, LANES), F32) for _ in range(nb)]
            for j in range(HEAD):
                kaj = row_of(1, j)
                kj = row_of(2, j)
                rj = row_of(3, j)
                for ib in range(nb):
                    sl = slice(ib * SUBLANES, (ib + 1) * SUBLANES)
                    sn = s_ref[j, sl, :] + sa[ib] * kaj + vt[ib] * kj
                    s_ref[j, sl, :] = sn
                    out[ib] = out[ib] + sn * rj
            o = jnp.concatenate(out, axis=0)
            mean = jnp.mean(o, axis=0, keepdims=True)
            dev = o - mean
            var = jnp.mean(dev * dev, axis=0, keepdims=True)
            y = dev * lax.rsqrt(var + LNX_EPS) * lg_ref[...] + lb_ref[...]
            o_ref[q * live + r] = y + bf_ref[q, pl.ds(r, 1), :] * jnp.concatenate(vt, axis=0)
        return carry

    lax.fori_loop(0, groups, group, 0)

    for j in range(HEAD):
        s_ref[j] = s_ref[j] * jnp.concatenate([ge_ref[j]] * nb, axis=0)


def _rwkv_scan(seqs, s0, row_consts, step_consts, *, seq_len, steps):
    t8, _, p = seqs[0].shape
    n = HEAD
    if seq_len >= SUBLANES:
        groups, live = steps // SUBLANES, SUBLANES
    else:
        groups, live = 1, seq_len
    grid = (p // LANES, t8 // groups)
    in_spec = pl.BlockSpec((groups, n * SUBLANES, LANES), lambda g, t: (t, 0, g))
    out_spec = pl.BlockSpec((groups * live, n, LANES), lambda g, t: (t, 0, g))
    st_spec = pl.BlockSpec((n, n, LANES), lambda g, t: (0, 0, g))
    rc_spec = pl.BlockSpec((n, LANES), lambda g, t: (0, g))
    sc_spec = pl.BlockSpec((n, SUBLANES, LANES), lambda g, t: (0, 0, g))
    kern = functools.partial(_rwkv_scan_kernel, groups=groups, live=live)
    return pl.pallas_call(
        kern, grid=grid,
        in_specs=[in_spec] * 5 + [st_spec] + [rc_spec] * 2 + [sc_spec] * 3,
        out_specs=[out_spec, st_spec],
        out_shape=[jax.ShapeDtypeStruct((seq_len, n, p), F32), jax.ShapeDtypeStruct((n, n, p), F32)],
        scratch_shapes=[pltpu.VMEM((4, groups, n * SUBLANES, LANES), F32), pltpu.VMEM((groups, SUBLANES, LANES), F32),
                        pltpu.VMEM((n, SUBLANES, LANES), F32)],
        compiler_params=_cparams(2), name="rwkv_scan")(*seqs, s0, *row_consts, *step_consts)


def _nh_order(x, axis):
    shape = x.shape
    x = x.reshape(shape[:axis] + (RWKV_HEADS, HEAD) + shape[axis + 1:])
    return jnp.swapaxes(x, axis, axis + 1).reshape(shape)


def _hn_order(x, axis):
    shape = x.shape
    x = x.reshape(shape[:axis] + (HEAD, RWKV_HEADS) + shape[axis + 1:])
    return jnp.swapaxes(x, axis, axis + 1).reshape(shape)


def _to_pairs_kernel(x_ref, o_ref, a_scr):
    nb = x_ref.shape[0]
    for b in range(nb):
        a_scr[b] = x_ref[b].T
    for n in range(HEAD):
        blk = a_scr[:, n * RWKV_HEADS:(n + 1) * RWKV_HEADS, :]
        y = blk.reshape(nb * RWKV_HEADS, blk.shape[2]).T
        o_ref[:, n * SUBLANES:(n + 1) * SUBLANES, :] = y.reshape(y.shape[0] // SUBLANES, SUBLANES, LANES)


def _to_pairs(x, batch, seq_len, tt):
    assert batch * RWKV_HEADS == LANES
    x3 = x.reshape(batch, seq_len, D_MODEL)
    return pl.pallas_call(
        _to_pairs_kernel, grid=(seq_len // tt,),
        in_specs=[pl.BlockSpec((batch, tt, D_MODEL), lambda t: (0, t, 0))],
        out_specs=pl.BlockSpec((tt // SUBLANES, HEAD * SUBLANES, LANES), lambda t: (t, 0, 0)),
        out_shape=jax.ShapeDtypeStruct((seq_len // SUBLANES, HEAD * SUBLANES, LANES), F32),
        scratch_shapes=[pltpu.VMEM((batch, D_MODEL, tt), F32)],
        compiler_params=_cparams(1), name="to_pairs")(x3)


def _from_pairs_kernel(o_ref, x_ref, a_scr):
    nb = x_ref.shape[0]
    for n in range(HEAD):
        blk = o_ref[:, n, :].T
        a_scr[:, n * RWKV_HEADS:(n + 1) * RWKV_HEADS, :] = blk.reshape(nb, RWKV_HEADS, blk.shape[1])
    for b in range(nb):
        x_ref[b] = a_scr[b].T


def _from_pairs(o, batch, seq_len, tt):
    assert batch * RWKV_HEADS == LANES
    out = pl.pallas_call(
        _from_pairs_kernel, grid=(seq_len // tt,),
        in_specs=[pl.BlockSpec((tt, HEAD, LANES), lambda t: (t, 0, 0))],
        out_specs=pl.BlockSpec((batch, tt, D_MODEL), lambda t: (0, t, 0)),
        out_shape=jax.ShapeDtypeStruct((batch, seq_len, D_MODEL), F32),
        scratch_shapes=[pltpu.VMEM((batch, D_MODEL, tt), F32)],
        compiler_params=_cparams(1), name="from_pairs")(o)
    return out.reshape(batch * seq_len, D_MODEL)


def _to_pairs_xla(x, batch, seq_len):
    assert seq_len < SUBLANES
    x = jnp.transpose(x.reshape(batch, seq_len, HEAD, RWKV_HEADS), (2, 1, 0, 3))
    x = x.reshape(HEAD, seq_len, batch * RWKV_HEADS)
    x = jnp.pad(x, ((0, 0), (0, SUBLANES - seq_len), (0, 0)), constant_values=1.0)
    return x.reshape(1, HEAD * SUBLANES, batch * RWKV_HEADS)


def _from_pairs_xla(o, batch, seq_len):
    o = o.reshape(seq_len, HEAD, batch, RWKV_HEADS)
    return jnp.transpose(o, (2, 0, 1, 3)).reshape(batch * seq_len, D_MODEL)


def _head_const_pairs(x, batch):
    x = x.reshape(RWKV_HEADS, HEAD).T
    return jnp.tile(x, (1, batch))


def _count_ge(score, thr):
    return jnp.sum(jnp.where(score >= thr, 1.0, 0.0), axis=-1, keepdims=True)


def _select_topk(score, kpos, k_sel, bisect_steps):
    kf = float(k_sel)
    vis = score > _NEG_INF
    nvis = jnp.sum(jnp.where(vis, 1.0, 0.0), axis=-1, keepdims=True)
    need_sel = nvis > kf
    rowmax = jnp.where(need_sel, jnp.max(score, axis=-1, keepdims=True), 0.0)
    rowmin = jnp.where(need_sel, jnp.min(jnp.where(vis, score, _POS_INF), axis=-1, keepdims=True), 0.0)

    def bis(_, c):
        lo, hi, chi = c
        hfin = jnp.where(hi == _POS_INF, rowmax, hi)
        piv = 0.5 * lo + 0.5 * hfin
        cnt = _count_ge(score, piv)
        ge = cnt >= kf
        return jnp.where(ge, piv, lo), jnp.where(ge, hi, piv), jnp.where(ge, chi, cnt)

    lo, hi, chi = lax.fori_loop(0, bisect_steps, bis,
                                (rowmin, jnp.full_like(rowmin, _POS_INF), jnp.zeros_like(rowmin)))

    def walk_cond(c):
        return jnp.min(c[4]) < 0.5

    def walk(c):
        hi, chi, tau, cgt, done, ceq = c
        pending = done < 0.5
        bmax = jnp.max(jnp.where(score < hi, score, _NEG_INF), axis=-1, keepdims=True)
        cnt = _count_ge(score, bmax)
        fin = jnp.logical_and(cnt >= kf, pending)
        tau = jnp.where(fin, bmax, tau)
        cgt = jnp.where(fin, chi, cgt)
        ceq = jnp.where(fin, cnt - chi, ceq)
        adv = jnp.logical_and(cnt < kf, pending)
        hi = jnp.where(adv, bmax, hi)
        chi = jnp.where(adv, cnt, chi)
        return hi, chi, tau, cgt, jnp.where(fin, 1.0, done), ceq

    done0 = jnp.where(need_sel, 0.0, 1.0)
    neg = jnp.full_like(rowmin, _NEG_INF)
    zero = jnp.zeros_like(rowmin)
    _, _, tau, cgt, _, ceq = lax.while_loop(walk_cond, walk, (hi, chi, neg, zero, done0, zero))

    need = kf - cgt
    excess = jnp.logical_and(need_sel, ceq > need)
    eq = score == tau
    n_keys = score.shape[-1]

    def tie_break(_):
        def body(_, c):
            plo, phi = c
            mid = lax.shift_right_arithmetic(plo + phi, 1)
            cnt = jnp.sum(jnp.where(jnp.logical_and(eq, kpos <= mid), 1.0, 0.0), axis=-1, keepdims=True)
            ge = cnt >= need
            return jnp.where(ge, plo, mid), jnp.where(ge, mid, phi)
        plo0 = jnp.full(tau.shape, -1, jnp.int32)
        phi0 = jnp.full(tau.shape, n_keys - 1, jnp.int32)
        _, phi = lax.fori_loop(0, n_keys.bit_length() + 1, body, (plo0, phi0))
        return jnp.where(excess, phi, n_keys)

    any_excess = jnp.max(jnp.where(excess, 1.0, 0.0)) > 0.0
    pcut = lax.cond(any_excess, tie_break, lambda _: jnp.full(tau.shape, n_keys, jnp.int32), 0)
    pcut = jnp.where(need_sel, pcut, -1)
    return jnp.logical_or(score > tau, jnp.logical_and(eq, kpos <= pcut))


_MASKED = -1e30


def _select_topk_chunked(s_ref, nch, ck, k_sel, nvis, bisect_steps):
    rows = s_ref.shape[0]
    kf = float(k_sel)
    nfold = ck // LANES

    def chunk(c):
        return s_ref[:, pl.ds(pl.multiple_of(c * ck, ck), ck)]

    def fold(x, op):
        out = x[:, 0:LANES]
        for i in range(1, nfold):
            out = op(out, x[:, i * LANES:(i + 1) * LANES])
        return out

    def count(pred):
        def body(c, acc):
            return acc + fold(jnp.where(pred(chunk(c), c), 1.0, 0.0), jnp.add)
        acc = lax.fori_loop(0, nch, body, jnp.zeros((rows, LANES), F32))
        return jnp.sum(acc, axis=-1, keepdims=True)

    def row_max(val):
        def body(c, acc):
            return jnp.maximum(acc, fold(val(chunk(c)), jnp.maximum))
        acc = lax.fori_loop(0, nch, body, jnp.full((rows, LANES), _NEG_INF, F32))
        return jnp.max(acc, axis=-1, keepdims=True)

    need_sel = nvis > kf
    rowmax = jnp.where(need_sel, row_max(lambda x: x), 0.0)
    rowmin = jnp.where(need_sel, -row_max(lambda x: jnp.where(x > _NEG_INF, -x, _NEG_INF)), 0.0)

    def bis(_, c):
        lo, hi, chi = c
        hfin = jnp.where(hi == _POS_INF, rowmax, hi)
        piv = 0.5 * lo + 0.5 * hfin
        cnt = count(lambda x, _c: x >= piv)
        ge = cnt >= kf
        return jnp.where(ge, piv, lo), jnp.where(ge, hi, piv), jnp.where(ge, chi, cnt)

    lo, hi, chi = lax.fori_loop(0, bisect_steps, bis,
                                (rowmin, jnp.full_like(rowmin, _POS_INF), jnp.zeros_like(rowmin)))

    def walk_cond(c):
        return jnp.min(c[4]) < 0.5

    def walk(c):
        hi, chi, tau, cgt, done = c
        pending = done < 0.5
        bmax = row_max(lambda x: jnp.where(x < hi, x, _NEG_INF))
        cnt = count(lambda x, _c: x >= bmax)
        fin = jnp.logical_and(cnt >= kf, pending)
        tau = jnp.where(fin, bmax, tau)
        cgt = jnp.where(fin, chi, cgt)
        adv = jnp.logical_and(cnt < kf, pending)
        hi = jnp.where(adv, bmax, hi)
        chi = jnp.where(adv, cnt, chi)
        return hi, chi, tau, cgt, jnp.where(fin, 1.0, done)

    neg = jnp.full_like(rowmin, _NEG_INF)
    zero = jnp.zeros_like(rowmin)
    _, _, tau, cgt, _ = lax.while_loop(walk_cond, walk, (hi, chi, neg, zero, jnp.where(need_sel, 0.0, 1.0)))

    need = jnp.where(need_sel, kf - cgt, 0.0)
    tri = jnp.where(lax.broadcasted_iota(jnp.int32, (ck, ck), 0) <= lax.broadcasted_iota(jnp.int32, (ck, ck), 1),
                    1.0, 0.0).astype(BF16)

    def write(c, ties_before):
        x = chunk(c)
        eq = x == tau
        rank = jnp.dot(jnp.where(eq, 1.0, 0.0).astype(BF16), tri, preferred_element_type=F32) + ties_before
        sel = jnp.logical_or(x > tau, jnp.logical_and(eq, rank <= need))
        s_ref[:, pl.ds(pl.multiple_of(c * ck, ck), ck)] = jnp.where(sel, 0.0, _MASKED)
        return rank[:, ck - 1:ck]

    lax.fori_loop(0, nch, write, zero)


def _attn_prep_kernel(q_ref, qi_ref, kv_ref, kiw_ref, qg_ref, kg_ref, kn_o, qb_o, qib_o, kvb_o, kib_o):
    w = ATT_KV_HEADS * HEAD
    for h in range(ATT_KV_HEADS):
        hs = slice(h * HEAD, (h + 1) * HEAD)
        x = kv_ref[:, hs]
        kn = x * lax.rsqrt(jnp.mean(x * x, axis=-1, keepdims=True) + NORM_EPS) * kg_ref[...]
        kn_o[:, hs] = kn
        kvb_o[:, hs] = kn.astype(BF16)
        v = kv_ref[:, w + h * HEAD:w + (h + 1) * HEAD].astype(BF16)
        kvb_o[:, w + h * LANES:w + (h + 1) * LANES] = jnp.concatenate([v, jnp.ones_like(v)], axis=1)
    for h in range(ATT_HEADS):
        x = q_ref[:, h * HEAD:(h + 1) * HEAD]
        qn = x * lax.rsqrt(jnp.mean(x * x, axis=-1, keepdims=True) + NORM_EPS) * (qg_ref[...] * HEAD ** -0.5)
        qb_o[h] = qn.astype(BF16)
    qib_o[...] = (qi_ref[...] * HEAD ** -0.5).astype(BF16)
    kib_o[...] = kiw_ref[:, 0:HEAD].astype(BF16)


def _attn_prep(za, qn_g, kn_g, tm):
    m = za.shape[0]
    w = ATT_KV_HEADS * HEAD
    wi = IDX_HEADS * HEAD
    row = lambda width, blk: pl.BlockSpec((tm, width), lambda i, blk=blk: (i, blk))
    gspec = pl.BlockSpec((1, HEAD), lambda i: (0, 0))
    return pl.pallas_call(
        _attn_prep_kernel, grid=(m // tm,),
        in_specs=[row(D_MODEL, ZA_Q // D_MODEL), row(wi, ZA_QI // wi), row(2 * w, ZA_KV // (2 * w)),
                  row(LANES, ZA_KIW // LANES), gspec, gspec],
        out_specs=[row(w, 0), pl.BlockSpec((ATT_HEADS, tm, HEAD), lambda i: (0, i, 0)), row(wi, 0),
                   row(w + ATT_KV_HEADS * LANES, 0), row(HEAD, 0)],
        out_shape=[jax.ShapeDtypeStruct((m, w), F32), jax.ShapeDtypeStruct((ATT_HEADS, m, HEAD), BF16),
                   jax.ShapeDtypeStruct((m, wi), BF16), jax.ShapeDtypeStruct((m, w + ATT_KV_HEADS * LANES), BF16),
                   jax.ShapeDtypeStruct((m, HEAD), BF16)],
        compiler_params=_cparams(1), name="attn_prep")(za, za, za, za, qn_g.reshape(1, HEAD), kn_g.reshape(1, HEAD))


def _bucket_edges():
    max_exact = REL_BUCKETS // 2
    d = np.arange(REL_MAX_DIST + 1)
    df = np.maximum(d, 1).astype(np.float32)
    large = max_exact + (np.log(df / max_exact) / math.log(REL_MAX_DIST / max_exact)
                         * (REL_BUCKETS - max_exact)).astype(np.int32)
    bucket = np.where(d < max_exact, d, np.minimum(large, REL_BUCKETS - 1))
    return [int(np.argmax(bucket >= b)) for b in range(REL_BUCKETS)]


_BUCKET_EDGES = _bucket_edges()


def _rel_bias_lookup(dist, value_of_bucket):
    bias = value_of_bucket(REL_BUCKETS - 1)
    for b in range(REL_BUCKETS - 2, -1, -1):
        bias = jnp.where(dist < _BUCKET_EDGES[b + 1], value_of_bucket(b), bias)
    return bias


def _attn_prompt_kernel(rb_ref, q_ref, qi_ref, kiwq_ref, g_ref, kvb_ref, kib_ref, o_ref, s_scr, tbd_ref,
                        l_scr, mx_scr, acc_scr, *, tq, k_sel, bisect_steps):
    qt = pl.program_id(0)
    q0 = qt * tq
    ck = tq
    rg_rows = LANES
    n_rg = tq // rg_rows
    w = ATT_KV_HEADS * HEAD

    @pl.when(jnp.logical_and(pl.program_id(1) == 0, qt == 0))
    def _():
        rr = lax.broadcasted_iota(jnp.int32, (tq, tq), 0) - lax.broadcasted_iota(jnp.int32, (tq, tq), 1)
        for h in range(ATT_HEADS):
            far = rb_ref[REL_BUCKETS - 1, h]
            tbd_ref[h, 0] = _rel_bias_lookup(rr, lambda b: rb_ref[b, h]) - far
            tbd_ref[h, 1] = _rel_bias_lookup(rr + tq, lambda b: rb_ref[b, h]) - far

    def chunk_start(c):
        return pl.multiple_of(c * ck, ck)

    wcol = kiwq_ref[:, HEAD:HEAD + IDX_HEADS] * (IDX_HEADS ** -0.5)
    qi_h = [[qi_ref[rg * rg_rows:(rg + 1) * rg_rows, h * HEAD:(h + 1) * HEAD] for h in range(IDX_HEADS)]
            for rg in range(n_rg)]
    w_h = [[wcol[rg * rg_rows:(rg + 1) * rg_rows, h:h + 1] for h in range(IDX_HEADS)] for rg in range(n_rg)]

    def score_chunk(c, carry):
        k0 = chunk_start(c)
        kc = kib_ref[pl.ds(k0, ck), :]
        kpos = k0 + lax.broadcasted_iota(jnp.int32, (rg_rows, ck), 1)
        for rg in range(n_rg):
            acc = jnp.zeros((rg_rows, ck), F32)
            for h in range(IDX_HEADS):
                acc = acc + jnp.maximum(_dot_nt(qi_h[rg][h], kc), 0.0) * w_h[rg][h]
            qpos = q0 + rg * rg_rows + lax.broadcasted_iota(jnp.int32, (rg_rows, ck), 0)
            s_scr[rg * rg_rows:(rg + 1) * rg_rows, pl.ds(k0, ck)] = jnp.where(kpos <= qpos, acc, _NEG_INF)
        return carry

    nch = qt + 1
    lax.fori_loop(0, nch, score_chunk, 0)

    nvis = (q0 + 1 + lax.broadcasted_iota(jnp.int32, (tq, 1), 0)).astype(F32)
    _select_topk_chunked(s_scr, nch, ck, k_sel, nvis, bisect_steps)

    g_rows = ATT_GROUP * tq
    n_far = jnp.maximum(qt - 1, 0)
    for kvh in range(ATT_KV_HEADS):
        gs = slice(kvh * ATT_GROUP, (kvh + 1) * ATT_GROUP)
        qg = q_ref[gs, :, :].reshape(g_rows, HEAD)
        mx_scr[...] = jnp.full(mx_scr.shape, _MASKED, F32)
        acc_scr[...] = jnp.zeros(acc_scr.shape, F32)

        def logits_chunk(c, carry, near, qg=qg, gs=gs, kvh=kvh):
            k0 = chunk_start(c)
            kc = kvb_ref[pl.ds(k0, ck), kvh * HEAD:(kvh + 1) * HEAD]
            s3 = _dot_nt(qg, kc).reshape(ATT_GROUP, tq, ck) + s_scr[:, pl.ds(k0, ck)][None]
            if near:
                s3 = s3 + tbd_ref[gs, qt - c, :, :]
            s = s3.reshape(g_rows, ck)
            l_scr[:, pl.ds(k0, ck)] = s
            m = mx_scr[...]
            for i in range(ck // LANES):
                m = jnp.maximum(m, s[:, i * LANES:(i + 1) * LANES])
            mx_scr[...] = m
            return carry

        lax.fori_loop(0, n_far, functools.partial(logits_chunk, near=False), 0)
        lax.fori_loop(n_far, nch, functools.partial(logits_chunk, near=True), 0)
        mx_scr[...] = jnp.broadcast_to(jnp.max(mx_scr[...], axis=-1, keepdims=True), mx_scr.shape)

        def pv_chunk(c, carry, kvh=kvh):
            k0 = chunk_start(c)
            vx = kvb_ref[pl.ds(k0, ck), w + kvh * LANES:w + (kvh + 1) * LANES]
            m = mx_scr[...]
            p = jnp.concatenate([jnp.exp(l_scr[:, pl.ds(k0 + i * LANES, LANES)] - m)
                                 for i in range(ck // LANES)], axis=1).astype(BF16)
            acc_scr[...] += jnp.dot(p, vx, preferred_element_type=F32)
            return carry

        lax.fori_loop(0, nch, pv_chunk, 0)

        acc = acc_scr[...]
        o = acc[:, 0:HEAD] / acc[:, HEAD:HEAD + 1]
        for pp in range(ATT_GROUP // 2):
            cols = slice((kvh * ATT_GROUP // 2 + pp) * LANES, (kvh * ATT_GROUP // 2 + pp + 1) * LANES)
            gh = g_ref[:, cols]
            pair = jnp.concatenate([o[2 * pp * tq:(2 * pp + 1) * tq], o[(2 * pp + 1) * tq:(2 * pp + 2) * tq]], axis=1)
            o_ref[:, cols] = pair * (gh * _sigmoid(gh))


def _attn_prompt(za, kn, qb, qib, kvb, kib, *, rel_bias, batch, seq_len, tq, k_sel):
    m = za.shape[0]
    nq = seq_len // tq
    w = ATT_KV_HEADS * HEAD
    wi = IDX_HEADS * HEAD
    assert REL_MAX_DIST <= tq and tq % LANES == 0
    row_map = lambda blk: (lambda t, b, blk=blk: (b * nq + t, blk))
    key_map = lambda t, b: (b, 0)
    kern = functools.partial(_attn_prompt_kernel, tq=tq, k_sel=k_sel, bisect_steps=14)
    return pl.pallas_call(
        kern, grid=(nq, batch),
        in_specs=[
            pl.BlockSpec(memory_space=pltpu.SMEM),
            pl.BlockSpec((ATT_HEADS, tq, HEAD), lambda t, b: (0, b * nq + t, 0)),
            pl.BlockSpec((tq, wi), row_map(0)),
            pl.BlockSpec((tq, LANES), row_map(ZA_KIW // LANES)),
            pl.BlockSpec((tq, D_MODEL), row_map(ZA_GATT // D_MODEL)),
            pl.BlockSpec((seq_len, w + ATT_KV_HEADS * LANES), key_map),
            pl.BlockSpec((seq_len, HEAD), key_map),
        ],
        out_specs=pl.BlockSpec((tq, D_MODEL), row_map(0)),
        out_shape=jax.ShapeDtypeStruct((m, D_MODEL), F32),
        scratch_shapes=[pltpu.VMEM((tq, seq_len), F32), pltpu.VMEM((ATT_HEADS, 2, tq, tq), F32),
                        pltpu.VMEM((ATT_GROUP * tq, seq_len), F32), pltpu.VMEM((ATT_GROUP * tq, LANES), F32),
                        pltpu.VMEM((ATT_GROUP * tq, LANES), F32)],
        compiler_params=_cparams(2), name="attn_prompt")(rel_bias, qb, qib, za, za, kvb, kib)


def _sample_score_kernel(pt_ref, q_ref, w_ref, kiw_new_ref, *rest, n_pages, page, dec_seq):
    page_refs = rest[:n_pages]
    o_ref = rest[n_pages]
    past = n_pages * page
    kidx = jnp.concatenate([r[...] for r in page_refs], axis=0).astype(BF16)
    new = kiw_new_ref[:, 0:HEAD]
    new = jnp.concatenate([new, jnp.zeros((LANES - dec_seq, HEAD), F32)], axis=0).astype(BF16)
    q = (q_ref[...] * (HEAD ** -0.5)).astype(BF16)
    lg = jnp.concatenate([_dot_nt(q, kidx), _dot_nt(q, new)], axis=1)
    wr = jnp.maximum(lg, 0.0) * (w_ref[...] * (IDX_HEADS ** -0.5))
    n_keys = past + LANES
    sc = jnp.sum(wr.reshape(dec_seq, IDX_HEADS, n_keys), axis=1)
    kpos = lax.broadcasted_iota(jnp.int32, (dec_seq, n_keys), 1)
    tpos = lax.broadcasted_iota(jnp.int32, (dec_seq, n_keys), 0)
    o_ref[...] = jnp.where(kpos <= past + tpos, sc, _NEG_INF)


def _sample_select_kernel(s_ref, o_ref, *, k_sel, bisect_steps):
    score = s_ref[...]
    kpos = lax.broadcasted_iota(jnp.int32, score.shape, 1)
    sel = _select_topk(score, kpos, k_sel, bisect_steps)
    o_ref[...] = jnp.where(sel, 0.0, _NEG_INF)


def _sample_attn_kernel(pt_ref, q_ref, g_ref, kn_new_ref, kv_new_ref, mask_ref, rb_rows_ref, qg_ref, *rest,
                        n_pages, page, dec_seq):
    k_pages = rest[:n_pages]
    v_pages = rest[n_pages:2 * n_pages]
    o_ref = rest[2 * n_pages]
    bias_ref = rest[2 * n_pages + 1]
    w = ATT_KV_HEADS * HEAD
    rows = dec_seq * ATT_HEADS
    n_keys = n_pages * page + LANES

    @pl.when(pl.program_id(0) == 0)
    def _():
        t_row = lax.div(lax.broadcasted_iota(jnp.int32, (rows, n_keys), 0), ATT_HEADS)
        dist = n_pages * page + t_row - lax.broadcasted_iota(jnp.int32, (rows, n_keys), 1)
        bias_ref[...] = _rel_bias_lookup(dist, lambda b: rb_rows_ref[:, b:b + 1])

    pad = jnp.zeros((LANES - dec_seq, w), F32)
    k_all = jnp.concatenate([r[...] for r in k_pages] + [kn_new_ref[...], pad], axis=0).astype(BF16)
    v_all = jnp.concatenate([r[...] for r in v_pages] + [kv_new_ref[:, w:2 * w], pad], axis=0).astype(BF16)
    q = q_ref[...]
    q = q * lax.rsqrt(jnp.mean(q * q, axis=-1, keepdims=True) + NORM_EPS) * qg_ref[...]
    qb = q.astype(BF16)
    head = lax.rem(lax.broadcasted_iota(jnp.int32, (rows, 1), 0), ATT_HEADS)
    first = head < ATT_GROUP
    lg = jnp.where(first, _dot_nt(qb, k_all[:, 0:HEAD]), _dot_nt(qb, k_all[:, HEAD:2 * HEAD]))
    mask = jnp.broadcast_to(mask_ref[...][:, None, :], (dec_seq, ATT_HEADS, n_keys)).reshape(rows, n_keys)
    s = lg * (HEAD ** -0.5) + bias_ref[...] + mask
    p = jnp.exp(s - jnp.max(s, axis=-1, keepdims=True))
    l = jnp.sum(p, axis=-1, keepdims=True)
    pb = p.astype(BF16)
    o = jnp.where(first, jnp.dot(pb, v_all[:, 0:HEAD], preferred_element_type=F32),
                  jnp.dot(pb, v_all[:, HEAD:2 * HEAD], preferred_element_type=F32)) / l
    g = g_ref[...]
    o_ref[...] = o * (g * _sigmoid(g))


def _attn_sample(za, kn, qb, qib, kvb, kib, *, cache_k, cache_v, cache_kidx, page_table, rel_bias, qn_g,
                 dec_seq, k_sel):
    nb, n_pages = page_table.shape
    n_phys, page = cache_k.shape[0], cache_k.shape[1]
    past = n_pages * page
    n_keys = past + LANES
    w = ATT_KV_HEADS * HEAD
    ck = cache_k.reshape(n_phys, page, w)
    cv = cache_v.reshape(n_phys, page, w)
    ci = cache_kidx.reshape(n_phys, page, HEAD)
    za3 = za.reshape(nb, dec_seq, ZA_COLS)
    kn3 = kn.reshape(nb, dec_seq, w)
    qi = za[:, ZA_QI:ZA_QI + IDX_HEADS * HEAD].reshape(nb, dec_seq * IDX_HEADS, HEAD)
    wi = za[:, ZA_KIW + HEAD:ZA_KIW + HEAD + IDX_HEADS].reshape(nb, dec_seq * IDX_HEADS, 1)
    qa = za[:, ZA_Q:ZA_Q + D_MODEL].reshape(nb, dec_seq * ATT_HEADS, HEAD)
    ga = za[:, ZA_GATT:ZA_GATT + D_MODEL].reshape(nb, dec_seq * ATT_HEADS, HEAD)

    def page_specs(width):
        return [pl.BlockSpec((None, page, width), lambda b, pt, j=j: (pt[b, j], 0, 0)) for j in range(n_pages)]

    per_b = lambda r, c: pl.BlockSpec((None, r, c), lambda b, pt: (b, 0, 0))
    kiw_new = pl.BlockSpec((None, dec_seq, LANES), lambda b, pt: (b, 0, ZA_KIW // LANES))

    scores = pl.pallas_call(
        functools.partial(_sample_score_kernel, n_pages=n_pages, page=page, dec_seq=dec_seq),
        grid_spec=pltpu.PrefetchScalarGridSpec(
            num_scalar_prefetch=1, grid=(nb,),
            in_specs=[per_b(dec_seq * IDX_HEADS, HEAD), per_b(dec_seq * IDX_HEADS, 1), kiw_new] + page_specs(HEAD),
            out_specs=per_b(dec_seq, n_keys)),
        out_shape=jax.ShapeDtypeStruct((nb, dec_seq, n_keys), F32),
        compiler_params=_cparams(1), name="sample_scores")(page_table, qi, wi, za3, *([ci] * n_pages))

    rows = nb * dec_seq
    tr = min(rows, 128)
    mask = pl.pallas_call(
        functools.partial(_sample_select_kernel, k_sel=k_sel, bisect_steps=14),
        grid=(rows // tr,),
        in_specs=[pl.BlockSpec((tr, n_keys), lambda i: (i, 0))],
        out_specs=pl.BlockSpec((tr, n_keys), lambda i: (i, 0)),
        out_shape=jax.ShapeDtypeStruct((rows, n_keys), F32),
        compiler_params=_cparams(1), name="sample_select")(scores.reshape(rows, n_keys))

    rb_rows = jnp.tile(rel_bias.T, (dec_seq, 1))

    const = lambda shape: pl.BlockSpec(shape, lambda b, pt: (0,) * len(shape))
    kv_new = pl.BlockSpec((None, dec_seq, 2 * w), lambda b, pt: (b, 0, ZA_KV // (2 * w)))
    out = pl.pallas_call(
        functools.partial(_sample_attn_kernel, n_pages=n_pages, page=page, dec_seq=dec_seq),
        grid_spec=pltpu.PrefetchScalarGridSpec(
            num_scalar_prefetch=1, grid=(nb,),
            in_specs=[per_b(dec_seq * ATT_HEADS, HEAD), per_b(dec_seq * ATT_HEADS, HEAD), per_b(dec_seq, w),
                      kv_new, per_b(dec_seq, n_keys), const((dec_seq * ATT_HEADS, REL_BUCKETS)), const((1, HEAD))]
            + page_specs(w) + page_specs(w),
            out_specs=per_b(dec_seq * ATT_HEADS, HEAD),
            scratch_shapes=[pltpu.VMEM((dec_seq * ATT_HEADS, n_keys), F32)]),
        out_shape=jax.ShapeDtypeStruct((nb, dec_seq * ATT_HEADS, HEAD), F32),
        compiler_params=_cparams(1), name="sample_attn")(
            page_table, qa, ga, kn3, za3, mask.reshape(nb, dec_seq, n_keys), rb_rows, qn_g.reshape(1, HEAD),
            *([ck] * n_pages), *([cv] * n_pages))
    return out.reshape(rows, D_MODEL)


def _merge_kernel(x_ref, oa_ref, sg_ref, ob_ref, ga_ref, gb_ref, wpa_ref, wpb_ref, wo_ref, y_ref):
    oa = (oa_ref[...] * sg_ref[...]).astype(BF16)
    pa = jnp.dot(oa, wpa_ref[...], preferred_element_type=F32)
    pb = jnp.dot(ob_ref[...].astype(BF16), wpb_ref[...], preferred_element_type=F32)
    merged = _sigmoid(ga_ref[...]) * pa + _sigmoid(gb_ref[...]) * pb
    y_ref[...] = x_ref[...] + jnp.dot(merged.astype(BF16), wo_ref[...], preferred_element_type=F32)


def _merge(x, oa, sg, ob, za, w_pa, w_pb, w_out, tm):
    m, d = x.shape
    row = pl.BlockSpec((tm, d), lambda i: (i, 0))
    wsp = pl.BlockSpec((d, d), lambda i: (0, 0))
    return pl.pallas_call(
        _merge_kernel, grid=(m // tm,),
        in_specs=[row, row, row, row,
                  pl.BlockSpec((tm, d), lambda i: (i, ZA_GA // d)), pl.BlockSpec((tm, d), lambda i: (i, ZA_GB // d)),
                  wsp, wsp, wsp],
        out_specs=row,
        out_shape=jax.ShapeDtypeStruct((m, d), F32),
        compiler_params=_cparams(1), name="merge")(x, oa, sg, ob, za, za, w_pa, w_pb, w_out)


def _layer(x, shift, s0, params, *, batch, seq_len, attend):
    (norm_g, w_r, w_a, mu, w0, w2, a0, a2, k_k, k_a, r_k, lnx_g, lnx_b, qn_g, kn_g, w_pa, w_pb, w_out) = params
    m = batch * seq_len
    tm = min(m, 512)
    xn = _rmsnorm(x, norm_g, tm)
    tm_proj = min(m, 1024)
    zr = _matmul(xn, w_r, tm_proj, RWKV_COLS // 3, "inproj_rwkv")
    za = _matmul(xn, w_a, tm_proj, ZA_COLS // 4, "inproj_attn")

    tile = min(m, 256)
    r, w, k, v, a, sg = _rwkv_prep(zr, shift, mu, w0, w2, a0, a2, batch=batch, seq_len=seq_len, tile=tile)
    relayout_tile = 128
    in_kernel_relayout = seq_len % relayout_tile == 0 and batch * RWKV_HEADS == LANES
    if in_kernel_relayout:
        seqs = [_to_pairs(t, batch, seq_len, relayout_tile) for t in (r, w, k, v, a)]
    else:
        seqs = [_to_pairs_xla(t, batch, seq_len) for t in (r, w, k, v, a)]
    row_consts = [_head_const_pairs(t, batch) for t in (lnx_g, lnx_b)]
    step_consts = [jnp.broadcast_to(_head_const_pairs(t, batch)[:, None, :], (HEAD, SUBLANES, batch * RWKV_HEADS))
                   for t in (r_k.reshape(-1), k_k, k_a)]
    o_t, s_t = _rwkv_scan(seqs, s0, row_consts, step_consts, seq_len=seq_len, steps=min(seq_len, 32))
    if in_kernel_relayout:
        oa = _from_pairs(o_t, batch, seq_len, relayout_tile)
    else:
        oa = _from_pairs_xla(o_t, batch, seq_len)

    kn, qb, qib, kvb, kib = _attn_prep(za, qn_g, kn_g, tm)
    ob = attend(za, kn, qb, qib, kvb, kib)

    y = _merge(x, oa, sg, ob, za, w_pa, w_pb, w_out, min(m, 256))
    return y, zr, za, kn, s_t


def _state_to_pairs(s):
    b, h, n, _ = s.shape
    return jnp.transpose(s, (3, 2, 0, 1)).reshape(n, n, b * h)


def _state_from_pairs(s, batch):
    n = s.shape[0]
    return jnp.transpose(s.reshape(n, n, batch, RWKV_HEADS), (2, 3, 1, 0))


def kernel(x_prompt, x_sample, cache_k, cache_v, cache_kidx, state_wkv, state_shift, page_table, norm_g, w_in,
           shift_mu, w0, w2, a0, a2, k_k, k_a, r_k, lnx_g, lnx_b, q_norm_g, k_norm_g, rel_bias, w_pa, w_pb, w_out):
    bsz, seq, d = x_prompt.shape
    dec_bsz, dec_seq, _ = x_sample.shape
    depth = w_in.shape[0]
    assert depth == 1 and d == D_MODEL
    past_len = page_table.shape[1] * cache_k.shape[2]
    topk_p = min(TOPK_MAX, seq // 4)
    topk_s = min(TOPK_MAX, (past_len + dec_seq) // 4)
    l = 0

    wl = w_in[l]
    c0 = RWKV_COLS
    q_w, kv_w, qi_w = wl[:, c0:c0 + 1024], wl[:, c0 + 1024:c0 + 1280], wl[:, c0 + 1280:c0 + 1792]
    kiw_w = wl[:, c0 + 1792:c0 + 1864]
    rest_w = wl[:, c0 + 1864:]
    zpad = lambda n: jnp.zeros((d, n), wl.dtype)
    w_a = jnp.concatenate([q_w, qi_w, kv_w, kiw_w, zpad(LANES - kiw_w.shape[1]), zpad(LANES), rest_w],
                          axis=1).astype(BF16)
    assert w_a.shape[1] == ZA_COLS
    def rwkv_cols(x, fn):
        lead = x.shape[:-1]
        main = fn(x[..., :4 * d].reshape(lead + (4, d)), len(lead) + 1).reshape(lead + (4 * d,))
        return jnp.concatenate([main, x[..., 4 * d:]], axis=-1)

    w_r = rwkv_cols(wl[:, :c0], _nh_order).astype(BF16)
    params = (norm_g[l], w_r, w_a, rwkv_cols(shift_mu[l], _nh_order), _nh_order(w0[l], 0), _nh_order(w2[l], 1),
              _nh_order(a0[l], 0), _nh_order(a2[l], 1), k_k[l], k_a[l], r_k[l],
              lnx_g[l], lnx_b[l], q_norm_g[l], k_norm_g[l],
              _nh_order(w_pa[l], 0).astype(BF16), w_pb[l].astype(BF16), w_out[l].astype(BF16))

    xp = x_prompt.reshape(bsz * seq, d)
    attend_p = functools.partial(_attn_prompt, rel_bias=rel_bias, batch=bsz, seq_len=seq,
                                 tq=min(seq, 256), k_sel=topk_p)
    yp, zr_p, za_p, kn_p, st_p = _layer(
        xp, jnp.zeros((bsz, 1, RWKV_COLS), F32), jnp.zeros((HEAD, HEAD, bsz * RWKV_HEADS), F32), params,
        batch=bsz, seq_len=seq, attend=attend_p)

    xs = x_sample.reshape(dec_bsz * dec_seq, d)
    attend_s = functools.partial(_attn_sample, cache_k=cache_k[l], cache_v=cache_v[l], cache_kidx=cache_kidx[l],
                                 page_table=page_table, rel_bias=rel_bias, qn_g=q_norm_g[l], dec_seq=dec_seq,
                                 k_sel=topk_s)
    shift_rows = jnp.repeat(rwkv_cols(state_shift[l], _nh_order), dec_seq, axis=0)
    ys, zr_s, za_s, kn_s, st_s = _layer(
        xs, shift_rows, _state_to_pairs(state_wkv[l]), params, batch=dec_bsz, seq_len=dec_seq, attend=attend_s)

    w = ATT_KV_HEADS * HEAD

    def pack(y, zr, za, kn, st, b, t):
        v = za[:, ZA_KV + w:ZA_KV + 2 * w]
        kidx = za[:, ZA_KIW:ZA_KIW + HEAD]
        return (y.reshape(b, t, d),
                kn.reshape(1, b, t, ATT_KV_HEADS, HEAD), v.reshape(1, b, t, ATT_KV_HEADS, HEAD),
                kidx.reshape(1, b, t, HEAD), _state_from_pairs(st, b)[None],
                rwkv_cols(zr.reshape(b, t, RWKV_COLS)[:, -1], _hn_order)[None])

    p = pack(yp, zr_p, za_p, kn_p, st_p, bsz, seq)
    s = pack(ys, zr_s, za_s, kn_s, st_s, dec_bsz, dec_seq)
    return (p[0], s[0]) + p[1:] + s[1:]
```

```python
import functools
import math

import numpy as np
import jax
import jax.numpy as jnp
from jax import lax
from jax.experimental import pallas as pl
from jax.experimental.pallas import tpu as pltpu

F32 = jnp.float32
BF16 = jnp.bfloat16

D_MODEL = 1024
HEAD = 64
RWKV_HEADS = D_MODEL // HEAD
LORA = 64
LNX_EPS = 64e-5
ATT_HEADS = D_MODEL // HEAD
ATT_KV_HEADS = 2
ATT_GROUP = ATT_HEADS // ATT_KV_HEADS
IDX_HEADS = 8
TOPK_MAX = 256
REL_BUCKETS = 32
REL_MAX_DIST = 128
NORM_EPS = 1e-6
RWKV_COLS = 4 * D_MODEL + 2 * LORA

LANES = 128
SUBLANES = 8
VMEM_LIMIT_BYTES = 56 * 1024 * 1024

ZA_Q = 0
ZA_QI = 1024
ZA_KV = 1536
ZA_KIW = 1792
ZA_GATT = 2048
ZA_GA = 3072
ZA_GB = 4096
ZA_COLS = 5120

_NEG_INF = float("-inf")
_POS_INF = float("inf")


def _cparams(n_axes):
    return pltpu.CompilerParams(dimension_semantics=("arbitrary",) * n_axes,
                                vmem_limit_bytes=VMEM_LIMIT_BYTES)


def _sigmoid(x):
    return 1.0 / (1.0 + jnp.exp(-x))


def _dot_nt(a, b):
    return lax.dot_general(a, b, (((1,), (1,)), ((), ())), preferred_element_type=F32)


def _rmsnorm_kernel(x_ref, g_ref, o_ref):
    x = x_ref[...]
    ms = jnp.mean(x * x, axis=-1, keepdims=True)
    o_ref[...] = (x * lax.rsqrt(ms + NORM_EPS) * g_ref[...]).astype(o_ref.dtype)


def _rmsnorm(x, g, tm):
    m, d = x.shape
    return pl.pallas_call(
        _rmsnorm_kernel, grid=(m // tm,),
        in_specs=[pl.BlockSpec((tm, d), lambda i: (i, 0)), pl.BlockSpec((1, d), lambda i: (0, 0))],
        out_specs=pl.BlockSpec((tm, d), lambda i: (i, 0)),
        out_shape=jax.ShapeDtypeStruct((m, d), BF16),
        compiler_params=_cparams(1), name="rmsnorm")(x, g.reshape(1, d))


def _mm_kernel(x_ref, w_ref, o_ref):
    o_ref[...] = jnp.dot(x_ref[...], w_ref[...], preferred_element_type=F32)


def _matmul(x, w, tm, tn, name):
    m, k = x.shape
    n = w.shape[1]
    return pl.pallas_call(
        _mm_kernel, grid=(m // tm, n // tn),
        in_specs=[pl.BlockSpec((tm, k), lambda i, j: (i, 0)), pl.BlockSpec((k, tn), lambda i, j: (0, j))],
        out_specs=pl.BlockSpec((tm, tn), lambda i, j: (i, j)),
        out_shape=jax.ShapeDtypeStruct((m, n), F32),
        compiler_params=_cparams(2), name=name)(x, w)


def _rwkv_prep_kernel(z_ref, prev_ref, shift_ref, mu_ref, w0_ref, w2_ref, a0_ref, a2_ref,
                      r_o, w_o, k_o, v_o, a_o, g_o, *, seq_len, tile):
    z = z_ref[...]
    rows = lax.broadcasted_iota(jnp.int32, z.shape, 0)
    rolled = pltpu.roll(z, 1, 0)
    if seq_len >= tile:
        first = jnp.where(pl.program_id(1) == 0, shift_ref[...], prev_ref[SUBLANES - 1:SUBLANES, :])
        prev = jnp.where(rows == 0, first, rolled)
    else:
        prev = jnp.where(lax.rem(rows, seq_len) == 0, shift_ref[...], rolled)
    zs = z + (prev - z) * mu_ref[...]
    d = D_MODEL
    r = zs[:, 0:d]
    k = zs[:, d:2 * d]
    v = zs[:, 2 * d:3 * d]
    g = zs[:, 3 * d:4 * d]
    wd = zs[:, 4 * d:4 * d + LORA]
    ad = zs[:, 4 * d + LORA:4 * d + 2 * LORA]
    wl = w0_ref[...] + jnp.dot(jnp.tanh(wd).astype(BF16), w2_ref[...], preferred_element_type=F32)
    decay = jnp.exp(-_sigmoid(wl) * math.exp(-0.5))
    a = _sigmoid(a0_ref[...] + jnp.dot(ad.astype(BF16), a2_ref[...], preferred_element_type=F32))
    r_o[...] = r
    w_o[...] = decay
    k_o[...] = k
    v_o[...] = v
    a_o[...] = a
    g_o[...] = g * _sigmoid(g)


def _rwkv_prep(zr, shift, mu, w0, w2, a0, a2, *, batch, seq_len, tile):
    m, c = zr.shape
    d = D_MODEL
    row = lambda x: x.reshape(1, -1)
    consts = [row(mu), row(w0), w2.astype(BF16), row(a0), a2.astype(BF16)]
    const_specs = [pl.BlockSpec(x.shape, lambda *_: (0, 0)) for x in consts]
    if seq_len >= tile:
        nt = seq_len // tile
        grid = (batch, nt)
        zmap = lambda b, t: (b * nt + t, 0)
        pmap = lambda b, t: (jnp.maximum((b * seq_len + t * tile) // SUBLANES - 1, 0), 0)
        shift_spec = pl.BlockSpec((None, 1, c), lambda b, t: (b, 0, 0))
    else:
        grid = (1, m // tile)
        zmap = lambda b, t: (t, 0)
        pmap = lambda b, t: (0, 0)
        shift_spec = pl.BlockSpec((tile, c), zmap)
    out_spec = pl.BlockSpec((tile, d), zmap)
    kern = functools.partial(_rwkv_prep_kernel, seq_len=seq_len, tile=tile)
    return pl.pallas_call(
        kern, grid=grid,
        in_specs=[pl.BlockSpec((tile, c), zmap), pl.BlockSpec((SUBLANES, c), pmap), shift_spec] + const_specs,
        out_specs=[out_spec] * 6,
        out_shape=[jax.ShapeDtypeStruct((m, d), F32)] * 6,
        compiler_params=_cparams(2), name="rwkv_prep")(zr, zr, shift, *consts)


def _rwkv_scan_kernel(r_in, w_in, k_in, v_in, a_in, s0_ref, lg_ref, lb_ref, rk_ref, kkc_ref, kac_ref,
                      o_ref, s_ref, vec_ref, bf_ref, ge_ref, *, groups, live):
    nb = HEAD // SUBLANES

    @pl.when(pl.program_id(1) == 0)
    def _():
        s_ref[...] = s0_ref[...]

    shape4 = (groups, HEAD, SUBLANES, LANES)
    a = a_in[...].reshape(shape4)
    kraw = k_in[...].reshape(shape4)
    r_all = r_in[...].reshape(shape4)
    kkraw = kraw * kkc_ref[...][None]
    n2 = jnp.sum(kkraw * kkraw, axis=1, keepdims=True)
    kk = kkraw / jnp.maximum(jnp.sqrt(n2), 1e-12)
    kmod = kraw * (1.0 + (a - 1.0) * kac_ref[...][None])
    bf_ref[...] = jnp.sum(r_all * kmod * rk_ref[...][None], axis=1)
    row = lax.broadcasted_iota(jnp.int32, (HEAD, SUBLANES, LANES), 1)
    g_end = jnp.ones((HEAD, SUBLANES, LANES), F32)
    for q in range(groups):
        g = w_in[q].reshape(HEAD, SUBLANES, LANES)
        for sh in (1, 2, 4):
            g = g * jnp.where(row >= sh, pltpu.roll(g, sh, 1), 1.0)
        g = g * g_end
        g_prev = jnp.where(row >= 1, pltpu.roll(g, 1, 1), g_end)
        g_inv = 1.0 / g
        flat = (HEAD * SUBLANES, LANES)
        vec_ref[0, q] = (-kk[q] * g_prev).reshape(flat)
        vec_ref[1, q] = (kk[q] * a[q] * g_inv).reshape(flat)
        vec_ref[2, q] = (kmod[q] * g_inv).reshape(flat)
        vec_ref[3, q] = (r_all[q] * g).reshape(flat)
        last = live - 1 if q == groups - 1 else SUBLANES - 1
        g_end = jnp.broadcast_to(g[:, last:last + 1, :], (HEAD, SUBLANES, LANES))
    ge_ref[...] = g_end

    def group(q, carry):
        for r in range(live):
            def row_of(idx, j):
                return jnp.broadcast_to(vec_ref[idx, q, pl.ds(j * SUBLANES + r, 1), :], (SUBLANES, LANES))

            sa = [jnp.zeros((SUBLANES, LANES), F32) for _ in range(nb)]
            for j in range(HEAD):
                nk = row_of(0, j)
                for ib in range(nb):
                    sa[ib] = sa[ib] + s_ref[j, ib * SUBLANES:(ib + 1) * SUBLANES, :] * nk
            vt = [v_in[q, pl.ds(ib * SUBLANES * SUBLANES + r, SUBLANES, stride=SUBLANES), :] for ib in range(nb)]
            out = [jnp.zeros((SUBLANES, LANES), F32) for _ in range(nb)]
            for j in range(HEAD):
                kaj = row_of(1, j)
                kj = row_of(2, j)
                rj = row_of(3, j)
                for ib in range(nb):
                    sl = slice(ib * SUBLANES, (ib + 1) * SUBLANES)
                    sn = s_ref[j, sl, :] + sa[ib] * kaj + vt[ib] * kj
                    s_ref[j, sl, :] = sn
                    out[ib] = out[ib] + sn * rj
            o = jnp.concatenate(out, axis=0)
            mean = jnp.mean(o, axis=0, keepdims=True)
            dev = o - mean
            var = jnp.mean(dev * dev, axis=0, keepdims=True)
            y = dev * lax.rsqrt(var + LNX_EPS) * lg_ref[...] + lb_ref[...]
            o_ref[q * live + r] = y + bf_ref[q, pl.ds(r, 1), :] * jnp.concatenate(vt, axis=0)
        return carry

    lax.fori_loop(0, groups, group, 0)

    for j in range(HEAD):
        s_ref[j] = s_ref[j] * jnp.concatenate([ge_ref[j]] * nb, axis=0)


def _rwkv_scan(seqs, s0, row_consts, step_consts, *, seq_len, steps):
    t8, _, p = seqs[0].shape
    n = HEAD
    if seq_len >= SUBLANES:
        groups, live = steps // SUBLANES, SUBLANES
    else:
        groups, live = 1, seq_len
    grid = (p // LANES, t8 // groups)
    in_spec = pl.BlockSpec((groups, n * SUBLANES, LANES), lambda g, t: (t, 0, g))
    out_spec = pl.BlockSpec((groups * live, n, LANES), lambda g, t: (t, 0, g))
    st_spec = pl.BlockSpec((n, n, LANES), lambda g, t: (0, 0, g))
    rc_spec = pl.BlockSpec((n, LANES), lambda g, t: (0, g))
    sc_spec = pl.BlockSpec((n, SUBLANES, LANES), lambda g, t: (0, 0, g))
    kern = functools.partial(_rwkv_scan_kernel, groups=groups, live=live)
    return pl.pallas_call(
        kern, grid=grid,
        in_specs=[in_spec] * 5 + [st_spec] + [rc_spec] * 2 + [sc_spec] * 3,
        out_specs=[out_spec, st_spec],
        out_shape=[jax.ShapeDtypeStruct((seq_len, n, p), F32), jax.ShapeDtypeStruct((n, n, p), F32)],
        scratch_shapes=[pltpu.VMEM((4, groups, n * SUBLANES, LANES), F32), pltpu.VMEM((groups, SUBLANES, LANES), F32),
                        pltpu.VMEM((n, SUBLANES, LANES), F32)],
        compiler_params=_cparams(2), name="rwkv_scan")(*seqs, s0, *row_consts, *step_consts)


def _nh_order(x, axis):
    shape = x.shape
    x = x.reshape(shape[:axis] + (RWKV_HEADS, HEAD) + shape[axis + 1:])
    return jnp.swapaxes(x, axis, axis + 1).reshape(shape)


def _hn_order(x, axis):
    shape = x.shape
    x = x.reshape(shape[:axis] + (HEAD, RWKV_HEADS) + shape[axis + 1:])
    return jnp.swapaxes(x, axis, axis + 1).reshape(shape)


def _to_pairs_kernel(x_ref, o_ref, a_scr):
    nb = x_ref.shape[0]
    for b in range(nb):
        a_scr[b] = x_ref[b].T
    for n in range(HEAD):
        blk = a_scr[:, n * RWKV_HEADS:(n + 1) * RWKV_HEADS, :]
        y = blk.reshape(nb * RWKV_HEADS, blk.shape[2]).T
        o_ref[:, n * SUBLANES:(n + 1) * SUBLANES, :] = y.reshape(y.shape[0] // SUBLANES, SUBLANES, LANES)


def _to_pairs(x, batch, seq_len, tt):
    assert batch * RWKV_HEADS == LANES
    x3 = x.reshape(batch, seq_len, D_MODEL)
    return pl.pallas_call(
        _to_pairs_kernel, grid=(seq_len // tt,),
        in_specs=[pl.BlockSpec((batch, tt, D_MODEL), lambda t: (0, t, 0))],
        out_specs=pl.BlockSpec((tt // SUBLANES, HEAD * SUBLANES, LANES), lambda t: (t, 0, 0)),
        out_shape=jax.ShapeDtypeStruct((seq_len // SUBLANES, HEAD * SUBLANES, LANES), F32),
        scratch_shapes=[pltpu.VMEM((batch, D_MODEL, tt), F32)],
        compiler_params=_cparams(1), name="to_pairs")(x3)


def _from_pairs_kernel(o_ref, x_ref, a_scr):
    nb = x_ref.shape[0]
    for n in range(HEAD):
        blk = o_ref[:, n, :].T
        a_scr[:, n * RWKV_HEADS:(n + 1) * RWKV_HEADS, :] = blk.reshape(nb, RWKV_HEADS, blk.shape[1])
    for b in range(nb):
        x_ref[b] = a_scr[b].T


def _from_pairs(o, batch, seq_len, tt):
    assert batch * RWKV_HEADS == LANES
    out = pl.pallas_call(
        _from_pairs_kernel, grid=(seq_len // tt,),
        in_specs=[pl.BlockSpec((tt, HEAD, LANES), lambda t: (t, 0, 0))],
        out_specs=pl.BlockSpec((batch, tt, D_MODEL), lambda t: (0, t, 0)),
        out_shape=jax.ShapeDtypeStruct((batch, seq_len, D_MODEL), F32),
        scratch_shapes=[pltpu.VMEM((batch, D_MODEL, tt), F32)],
        compiler_params=_cparams(1), name="from_pairs")(o)
    return out.reshape(batch * seq_len, D_MODEL)


def _to_pairs_xla(x, batch, seq_len):
    assert seq_len < SUBLANES
    x = jnp.transpose(x.reshape(batch, seq_len, HEAD, RWKV_HEADS), (2, 1, 0, 3))
    x = x.reshape(HEAD, seq_len, batch * RWKV_HEADS)
    x = jnp.pad(x, ((0, 0), (0, SUBLANES - seq_len), (0, 0)), constant_values=1.0)
    return x.reshape(1, HEAD * SUBLANES, batch * RWKV_HEADS)


def _from_pairs_xla(o, batch, seq_len):
    o = o.reshape(seq_len, HEAD, batch, RWKV_HEADS)
    return jnp.transpose(o, (2, 0, 1, 3)).reshape(batch * seq_len, D_MODEL)


def _head_const_pairs(x, batch):
    x = x.reshape(RWKV_HEADS, HEAD).T
    return jnp.tile(x, (1, batch))


def _count_ge(score, thr):
    return jnp.sum(jnp.where(score >= thr, 1.0, 0.0), axis=-1, keepdims=True)


def _select_topk(score, kpos, k_sel, bisect_steps):
    kf = float(k_sel)
    vis = score > _NEG_INF
    nvis = jnp.sum(jnp.where(vis, 1.0, 0.0), axis=-1, keepdims=True)
    need_sel = nvis > kf
    rowmax = jnp.where(need_sel, jnp.max(score, axis=-1, keepdims=True), 0.0)
    rowmin = jnp.where(need_sel, jnp.min(jnp.where(vis, score, _POS_INF), axis=-1, keepdims=True), 0.0)

    def bis(_, c):
        lo, hi, chi = c
        hfin = jnp.where(hi == _POS_INF, rowmax, hi)
        piv = 0.5 * lo + 0.5 * hfin
        cnt = _count_ge(score, piv)
        ge = cnt >= kf
        return jnp.where(ge, piv, lo), jnp.where(ge, hi, piv), jnp.where(ge, chi, cnt)

    lo, hi, chi = lax.fori_loop(0, bisect_steps, bis,
                                (rowmin, jnp.full_like(rowmin, _POS_INF), jnp.zeros_like(rowmin)))

    def walk_cond(c):
        return jnp.min(c[4]) < 0.5

    def walk(c):
        hi, chi, tau, cgt, done, ceq = c
        pending = done < 0.5
        bmax = jnp.max(jnp.where(score < hi, score, _NEG_INF), axis=-1, keepdims=True)
        cnt = _count_ge(score, bmax)
        fin = jnp.logical_and(cnt >= kf, pending)
        tau = jnp.where(fin, bmax, tau)
        cgt = jnp.where(fin, chi, cgt)
        ceq = jnp.where(fin, cnt - chi, ceq)
        adv = jnp.logical_and(cnt < kf, pending)
        hi = jnp.where(adv, bmax, hi)
        chi = jnp.where(adv, cnt, chi)
        return hi, chi, tau, cgt, jnp.where(fin, 1.0, done), ceq

    done0 = jnp.where(need_sel, 0.0, 1.0)
    neg = jnp.full_like(rowmin, _NEG_INF)
    zero = jnp.zeros_like(rowmin)
    _, _, tau, cgt, _, ceq = lax.while_loop(walk_cond, walk, (hi, chi, neg, zero, done0, zero))

    need = kf - cgt
    excess = jnp.logical_and(need_sel, ceq > need)
    eq = score == tau
    n_keys = score.shape[-1]

    def tie_break(_):
        def body(_, c):
            plo, phi = c
            mid = lax.shift_right_arithmetic(plo + phi, 1)
            cnt = jnp.sum(jnp.where(jnp.logical_and(eq, kpos <= mid), 1.0, 0.0), axis=-1, keepdims=True)
            ge = cnt >= need
            return jnp.where(ge, plo, mid), jnp.where(ge, mid, phi)
        plo0 = jnp.full(tau.shape, -1, jnp.int32)
        phi0 = jnp.full(tau.shape, n_keys - 1, jnp.int32)
        _, phi = lax.fori_loop(0, n_keys.bit_length() + 1, body, (plo0, phi0))
        return jnp.where(excess, phi, n_keys)

    any_excess = jnp.max(jnp.where(excess, 1.0, 0.0)) > 0.0
    pcut = lax.cond(any_excess, tie_break, lambda _: jnp.full(tau.shape, n_keys, jnp.int32), 0)
    pcut = jnp.where(need_sel, pcut, -1)
    return jnp.logical_or(score > tau, jnp.logical_and(eq, kpos <= pcut))


_MASKED = -1e30


def _select_topk_chunked(s_ref, nch, ck, k_sel, nvis, bisect_steps):
    rows = s_ref.shape[0]
    kf = float(k_sel)
    nfold = ck // LANES

    def chunk(c):
        return s_ref[:, pl.ds(pl.multiple_of(c * ck, ck), ck)]

    def fold(x, op):
        out = x[:, 0:LANES]
        for i in range(1, nfold):
            out = op(out, x[:, i * LANES:(i + 1) * LANES])
        return out

    def count(pred):
        def body(c, acc):
            return acc + fold(jnp.where(pred(chunk(c), c), 1.0, 0.0), jnp.add)
        acc = lax.fori_loop(0, nch, body, jnp.zeros((rows, LANES), F32))
        return jnp.sum(acc, axis=-1, keepdims=True)

    def row_max(val):
        def body(c, acc):
            return jnp.maximum(acc, fold(val(chunk(c)), jnp.maximum))
        acc = lax.fori_loop(0, nch, body, jnp.full((rows, LANES), _NEG_INF, F32))
        return jnp.max(acc, axis=-1, keepdims=True)

    need_sel = nvis > kf
    rowmax = jnp.where(need_sel, row_max(lambda x: x), 0.0)
    rowmin = jnp.where(need_sel, -row_max(lambda x: jnp.where(x > _NEG_INF, -x, _NEG_INF)), 0.0)

    def bis(_, c):
        lo, hi, chi = c
        hfin = jnp.where(hi == _POS_INF, rowmax, hi)
        piv = 0.5 * lo + 0.5 * hfin
        cnt = count(lambda x, _c: x >= piv)
        ge = cnt >= kf
        return jnp.where(ge, piv, lo), jnp.where(ge, hi, piv), jnp.where(ge, chi, cnt)

    lo, hi, chi = lax.fori_loop(0, bisect_steps, bis,
                                (rowmin, jnp.full_like(rowmin, _POS_INF), jnp.zeros_like(rowmin)))

    def walk_cond(c):
        return jnp.min(c[4]) < 0.5

    def walk(c):
        hi, chi, tau, cgt, done = c
        pending = done < 0.5
        bmax = row_max(lambda x: jnp.where(x < hi, x, _NEG_INF))
        cnt = count(lambda x, _c: x >= bmax)
        fin = jnp.logical_and(cnt >= kf, pending)
        tau = jnp.where(fin, bmax, tau)
        cgt = jnp.where(fin, chi, cgt)
        adv = jnp.logical_and(cnt < kf, pending)
        hi = jnp.where(adv, bmax, hi)
        chi = jnp.where(adv, cnt, chi)
        return hi, chi, tau, cgt, jnp.where(fin, 1.0, done)

    neg = jnp.full_like(rowmin, _NEG_INF)
    zero = jnp.zeros_like(rowmin)
    _, _, tau, cgt, _ = lax.while_loop(walk_cond, walk, (hi, chi, neg, zero, jnp.where(need_sel, 0.0, 1.0)))

    need = jnp.where(need_sel, kf - cgt, 0.0)
    tri = jnp.where(lax.broadcasted_iota(jnp.int32, (ck, ck), 0) <= lax.broadcasted_iota(jnp.int32, (ck, ck), 1),
                    1.0, 0.0).astype(BF16)

    def write(c, ties_before):
        x = chunk(c)
        eq = x == tau
        rank = jnp.dot(jnp.where(eq, 1.0, 0.0).astype(BF16), tri, preferred_element_type=F32) + ties_before
        sel = jnp.logical_or(x > tau, jnp.logical_and(eq, rank <= need))
        s_ref[:, pl.ds(pl.multiple_of(c * ck, ck), ck)] = jnp.where(sel, 0.0, _MASKED)
        return rank[:, ck - 1:ck]

    lax.fori_loop(0, nch, write, zero)


def _attn_prep_kernel(q_ref, qi_ref, kv_ref, kiw_ref, qg_ref, kg_ref, kn_o, qb_o, qib_o, kvb_o, kib_o):
    w = ATT_KV_HEADS * HEAD
    for h in range(ATT_KV_HEADS):
        hs = slice(h * HEAD, (h + 1) * HEAD)
        x = kv_ref[:, hs]
        kn = x * lax.rsqrt(jnp.mean(x * x, axis=-1, keepdims=True) + NORM_EPS) * kg_ref[...]
        kn_o[:, hs] = kn
        kvb_o[:, hs] = kn.astype(BF16)
        v = kv_ref[:, w + h * HEAD:w + (h + 1) * HEAD].astype(BF16)
        kvb_o[:, w + h * LANES:w + (h + 1) * LANES] = jnp.concatenate([v, jnp.ones_like(v)], axis=1)
    for h in range(ATT_HEADS):
        x = q_ref[:, h * HEAD:(h + 1) * HEAD]
        qn = x * lax.rsqrt(jnp.mean(x * x, axis=-1, keepdims=True) + NORM_EPS) * (qg_ref[...] * HEAD ** -0.5)
        qb_o[h] = qn.astype(BF16)
    qib_o[...] = (qi_ref[...] * HEAD ** -0.5).astype(BF16)
    kib_o[...] = kiw_ref[:, 0:HEAD].astype(BF16)


def _attn_prep(za, qn_g, kn_g, tm):
    m = za.shape[0]
    w = ATT_KV_HEADS * HEAD
    wi = IDX_HEADS * HEAD
    row = lambda width, blk: pl.BlockSpec((tm, width), lambda i, blk=blk: (i, blk))
    gspec = pl.BlockSpec((1, HEAD), lambda i: (0, 0))
    return pl.pallas_call(
        _attn_prep_kernel, grid=(m // tm,),
        in_specs=[row(D_MODEL, ZA_Q // D_MODEL), row(wi, ZA_QI // wi), row(2 * w, ZA_KV // (2 * w)),
                  row(LANES, ZA_KIW // LANES), gspec, gspec],
        out_specs=[row(w, 0), pl.BlockSpec((ATT_HEADS, tm, HEAD), lambda i: (0, i, 0)), row(wi, 0),
                   row(w + ATT_KV_HEADS * LANES, 0), row(HEAD, 0)],
        out_shape=[jax.ShapeDtypeStruct((m, w), F32), jax.ShapeDtypeStruct((ATT_HEADS, m, HEAD), BF16),
                   jax.ShapeDtypeStruct((m, wi), BF16), jax.ShapeDtypeStruct((m, w + ATT_KV_HEADS * LANES), BF16),
                   jax.ShapeDtypeStruct((m, HEAD), BF16)],
        compiler_params=_cparams(1), name="attn_prep")(za, za, za, za, qn_g.reshape(1, HEAD), kn_g.reshape(1, HEAD))


def _bucket_edges():
    max_exact = REL_BUCKETS // 2
    d = np.arange(REL_MAX_DIST + 1)
    df = np.maximum(d, 1).astype(np.float32)
    large = max_exact + (np.log(df / max_exact) / math.log(REL_MAX_DIST / max_exact)
                         * (REL_BUCKETS - max_exact)).astype(np.int32)
    bucket = np.where(d < max_exact, d, np.minimum(large, REL_BUCKETS - 1))
    return [int(np.argmax(bucket >= b)) for b in range(REL_BUCKETS)]


_BUCKET_EDGES = _bucket_edges()


def _rel_bias_lookup(dist, value_of_bucket):
    bias = value_of_bucket(REL_BUCKETS - 1)
    for b in range(REL_BUCKETS - 2, -1, -1):
        bias = jnp.where(dist < _BUCKET_EDGES[b + 1], value_of_bucket(b), bias)
    return bias


def _attn_prompt_kernel(rb_ref, q_ref, qi_ref, kiwq_ref, g_ref, kvb_ref, kib_ref, o_ref, s_scr, tbd_ref,
                        l_scr, mx_scr, acc_scr, *, tq, k_sel, bisect_steps):
    qt = pl.program_id(0)
    q0 = qt * tq
    ck = tq
    rg_rows = LANES
    n_rg = tq // rg_rows
    w = ATT_KV_HEADS * HEAD

    @pl.when(jnp.logical_and(pl.program_id(1) == 0, qt == 0))
    def _():
        rr = lax.broadcasted_iota(jnp.int32, (tq, tq), 0) - lax.broadcasted_iota(jnp.int32, (tq, tq), 1)
        for h in range(ATT_HEADS):
            far = rb_ref[REL_BUCKETS - 1, h]
            tbd_ref[h, 0] = _rel_bias_lookup(rr, lambda b: rb_ref[b, h]) - far
            tbd_ref[h, 1] = _rel_bias_lookup(rr + tq, lambda b: rb_ref[b, h]) - far

    def chunk_start(c):
        return pl.multiple_of(c * ck, ck)

    wcol = kiwq_ref[:, HEAD:HEAD + IDX_HEADS] * (IDX_HEADS ** -0.5)
    qi_h = [[qi_ref[rg * rg_rows:(rg + 1) * rg_rows, h * HEAD:(h + 1) * HEAD] for h in range(IDX_HEADS)]
            for rg in range(n_rg)]
    w_h = [[wcol[rg * rg_rows:(rg + 1) * rg_rows, h:h + 1] for h in range(IDX_HEADS)] for rg in range(n_rg)]

    def score_chunk(c, carry):
        k0 = chunk_start(c)
        kc = kib_ref[pl.ds(k0, ck), :]
        kpos = k0 + lax.broadcasted_iota(jnp.int32, (rg_rows, ck), 1)
        for rg in range(n_rg):
            acc = jnp.zeros((rg_rows, ck), F32)
            for h in range(IDX_HEADS):
                acc = acc + jnp.maximum(_dot_nt(qi_h[rg][h], kc), 0.0) * w_h[rg][h]
            qpos = q0 + rg * rg_rows + lax.broadcasted_iota(jnp.int32, (rg_rows, ck), 0)
            s_scr[rg * rg_rows:(rg + 1) * rg_rows, pl.ds(k0, ck)] = jnp.where(kpos <= qpos, acc, _NEG_INF)
        return carry

    nch = qt + 1
    lax.fori_loop(0, nch, score_chunk, 0)

    nvis = (q0 + 1 + lax.broadcasted_iota(jnp.int32, (tq, 1), 0)).astype(F32)
    _select_topk_chunked(s_scr, nch, ck, k_sel, nvis, bisect_steps)

    g_rows = ATT_GROUP * tq
    n_far = jnp.maximum(qt - 1, 0)
    for kvh in range(ATT_KV_HEADS):
        gs = slice(kvh * ATT_GROUP, (kvh + 1) * ATT_GROUP)
        qg = q_ref[gs, :, :].reshape(g_rows, HEAD)
        mx_scr[...] = jnp.full(mx_scr.shape, _MASKED, F32)
        acc_scr[...] = jnp.zeros(acc_scr.shape, F32)

        def logits_chunk(c, carry, near, qg=qg, gs=gs, kvh=kvh):
            k0 = chunk_start(c)
            kc = kvb_ref[pl.ds(k0, ck), kvh * HEAD:(kvh + 1) * HEAD]
            s3 = _dot_nt(qg, kc).reshape(ATT_GROUP, tq, ck) + s_scr[:, pl.ds(k0, ck)][None]
            if near:
                s3 = s3 + tbd_ref[gs, qt - c, :, :]
            s = s3.reshape(g_rows, ck)
            l_scr[:, pl.ds(k0, ck)] = s
            m = mx_scr[...]
            for i in range(ck // LANES):
                m = jnp.maximum(m, s[:, i * LANES:(i + 1) * LANES])
            mx_scr[...] = m
            return carry

        lax.fori_loop(0, n_far, functools.partial(logits_chunk, near=False), 0)
        lax.fori_loop(n_far, nch, functools.partial(logits_chunk, near=True), 0)
        mx_scr[...] = jnp.broadcast_to(jnp.max(mx_scr[...], axis=-1, keepdims=True), mx_scr.shape)

        def pv_chunk(c, carry, kvh=kvh):
            k0 = chunk_start(c)
            vx = kvb_ref[pl.ds(k0, ck), w + kvh * LANES:w + (kvh + 1) * LANES]
            m = mx_scr[...]
            p = jnp.concatenate([jnp.exp(l_scr[:, pl.ds(k0 + i * LANES, LANES)] - m)
                                 for i in range(ck // LANES)], axis=1).astype(BF16)
            acc_scr[...] += jnp.dot(p, vx, preferred_element_type=F32)
            return carry

        lax.fori_loop(0, nch, pv_chunk, 0)

        acc = acc_scr[...]
        o = acc[:, 0:HEAD] / acc[:, HEAD:HEAD + 1]
        for pp in range(ATT_GROUP // 2):
            cols = slice((kvh * ATT_GROUP // 2 + pp) * LANES, (kvh * ATT_GROUP // 2 + pp + 1) * LANES)
            gh = g_ref[:, cols]
            pair = jnp.concatenate([o[2 * pp * tq:(2 * pp + 1) * tq], o[(2 * pp + 1) * tq:(2 * pp + 2) * tq]], axis=1)
            o_ref[:, cols] = pair * (gh * _sigmoid(gh))


def _attn_prompt(za, kn, qb, qib, kvb, kib, *, rel_bias, batch, seq_len, tq, k_sel):
    m = za.shape[0]
    nq = seq_len // tq
    w = ATT_KV_HEADS * HEAD
    wi = IDX_HEADS * HEAD
    assert REL_MAX_DIST <= tq and tq % LANES == 0
    row_map = lambda blk: (lambda t, b, blk=blk: (b * nq + t, blk))
    key_map = lambda t, b: (b, 0)
    kern = functools.partial(_attn_prompt_kernel, tq=tq, k_sel=k_sel, bisect_steps=14)
    return pl.pallas_call(
        kern, grid=(nq, batch),
        in_specs=[
            pl.BlockSpec(memory_space=pltpu.SMEM),
            pl.BlockSpec((ATT_HEADS, tq, HEAD), lambda t, b: (0, b * nq + t, 0)),
            pl.BlockSpec((tq, wi), row_map(0)),
            pl.BlockSpec((tq, LANES), row_map(ZA_KIW // LANES)),
            pl.BlockSpec((tq, D_MODEL), row_map(ZA_GATT // D_MODEL)),
            pl.BlockSpec((seq_len, w + ATT_KV_HEADS * LANES), key_map),
            pl.BlockSpec((seq_len, HEAD), key_map),
        ],
        out_specs=pl.BlockSpec((tq, D_MODEL), row_map(0)),
        out_shape=jax.ShapeDtypeStruct((m, D_MODEL), F32),
        scratch_shapes=[pltpu.VMEM((tq, seq_len), F32), pltpu.VMEM((ATT_HEADS, 2, tq, tq), F32),
                        pltpu.VMEM((ATT_GROUP * tq, seq_len), F32), pltpu.VMEM((ATT_GROUP * tq, LANES), F32),
                        pltpu.VMEM((ATT_GROUP * tq, LANES), F32)],
        compiler_params=_cparams(2), name="attn_prompt")(rel_bias, qb, qib, za, za, kvb, kib)


def _sample_score_kernel(pt_ref, q_ref, w_ref, kiw_new_ref, *rest, n_pages, page, dec_seq):
    page_refs = rest[:n_pages]
    o_ref = rest[n_pages]
    past = n_pages * page
    kidx = jnp.concatenate([r[...] for r in page_refs], axis=0).astype(BF16)
    new = kiw_new_ref[:, 0:HEAD]
    new = jnp.concatenate([new, jnp.zeros((LANES - dec_seq, HEAD), F32)], axis=0).astype(BF16)
    q = (q_ref[...] * (HEAD ** -0.5)).astype(BF16)
    lg = jnp.concatenate([_dot_nt(q, kidx), _dot_nt(q, new)], axis=1)
    wr = jnp.maximum(lg, 0.0) * (w_ref[...] * (IDX_HEADS ** -0.5))
    n_keys = past + LANES
    sc = jnp.sum(wr.reshape(dec_seq, IDX_HEADS, n_keys), axis=1)
    kpos = lax.broadcasted_iota(jnp.int32, (dec_seq, n_keys), 1)
    tpos = lax.broadcasted_iota(jnp.int32, (dec_seq, n_keys), 0)
    o_ref[...] = jnp.where(kpos <= past + tpos, sc, _NEG_INF)


def _sample_select_kernel(s_ref, o_ref, *, k_sel, bisect_steps):
    score = s_ref[...]
    kpos = lax.broadcasted_iota(jnp.int32, score.shape, 1)
    sel = _select_topk(score, kpos, k_sel, bisect_steps)
    o_ref[...] = jnp.where(sel, 0.0, _NEG_INF)


def _sample_attn_kernel(pt_ref, q_ref, g_ref, kn_new_ref, kv_new_ref, mask_ref, rb_rows_ref, qg_ref, *rest,
                        n_pages, page, dec_seq):
    k_pages = rest[:n_pages]
    v_pages = rest[n_pages:2 * n_pages]
    o_ref = rest[2 * n_pages]
    bias_ref = rest[2 * n_pages + 1]
    w = ATT_KV_HEADS * HEAD
    rows = dec_seq * ATT_HEADS
    n_keys = n_pages * page + LANES

    @pl.when(pl.program_id(0) == 0)
    def _():
        t_row = lax.div(lax.broadcasted_iota(jnp.int32, (rows, n_keys), 0), ATT_HEADS)
        dist = n_pages * page + t_row - lax.broadcasted_iota(jnp.int32, (rows, n_keys), 1)
        bias_ref[...] = _rel_bias_lookup(dist, lambda b: rb_rows_ref[:, b:b + 1])

    pad = jnp.zeros((LANES - dec_seq, HEAD), F32)

    def keys_of(pages, new, kvh):
        parts = [r[:, kvh, :] for r in pages] + [new[:, kvh * HEAD:(kvh + 1) * HEAD], pad]
        return jnp.concatenate(parts, axis=0).astype(BF16)

    q = q_ref[...]
    q = q * lax.rsqrt(jnp.mean(q * q, axis=-1, keepdims=True) + NORM_EPS) * qg_ref[...]
    qb = q.astype(BF16)
    head = lax.rem(lax.broadcasted_iota(jnp.int32, (rows, 1), 0), ATT_HEADS)
    first = head < ATT_GROUP
    k_new = kn_new_ref[...]
    lg = jnp.where(first, _dot_nt(qb, keys_of(k_pages, k_new, 0)), _dot_nt(qb, keys_of(k_pages, k_new, 1)))
    mask = jnp.broadcast_to(mask_ref[...][:, None, :], (dec_seq, ATT_HEADS, n_keys)).reshape(rows, n_keys)
    s = lg * (HEAD ** -0.5) + bias_ref[...] + mask
    p = jnp.exp(s - jnp.max(s, axis=-1, keepdims=True))
    l = jnp.sum(p, axis=-1, keepdims=True)
    pb = p.astype(BF16)
    v_new = kv_new_ref[:, w:2 * w]
    o = jnp.where(first, jnp.dot(pb, keys_of(v_pages, v_new, 0), preferred_element_type=F32),
                  jnp.dot(pb, keys_of(v_pages, v_new, 1), preferred_element_type=F32)) / l
    g = g_ref[...]
    o_ref[...] = o * (g * _sigmoid(g))


def _attn_sample(za, kn, qb, qib, kvb, kib, *, cache_k, cache_v, cache_kidx, page_table, rel_bias, qn_g,
                 dec_seq, k_sel):
    nb, n_pages = page_table.shape
    n_phys, page = cache_k.shape[0], cache_k.shape[1]
    past = n_pages * page
    n_keys = past + LANES
    w = ATT_KV_HEADS * HEAD
    ck, cv = cache_k, cache_v
    ci = cache_kidx.reshape(n_phys, page, HEAD)
    kv_page_specs = [pl.BlockSpec((None, page, ATT_KV_HEADS, HEAD), lambda b, pt, j=j: (pt[b, j], 0, 0, 0))
                     for j in range(n_pages)]
    za3 = za.reshape(nb, dec_seq, ZA_COLS)
    kn3 = kn.reshape(nb, dec_seq, w)
    qi = za[:, ZA_QI:ZA_QI + IDX_HEADS * HEAD].reshape(nb, dec_seq * IDX_HEADS, HEAD)
    wi = za[:, ZA_KIW + HEAD:ZA_KIW + HEAD + IDX_HEADS].reshape(nb, dec_seq * IDX_HEADS, 1)
    qa = za[:, ZA_Q:ZA_Q + D_MODEL].reshape(nb, dec_seq * ATT_HEADS, HEAD)
    ga = za[:, ZA_GATT:ZA_GATT + D_MODEL].reshape(nb, dec_seq * ATT_HEADS, HEAD)

    def page_specs(width):
        return [pl.BlockSpec((None, page, width), lambda b, pt, j=j: (pt[b, j], 0, 0)) for j in range(n_pages)]

    per_b = lambda r, c: pl.BlockSpec((None, r, c), lambda b, pt: (b, 0, 0))
    kiw_new = pl.BlockSpec((None, dec_seq, LANES), lambda b, pt: (b, 0, ZA_KIW // LANES))

    scores = pl.pallas_call(
        functools.partial(_sample_score_kernel, n_pages=n_pages, page=page, dec_seq=dec_seq),
        grid_spec=pltpu.PrefetchScalarGridSpec(
            num_scalar_prefetch=1, grid=(nb,),
            in_specs=[per_b(dec_seq * IDX_HEADS, HEAD), per_b(dec_seq * IDX_HEADS, 1), kiw_new] + page_specs(HEAD),
            out_specs=per_b(dec_seq, n_keys)),
        out_shape=jax.ShapeDtypeStruct((nb, dec_seq, n_keys), F32),
        compiler_params=_cparams(1), name="sample_scores")(page_table, qi, wi, za3, *([ci] * n_pages))

    rows = nb * dec_seq
    tr = min(rows, 128)
    mask = pl.pallas_call(
        functools.partial(_sample_select_kernel, k_sel=k_sel, bisect_steps=14),
        grid=(rows // tr,),
        in_specs=[pl.BlockSpec((tr, n_keys), lambda i: (i, 0))],
        out_specs=pl.BlockSpec((tr, n_keys), lambda i: (i, 0)),
        out_shape=jax.ShapeDtypeStruct((rows, n_keys), F32),
        compiler_params=_cparams(1), name="sample_select")(scores.reshape(rows, n_keys))

    rb_rows = jnp.tile(rel_bias.T, (dec_seq, 1))

    const = lambda shape: pl.BlockSpec(shape, lambda b, pt: (0,) * len(shape))
    kv_new = pl.BlockSpec((None, dec_seq, 2 * w), lambda b, pt: (b, 0, ZA_KV // (2 * w)))
    out = pl.pallas_call(
        functools.partial(_sample_attn_kernel, n_pages=n_pages, page=page, dec_seq=dec_seq),
        grid_spec=pltpu.PrefetchScalarGridSpec(
            num_scalar_prefetch=1, grid=(nb,),
            in_specs=[per_b(dec_seq * ATT_HEADS, HEAD), per_b(dec_seq * ATT_HEADS, HEAD), per_b(dec_seq, w),
                      kv_new, per_b(dec_seq, n_keys), const((dec_seq * ATT_HEADS, REL_BUCKETS)), const((1, HEAD))]
            + kv_page_specs + kv_page_specs,
            out_specs=per_b(dec_seq * ATT_HEADS, HEAD),
            scratch_shapes=[pltpu.VMEM((dec_seq * ATT_HEADS, n_keys), F32)]),
        out_shape=jax.ShapeDtypeStruct((nb, dec_seq * ATT_HEADS, HEAD), F32),
        compiler_params=_cparams(1), name="sample_attn")(
            page_table, qa, ga, kn3, za3, mask.reshape(nb, dec_seq, n_keys), rb_rows, qn_g.reshape(1, HEAD),
            *([ck] * n_pages), *([cv] * n_pages))
    return out.reshape(rows, D_MODEL)


def _merge_kernel(x_ref, oa_ref, sg_ref, ob_ref, ga_ref, gb_ref, wpa_ref, wpb_ref, wo_ref, y_ref):
    oa = (oa_ref[...] * sg_ref[...]).astype(BF16)
    pa = jnp.dot(oa, wpa_ref[...], preferred_element_type=F32)
    pb = jnp.dot(ob_ref[...].astype(BF16), wpb_ref[...], preferred_element_type=F32)
    merged = _sigmoid(ga_ref[...]) * pa + _sigmoid(gb_ref[...]) * pb
    y_ref[...] = x_ref[...] + jnp.dot(merged.astype(BF16), wo_ref[...], preferred_element_type=F32)


def _merge(x, oa, sg, ob, za, w_pa, w_pb, w_out, tm):
    m, d = x.shape
    row = pl.BlockSpec((tm, d), lambda i: (i, 0))
    wsp = pl.BlockSpec((d, d), lambda i: (0, 0))
    return pl.pallas_call(
        _merge_kernel, grid=(m // tm,),
        in_specs=[row, row, row, row,
                  pl.BlockSpec((tm, d), lambda i: (i, ZA_GA // d)), pl.BlockSpec((tm, d), lambda i: (i, ZA_GB // d)),
                  wsp, wsp, wsp],
        out_specs=row,
        out_shape=jax.ShapeDtypeStruct((m, d), F32),
        compiler_params=_cparams(1), name="merge")(x, oa, sg, ob, za, za, w_pa, w_pb, w_out)


def _layer(x, shift, s0, params, *, batch, seq_len, attend):
    (norm_g, w_r, w_a, mu, w0, w2, a0, a2, k_k, k_a, r_k, lnx_g, lnx_b, qn_g, kn_g, w_pa, w_pb, w_out) = params
    m = batch * seq_len
    tm = min(m, 512)
    xn = _rmsnorm(x, norm_g, tm)
    tm_proj = min(m, 2048)
    zr = _matmul(xn, w_r, tm_proj, RWKV_COLS // 3, "inproj_rwkv")
    za = _matmul(xn, w_a, tm_proj, ZA_COLS // 4, "inproj_attn")

    tile = min(m, 256)
    r, w, k, v, a, sg = _rwkv_prep(zr, shift, mu, w0, w2, a0, a2, batch=batch, seq_len=seq_len, tile=tile)
    relayout_tile = 128
    in_kernel_relayout = seq_len % relayout_tile == 0 and batch * RWKV_HEADS == LANES
    if in_kernel_relayout:
        seqs = [_to_pairs(t, batch, seq_len, relayout_tile) for t in (r, w, k, v, a)]
    else:
        seqs = [_to_pairs_xla(t, batch, seq_len) for t in (r, w, k, v, a)]
    row_consts = [_head_const_pairs(t, batch) for t in (lnx_g, lnx_b)]
    step_consts = [jnp.broadcast_to(_head_const_pairs(t, batch)[:, None, :], (HEAD, SUBLANES, batch * RWKV_HEADS))
                   for t in (r_k.reshape(-1), k_k, k_a)]
    o_t, s_t = _rwkv_scan(seqs, s0, row_consts, step_consts, seq_len=seq_len, steps=min(seq_len, 32))
    if in_kernel_relayout:
        oa = _from_pairs(o_t, batch, seq_len, relayout_tile)
    else:
        oa = _from_pairs_xla(o_t, batch, seq_len)

    kn, qb, qib, kvb, kib = _attn_prep(za, qn_g, kn_g, tm)
    ob = attend(za, kn, qb, qib, kvb, kib)

    y = _merge(x, oa, sg, ob, za, w_pa, w_pb, w_out, min(m, 256))
    return y, zr, za, kn, s_t


def _state_to_pairs(s):
    b, h, n, _ = s.shape
    return jnp.transpose(s, (3, 2, 0, 1)).reshape(n, n, b * h)


def _state_from_pairs(s, batch):
    n = s.shape[0]
    return jnp.transpose(s.reshape(n, n, batch, RWKV_HEADS), (2, 3, 1, 0))


def kernel(x_prompt, x_sample, cache_k, cache_v, cache_kidx, state_wkv, state_shift, page_table, norm_g, w_in,
           shift_mu, w0, w2, a0, a2, k_k, k_a, r_k, lnx_g, lnx_b, q_norm_g, k_norm_g, rel_bias, w_pa, w_pb, w_out):
    bsz, seq, d = x_prompt.shape
    dec_bsz, dec_seq, _ = x_sample.shape
    depth = w_in.shape[0]
    assert depth == 1 and d == D_MODEL
    past_len = page_table.shape[1] * cache_k.shape[2]
    topk_p = min(TOPK_MAX, seq // 4)
    topk_s = min(TOPK_MAX, (past_len + dec_seq) // 4)
    l = 0

    wl = w_in[l]
    c0 = RWKV_COLS
    q_w, kv_w, qi_w = wl[:, c0:c0 + 1024], wl[:, c0 + 1024:c0 + 1280], wl[:, c0 + 1280:c0 + 1792]
    kiw_w = wl[:, c0 + 1792:c0 + 1864]
    rest_w = wl[:, c0 + 1864:]
    zpad = lambda n: jnp.zeros((d, n), wl.dtype)
    w_a = jnp.concatenate([q_w, qi_w, kv_w, kiw_w, zpad(LANES - kiw_w.shape[1]), zpad(LANES), rest_w],
                          axis=1).astype(BF16)
    assert w_a.shape[1] == ZA_COLS
    def rwkv_cols(x, fn):
        lead = x.shape[:-1]
        main = fn(x[..., :4 * d].reshape(lead + (4, d)), len(lead) + 1).reshape(lead + (4 * d,))
        return jnp.concatenate([main, x[..., 4 * d:]], axis=-1)

    w_r = rwkv_cols(wl[:, :c0], _nh_order).astype(BF16)
    params = (norm_g[l], w_r, w_a, rwkv_cols(shift_mu[l], _nh_order), _nh_order(w0[l], 0), _nh_order(w2[l], 1),
              _nh_order(a0[l], 0), _nh_order(a2[l], 1), k_k[l], k_a[l], r_k[l],
              lnx_g[l], lnx_b[l], q_norm_g[l], k_norm_g[l],
              _nh_order(w_pa[l], 0).astype(BF16), w_pb[l].astype(BF16), w_out[l].astype(BF16))

    xp = x_prompt.reshape(bsz * seq, d)
    attend_p = functools.partial(_attn_prompt, rel_bias=rel_bias, batch=bsz, seq_len=seq,
                                 tq=min(seq, 256), k_sel=topk_p)
    yp, zr_p, za_p, kn_p, st_p = _layer(
        xp, jnp.zeros((bsz, 1, RWKV_COLS), F32), jnp.zeros((HEAD, HEAD, bsz * RWKV_HEADS), F32), params,
        batch=bsz, seq_len=seq, attend=attend_p)

    xs = x_sample.reshape(dec_bsz * dec_seq, d)
    attend_s = functools.partial(_attn_sample, cache_k=cache_k[l], cache_v=cache_v[l], cache_kidx=cache_kidx[l],
                                 page_table=page_table, rel_bias=rel_bias, qn_g=q_norm_g[l], dec_seq=dec_seq,
                                 k_sel=topk_s)
    shift_rows = jnp.repeat(rwkv_cols(state_shift[l], _nh_order), dec_seq, axis=0)
    ys, zr_s, za_s, kn_s, st_s = _layer(
        xs, shift_rows, _state_to_pairs(state_wkv[l]), params, batch=dec_bsz, seq_len=dec_seq, attend=attend_s)

    w = ATT_KV_HEADS * HEAD

    def pack(y, zr, za, kn, st, b, t):
        v = za[:, ZA_KV + w:ZA_KV + 2 * w]
        kidx = za[:, ZA_KIW:ZA_KIW + HEAD]
        return (y.reshape(b, t, d),
                kn.reshape(1, b, t, ATT_KV_HEADS, HEAD), v.reshape(1, b, t, ATT_KV_HEADS, HEAD),
                kidx.reshape(1, b, t, HEAD), _state_from_pairs(st, b)[None],
                rwkv_cols(zr.reshape(b, t, RWKV_COLS)[:, -1], _hn_order)[None])

    p = pack(yp, zr_p, za_p, kn_p, st_p, bsz, seq)
    s = pack(ys, zr_s, za_s, kn_s, st_s, dec_bsz, dec_seq)
    return (p[0], s[0]) + p[1:] + s[1:]
```

```python
import functools
import math

import numpy as np
import jax
import jax.numpy as jnp
from jax import lax
from jax.experimental import pallas as pl
from jax.experimental.pallas import tpu as pltpu

F32 = jnp.float32
BF16 = jnp.bfloat16

D_MODEL = 1024
HEAD = 64
RWKV_HEADS = D_MODEL // HEAD
LORA = 64
LNX_EPS = 64e-5
ATT_HEADS = D_MODEL // HEAD
ATT_KV_HEADS = 2
ATT_GROUP = ATT_HEADS // ATT_KV_HEADS
IDX_HEADS = 8
TOPK_MAX = 256
REL_BUCKETS = 32
REL_MAX_DIST = 128
NORM_EPS = 1e-6
RWKV_COLS = 4 * D_MODEL + 2 * LORA

LANES = 128
SUBLANES = 8
VMEM_LIMIT_BYTES = 56 * 1024 * 1024

ZA_Q = 0
ZA_QI = 1024
ZA_KV = 1536
ZA_KIW = 1792
ZA_GATT = 2048
ZA_GA = 3072
ZA_GB = 4096
ZA_COLS = 5120

_NEG_INF = float("-inf")
_POS_INF = float("inf")


def _cparams(n_axes):
    return pltpu.CompilerParams(dimension_semantics=("arbitrary",) * n_axes,
                                vmem_limit_bytes=VMEM_LIMIT_BYTES)


def _sigmoid(x):
    return 1.0 / (1.0 + jnp.exp(-x))


def _dot_nt(a, b):
    return lax.dot_general(a, b, (((1,), (1,)), ((), ())), preferred_element_type=F32)


def _rmsnorm_kernel(x_ref, g_ref, o_ref):
    x = x_ref[...]
    ms = jnp.mean(x * x, axis=-1, keepdims=True)
    o_ref[...] = (x * lax.rsqrt(ms + NORM_EPS) * g_ref[...]).astype(o_ref.dtype)


def _rmsnorm(x, g, tm):
    m, d = x.shape
    return pl.pallas_call(
        _rmsnorm_kernel, grid=(m // tm,),
        in_specs=[pl.BlockSpec((tm, d), lambda i: (i, 0)), pl.BlockSpec((1, d), lambda i: (0, 0))],
        out_specs=pl.BlockSpec((tm, d), lambda i: (i, 0)),
        out_shape=jax.ShapeDtypeStruct((m, d), BF16),
        compiler_params=_cparams(1), name="rmsnorm")(x, g.reshape(1, d))


def _mm_kernel(x_ref, w_ref, o_ref):
    o_ref[...] = jnp.dot(x_ref[...], w_ref[...], preferred_element_type=F32)


def _matmul(x, w, tm, tn, name):
    m, k = x.shape
    n = w.shape[1]
    return pl.pallas_call(
        _mm_kernel, grid=(m // tm, n // tn),
        in_specs=[pl.BlockSpec((tm, k), lambda i, j: (i, 0)), pl.BlockSpec((k, tn), lambda i, j: (0, j))],
        out_specs=pl.BlockSpec((tm, tn), lambda i, j: (i, j)),
        out_shape=jax.ShapeDtypeStruct((m, n), F32),
        compiler_params=_cparams(2), name=name)(x, w)


def _rwkv_prep_kernel(z_ref, prev_ref, shift_ref, mu_ref, w0_ref, w2_ref, a0_ref, a2_ref,
                      r_o, w_o, k_o, v_o, a_o, g_o, *, seq_len, tile):
    z = z_ref[...]
    rows = lax.broadcasted_iota(jnp.int32, z.shape, 0)
    rolled = pltpu.roll(z, 1, 0)
    if seq_len >= tile:
        first = jnp.where(pl.program_id(1) == 0, shift_ref[...], prev_ref[SUBLANES - 1:SUBLANES, :])
        prev = jnp.where(rows == 0, first, rolled)
    else:
        prev = jnp.where(lax.rem(rows, seq_len) == 0, shift_ref[...], rolled)
    zs = z + (prev - z) * mu_ref[...]
    d = D_MODEL
    r = zs[:, 0:d]
    k = zs[:, d:2 * d]
    v = zs[:, 2 * d:3 * d]
    g = zs[:, 3 * d:4 * d]
    wd = zs[:, 4 * d:4 * d + LORA]
    ad = zs[:, 4 * d + LORA:4 * d + 2 * LORA]
    wl = w0_ref[...] + jnp.dot(jnp.tanh(wd).astype(BF16), w2_ref[...], preferred_element_type=F32)
    decay = jnp.exp(-_sigmoid(wl) * math.exp(-0.5))
    a = _sigmoid(a0_ref[...] + jnp.dot(ad.astype(BF16), a2_ref[...], preferred_element_type=F32))
    r_o[...] = r
    w_o[...] = decay
    k_o[...] = k
    v_o[...] = v
    a_o[...] = a
    g_o[...] = g * _sigmoid(g)


def _rwkv_prep(zr, shift, mu, w0, w2, a0, a2, *, batch, seq_len, tile):
    m, c = zr.shape
    d = D_MODEL
    row = lambda x: x.reshape(1, -1)
    consts = [row(mu), row(w0), w2.astype(BF16), row(a0), a2.astype(BF16)]
    const_specs = [pl.BlockSpec(x.shape, lambda *_: (0, 0)) for x in consts]
    if seq_len >= tile:
        nt = seq_len // tile
        grid = (batch, nt)
        zmap = lambda b, t: (b * nt + t, 0)
        pmap = lambda b, t: (jnp.maximum((b * seq_len + t * tile) // SUBLANES - 1, 0), 0)
        shift_spec = pl.BlockSpec((None, 1, c), lambda b, t: (b, 0, 0))
    else:
        grid = (1, m // tile)
        zmap = lambda b, t: (t, 0)
        pmap = lambda b, t: (0, 0)
        shift_spec = pl.BlockSpec((tile, c), zmap)
    out_spec = pl.BlockSpec((tile, d), zmap)
    kern = functools.partial(_rwkv_prep_kernel, seq_len=seq_len, tile=tile)
    return pl.pallas_call(
        kern, grid=grid,
        in_specs=[pl.BlockSpec((tile, c), zmap), pl.BlockSpec((SUBLANES, c), pmap), shift_spec] + const_specs,
        out_specs=[out_spec] * 6,
        out_shape=[jax.ShapeDtypeStruct((m, d), F32)] * 6,
        compiler_params=_cparams(2), name="rwkv_prep")(zr, zr, shift, *consts)


def _rwkv_scan_kernel(r_in, w_in, k_in, v_in, a_in, s0_ref, lg_ref, lb_ref, rk_ref, kkc_ref, kac_ref,
                      o_ref, s_ref, vec_ref, bf_ref, ge_ref, *, groups, live):
    nb = HEAD // SUBLANES

    @pl.when(pl.program_id(1) == 0)
    def _():
        s_ref[...] = s0_ref[...]

    shape4 = (groups, HEAD, SUBLANES, LANES)
    a = a_in[...].reshape(shape4)
    kraw = k_in[...].reshape(shape4)
    r_all = r_in[...].reshape(shape4)
    kkraw = kraw * kkc_ref[...][None]
    n2 = jnp.sum(kkraw * kkraw, axis=1, keepdims=True)
    kk = kkraw / jnp.maximum(jnp.sqrt(n2), 1e-12)
    kmod = kraw * (1.0 + (a - 1.0) * kac_ref[...][None])
    bf_ref[...] = jnp.sum(r_all * kmod * rk_ref[...][None], axis=1)
    row = lax.broadcasted_iota(jnp.int32, (HEAD, SUBLANES, LANES), 1)
    g_end = jnp.ones((HEAD, SUBLANES, LANES), F32)
    for q in range(groups):
        g = w_in[q].reshape(HEAD, SUBLANES, LANES)
        for sh in (1, 2, 4):
            g = g * jnp.where(row >= sh, pltpu.roll(g, sh, 1), 1.0)
        g = g * g_end
        g_prev = jnp.where(row >= 1, pltpu.roll(g, 1, 1), g_end)
        g_inv = 1.0 / g
        flat = (HEAD * SUBLANES, LANES)
        vec_ref[0, q] = (-kk[q] * g_prev).reshape(flat)
        vec_ref[1, q] = (kk[q] * a[q] * g_inv).reshape(flat)
        vec_ref[2, q] = (kmod[q] * g_inv).reshape(flat)
        vec_ref[3, q] = (r_all[q] * g).reshape(flat)
        last = live - 1 if q == groups - 1 else SUBLANES - 1
        g_end = jnp.broadcast_to(g[:, last:last + 1, :], (HEAD, SUBLANES, LANES))
    ge_ref[...] = g_end

    def group(q, carry):
        for r in range(live):
            def row_of(idx, j):
                return jnp.broadcast_to(vec_ref[idx, q, pl.ds(j * SUBLANES + r, 1), :], (SUBLANES, LANES))

            sa = [jnp.zeros((SUBLANES, LANES), F32) for _ in range(nb)]
            for j in range(HEAD):
                nk = row_of(0, j)
                for ib in range(nb):
                    sa[ib] = sa[ib] + s_ref[j, ib * SUBLANES:(ib + 1) * SUBLANES, :] * nk
            vt = [v_in[q, pl.ds(ib * SUBLANES * SUBLANES + r, SUBLANES, stride=SUBLANES), :] for ib in range(nb)]
            out = [jnp.zeros((SUBLANES, LANES), F32) for _ in range(nb)]
            for j in range(HEAD):
                kaj = row_of(1, j)
                kj = row_of(2, j)
                rj = row_of(3, j)
                for ib in range(nb):
                    sl = slice(ib * SUBLANES, (ib + 1) * SUBLANES)
                    sn = s_ref[j, sl, :] + sa[ib] * kaj + vt[ib] * kj
                    s_ref[j, sl, :] = sn
                    out[ib] = out[ib] + sn * rj
            o = jnp.concatenate(out, axis=0)
            mean = jnp.mean(o, axis=0, keepdims=True)
            dev = o - mean
            var = jnp.mean(dev * dev, axis=0, keepdims=True)
            y = dev * lax.rsqrt(var + LNX_EPS) * lg_ref[...] + lb_ref[...]
            o_ref[q * live + r] = y + bf_ref[q, pl.ds(r, 1), :] * jnp.concatenate(vt, axis=0)
        return carry

    lax.fori_loop(0, groups, group, 0)

    for j in range(HEAD):
        s_ref[j] = s_ref[j] * jnp.concatenate([ge_ref[j]] * nb, axis=0)


def _rwkv_scan(seqs, s0, row_consts, step_consts, *, seq_len, steps):
    t8, _, p = seqs[0].shape
    n = HEAD
    if seq_len >= SUBLANES:
        groups, live = steps // SUBLANES, SUBLANES
    else:
        groups, live = 1, seq_len
    grid = (p // LANES, t8 // groups)
    in_spec = pl.BlockSpec((groups, n * SUBLANES, LANES), lambda g, t: (t, 0, g))
    out_spec = pl.BlockSpec((groups * live, n, LANES), lambda g, t: (t, 0, g))
    st_spec = pl.BlockSpec((n, n, LANES), lambda g, t: (0, 0, g))
    rc_spec = pl.BlockSpec((n, LANES), lambda g, t: (0, g))
    sc_spec = pl.BlockSpec((n, SUBLANES, LANES), lambda g, t: (0, 0, g))
    kern = functools.partial(_rwkv_scan_kernel, groups=groups, live=live)
    return pl.pallas_call(
        kern, grid=grid,
        in_specs=[in_spec] * 5 + [st_spec] + [rc_spec] * 2 + [sc_spec] * 3,
        out_specs=[out_spec, st_spec],
        out_shape=[jax.ShapeDtypeStruct((seq_len, n, p), F32), jax.ShapeDtypeStruct((n, n, p), F32)],
        scratch_shapes=[pltpu.VMEM((4, groups, n * SUBLANES, LANES), F32), pltpu.VMEM((groups, SUBLANES, LANES), F32),
                        pltpu.VMEM((n, SUBLANES, LANES), F32)],
        compiler_params=_cparams(2), name="rwkv_scan")(*seqs, s0, *row_consts, *step_consts)


def _nh_order(x, axis):
    shape = x.shape
    x = x.reshape(shape[:axis] + (RWKV_HEADS, HEAD) + shape[axis + 1:])
    return jnp.swapaxes(x, axis, axis + 1).reshape(shape)


def _hn_order(x, axis):
    shape = x.shape
    x = x.reshape(shape[:axis] + (HEAD, RWKV_HEADS) + shape[axis + 1:])
    return jnp.swapaxes(x, axis, axis + 1).reshape(shape)


def _to_pairs_kernel(x_ref, o_ref, a_scr):
    nb = x_ref.shape[0]
    for b in range(nb):
        a_scr[b] = x_ref[b].T
    for n in range(HEAD):
        blk = a_scr[:, n * RWKV_HEADS:(n + 1) * RWKV_HEADS, :]
        y = blk.reshape(nb * RWKV_HEADS, blk.shape[2]).T
        o_ref[:, n * SUBLANES:(n + 1) * SUBLANES, :] = y.reshape(y.shape[0] // SUBLANES, SUBLANES, LANES)


def _to_pairs(x, batch, seq_len, tt):
    assert batch * RWKV_HEADS == LANES
    x3 = x.reshape(batch, seq_len, D_MODEL)
    return pl.pallas_call(
        _to_pairs_kernel, grid=(seq_len // tt,),
        in_specs=[pl.BlockSpec((batch, tt, D_MODEL), lambda t: (0, t, 0))],
        out_specs=pl.BlockSpec((tt // SUBLANES, HEAD * SUBLANES, LANES), lambda t: (t, 0, 0)),
        out_shape=jax.ShapeDtypeStruct((seq_len // SUBLANES, HEAD * SUBLANES, LANES), F32),
        scratch_shapes=[pltpu.VMEM((batch, D_MODEL, tt), F32)],
        compiler_params=_cparams(1), name="to_pairs")(x3)


def _from_pairs_kernel(o_ref, x_ref, a_scr):
    nb = x_ref.shape[0]
    for n in range(HEAD):
        blk = o_ref[:, n, :].T
        a_scr[:, n * RWKV_HEADS:(n + 1) * RWKV_HEADS, :] = blk.reshape(nb, RWKV_HEADS, blk.shape[1])
    for b in range(nb):
        x_ref[b] = a_scr[b].T


def _from_pairs(o, batch, seq_len, tt):
    assert batch * RWKV_HEADS == LANES
    out = pl.pallas_call(
        _from_pairs_kernel, grid=(seq_len // tt,),
        in_specs=[pl.BlockSpec((tt, HEAD, LANES), lambda t: (t, 0, 0))],
        out_specs=pl.BlockSpec((batch, tt, D_MODEL), lambda t: (0, t, 0)),
        out_shape=jax.ShapeDtypeStruct((batch, seq_len, D_MODEL), F32),
        scratch_shapes=[pltpu.VMEM((batch, D_MODEL, tt), F32)],
        compiler_params=_cparams(1), name="from_pairs")(o)
    return out.reshape(batch * seq_len, D_MODEL)


def _to_pairs_xla(x, batch, seq_len):
    assert seq_len < SUBLANES
    x = jnp.transpose(x.reshape(batch, seq_len, HEAD, RWKV_HEADS), (2, 1, 0, 3))
    x = x.reshape(HEAD, seq_len, batch * RWKV_HEADS)
    x = jnp.pad(x, ((0, 0), (0, SUBLANES - seq_len), (0, 0)), constant_values=1.0)
    return x.reshape(1, HEAD * SUBLANES, batch * RWKV_HEADS)


def _from_pairs_xla(o, batch, seq_len):
    o = o.reshape(seq_len, HEAD, batch, RWKV_HEADS)
    return jnp.transpose(o, (2, 0, 1, 3)).reshape(batch * seq_len, D_MODEL)


def _head_const_pairs(x, batch):
    x = x.reshape(RWKV_HEADS, HEAD).T
    return jnp.tile(x, (1, batch))


def _count_ge(score, thr):
    return jnp.sum(jnp.where(score >= thr, 1.0, 0.0), axis=-1, keepdims=True)


def _select_topk(score, kpos, k_sel, bisect_steps):
    kf = float(k_sel)
    vis = score > _NEG_INF
    nvis = jnp.sum(jnp.where(vis, 1.0, 0.0), axis=-1, keepdims=True)
    need_sel = nvis > kf
    rowmax = jnp.where(need_sel, jnp.max(score, axis=-1, keepdims=True), 0.0)
    rowmin = jnp.where(need_sel, jnp.min(jnp.where(vis, score, _POS_INF), axis=-1, keepdims=True), 0.0)

    def bis(_, c):
        lo, hi, chi = c
        hfin = jnp.where(hi == _POS_INF, rowmax, hi)
        piv = 0.5 * lo + 0.5 * hfin
        cnt = _count_ge(score, piv)
        ge = cnt >= kf
        return jnp.where(ge, piv, lo), jnp.where(ge, hi, piv), jnp.where(ge, chi, cnt)

    lo, hi, chi = lax.fori_loop(0, bisect_steps, bis,
                                (rowmin, jnp.full_like(rowmin, _POS_INF), jnp.zeros_like(rowmin)))

    def walk_cond(c):
        return jnp.min(c[4]) < 0.5

    def walk(c):
        hi, chi, tau, cgt, done, ceq = c
        pending = done < 0.5
        bmax = jnp.max(jnp.where(score < hi, score, _NEG_INF), axis=-1, keepdims=True)
        cnt = _count_ge(score, bmax)
        fin = jnp.logical_and(cnt >= kf, pending)
        tau = jnp.where(fin, bmax, tau)
        cgt = jnp.where(fin, chi, cgt)
        ceq = jnp.where(fin, cnt - chi, ceq)
        adv = jnp.logical_and(cnt < kf, pending)
        hi = jnp.where(adv, bmax, hi)
        chi = jnp.where(adv, cnt, chi)
        return hi, chi, tau, cgt, jnp.where(fin, 1.0, done), ceq

    done0 = jnp.where(need_sel, 0.0, 1.0)
    neg = jnp.full_like(rowmin, _NEG_INF)
    zero = jnp.zeros_like(rowmin)
    _, _, tau, cgt, _, ceq = lax.while_loop(walk_cond, walk, (hi, chi, neg, zero, done0, zero))

    need = kf - cgt
    excess = jnp.logical_and(need_sel, ceq > need)
    eq = score == tau
    n_keys = score.shape[-1]

    def tie_break(_):
        def body(_, c):
            plo, phi = c
            mid = lax.shift_right_arithmetic(plo + phi, 1)
            cnt = jnp.sum(jnp.where(jnp.logical_and(eq, kpos <= mid), 1.0, 0.0), axis=-1, keepdims=True)
            ge = cnt >= need
            return jnp.where(ge, plo, mid), jnp.where(ge, mid, phi)
        plo0 = jnp.full(tau.shape, -1, jnp.int32)
        phi0 = jnp.full(tau.shape, n_keys - 1, jnp.int32)
        _, phi = lax.fori_loop(0, n_keys.bit_length() + 1, body, (plo0, phi0))
        return jnp.where(excess, phi, n_keys)

    any_excess = jnp.max(jnp.where(excess, 1.0, 0.0)) > 0.0
    pcut = lax.cond(any_excess, tie_break, lambda _: jnp.full(tau.shape, n_keys, jnp.int32), 0)
    pcut = jnp.where(need_sel, pcut, -1)
    return jnp.logical_or(score > tau, jnp.logical_and(eq, kpos <= pcut))


_MASKED = -1e30


def _select_topk_chunked(s_ref, nch, ck, k_sel, nvis, bisect_steps):
    rows = s_ref.shape[0]
    kf = float(k_sel)
    nfold = ck // LANES

    def chunk(c):
        return s_ref[:, pl.ds(pl.multiple_of(c * ck, ck), ck)]

    def fold(x, op):
        out = x[:, 0:LANES]
        for i in range(1, nfold):
            out = op(out, x[:, i * LANES:(i + 1) * LANES])
        return out

    def count(pred):
        def body(c, acc):
            return acc + fold(jnp.where(pred(chunk(c), c), 1.0, 0.0), jnp.add)
        acc = lax.fori_loop(0, nch, body, jnp.zeros((rows, LANES), F32))
        return jnp.sum(acc, axis=-1, keepdims=True)

    def row_max(val):
        def body(c, acc):
            return jnp.maximum(acc, fold(val(chunk(c)), jnp.maximum))
        acc = lax.fori_loop(0, nch, body, jnp.full((rows, LANES), _NEG_INF, F32))
        return jnp.max(acc, axis=-1, keepdims=True)

    need_sel = nvis > kf
    rowmax = jnp.where(need_sel, row_max(lambda x: x), 0.0)
    rowmin = jnp.where(need_sel, -row_max(lambda x: jnp.where(x > _NEG_INF, -x, _NEG_INF)), 0.0)

    def bis(_, c):
        lo, hi, chi = c
        hfin = jnp.where(hi == _POS_INF, rowmax, hi)
        piv = 0.5 * lo + 0.5 * hfin
        cnt = count(lambda x, _c: x >= piv)
        ge = cnt >= kf
        return jnp.where(ge, piv, lo), jnp.where(ge, hi, piv), jnp.where(ge, chi, cnt)

    lo, hi, chi = lax.fori_loop(0, bisect_steps, bis,
                                (rowmin, jnp.full_like(rowmin, _POS_INF), jnp.zeros_like(rowmin)))

    def walk_cond(c):
        return jnp.min(c[4]) < 0.5

    def walk(c):
        hi, chi, tau, cgt, done = c
        pending = done < 0.5
        bmax = row_max(lambda x: jnp.where(x < hi, x, _NEG_INF))
        cnt = count(lambda x, _c: x >= bmax)
        fin = jnp.logical_and(cnt >= kf, pending)
        tau = jnp.where(fin, bmax, tau)
        cgt = jnp.where(fin, chi, cgt)
        adv = jnp.logical_and(cnt < kf, pending)
        hi = jnp.where(adv, bmax, hi)
        chi = jnp.where(adv, cnt, chi)
        return hi, chi, tau, cgt, jnp.where(fin, 1.0, done)

    neg = jnp.full_like(rowmin, _NEG_INF)
    zero = jnp.zeros_like(rowmin)
    _, _, tau, cgt, _ = lax.while_loop(walk_cond, walk, (hi, chi, neg, zero, jnp.where(need_sel, 0.0, 1.0)))

    need = jnp.where(need_sel, kf - cgt, 0.0)
    tri = jnp.where(lax.broadcasted_iota(jnp.int32, (ck, ck), 0) <= lax.broadcasted_iota(jnp.int32, (ck, ck), 1),
                    1.0, 0.0).astype(BF16)

    def write(c, ties_before):
        x = chunk(c)
        eq = x == tau
        rank = jnp.dot(jnp.where(eq, 1.0, 0.0).astype(BF16), tri, preferred_element_type=F32) + ties_before
        sel = jnp.logical_or(x > tau, jnp.logical_and(eq, rank <= need))
        s_ref[:, pl.ds(pl.multiple_of(c * ck, ck), ck)] = jnp.where(sel, 0.0, _MASKED)
        return rank[:, ck - 1:ck]

    lax.fori_loop(0, nch, write, zero)


def _attn_prep_kernel(q_ref, qi_ref, kv_ref, kiw_ref, qg_ref, kg_ref, kn_o, qb_o, qib_o, kvb_o, kib_o):
    w = ATT_KV_HEADS * HEAD
    for h in range(ATT_KV_HEADS):
        hs = slice(h * HEAD, (h + 1) * HEAD)
        x = kv_ref[:, hs]
        kn = x * lax.rsqrt(jnp.mean(x * x, axis=-1, keepdims=True) + NORM_EPS) * kg_ref[...]
        kn_o[:, hs] = kn
        kvb_o[:, hs] = kn.astype(BF16)
        v = kv_ref[:, w + h * HEAD:w + (h + 1) * HEAD].astype(BF16)
        kvb_o[:, w + h * LANES:w + (h + 1) * LANES] = jnp.concatenate([v, jnp.ones_like(v)], axis=1)
    for h in range(ATT_HEADS):
        x = q_ref[:, h * HEAD:(h + 1) * HEAD]
        qn = x * lax.rsqrt(jnp.mean(x * x, axis=-1, keepdims=True) + NORM_EPS) * (qg_ref[...] * HEAD ** -0.5)
        qb_o[h] = qn.astype(BF16)
    qib_o[...] = (qi_ref[...] * HEAD ** -0.5).astype(BF16)
    kib_o[...] = kiw_ref[:, 0:HEAD].astype(BF16)


def _attn_prep(za, qn_g, kn_g, tm):
    m = za.shape[0]
    w = ATT_KV_HEADS * HEAD
    wi = IDX_HEADS * HEAD
    row = lambda width, blk: pl.BlockSpec((tm, width), lambda i, blk=blk: (i, blk))
    gspec = pl.BlockSpec((1, HEAD), lambda i: (0, 0))
    return pl.pallas_call(
        _attn_prep_kernel, grid=(m // tm,),
        in_specs=[row(D_MODEL, ZA_Q // D_MODEL), row(wi, ZA_QI // wi), row(2 * w, ZA_KV // (2 * w)),
                  row(LANES, ZA_KIW // LANES), gspec, gspec],
        out_specs=[row(w, 0), pl.BlockSpec((ATT_HEADS, tm, HEAD), lambda i: (0, i, 0)), row(wi, 0),
                   row(w + ATT_KV_HEADS * LANES, 0), row(HEAD, 0)],
        out_shape=[jax.ShapeDtypeStruct((m, w), F32), jax.ShapeDtypeStruct((ATT_HEADS, m, HEAD), BF16),
                   jax.ShapeDtypeStruct((m, wi), BF16), jax.ShapeDtypeStruct((m, w + ATT_KV_HEADS * LANES), BF16),
                   jax.ShapeDtypeStruct((m, HEAD), BF16)],
        compiler_params=_cparams(1), name="attn_prep")(za, za, za, za, qn_g.reshape(1, HEAD), kn_g.reshape(1, HEAD))


def _bucket_edges():
    max_exact = REL_BUCKETS // 2
    d = np.arange(REL_MAX_DIST + 1)
    df = np.maximum(d, 1).astype(np.float32)
    large = max_exact + (np.log(df / max_exact) / math.log(REL_MAX_DIST / max_exact)
                         * (REL_BUCKETS - max_exact)).astype(np.int32)
    bucket = np.where(d < max_exact, d, np.minimum(large, REL_BUCKETS - 1))
    return [int(np.argmax(bucket >= b)) for b in range(REL_BUCKETS)]


_BUCKET_EDGES = _bucket_edges()


def _rel_bias_lookup(dist, value_of_bucket):
    bias = value_of_bucket(REL_BUCKETS - 1)
    for b in range(REL_BUCKETS - 2, -1, -1):
        bias = jnp.where(dist < _BUCKET_EDGES[b + 1], value_of_bucket(b), bias)
    return bias


def _attn_prompt_kernel(rb_ref, q_ref, qi_ref, kiwq_ref, g_ref, kvb_ref, kib_ref, o_ref, s_scr, tbd_ref,
                        l_scr, mx_scr, acc_scr, *, tq, k_sel, bisect_steps):
    qt = pl.program_id(0)
    q0 = qt * tq
    ck = tq
    rg_rows = LANES
    n_rg = tq // rg_rows
    w = ATT_KV_HEADS * HEAD

    @pl.when(jnp.logical_and(pl.program_id(1) == 0, qt == 0))
    def _():
        rr = lax.broadcasted_iota(jnp.int32, (tq, tq), 0) - lax.broadcasted_iota(jnp.int32, (tq, tq), 1)
        for h in range(ATT_HEADS):
            far = rb_ref[REL_BUCKETS - 1, h]
            tbd_ref[h, 0] = _rel_bias_lookup(rr, lambda b: rb_ref[b, h]) - far
            tbd_ref[h, 1] = _rel_bias_lookup(rr + tq, lambda b: rb_ref[b, h]) - far

    def chunk_start(c):
        return pl.multiple_of(c * ck, ck)

    wcol = kiwq_ref[:, HEAD:HEAD + IDX_HEADS] * (IDX_HEADS ** -0.5)
    qi_h = [[qi_ref[rg * rg_rows:(rg + 1) * rg_rows, h * HEAD:(h + 1) * HEAD] for h in range(IDX_HEADS)]
            for rg in range(n_rg)]
    w_h = [[wcol[rg * rg_rows:(rg + 1) * rg_rows, h:h + 1] for h in range(IDX_HEADS)] for rg in range(n_rg)]

    def score_chunk(c, carry):
        k0 = chunk_start(c)
        kc = kib_ref[pl.ds(k0, ck), :]
        kpos = k0 + lax.broadcasted_iota(jnp.int32, (rg_rows, ck), 1)
        for rg in range(n_rg):
            acc = jnp.zeros((rg_rows, ck), F32)
            for h in range(IDX_HEADS):
                acc = acc + jnp.maximum(_dot_nt(qi_h[rg][h], kc), 0.0) * w_h[rg][h]
            qpos = q0 + rg * rg_rows + lax.broadcasted_iota(jnp.int32, (rg_rows, ck), 0)
            s_scr[rg * rg_rows:(rg + 1) * rg_rows, pl.ds(k0, ck)] = jnp.where(kpos <= qpos, acc, _NEG_INF)
        return carry

    nch = qt + 1
    lax.fori_loop(0, nch, score_chunk, 0)

    nvis = (q0 + 1 + lax.broadcasted_iota(jnp.int32, (tq, 1), 0)).astype(F32)
    _select_topk_chunked(s_scr, nch, ck, k_sel, nvis, bisect_steps)

    g_rows = ATT_GROUP * tq
    n_far = jnp.maximum(qt - 1, 0)
    for kvh in range(ATT_KV_HEADS):
        gs = slice(kvh * ATT_GROUP, (kvh + 1) * ATT_GROUP)
        qg = q_ref[gs, :, :].reshape(g_rows, HEAD)
        mx_scr[...] = jnp.full(mx_scr.shape, _MASKED, F32)
        acc_scr[...] = jnp.zeros(acc_scr.shape, F32)

        def logits_chunk(c, carry, near, qg=qg, gs=gs, kvh=kvh):
            k0 = chunk_start(c)
            kc = kvb_ref[pl.ds(k0, ck), kvh * HEAD:(kvh + 1) * HEAD]
            s3 = _dot_nt(qg, kc).reshape(ATT_GROUP, tq, ck) + s_scr[:, pl.ds(k0, ck)][None]
            if near:
                s3 = s3 + tbd_ref[gs, qt - c, :, :]
            s = s3.reshape(g_rows, ck)
            l_scr[:, pl.ds(k0, ck)] = s
            m = mx_scr[...]
            for i in range(ck // LANES):
                m = jnp.maximum(m, s[:, i * LANES:(i + 1) * LANES])
            mx_scr[...] = m
            return carry

        lax.fori_loop(0, n_far, functools.partial(logits_chunk, near=False), 0)
        lax.fori_loop(n_far, nch, functools.partial(logits_chunk, near=True), 0)
        mx_scr[...] = jnp.broadcast_to(jnp.max(mx_scr[...], axis=-1, keepdims=True), mx_scr.shape)

        def pv_chunk(c, carry, kvh=kvh):
            k0 = chunk_start(c)
            vx = kvb_ref[pl.ds(k0, ck), w + kvh * LANES:w + (kvh + 1) * LANES]
            m = mx_scr[...]
            p = jnp.concatenate([jnp.exp(l_scr[:, pl.ds(k0 + i * LANES, LANES)] - m)
                                 for i in range(ck // LANES)], axis=1).astype(BF16)
            acc_scr[...] += jnp.dot(p, vx, preferred_element_type=F32)
            return carry

        lax.fori_loop(0, nch, pv_chunk, 0)

        acc = acc_scr[...]
        o = acc[:, 0:HEAD] / acc[:, HEAD:HEAD + 1]
        for pp in range(ATT_GROUP // 2):
            cols = slice((kvh * ATT_GROUP // 2 + pp) * LANES, (kvh * ATT_GROUP // 2 + pp + 1) * LANES)
            gh = g_ref[:, cols]
            pair = jnp.concatenate([o[2 * pp * tq:(2 * pp + 1) * tq], o[(2 * pp + 1) * tq:(2 * pp + 2) * tq]], axis=1)
            o_ref[:, cols] = pair * (gh * _sigmoid(gh))


def _attn_prompt(za, kn, qb, qib, kvb, kib, *, rel_bias, batch, seq_len, tq, k_sel):
    m = za.shape[0]
    nq = seq_len // tq
    w = ATT_KV_HEADS * HEAD
    wi = IDX_HEADS * HEAD
    assert REL_MAX_DIST <= tq and tq % LANES == 0
    row_map = lambda blk: (lambda t, b, blk=blk: (b * nq + t, blk))
    key_map = lambda t, b: (b, 0)
    kern = functools.partial(_attn_prompt_kernel, tq=tq, k_sel=k_sel, bisect_steps=14)
    return pl.pallas_call(
        kern, grid=(nq, batch),
        in_specs=[
            pl.BlockSpec(memory_space=pltpu.SMEM),
            pl.BlockSpec((ATT_HEADS, tq, HEAD), lambda t, b: (0, b * nq + t, 0)),
            pl.BlockSpec((tq, wi), row_map(0)),
            pl.BlockSpec((tq, LANES), row_map(ZA_KIW // LANES)),
            pl.BlockSpec((tq, D_MODEL), row_map(ZA_GATT // D_MODEL)),
            pl.BlockSpec((seq_len, w + ATT_KV_HEADS * LANES), key_map),
            pl.BlockSpec((seq_len, HEAD), key_map),
        ],
        out_specs=pl.BlockSpec((tq, D_MODEL), row_map(0)),
        out_shape=jax.ShapeDtypeStruct((m, D_MODEL), F32),
        scratch_shapes=[pltpu.VMEM((tq, seq_len), F32), pltpu.VMEM((ATT_HEADS, 2, tq, tq), F32),
                        pltpu.VMEM((ATT_GROUP * tq, seq_len), F32), pltpu.VMEM((ATT_GROUP * tq, LANES), F32),
                        pltpu.VMEM((ATT_GROUP * tq, LANES), F32)],
        compiler_params=_cparams(2), name="attn_prompt")(rel_bias, qb, qib, za, za, kvb, kib)


def _sample_score_kernel(pt_ref, q_ref, w_ref, kiw_new_ref, *rest, n_pages, page, dec_seq):
    page_refs = rest[:n_pages]
    o_ref = rest[n_pages]
    past = n_pages * page
    kidx = jnp.concatenate([r[...] for r in page_refs], axis=0).astype(BF16)
    new = kiw_new_ref[:, 0:HEAD]
    new = jnp.concatenate([new, jnp.zeros((LANES - dec_seq, HEAD), F32)], axis=0).astype(BF16)
    q = (q_ref[...] * (HEAD ** -0.5)).astype(BF16)
    lg = jnp.concatenate([_dot_nt(q, kidx), _dot_nt(q, new)], axis=1)
    wr = jnp.maximum(lg, 0.0) * (w_ref[...] * (IDX_HEADS ** -0.5))
    n_keys = past + LANES
    sc = jnp.sum(wr.reshape(dec_seq, IDX_HEADS, n_keys), axis=1)
    kpos = lax.broadcasted_iota(jnp.int32, (dec_seq, n_keys), 1)
    tpos = lax.broadcasted_iota(jnp.int32, (dec_seq, n_keys), 0)
    o_ref[...] = jnp.where(kpos <= past + tpos, sc, _NEG_INF)


def _sample_select_kernel(s_ref, o_ref, *, k_sel, bisect_steps):
    score = s_ref[...]
    kpos = lax.broadcasted_iota(jnp.int32, score.shape, 1)
    sel = _select_topk(score, kpos, k_sel, bisect_steps)
    o_ref[...] = jnp.where(sel, 0.0, _NEG_INF)


def _sample_attn_kernel(pt_ref, q_ref, g_ref, kn_new_ref, kv_new_ref, mask_ref, rb_rows_ref, qg_ref, *rest,
                        n_pages, page, dec_seq):
    k_pages = rest[:n_pages]
    v_pages = rest[n_pages:2 * n_pages]
    o_ref = rest[2 * n_pages]
    bias_ref = rest[2 * n_pages + 1]
    w = ATT_KV_HEADS * HEAD
    rows = dec_seq * ATT_HEADS
    n_keys = n_pages * page + LANES

    @pl.when(pl.program_id(0) == 0)
    def _():
        t_row = lax.div(lax.broadcasted_iota(jnp.int32, (rows, n_keys), 0), ATT_HEADS)
        dist = n_pages * page + t_row - lax.broadcasted_iota(jnp.int32, (rows, n_keys), 1)
        bias_ref[...] = _rel_bias_lookup(dist, lambda b: rb_rows_ref[:, b:b + 1])

    pad = jnp.zeros((LANES - dec_seq, w), F32)
    k_all = jnp.concatenate([r[...] for r in k_pages] + [kn_new_ref[...], pad], axis=0).astype(BF16)
    v_all = jnp.concatenate([r[...] for r in v_pages] + [kv_new_ref[:, w:2 * w], pad], axis=0).astype(BF16)
    q = q_ref[...]
    q = q * lax.rsqrt(jnp.mean(q * q, axis=-1, keepdims=True) + NORM_EPS) * qg_ref[...]
    qb = q.astype(BF16)
    head = lax.rem(lax.broadcasted_iota(jnp.int32, (rows, 1), 0), ATT_HEADS)
    first = head < ATT_GROUP
    lg = jnp.where(first, _dot_nt(qb, k_all[:, 0:HEAD]), _dot_nt(qb, k_all[:, HEAD:2 * HEAD]))
    mask = jnp.broadcast_to(mask_ref[...][:, None, :], (dec_seq, ATT_HEADS, n_keys)).reshape(rows, n_keys)
    s = lg * (HEAD ** -0.5) + bias_ref[...] + mask
    p = jnp.exp(s - jnp.max(s, axis=-1, keepdims=True))
    l = jnp.sum(p, axis=-1, keepdims=True)
    pb = p.astype(BF16)
    o = jnp.where(first, jnp.dot(pb, v_all[:, 0:HEAD], preferred_element_type=F32),
                  jnp.dot(pb, v_all[:, HEAD:2 * HEAD], preferred_element_type=F32)) / l
    g = g_ref[...]
    o_ref[...] = o * (g * _sigmoid(g))


def _attn_sample(za, kn, qb, qib, kvb, kib, *, cache_k, cache_v, cache_kidx, page_table, rel_bias, qn_g,
                 dec_seq, k_sel):
    nb, n_pages = page_table.shape
    n_phys, page = cache_k.shape[0], cache_k.shape[1]
    past = n_pages * page
    n_keys = past + LANES
    w = ATT_KV_HEADS * HEAD
    ck = cache_k.reshape(n_phys, page, w)
    cv = cache_v.reshape(n_phys, page, w)
    ci = cache_kidx.reshape(n_phys, page, HEAD)
    za3 = za.reshape(nb, dec_seq, ZA_COLS)
    kn3 = kn.reshape(nb, dec_seq, w)
    qi = za[:, ZA_QI:ZA_QI + IDX_HEADS * HEAD].reshape(nb, dec_seq * IDX_HEADS, HEAD)
    wi = za[:, ZA_KIW + HEAD:ZA_KIW + HEAD + IDX_HEADS].reshape(nb, dec_seq * IDX_HEADS, 1)
    qa = za[:, ZA_Q:ZA_Q + D_MODEL].reshape(nb, dec_seq * ATT_HEADS, HEAD)
    ga = za[:, ZA_GATT:ZA_GATT + D_MODEL].reshape(nb, dec_seq * ATT_HEADS, HEAD)

    def page_specs(width):
        return [pl.BlockSpec((None, page, width), lambda b, pt, j=j: (pt[b, j], 0, 0)) for j in range(n_pages)]

    per_b = lambda r, c: pl.BlockSpec((None, r, c), lambda b, pt: (b, 0, 0))
    kiw_new = pl.BlockSpec((None, dec_seq, LANES), lambda b, pt: (b, 0, ZA_KIW // LANES))

    scores = pl.pallas_call(
        functools.partial(_sample_score_kernel, n_pages=n_pages, page=page, dec_seq=dec_seq),
        grid_spec=pltpu.PrefetchScalarGridSpec(
            num_scalar_prefetch=1, grid=(nb,),
            in_specs=[per_b(dec_seq * IDX_HEADS, HEAD), per_b(dec_seq * IDX_HEADS, 1), kiw_new] + page_specs(HEAD),
            out_specs=per_b(dec_seq, n_keys)),
        out_shape=jax.ShapeDtypeStruct((nb, dec_seq, n_keys), F32),
        compiler_params=_cparams(1), name="sample_scores")(page_table, qi, wi, za3, *([ci] * n_pages))

    rows = nb * dec_seq
    tr = min(rows, 128)
    mask = pl.pallas_call(
        functools.partial(_sample_select_kernel, k_sel=k_sel, bisect_steps=14),
        grid=(rows // tr,),
        in_specs=[pl.BlockSpec((tr, n_keys), lambda i: (i, 0))],
        out_specs=pl.BlockSpec((tr, n_keys), lambda i: (i, 0)),
        out_shape=jax.ShapeDtypeStruct((rows, n_keys), F32),
        compiler_params=_cparams(1), name="sample_select")(scores.reshape(rows, n_keys))

    rb_rows = jnp.tile(rel_bias.T, (dec_seq, 1))

    const = lambda shape: pl.BlockSpec(shape, lambda b, pt: (0,) * len(shape))
    kv_new = pl.BlockSpec((None, dec_seq, 2 * w), lambda b, pt: (b, 0, ZA_KV // (2 * w)))
    out = pl.pallas_call(
        functools.partial(_sample_attn_kernel, n_pages=n_pages, page=page, dec_seq=dec_seq),
        grid_spec=pltpu.PrefetchScalarGridSpec(
            num_scalar_prefetch=1, grid=(nb,),
            in_specs=[per_b(dec_seq * ATT_HEADS, HEAD), per_b(dec_seq * ATT_HEADS, HEAD), per_b(dec_seq, w),
                      kv_new, per_b(dec_seq, n_keys), const((dec_seq * ATT_HEADS, REL_BUCKETS)), const((1, HEAD))]
            + page_specs(w) + page_specs(w),
            out_specs=per_b(dec_seq * ATT_HEADS, HEAD),
            scratch_shapes=[pltpu.VMEM((dec_seq * ATT_HEADS, n_keys), F32)]),
        out_shape=jax.ShapeDtypeStruct((nb, dec_seq * ATT_HEADS, HEAD), F32),
        compiler_params=_cparams(1), name="sample_attn")(
            page_table, qa, ga, kn3, za3, mask.reshape(nb, dec_seq, n_keys), rb_rows, qn_g.reshape(1, HEAD),
            *([ck] * n_pages), *([cv] * n_pages))
    return out.reshape(rows, D_MODEL)


def _merge_kernel(x_ref, oa_ref, sg_ref, ob_ref, ga_ref, gb_ref, wpa_ref, wpb_ref, wo_ref, y_ref):
    oa = (oa_ref[...] * sg_ref[...]).astype(BF16)
    pa = jnp.dot(oa, wpa_ref[...], preferred_element_type=F32)
    pb = jnp.dot(ob_ref[...].astype(BF16), wpb_ref[...], preferred_element_type=F32)
    merged = _sigmoid(ga_ref[...]) * pa + _sigmoid(gb_ref[...]) * pb
    y_ref[...] = x_ref[...] + jnp.dot(merged.astype(BF16), wo_ref[...], preferred_element_type=F32)


def _merge(x, oa, sg, ob, za, w_pa, w_pb, w_out, tm):
    m, d = x.shape
    row = pl.BlockSpec((tm, d), lambda i: (i, 0))
    wsp = pl.BlockSpec((d, d), lambda i: (0, 0))
    return pl.pallas_call(
        _merge_kernel, grid=(m // tm,),
        in_specs=[row, row, row, row,
                  pl.BlockSpec((tm, d), lambda i: (i, ZA_GA // d)), pl.BlockSpec((tm, d), lambda i: (i, ZA_GB // d)),
                  wsp, wsp, wsp],
        out_specs=row,
        out_shape=jax.ShapeDtypeStruct((m, d), F32),
        compiler_params=_cparams(1), name="merge")(x, oa, sg, ob, za, za, w_pa, w_pb, w_out)


def _layer(x, shift, s0, params, *, batch, seq_len, attend):
    (norm_g, w_r, w_a, mu, w0, w2, a0, a2, k_k, k_a, r_k, lnx_g, lnx_b, qn_g, kn_g, w_pa, w_pb, w_out) = params
    m = batch * seq_len
    tm = min(m, 512)
    xn = _rmsnorm(x, norm_g, tm)
    tm_proj = min(m, 2048)
    zr = _matmul(xn, w_r, tm_proj, RWKV_COLS // 3, "inproj_rwkv")
    za = _matmul(xn, w_a, tm_proj, ZA_COLS // 4, "inproj_attn")

    tile = min(m, 256)
    r, w, k, v, a, sg = _rwkv_prep(zr, shift, mu, w0, w2, a0, a2, batch=batch, seq_len=seq_len, tile=tile)
    relayout_tile = 128
    in_kernel_relayout = seq_len % relayout_tile == 0 and batch * RWKV_HEADS == LANES
    if in_kernel_relayout:
        seqs = [_to_pairs(t, batch, seq_len, relayout_tile) for t in (r, w, k, v, a)]
    else:
        seqs = [_to_pairs_xla(t, batch, seq_len) for t in (r, w, k, v, a)]
    row_consts = [_head_const_pairs(t, batch) for t in (lnx_g, lnx_b)]
    step_consts = [jnp.broadcast_to(_head_const_pairs(t, batch)[:, None, :], (HEAD, SUBLANES, batch * RWKV_HEADS))
                   for t in (r_k.reshape(-1), k_k, k_a)]
    o_t, s_t = _rwkv_scan(seqs, s0, row_consts, step_consts, seq_len=seq_len, steps=min(seq_len, 32))
    if in_kernel_relayout:
        oa = _from_pairs(o_t, batch, seq_len, relayout_tile)
    else:
        oa = _from_pairs_xla(o_t, batch, seq_len)

    kn, qb, qib, kvb, kib = _attn_prep(za, qn_g, kn_g, tm)
    ob = attend(za, kn, qb, qib, kvb, kib)

    y = _merge(x, oa, sg, ob, za, w_pa, w_pb, w_out, min(m, 256))
    return y, zr, za, kn, s_t


def _state_to_pairs(s):
    b, h, n, _ = s.shape
    return jnp.transpose(s, (3, 2, 0, 1)).reshape(n, n, b * h)


def _state_from_pairs(s, batch):
    n = s.shape[0]
    return jnp.transpose(s.reshape(n, n, batch, RWKV_HEADS), (2, 3, 1, 0))


def kernel(x_prompt, x_sample, cache_k, cache_v, cache_kidx, state_wkv, state_shift, page_table, norm_g, w_in,
           shift_mu, w0, w2, a0, a2, k_k, k_a, r_k, lnx_g, lnx_b, q_norm_g, k_norm_g, rel_bias, w_pa, w_pb, w_out):
    bsz, seq, d = x_prompt.shape
    dec_bsz, dec_seq, _ = x_sample.shape
    depth = w_in.shape[0]
    assert depth == 1 and d == D_MODEL
    past_len = page_table.shape[1] * cache_k.shape[2]
    topk_p = min(TOPK_MAX, seq // 4)
    topk_s = min(TOPK_MAX, (past_len + dec_seq) // 4)
    l = 0

    wl = w_in[l]
    c0 = RWKV_COLS
    q_w, kv_w, qi_w = wl[:, c0:c0 + 1024], wl[:, c0 + 1024:c0 + 1280], wl[:, c0 + 1280:c0 + 1792]
    kiw_w = wl[:, c0 + 1792:c0 + 1864]
    rest_w = wl[:, c0 + 1864:]
    zpad = lambda n: jnp.zeros((d, n), wl.dtype)
    w_a = jnp.concatenate([q_w, qi_w, kv_w, kiw_w, zpad(LANES - kiw_w.shape[1]), zpad(LANES), rest_w],
                          axis=1).astype(BF16)
    assert w_a.shape[1] == ZA_COLS
    def rwkv_cols(x, fn):
        lead = x.shape[:-1]
        main = fn(x[..., :4 * d].reshape(lead + (4, d)), len(lead) + 1).reshape(lead + (4 * d,))
        return jnp.concatenate([main, x[..., 4 * d:]], axis=-1)

    w_r = rwkv_cols(wl[:, :c0], _nh_order).astype(BF16)
    params = (norm_g[l], w_r, w_a, rwkv_cols(shift_mu[l], _nh_order), _nh_order(w0[l], 0), _nh_order(w2[l], 1),
              _nh_order(a0[l], 0), _nh_order(a2[l], 1), k_k[l], k_a[l], r_k[l],
              lnx_g[l], lnx_b[l], q_norm_g[l], k_norm_g[l],
              _nh_order(w_pa[l], 0).astype(BF16), w_pb[l].astype(BF16), w_out[l].astype(BF16))

    xp = x_prompt.reshape(bsz * seq, d)
    attend_p = functools.partial(_attn_prompt, rel_bias=rel_bias, batch=bsz, seq_len=seq,
                                 tq=min(seq, 256), k_sel=topk_p)
    yp, zr_p, za_p, kn_p, st_p = _layer(
        xp, jnp.zeros((bsz, 1, RWKV_COLS), F32), jnp.zeros((HEAD, HEAD, bsz * RWKV_HEADS), F32), params,
        batch=bsz, seq_len=seq, attend=attend_p)

    xs = x_sample.reshape(dec_bsz * dec_seq, d)
    attend_s = functools.partial(_attn_sample, cache_k=cache_k[l], cache_v=cache_v[l], cache_kidx=cache_kidx[l],
                                 page_table=page_table, rel_bias=rel_bias, qn_g=q_norm_g[l], dec_seq=dec_seq,
                                 k_sel=topk_s)
    shift_rows = jnp.repeat(rwkv_cols(state_shift[l], _nh_order), dec_seq, axis=0)
    ys, zr_s, za_s, kn_s, st_s = _layer(
        xs, shift_rows, _state_to_pairs(state_wkv[l]), params, batch=dec_bsz, seq_len=dec_seq, attend=attend_s)

    w = ATT_KV_HEADS * HEAD

    def pack(y, zr, za, kn, st, b, t):
        v = za[:, ZA_KV + w:ZA_KV + 2 * w]
        kidx = za[:, ZA_KIW:ZA_KIW + HEAD]
        return (y.reshape(b, t, d),
                kn.reshape(1, b, t, ATT_KV_HEADS, HEAD), v.reshape(1, b, t, ATT_KV_HEADS, HEAD),
                kidx.reshape(1, b, t, HEAD), _state_from_pairs(st, b)[None],
                rwkv_cols(zr.reshape(b, t, RWKV_COLS)[:, -1], _hn_order)[None])

    p = pack(yp, zr_p, za_p, kn_p, st_p, bsz, seq)
    s = pack(ys, zr_s, za_s, kn_s, st_s, dec_bsz, dec_seq)
    return (p[0], s[0]) + p[1:] + s[1:]
```

```python
import functools
import math

import numpy as np
import jax
import jax.numpy as jnp
from jax import lax
from jax.experimental import pallas as pl
from jax.experimental.pallas import tpu as pltpu

F32 = jnp.float32
BF16 = jnp.bfloat16

D_MODEL = 1024
HEAD = 64
RWKV_HEADS = D_MODEL // HEAD
LORA = 64
LNX_EPS = 64e-5
ATT_HEADS = D_MODEL // HEAD
ATT_KV_HEADS = 2
ATT_GROUP = ATT_HEADS // ATT_KV_HEADS
IDX_HEADS = 8
TOPK_MAX = 256
REL_BUCKETS = 32
REL_MAX_DIST = 128
NORM_EPS = 1e-6
RWKV_COLS = 4 * D_MODEL + 2 * LORA

LANES = 128
SUBLANES = 8
VMEM_LIMIT_BYTES = 56 * 1024 * 1024

ZA_Q = 0
ZA_QI = 1024
ZA_KV = 1536
ZA_KIW = 1792
ZA_COLS = 2048
ZG_GATT = 0
ZG_GA = 1024
ZG_GB = 2048
ZG_COLS = 3072

_NEG_INF = float("-inf")
_POS_INF = float("inf")


def _cparams(n_axes):
    return pltpu.CompilerParams(dimension_semantics=("arbitrary",) * n_axes,
                                vmem_limit_bytes=VMEM_LIMIT_BYTES)


def _sigmoid(x):
    return 1.0 / (1.0 + jnp.exp(-x))


def _dot_nt(a, b):
    return lax.dot_general(a, b, (((1,), (1,)), ((), ())), preferred_element_type=F32)


def _rmsnorm_kernel(x_ref, g_ref, o_ref):
    x = x_ref[...]
    ms = jnp.mean(x * x, axis=-1, keepdims=True)
    o_ref[...] = (x * lax.rsqrt(ms + NORM_EPS) * g_ref[...]).astype(o_ref.dtype)


def _rmsnorm(x, g, tm):
    m, d = x.shape
    return pl.pallas_call(
        _rmsnorm_kernel, grid=(m // tm,),
        in_specs=[pl.BlockSpec((tm, d), lambda i: (i, 0)), pl.BlockSpec((1, d), lambda i: (0, 0))],
        out_specs=pl.BlockSpec((tm, d), lambda i: (i, 0)),
        out_shape=jax.ShapeDtypeStruct((m, d), BF16),
        compiler_params=_cparams(1), name="rmsnorm")(x, g.reshape(1, d))


def _mm_kernel(x_ref, w_ref, o_ref):
    o_ref[...] = jnp.dot(x_ref[...], w_ref[...], preferred_element_type=F32).astype(o_ref.dtype)


def _matmul(x, w, tm, tn, name, out_dtype=F32):
    m, k = x.shape
    n = w.shape[1]
    return pl.pallas_call(
        _mm_kernel, grid=(m // tm, n // tn),
        in_specs=[pl.BlockSpec((tm, k), lambda i, j: (i, 0)), pl.BlockSpec((k, tn), lambda i, j: (0, j))],
        out_specs=pl.BlockSpec((tm, tn), lambda i, j: (i, j)),
        out_shape=jax.ShapeDtypeStruct((m, n), out_dtype),
        compiler_params=_cparams(2), name=name)(x, w)


def _rwkv_prep_kernel(z_ref, prev_ref, shift_ref, mu_ref, w0_ref, w2_ref, a0_ref, a2_ref,
                      r_o, w_o, k_o, v_o, a_o, g_o, *, seq_len, tile):
    z = z_ref[...]
    rows = lax.broadcasted_iota(jnp.int32, z.shape, 0)
    rolled = pltpu.roll(z, 1, 0)
    if seq_len >= tile:
        first = jnp.where(pl.program_id(1) == 0, shift_ref[...], prev_ref[SUBLANES - 1:SUBLANES, :])
        prev = jnp.where(rows == 0, first, rolled)
    else:
        prev = jnp.where(lax.rem(rows, seq_len) == 0, shift_ref[...], rolled)
    zs = z + (prev - z) * mu_ref[...]
    d = D_MODEL
    r = zs[:, 0:d]
    k = zs[:, d:2 * d]
    v = zs[:, 2 * d:3 * d]
    g = zs[:, 3 * d:4 * d]
    wd = zs[:, 4 * d:4 * d + LORA]
    ad = zs[:, 4 * d + LORA:4 * d + 2 * LORA]
    wl = w0_ref[...] + jnp.dot(jnp.tanh(wd).astype(BF16), w2_ref[...], preferred_element_type=F32)
    decay = jnp.exp(-_sigmoid(wl) * math.exp(-0.5))
    a = _sigmoid(a0_ref[...] + jnp.dot(ad.astype(BF16), a2_ref[...], preferred_element_type=F32))
    r_o[...] = r
    w_o[...] = decay
    k_o[...] = k
    v_o[...] = v
    a_o[...] = a
    g_o[...] = g * _sigmoid(g)


def _rwkv_prep(zr, shift, mu, w0, w2, a0, a2, *, batch, seq_len, tile):
    m, c = zr.shape
    d = D_MODEL
    row = lambda x: x.reshape(1, -1)
    consts = [row(mu), row(w0), w2.astype(BF16), row(a0), a2.astype(BF16)]
    const_specs = [pl.BlockSpec(x.shape, lambda *_: (0, 0)) for x in consts]
    if seq_len >= tile:
        nt = seq_len // tile
        grid = (batch, nt)
        zmap = lambda b, t: (b * nt + t, 0)
        pmap = lambda b, t: (jnp.maximum((b * seq_len + t * tile) // SUBLANES - 1, 0), 0)
        shift_spec = pl.BlockSpec((None, 1, c), lambda b, t: (b, 0, 0))
    else:
        grid = (1, m // tile)
        zmap = lambda b, t: (t, 0)
        pmap = lambda b, t: (0, 0)
        shift_spec = pl.BlockSpec((tile, c), zmap)
    out_spec = pl.BlockSpec((tile, d), zmap)
    kern = functools.partial(_rwkv_prep_kernel, seq_len=seq_len, tile=tile)
    return pl.pallas_call(
        kern, grid=grid,
        in_specs=[pl.BlockSpec((tile, c), zmap), pl.BlockSpec((SUBLANES, c), pmap), shift_spec] + const_specs,
        out_specs=[out_spec] * 6,
        out_shape=[jax.ShapeDtypeStruct((m, d), F32)] * 6,
        compiler_params=_cparams(2), name="rwkv_prep")(zr, zr, shift, *consts)


def _rwkv_scan_kernel(r_in, w_in, k_in, v_in, a_in, s0_ref, lg_ref, lb_ref, rk_ref, kkc_ref, kac_ref,
                      o_ref, s_ref, vec_ref, bf_ref, ge_ref, *, groups, live):
    nb = HEAD // SUBLANES

    @pl.when(pl.program_id(1) == 0)
    def _():
        s_ref[...] = s0_ref[...]

    shape4 = (groups, HEAD, SUBLANES, LANES)
    a = a_in[...].reshape(shape4)
    kraw = k_in[...].reshape(shape4)
    r_all = r_in[...].reshape(shape4)
    kkraw = kraw * kkc_ref[...][None]
    n2 = jnp.sum(kkraw * kkraw, axis=1, keepdims=True)
    kk = kkraw / jnp.maximum(jnp.sqrt(n2), 1e-12)
    kmod = kraw * (1.0 + (a - 1.0) * kac_ref[...][None])
    bf_ref[...] = jnp.sum(r_all * kmod * rk_ref[...][None], axis=1)
    row = lax.broadcasted_iota(jnp.int32, (HEAD, SUBLANES, LANES), 1)
    g_end = jnp.ones((HEAD, SUBLANES, LANES), F32)
    for q in range(groups):
        g = w_in[q].reshape(HEAD, SUBLANES, LANES)
        for sh in (1, 2, 4):
            g = g * jnp.where(row >= sh, pltpu.roll(g, sh, 1), 1.0)
        g = g * g_end
        g_prev = jnp.where(row >= 1, pltpu.roll(g, 1, 1), g_end)
        g_inv = 1.0 / g
        flat = (HEAD * SUBLANES, LANES)
        vec_ref[0, q] = (-kk[q] * g_prev).reshape(flat)
        vec_ref[1, q] = (kk[q] * a[q] * g_inv).reshape(flat)
        vec_ref[2, q] = (kmod[q] * g_inv).reshape(flat)
        vec_ref[3, q] = (r_all[q] * g).reshape(flat)
        last = live - 1 if q == groups - 1 else SUBLANES - 1
        g_end = jnp.broadcast_to(g[:, last:last + 1, :], (HEAD, SUBLANES, LANES))
    ge_ref[...] = g_end

    def group(q, carry):
        for r in range(live):
            def row_of(idx, j):
                return jnp.broadcast_to(vec_ref[idx, q, pl.ds(j * SUBLANES + r, 1), :], (SUBLANES, LANES))

            sa = [jnp.zeros((SUBLANES, LANES), F32) for _ in range(nb)]
            for j in range(HEAD):
                nk = row_of(0, j)
                for ib in range(nb):
                    sa[ib] = sa[ib] + s_ref[j, ib * SUBLANES:(ib + 1) * SUBLANES, :] * nk
            vt = [v_in[q, pl.ds(ib * SUBLANES * SUBLANES + r, SUBLANES, stride=SUBLANES), :] for ib in range(nb)]
            out = [jnp.zeros((SUBLANES, LANES), F32) for _ in range(nb)]
            for j in range(HEAD):
                kaj = row_of(1, j)
                kj = row_of(2, j)
                rj = row_of(3, j)
                for ib in range(nb):
                    sl = slice(ib * SUBLANES, (ib + 1) * SUBLANES)
                    sn = s_ref[j, sl, :] + sa[ib] * kaj + vt[ib] * kj
                    s_ref[j, sl, :] = sn
                    out[ib] = out[ib] + sn * rj
            o = jnp.concatenate(out, axis=0)
            mean = jnp.mean(o, axis=0, keepdims=True)
            dev = o - mean
            var = jnp.mean(dev * dev, axis=0, keepdims=True)
            y = dev * lax.rsqrt(var + LNX_EPS) * lg_ref[...] + lb_ref[...]
            o_ref[q * live + r] = y + bf_ref[q, pl.ds(r, 1), :] * jnp.concatenate(vt, axis=0)
        return carry

    lax.fori_loop(0, groups, group, 0)

    for j in range(HEAD):
        s_ref[j] = s_ref[j] * jnp.concatenate([ge_ref[j]] * nb, axis=0)


def _rwkv_scan(seqs, s0, row_consts, step_consts, *, seq_len, steps):
    t8, _, p = seqs[0].shape
    n = HEAD
    if seq_len >= SUBLANES:
        groups, live = steps // SUBLANES, SUBLANES
    else:
        groups, live = 1, seq_len
    grid = (p // LANES, t8 // groups)
    in_spec = pl.BlockSpec((groups, n * SUBLANES, LANES), lambda g, t: (t, 0, g))
    out_spec = pl.BlockSpec((groups * live, n, LANES), lambda g, t: (t, 0, g))
    st_spec = pl.BlockSpec((n, n, LANES), lambda g, t: (0, 0, g))
    rc_spec = pl.BlockSpec((n, LANES), lambda g, t: (0, g))
    sc_spec = pl.BlockSpec((n, SUBLANES, LANES), lambda g, t: (0, 0, g))
    kern = functools.partial(_rwkv_scan_kernel, groups=groups, live=live)
    return pl.pallas_call(
        kern, grid=grid,
        in_specs=[in_spec] * 5 + [st_spec] + [rc_spec] * 2 + [sc_spec] * 3,
        out_specs=[out_spec, st_spec],
        out_shape=[jax.ShapeDtypeStruct((seq_len, n, p), F32), jax.ShapeDtypeStruct((n, n, p), F32)],
        scratch_shapes=[pltpu.VMEM((4, groups, n * SUBLANES, LANES), F32), pltpu.VMEM((groups, SUBLANES, LANES), F32),
                        pltpu.VMEM((n, SUBLANES, LANES), F32)],
        compiler_params=_cparams(2), name="rwkv_scan")(*seqs, s0, *row_consts, *step_consts)


def _nh_order(x, axis):
    shape = x.shape
    x = x.reshape(shape[:axis] + (RWKV_HEADS, HEAD) + shape[axis + 1:])
    return jnp.swapaxes(x, axis, axis + 1).reshape(shape)


def _hn_order(x, axis):
    shape = x.shape
    x = x.reshape(shape[:axis] + (HEAD, RWKV_HEADS) + shape[axis + 1:])
    return jnp.swapaxes(x, axis, axis + 1).reshape(shape)


def _to_pairs_kernel(x_ref, o_ref, a_scr):
    nb = x_ref.shape[0]
    for b in range(nb):
        a_scr[b] = x_ref[b].T
    for n in range(HEAD):
        blk = a_scr[:, n * RWKV_HEADS:(n + 1) * RWKV_HEADS, :]
        y = blk.reshape(nb * RWKV_HEADS, blk.shape[2]).T
        o_ref[:, n * SUBLANES:(n + 1) * SUBLANES, :] = y.reshape(y.shape[0] // SUBLANES, SUBLANES, LANES)


def _to_pairs(x, batch, seq_len, tt):
    assert batch * RWKV_HEADS == LANES
    x3 = x.reshape(batch, seq_len, D_MODEL)
    return pl.pallas_call(
        _to_pairs_kernel, grid=(seq_len // tt,),
        in_specs=[pl.BlockSpec((batch, tt, D_MODEL), lambda t: (0, t, 0))],
        out_specs=pl.BlockSpec((tt // SUBLANES, HEAD * SUBLANES, LANES), lambda t: (t, 0, 0)),
        out_shape=jax.ShapeDtypeStruct((seq_len // SUBLANES, HEAD * SUBLANES, LANES), F32),
        scratch_shapes=[pltpu.VMEM((batch, D_MODEL, tt), F32)],
        compiler_params=_cparams(1), name="to_pairs")(x3)


def _from_pairs_kernel(o_ref, x_ref, a_scr):
    nb = x_ref.shape[0]
    for n in range(HEAD):
        blk = o_ref[:, n, :].T
        a_scr[:, n * RWKV_HEADS:(n + 1) * RWKV_HEADS, :] = blk.reshape(nb, RWKV_HEADS, blk.shape[1])
    for b in range(nb):
        x_ref[b] = a_scr[b].T


def _from_pairs(o, batch, seq_len, tt):
    assert batch * RWKV_HEADS == LANES
    out = pl.pallas_call(
        _from_pairs_kernel, grid=(seq_len // tt,),
        in_specs=[pl.BlockSpec((tt, HEAD, LANES), lambda t: (t, 0, 0))],
        out_specs=pl.BlockSpec((batch, tt, D_MODEL), lambda t: (0, t, 0)),
        out_shape=jax.ShapeDtypeStruct((batch, seq_len, D_MODEL), F32),
        scratch_shapes=[pltpu.VMEM((batch, D_MODEL, tt), F32)],
        compiler_params=_cparams(1), name="from_pairs")(o)
    return out.reshape(batch * seq_len, D_MODEL)


def _to_pairs_xla(x, batch, seq_len):
    assert seq_len < SUBLANES
    x = jnp.transpose(x.reshape(batch, seq_len, HEAD, RWKV_HEADS), (2, 1, 0, 3))
    x = x.reshape(HEAD, seq_len, batch * RWKV_HEADS)
    x = jnp.pad(x, ((0, 0), (0, SUBLANES - seq_len), (0, 0)), constant_values=1.0)
    return x.reshape(1, HEAD * SUBLANES, batch * RWKV_HEADS)


def _from_pairs_xla(o, batch, seq_len):
    o = o.reshape(seq_len, HEAD, batch, RWKV_HEADS)
    return jnp.transpose(o, (2, 0, 1, 3)).reshape(batch * seq_len, D_MODEL)


def _head_const_pairs(x, batch):
    x = x.reshape(RWKV_HEADS, HEAD).T
    return jnp.tile(x, (1, batch))


def _count_ge(score, thr):
    return jnp.sum(jnp.where(score >= thr, 1.0, 0.0), axis=-1, keepdims=True)


def _select_topk(score, kpos, k_sel, bisect_steps):
    kf = float(k_sel)
    vis = score > _NEG_INF
    nvis = jnp.sum(jnp.where(vis, 1.0, 0.0), axis=-1, keepdims=True)
    need_sel = nvis > kf
    rowmax = jnp.where(need_sel, jnp.max(score, axis=-1, keepdims=True), 0.0)
    rowmin = jnp.where(need_sel, jnp.min(jnp.where(vis, score, _POS_INF), axis=-1, keepdims=True), 0.0)

    def bis(_, c):
        lo, hi, chi = c
        hfin = jnp.where(hi == _POS_INF, rowmax, hi)
        piv = 0.5 * lo + 0.5 * hfin
        cnt = _count_ge(score, piv)
        ge = cnt >= kf
        return jnp.where(ge, piv, lo), jnp.where(ge, hi, piv), jnp.where(ge, chi, cnt)

    lo, hi, chi = lax.fori_loop(0, bisect_steps, bis,
                                (rowmin, jnp.full_like(rowmin, _POS_INF), jnp.zeros_like(rowmin)))

    def walk_cond(c):
        return jnp.min(c[4]) < 0.5

    def walk(c):
        hi, chi, tau, cgt, done, ceq = c
        pending = done < 0.5
        bmax = jnp.max(jnp.where(score < hi, score, _NEG_INF), axis=-1, keepdims=True)
        cnt = _count_ge(score, bmax)
        fin = jnp.logical_and(cnt >= kf, pending)
        tau = jnp.where(fin, bmax, tau)
        cgt = jnp.where(fin, chi, cgt)
        ceq = jnp.where(fin, cnt - chi, ceq)
        adv = jnp.logical_and(cnt < kf, pending)
        hi = jnp.where(adv, bmax, hi)
        chi = jnp.where(adv, cnt, chi)
        return hi, chi, tau, cgt, jnp.where(fin, 1.0, done), ceq

    done0 = jnp.where(need_sel, 0.0, 1.0)
    neg = jnp.full_like(rowmin, _NEG_INF)
    zero = jnp.zeros_like(rowmin)
    _, _, tau, cgt, _, ceq = lax.while_loop(walk_cond, walk, (hi, chi, neg, zero, done0, zero))

    need = kf - cgt
    excess = jnp.logical_and(need_sel, ceq > need)
    eq = score == tau
    n_keys = score.shape[-1]

    def tie_break(_):
        def body(_, c):
            plo, phi = c
            mid = lax.shift_right_arithmetic(plo + phi, 1)
            cnt = jnp.sum(jnp.where(jnp.logical_and(eq, kpos <= mid), 1.0, 0.0), axis=-1, keepdims=True)
            ge = cnt >= need
            return jnp.where(ge, plo, mid), jnp.where(ge, mid, phi)
        plo0 = jnp.full(tau.shape, -1, jnp.int32)
        phi0 = jnp.full(tau.shape, n_keys - 1, jnp.int32)
        _, phi = lax.fori_loop(0, n_keys.bit_length() + 1, body, (plo0, phi0))
        return jnp.where(excess, phi, n_keys)

    any_excess = jnp.max(jnp.where(excess, 1.0, 0.0)) > 0.0
    pcut = lax.cond(any_excess, tie_break, lambda _: jnp.full(tau.shape, n_keys, jnp.int32), 0)
    pcut = jnp.where(need_sel, pcut, -1)
    return jnp.logical_or(score > tau, jnp.logical_and(eq, kpos <= pcut))


_MASKED = -1e30


def _select_topk_chunked(s_ref, nch, ck, k_sel, nvis, bisect_steps):
    rows = s_ref.shape[0]
    kf = float(k_sel)
    nfold = ck // LANES

    def chunk(c):
        return s_ref[:, pl.ds(pl.multiple_of(c * ck, ck), ck)]

    def fold(x, op):
        out = x[:, 0:LANES]
        for i in range(1, nfold):
            out = op(out, x[:, i * LANES:(i + 1) * LANES])
        return out

    def count(pred):
        def body(c, acc):
            return acc + fold(jnp.where(pred(chunk(c), c), 1.0, 0.0), jnp.add)
        acc = lax.fori_loop(0, nch, body, jnp.zeros((rows, LANES), F32))
        return jnp.sum(acc, axis=-1, keepdims=True)

    def row_max(val):
        def body(c, acc):
            return jnp.maximum(acc, fold(val(chunk(c)), jnp.maximum))
        acc = lax.fori_loop(0, nch, body, jnp.full((rows, LANES), _NEG_INF, F32))
        return jnp.max(acc, axis=-1, keepdims=True)

    need_sel = nvis > kf
    rowmax = jnp.where(need_sel, row_max(lambda x: x), 0.0)
    rowmin = jnp.where(need_sel, -row_max(lambda x: jnp.where(x > _NEG_INF, -x, _NEG_INF)), 0.0)

    def bis(_, c):
        lo, hi, chi = c
        hfin = jnp.where(hi == _POS_INF, rowmax, hi)
        piv = 0.5 * lo + 0.5 * hfin
        cnt = count(lambda x, _c: x >= piv)
        ge = cnt >= kf
        return jnp.where(ge, piv, lo), jnp.where(ge, hi, piv), jnp.where(ge, chi, cnt)

    lo, hi, chi = lax.fori_loop(0, bisect_steps, bis,
                                (rowmin, jnp.full_like(rowmin, _POS_INF), jnp.zeros_like(rowmin)))

    def walk_cond(c):
        return jnp.min(c[4]) < 0.5

    def walk(c):
        hi, chi, tau, cgt, done = c
        pending = done < 0.5
        bmax = row_max(lambda x: jnp.where(x < hi, x, _NEG_INF))
        cnt = count(lambda x, _c: x >= bmax)
        fin = jnp.logical_and(cnt >= kf, pending)
        tau = jnp.where(fin, bmax, tau)
        cgt = jnp.where(fin, chi, cgt)
        adv = jnp.logical_and(cnt < kf, pending)
        hi = jnp.where(adv, bmax, hi)
        chi = jnp.where(adv, cnt, chi)
        return hi, chi, tau, cgt, jnp.where(fin, 1.0, done)

    neg = jnp.full_like(rowmin, _NEG_INF)
    zero = jnp.zeros_like(rowmin)
    _, _, tau, cgt, _ = lax.while_loop(walk_cond, walk, (hi, chi, neg, zero, jnp.where(need_sel, 0.0, 1.0)))

    need = jnp.where(need_sel, kf - cgt, 0.0)
    tri = jnp.where(lax.broadcasted_iota(jnp.int32, (ck, ck), 0) <= lax.broadcasted_iota(jnp.int32, (ck, ck), 1),
                    1.0, 0.0).astype(BF16)

    def write(c, ties_before):
        x = chunk(c)
        eq = x == tau
        rank = jnp.dot(jnp.where(eq, 1.0, 0.0).astype(BF16), tri, preferred_element_type=F32) + ties_before
        sel = jnp.logical_or(x > tau, jnp.logical_and(eq, rank <= need))
        s_ref[:, pl.ds(pl.multiple_of(c * ck, ck), ck)] = jnp.where(sel, 0.0, _MASKED)
        return rank[:, ck - 1:ck]

    lax.fori_loop(0, nch, write, zero)


def _attn_prep_kernel(q_ref, qi_ref, kv_ref, kiw_ref, qg_ref, kg_ref, kn_o, qb_o, qib_o, kvb_o, kib_o):
    w = ATT_KV_HEADS * HEAD
    for h in range(ATT_KV_HEADS):
        hs = slice(h * HEAD, (h + 1) * HEAD)
        x = kv_ref[:, hs]
        kn = x * lax.rsqrt(jnp.mean(x * x, axis=-1, keepdims=True) + NORM_EPS) * kg_ref[...]
        kn_o[:, hs] = kn
        kvb_o[:, hs] = kn.astype(BF16)
        v = kv_ref[:, w + h * HEAD:w + (h + 1) * HEAD].astype(BF16)
        kvb_o[:, w + h * LANES:w + (h + 1) * LANES] = jnp.concatenate([v, jnp.ones_like(v)], axis=1)
    for h in range(ATT_HEADS):
        x = q_ref[:, h * HEAD:(h + 1) * HEAD]
        qn = x * lax.rsqrt(jnp.mean(x * x, axis=-1, keepdims=True) + NORM_EPS) * (qg_ref[...] * HEAD ** -0.5)
        qb_o[h] = qn.astype(BF16)
    qib_o[...] = (qi_ref[...] * HEAD ** -0.5).astype(BF16)
    kib_o[...] = kiw_ref[:, 0:HEAD].astype(BF16)


def _attn_prep(za, qn_g, kn_g, tm):
    m = za.shape[0]
    w = ATT_KV_HEADS * HEAD
    wi = IDX_HEADS * HEAD
    row = lambda width, blk: pl.BlockSpec((tm, width), lambda i, blk=blk: (i, blk))
    gspec = pl.BlockSpec((1, HEAD), lambda i: (0, 0))
    return pl.pallas_call(
        _attn_prep_kernel, grid=(m // tm,),
        in_specs=[row(D_MODEL, ZA_Q // D_MODEL), row(wi, ZA_QI // wi), row(2 * w, ZA_KV // (2 * w)),
                  row(LANES, ZA_KIW // LANES), gspec, gspec],
        out_specs=[row(w, 0), pl.BlockSpec((ATT_HEADS, tm, HEAD), lambda i: (0, i, 0)), row(wi, 0),
                   row(w + ATT_KV_HEADS * LANES, 0), row(HEAD, 0)],
        out_shape=[jax.ShapeDtypeStruct((m, w), F32), jax.ShapeDtypeStruct((ATT_HEADS, m, HEAD), BF16),
                   jax.ShapeDtypeStruct((m, wi), BF16), jax.ShapeDtypeStruct((m, w + ATT_KV_HEADS * LANES), BF16),
                   jax.ShapeDtypeStruct((m, HEAD), BF16)],
        compiler_params=_cparams(1), name="attn_prep")(za, za, za, za, qn_g.reshape(1, HEAD), kn_g.reshape(1, HEAD))


def _bucket_edges():
    max_exact = REL_BUCKETS // 2
    d = np.arange(REL_MAX_DIST + 1)
    df = np.maximum(d, 1).astype(np.float32)
    large = max_exact + (np.log(df / max_exact) / math.log(REL_MAX_DIST / max_exact)
                         * (REL_BUCKETS - max_exact)).astype(np.int32)
    bucket = np.where(d < max_exact, d, np.minimum(large, REL_BUCKETS - 1))
    return [int(np.argmax(bucket >= b)) for b in range(REL_BUCKETS)]


_BUCKET_EDGES = _bucket_edges()


def _rel_bias_lookup(dist, value_of_bucket):
    bias = value_of_bucket(REL_BUCKETS - 1)
    for b in range(REL_BUCKETS - 2, -1, -1):
        bias = jnp.where(dist < _BUCKET_EDGES[b + 1], value_of_bucket(b), bias)
    return bias


def _attn_prompt_kernel(rb_ref, q_ref, qi_ref, kiwq_ref, g_ref, kvb_ref, kib_ref, o_ref, s_scr, tbd_ref,
                        l_scr, mx_scr, acc_scr, *, tq, k_sel, bisect_steps):
    qt = pl.program_id(0)
    q0 = qt * tq
    ck = tq
    rg_rows = LANES
    n_rg = tq // rg_rows
    w = ATT_KV_HEADS * HEAD

    @pl.when(jnp.logical_and(pl.program_id(1) == 0, qt == 0))
    def _():
        rr = lax.broadcasted_iota(jnp.int32, (tq, tq), 0) - lax.broadcasted_iota(jnp.int32, (tq, tq), 1)
        for h in range(ATT_HEADS):
            far = rb_ref[REL_BUCKETS - 1, h]
            tbd_ref[h, 0] = _rel_bias_lookup(rr, lambda b: rb_ref[b, h]) - far
            tbd_ref[h, 1] = _rel_bias_lookup(rr + tq, lambda b: rb_ref[b, h]) - far

    def chunk_start(c):
        return pl.multiple_of(c * ck, ck)

    wcol = kiwq_ref[:, HEAD:HEAD + IDX_HEADS] * (IDX_HEADS ** -0.5)
    qi_h = [[qi_ref[rg * rg_rows:(rg + 1) * rg_rows, h * HEAD:(h + 1) * HEAD] for h in range(IDX_HEADS)]
            for rg in range(n_rg)]
    w_h = [[wcol[rg * rg_rows:(rg + 1) * rg_rows, h:h + 1] for h in range(IDX_HEADS)] for rg in range(n_rg)]

    def score_chunk(c, carry):
        k0 = chunk_start(c)
        kc = kib_ref[pl.ds(k0, ck), :]
        kpos = k0 + lax.broadcasted_iota(jnp.int32, (rg_rows, ck), 1)
        for rg in range(n_rg):
            acc = jnp.zeros((rg_rows, ck), F32)
            for h in range(IDX_HEADS):
                acc = acc + jnp.maximum(_dot_nt(qi_h[rg][h], kc), 0.0) * w_h[rg][h]
            qpos = q0 + rg * rg_rows + lax.broadcasted_iota(jnp.int32, (rg_rows, ck), 0)
            s_scr[rg * rg_rows:(rg + 1) * rg_rows, pl.ds(k0, ck)] = jnp.where(kpos <= qpos, acc, _NEG_INF)
        return carry

    nch = qt + 1
    lax.fori_loop(0, nch, score_chunk, 0)

    nvis = (q0 + 1 + lax.broadcasted_iota(jnp.int32, (tq, 1), 0)).astype(F32)
    _select_topk_chunked(s_scr, nch, ck, k_sel, nvis, bisect_steps)

    g_rows = ATT_GROUP * tq
    n_far = jnp.maximum(qt - 1, 0)
    for kvh in range(ATT_KV_HEADS):
        gs = slice(kvh * ATT_GROUP, (kvh + 1) * ATT_GROUP)
        qg = q_ref[gs, :, :].reshape(g_rows, HEAD)
        mx_scr[...] = jnp.full(mx_scr.shape, _MASKED, F32)
        acc_scr[...] = jnp.zeros(acc_scr.shape, F32)

        def logits_chunk(c, carry, near, qg=qg, gs=gs, kvh=kvh):
            k0 = chunk_start(c)
            kc = kvb_ref[pl.ds(k0, ck), kvh * HEAD:(kvh + 1) * HEAD]
            s3 = _dot_nt(qg, kc).reshape(ATT_GROUP, tq, ck) + s_scr[:, pl.ds(k0, ck)][None]
            if near:
                s3 = s3 + tbd_ref[gs, qt - c, :, :]
            s = s3.reshape(g_rows, ck)
            l_scr[:, pl.ds(k0, ck)] = s
            m = mx_scr[...]
            for i in range(ck // LANES):
                m = jnp.maximum(m, s[:, i * LANES:(i + 1) * LANES])
            mx_scr[...] = m
            return carry

        lax.fori_loop(0, n_far, functools.partial(logits_chunk, near=False), 0)
        lax.fori_loop(n_far, nch, functools.partial(logits_chunk, near=True), 0)
        mx_scr[...] = jnp.broadcast_to(jnp.max(mx_scr[...], axis=-1, keepdims=True), mx_scr.shape)

        def pv_chunk(c, carry, kvh=kvh):
            k0 = chunk_start(c)
            vx = kvb_ref[pl.ds(k0, ck), w + kvh * LANES:w + (kvh + 1) * LANES]
            m = mx_scr[...]
            p = jnp.concatenate([jnp.exp(l_scr[:, pl.ds(k0 + i * LANES, LANES)] - m)
                                 for i in range(ck // LANES)], axis=1).astype(BF16)
            acc_scr[...] += jnp.dot(p, vx, preferred_element_type=F32)
            return carry

        lax.fori_loop(0, nch, pv_chunk, 0)

        acc = acc_scr[...]
        o = acc[:, 0:HEAD] / acc[:, HEAD:HEAD + 1]
        for pp in range(ATT_GROUP // 2):
            cols = slice((kvh * ATT_GROUP // 2 + pp) * LANES, (kvh * ATT_GROUP // 2 + pp + 1) * LANES)
            gh = g_ref[:, cols].astype(F32)
            pair = jnp.concatenate([o[2 * pp * tq:(2 * pp + 1) * tq], o[(2 * pp + 1) * tq:(2 * pp + 2) * tq]], axis=1)
            o_ref[:, cols] = (pair * (gh * _sigmoid(gh))).astype(o_ref.dtype)


def _attn_prompt(za, zg, kn, qb, qib, kvb, kib, *, rel_bias, batch, seq_len, tq, k_sel):
    m = za.shape[0]
    nq = seq_len // tq
    w = ATT_KV_HEADS * HEAD
    wi = IDX_HEADS * HEAD
    assert REL_MAX_DIST <= tq and tq % LANES == 0
    row_map = lambda blk: (lambda t, b, blk=blk: (b * nq + t, blk))
    key_map = lambda t, b: (b, 0)
    kern = functools.partial(_attn_prompt_kernel, tq=tq, k_sel=k_sel, bisect_steps=14)
    return pl.pallas_call(
        kern, grid=(nq, batch),
        in_specs=[
            pl.BlockSpec(memory_space=pltpu.SMEM),
            pl.BlockSpec((ATT_HEADS, tq, HEAD), lambda t, b: (0, b * nq + t, 0)),
            pl.BlockSpec((tq, wi), row_map(0)),
            pl.BlockSpec((tq, LANES), row_map(ZA_KIW // LANES)),
            pl.BlockSpec((tq, D_MODEL), row_map(ZG_GATT // D_MODEL)),
            pl.BlockSpec((seq_len, w + ATT_KV_HEADS * LANES), key_map),
            pl.BlockSpec((seq_len, HEAD), key_map),
        ],
        out_specs=pl.BlockSpec((tq, D_MODEL), row_map(0)),
        out_shape=jax.ShapeDtypeStruct((m, D_MODEL), BF16),
        scratch_shapes=[pltpu.VMEM((tq, seq_len), F32), pltpu.VMEM((ATT_HEADS, 2, tq, tq), F32),
                        pltpu.VMEM((ATT_GROUP * tq, seq_len), F32), pltpu.VMEM((ATT_GROUP * tq, LANES), F32),
                        pltpu.VMEM((ATT_GROUP * tq, LANES), F32)],
        compiler_params=_cparams(2), name="attn_prompt")(rel_bias, qb, qib, za, zg, kvb, kib)


def _sample_score_kernel(pt_ref, q_ref, w_ref, kiw_new_ref, *rest, n_pages, page, dec_seq):
    page_refs = rest[:n_pages]
    o_ref = rest[n_pages]
    past = n_pages * page
    kidx = jnp.concatenate([r[...] for r in page_refs], axis=0).astype(BF16)
    new = kiw_new_ref[:, 0:HEAD]
    new = jnp.concatenate([new, jnp.zeros((LANES - dec_seq, HEAD), F32)], axis=0).astype(BF16)
    q = (q_ref[...] * (HEAD ** -0.5)).astype(BF16)
    lg = jnp.concatenate([_dot_nt(q, kidx), _dot_nt(q, new)], axis=1)
    wr = jnp.maximum(lg, 0.0) * (w_ref[...] * (IDX_HEADS ** -0.5))
    n_keys = past + LANES
    sc = jnp.sum(wr.reshape(dec_seq, IDX_HEADS, n_keys), axis=1)
    kpos = lax.broadcasted_iota(jnp.int32, (dec_seq, n_keys), 1)
    tpos = lax.broadcasted_iota(jnp.int32, (dec_seq, n_keys), 0)
    o_ref[...] = jnp.where(kpos <= past + tpos, sc, _NEG_INF)


def _sample_select_kernel(s_ref, o_ref, *, k_sel, bisect_steps):
    score = s_ref[...]
    kpos = lax.broadcasted_iota(jnp.int32, score.shape, 1)
    sel = _select_topk(score, kpos, k_sel, bisect_steps)
    o_ref[...] = jnp.where(sel, 0.0, _NEG_INF)


def _sample_attn_kernel(pt_ref, q_ref, g_ref, kn_new_ref, kv_new_ref, mask_ref, rb_rows_ref, qg_ref, *rest,
                        n_pages, page, dec_seq):
    k_pages = rest[:n_pages]
    v_pages = rest[n_pages:2 * n_pages]
    o_ref = rest[2 * n_pages]
    bias_ref = rest[2 * n_pages + 1]
    w = ATT_KV_HEADS * HEAD
    rows = dec_seq * ATT_HEADS
    n_keys = n_pages * page + LANES

    @pl.when(pl.program_id(0) == 0)
    def _():
        t_row = lax.div(lax.broadcasted_iota(jnp.int32, (rows, n_keys), 0), ATT_HEADS)
        dist = n_pages * page + t_row - lax.broadcasted_iota(jnp.int32, (rows, n_keys), 1)
        bias_ref[...] = _rel_bias_lookup(dist, lambda b: rb_rows_ref[:, b:b + 1])

    pad = jnp.zeros((LANES - dec_seq, w), F32)
    k_all = jnp.concatenate([r[...] for r in k_pages] + [kn_new_ref[...], pad], axis=0).astype(BF16)
    v_all = jnp.concatenate([r[...] for r in v_pages] + [kv_new_ref[:, w:2 * w], pad], axis=0).astype(BF16)
    q = q_ref[...]
    q = q * lax.rsqrt(jnp.mean(q * q, axis=-1, keepdims=True) + NORM_EPS) * qg_ref[...]
    qb = q.astype(BF16)
    head = lax.rem(lax.broadcasted_iota(jnp.int32, (rows, 1), 0), ATT_HEADS)
    first = head < ATT_GROUP
    lg = jnp.where(first, _dot_nt(qb, k_all[:, 0:HEAD]), _dot_nt(qb, k_all[:, HEAD:2 * HEAD]))
    mask = jnp.broadcast_to(mask_ref[...][:, None, :], (dec_seq, ATT_HEADS, n_keys)).reshape(rows, n_keys)
    s = lg * (HEAD ** -0.5) + bias_ref[...] + mask
    p = jnp.exp(s - jnp.max(s, axis=-1, keepdims=True))
    l = jnp.sum(p, axis=-1, keepdims=True)
    pb = p.astype(BF16)
    o = jnp.where(first, jnp.dot(pb, v_all[:, 0:HEAD], preferred_element_type=F32),
                  jnp.dot(pb, v_all[:, HEAD:2 * HEAD], preferred_element_type=F32)) / l
    g = g_ref[...].astype(F32)
    o_ref[...] = (o * (g * _sigmoid(g))).astype(o_ref.dtype)


def _attn_sample(za, zg, kn, qb, qib, kvb, kib, *, cache_k, cache_v, cache_kidx, page_table, rel_bias, qn_g,
                 dec_seq, k_sel):
    nb, n_pages = page_table.shape
    n_phys, page = cache_k.shape[0], cache_k.shape[1]
    past = n_pages * page
    n_keys = past + LANES
    w = ATT_KV_HEADS * HEAD
    ck = cache_k.reshape(n_phys, page, w)
    cv = cache_v.reshape(n_phys, page, w)
    ci = cache_kidx.reshape(n_phys, page, HEAD)
    za3 = za.reshape(nb, dec_seq, ZA_COLS)
    kn3 = kn.reshape(nb, dec_seq, w)
    qi = za[:, ZA_QI:ZA_QI + IDX_HEADS * HEAD].reshape(nb, dec_seq * IDX_HEADS, HEAD)
    wi = za[:, ZA_KIW + HEAD:ZA_KIW + HEAD + IDX_HEADS].reshape(nb, dec_seq * IDX_HEADS, 1)
    qa = za[:, ZA_Q:ZA_Q + D_MODEL].reshape(nb, dec_seq * ATT_HEADS, HEAD)
    ga = zg[:, ZG_GATT:ZG_GATT + D_MODEL].reshape(nb, dec_seq * ATT_HEADS, HEAD)

    def page_specs(width):
        return [pl.BlockSpec((None, page, width), lambda b, pt, j=j: (pt[b, j], 0, 0)) for j in range(n_pages)]

    per_b = lambda r, c: pl.BlockSpec((None, r, c), lambda b, pt: (b, 0, 0))
    kiw_new = pl.BlockSpec((None, dec_seq, LANES), lambda b, pt: (b, 0, ZA_KIW // LANES))

    scores = pl.pallas_call(
        functools.partial(_sample_score_kernel, n_pages=n_pages, page=page, dec_seq=dec_seq),
        grid_spec=pltpu.PrefetchScalarGridSpec(
            num_scalar_prefetch=1, grid=(nb,),
            in_specs=[per_b(dec_seq * IDX_HEADS, HEAD), per_b(dec_seq * IDX_HEADS, 1), kiw_new] + page_specs(HEAD),
            out_specs=per_b(dec_seq, n_keys)),
        out_shape=jax.ShapeDtypeStruct((nb, dec_seq, n_keys), F32),
        compiler_params=_cparams(1), name="sample_scores")(page_table, qi, wi, za3, *([ci] * n_pages))

    rows = nb * dec_seq
    tr = min(rows, 128)
    mask = pl.pallas_call(
        functools.partial(_sample_select_kernel, k_sel=k_sel, bisect_steps=14),
        grid=(rows // tr,),
        in_specs=[pl.BlockSpec((tr, n_keys), lambda i: (i, 0))],
        out_specs=pl.BlockSpec((tr, n_keys), lambda i: (i, 0)),
        out_shape=jax.ShapeDtypeStruct((rows, n_keys), F32),
        compiler_params=_cparams(1), name="sample_select")(scores.reshape(rows, n_keys))

    rb_rows = jnp.tile(rel_bias.T, (dec_seq, 1))

    const = lambda shape: pl.BlockSpec(shape, lambda b, pt: (0,) * len(shape))
    kv_new = pl.BlockSpec((None, dec_seq, 2 * w), lambda b, pt: (b, 0, ZA_KV // (2 * w)))
    out = pl.pallas_call(
        functools.partial(_sample_attn_kernel, n_pages=n_pages, page=page, dec_seq=dec_seq),
        grid_spec=pltpu.PrefetchScalarGridSpec(
            num_scalar_prefetch=1, grid=(nb,),
            in_specs=[per_b(dec_seq * ATT_HEADS, HEAD), per_b(dec_seq * ATT_HEADS, HEAD), per_b(dec_seq, w),
                      kv_new, per_b(dec_seq, n_keys), const((dec_seq * ATT_HEADS, REL_BUCKETS)), const((1, HEAD))]
            + page_specs(w) + page_specs(w),
            out_specs=per_b(dec_seq * ATT_HEADS, HEAD),
            scratch_shapes=[pltpu.VMEM((dec_seq * ATT_HEADS, n_keys), F32)]),
        out_shape=jax.ShapeDtypeStruct((nb, dec_seq * ATT_HEADS, HEAD), BF16),
        compiler_params=_cparams(1), name="sample_attn")(
            page_table, qa, ga, kn3, za3, mask.reshape(nb, dec_seq, n_keys), rb_rows, qn_g.reshape(1, HEAD),
            *([ck] * n_pages), *([cv] * n_pages))
    return out.reshape(rows, D_MODEL)


def _merge_kernel(x_ref, oa_ref, sg_ref, ob_ref, ga_ref, gb_ref, wpa_ref, wpb_ref, wo_ref, y_ref):
    oa = (oa_ref[...] * sg_ref[...]).astype(BF16)
    pa = jnp.dot(oa, wpa_ref[...], preferred_element_type=F32)
    pb = jnp.dot(ob_ref[...].astype(BF16), wpb_ref[...], preferred_element_type=F32)
    merged = _sigmoid(ga_ref[...].astype(F32)) * pa + _sigmoid(gb_ref[...].astype(F32)) * pb
    y_ref[...] = x_ref[...] + jnp.dot(merged.astype(BF16), wo_ref[...], preferred_element_type=F32)


def _merge(x, oa, sg, ob, zg, w_pa, w_pb, w_out, tm):
    m, d = x.shape
    row = pl.BlockSpec((tm, d), lambda i: (i, 0))
    wsp = pl.BlockSpec((d, d), lambda i: (0, 0))
    return pl.pallas_call(
        _merge_kernel, grid=(m // tm,),
        in_specs=[row, row, row, row,
                  pl.BlockSpec((tm, d), lambda i: (i, ZG_GA // d)), pl.BlockSpec((tm, d), lambda i: (i, ZG_GB // d)),
                  wsp, wsp, wsp],
        out_specs=row,
        out_shape=jax.ShapeDtypeStruct((m, d), F32),
        compiler_params=_cparams(1), name="merge")(x, oa, sg, ob, zg, zg, w_pa, w_pb, w_out)


def _layer(x, shift, s0, params, *, batch, seq_len, attend):
    (norm_g, w_r, w_a, mu, w0, w2, a0, a2, k_k, k_a, r_k, lnx_g, lnx_b, qn_g, kn_g, w_pa, w_pb, w_out) = params
    m = batch * seq_len
    tm = min(m, 512)
    xn = _rmsnorm(x, norm_g, tm)
    tm_proj = min(m, 2048)
    zr = _matmul(xn, w_r, tm_proj, RWKV_COLS // 3, "inproj_rwkv")
    w_a, w_g = w_a
    za = _matmul(xn, w_a, tm_proj, ZA_COLS // 2, "inproj_attn")
    zg = _matmul(xn, w_g, tm_proj, ZG_COLS // 3, "inproj_gate", out_dtype=BF16)

    tile = min(m, 256)
    r, w, k, v, a, sg = _rwkv_prep(zr, shift, mu, w0, w2, a0, a2, batch=batch, seq_len=seq_len, tile=tile)
    relayout_tile = 128
    in_kernel_relayout = seq_len % relayout_tile == 0 and batch * RWKV_HEADS == LANES
    if in_kernel_relayout:
        seqs = [_to_pairs(t, batch, seq_len, relayout_tile) for t in (r, w, k, v, a)]
    else:
        seqs = [_to_pairs_xla(t, batch, seq_len) for t in (r, w, k, v, a)]
    row_consts = [_head_const_pairs(t, batch) for t in (lnx_g, lnx_b)]
    step_consts = [jnp.broadcast_to(_head_const_pairs(t, batch)[:, None, :], (HEAD, SUBLANES, batch * RWKV_HEADS))
                   for t in (r_k.reshape(-1), k_k, k_a)]
    o_t, s_t = _rwkv_scan(seqs, s0, row_consts, step_consts, seq_len=seq_len, steps=min(seq_len, 32))
    if in_kernel_relayout:
        oa = _from_pairs(o_t, batch, seq_len, relayout_tile)
    else:
        oa = _from_pairs_xla(o_t, batch, seq_len)

    kn, qb, qib, kvb, kib = _attn_prep(za, qn_g, kn_g, tm)
    ob = attend(za, zg, kn, qb, qib, kvb, kib)

    y = _merge(x, oa, sg, ob, zg, w_pa, w_pb, w_out, min(m, 256))
    return y, zr, za, kn, s_t


def _state_to_pairs(s):
    b, h, n, _ = s.shape
    return jnp.transpose(s, (3, 2, 0, 1)).reshape(n, n, b * h)


def _state_from_pairs(s, batch):
    n = s.shape[0]
    return jnp.transpose(s.reshape(n, n, batch, RWKV_HEADS), (2, 3, 1, 0))


def kernel(x_prompt, x_sample, cache_k, cache_v, cache_kidx, state_wkv, state_shift, page_table, norm_g, w_in,
           shift_mu, w0, w2, a0, a2, k_k, k_a, r_k, lnx_g, lnx_b, q_norm_g, k_norm_g, rel_bias, w_pa, w_pb, w_out):
    bsz, seq, d = x_prompt.shape
    dec_bsz, dec_seq, _ = x_sample.shape
    depth = w_in.shape[0]
    assert depth == 1 and d == D_MODEL
    past_len = page_table.shape[1] * cache_k.shape[2]
    topk_p = min(TOPK_MAX, seq // 4)
    topk_s = min(TOPK_MAX, (past_len + dec_seq) // 4)
    l = 0

    wl = w_in[l]
    c0 = RWKV_COLS
    q_w, kv_w, qi_w = wl[:, c0:c0 + 1024], wl[:, c0 + 1024:c0 + 1280], wl[:, c0 + 1280:c0 + 1792]
    kiw_w = wl[:, c0 + 1792:c0 + 1864]
    rest_w = wl[:, c0 + 1864:]
    zpad = lambda n: jnp.zeros((d, n), wl.dtype)
    w_a = jnp.concatenate([q_w, qi_w, kv_w, kiw_w, zpad(LANES - kiw_w.shape[1]), zpad(LANES)], axis=1).astype(BF16)
    w_g = rest_w.astype(BF16)
    assert w_a.shape[1] == ZA_COLS and w_g.shape[1] == ZG_COLS
    w_a = (w_a, w_g)
    def rwkv_cols(x, fn):
        lead = x.shape[:-1]
        main = fn(x[..., :4 * d].reshape(lead + (4, d)), len(lead) + 1).reshape(lead + (4 * d,))
        return jnp.concatenate([main, x[..., 4 * d:]], axis=-1)

    w_r = rwkv_cols(wl[:, :c0], _nh_order).astype(BF16)
    params = (norm_g[l], w_r, w_a, rwkv_cols(shift_mu[l], _nh_order), _nh_order(w0[l], 0), _nh_order(w2[l], 1),
              _nh_order(a0[l], 0), _nh_order(a2[l], 1), k_k[l], k_a[l], r_k[l],
              lnx_g[l], lnx_b[l], q_norm_g[l], k_norm_g[l],
              _nh_order(w_pa[l], 0).astype(BF16), w_pb[l].astype(BF16), w_out[l].astype(BF16))

    xp = x_prompt.reshape(bsz * seq, d)
    attend_p = functools.partial(_attn_prompt, rel_bias=rel_bias, batch=bsz, seq_len=seq,
                                 tq=min(seq, 256), k_sel=topk_p)
    yp, zr_p, za_p, kn_p, st_p = _layer(
        xp, jnp.zeros((bsz, 1, RWKV_COLS), F32), jnp.zeros((HEAD, HEAD, bsz * RWKV_HEADS), F32), params,
        batch=bsz, seq_len=seq, attend=attend_p)

    xs = x_sample.reshape(dec_bsz * dec_seq, d)
    attend_s = functools.partial(_attn_sample, cache_k=cache_k[l], cache_v=cache_v[l], cache_kidx=cache_kidx[l],
                                 page_table=page_table, rel_bias=rel_bias, qn_g=q_norm_g[l], dec_seq=dec_seq,
                                 k_sel=topk_s)
    shift_rows = jnp.repeat(rwkv_cols(state_shift[l], _nh_order), dec_seq, axis=0)
    ys, zr_s, za_s, kn_s, st_s = _layer(
        xs, shift_rows, _state_to_pairs(state_wkv[l]), params, batch=dec_bsz, seq_len=dec_seq, attend=attend_s)

    w = ATT_KV_HEADS * HEAD

    def pack(y, zr, za, kn, st, b, t):
        v = za[:, ZA_KV + w:ZA_KV + 2 * w]
        kidx = za[:, ZA_KIW:ZA_KIW + HEAD]
        return (y.reshape(b, t, d),
                kn.reshape(1, b, t, ATT_KV_HEADS, HEAD), v.reshape(1, b, t, ATT_KV_HEADS, HEAD),
                kidx.reshape(1, b, t, HEAD), _state_from_pairs(st, b)[None],
                rwkv_cols(zr.reshape(b, t, RWKV_COLS)[:, -1], _hn_order)[None])

    p = pack(yp, zr_p, za_p, kn_p, st_p, bsz, seq)
    s = pack(ys, zr_s, za_s, kn_s, st_s, dec_bsz, dec_seq)
    return (p[0], s[0]) + p[1:] + s[1:]
```

```python
import functools
import math

import numpy as np
import jax
import jax.numpy as jnp
from jax import lax
from jax.experimental import pallas as pl
from jax.experimental.pallas import tpu as pltpu

F32 = jnp.float32
BF16 = jnp.bfloat16

D_MODEL = 1024
HEAD = 64
RWKV_HEADS = D_MODEL // HEAD
LORA = 64
LNX_EPS = 64e-5
ATT_HEADS = D_MODEL // HEAD
ATT_KV_HEADS = 2
ATT_GROUP = ATT_HEADS // ATT_KV_HEADS
IDX_HEADS = 8
TOPK_MAX = 256
REL_BUCKETS = 32
REL_MAX_DIST = 128
NORM_EPS = 1e-6
RWKV_COLS = 4 * D_MODEL + 2 * LORA

LANES = 128
SUBLANES = 8
VMEM_LIMIT_BYTES = 56 * 1024 * 1024

ZA_Q = 0
ZA_QI = 1024
ZA_KV = 1536
ZA_KIW = 1792
ZA_GATT = 2048
ZA_GA = 3072
ZA_GB = 4096
ZA_COLS = 5120

_NEG_INF = float("-inf")
_POS_INF = float("inf")
_BISECT_STEPS = 14


def _cparams(n_axes):
    return pltpu.CompilerParams(dimension_semantics=("arbitrary",) * n_axes,
                                vmem_limit_bytes=VMEM_LIMIT_BYTES)


def _sigmoid(x):
    return 1.0 / (1.0 + jnp.exp(-x))


def _dot_nt(a, b):
    return lax.dot_general(a, b, (((1,), (1,)), ((), ())), preferred_element_type=F32)


def _rmsnorm_kernel(x_ref, g_ref, o_ref):
    x = x_ref[...]
    ms = jnp.mean(x * x, axis=-1, keepdims=True)
    o_ref[...] = (x * lax.rsqrt(ms + NORM_EPS) * g_ref[...]).astype(o_ref.dtype)


def _rmsnorm(x, g, tm):
    m, d = x.shape
    return pl.pallas_call(
        _rmsnorm_kernel, grid=(m // tm,),
        in_specs=[pl.BlockSpec((tm, d), lambda i: (i, 0)), pl.BlockSpec((1, d), lambda i: (0, 0))],
        out_specs=pl.BlockSpec((tm, d), lambda i: (i, 0)),
        out_shape=jax.ShapeDtypeStruct((m, d), BF16),
        compiler_params=_cparams(1), name="rmsnorm")(x, g.reshape(1, d))


def _mm_kernel(x_ref, w_ref, o_ref):
    o_ref[...] = jnp.dot(x_ref[...], w_ref[...], preferred_element_type=F32)


def _matmul(x, w, tm, tn, name):
    m, k = x.shape
    n = w.shape[1]
    return pl.pallas_call(
        _mm_kernel, grid=(m // tm, n // tn),
        in_specs=[pl.BlockSpec((tm, k), lambda i, j: (i, 0)), pl.BlockSpec((k, tn), lambda i, j: (0, j))],
        out_specs=pl.BlockSpec((tm, tn), lambda i, j: (i, j)),
        out_shape=jax.ShapeDtypeStruct((m, n), F32),
        compiler_params=_cparams(2), name=name)(x, w)


def _rwkv_prep_kernel(z_ref, prev_ref, shift_ref, mu_ref, w0_ref, w2_ref, a0_ref, a2_ref,
                      r_o, w_o, k_o, v_o, a_o, g_o, *, seq_len, tile):
    z = z_ref[...]
    rows = lax.broadcasted_iota(jnp.int32, z.shape, 0)
    rolled = pltpu.roll(z, 1, 0)
    if seq_len >= tile:
        first = jnp.where(pl.program_id(1) == 0, shift_ref[...], prev_ref[SUBLANES - 1:SUBLANES, :])
        prev = jnp.where(rows == 0, first, rolled)
    else:
        prev = jnp.where(lax.rem(rows, seq_len) == 0, shift_ref[...], rolled)
    zs = z + (prev - z) * mu_ref[...]
    d = D_MODEL
    r = zs[:, 0:d]
    k = zs[:, d:2 * d]
    v = zs[:, 2 * d:3 * d]
    g = zs[:, 3 * d:4 * d]
    wd = zs[:, 4 * d:4 * d + LORA]
    ad = zs[:, 4 * d + LORA:4 * d + 2 * LORA]
    wl = w0_ref[...] + jnp.dot(jnp.tanh(wd).astype(BF16), w2_ref[...], preferred_element_type=F32)
    decay = jnp.exp(-_sigmoid(wl) * math.exp(-0.5))
    a = _sigmoid(a0_ref[...] + jnp.dot(ad.astype(BF16), a2_ref[...], preferred_element_type=F32))
    r_o[...] = r
    w_o[...] = decay
    k_o[...] = k
    v_o[...] = v
    a_o[...] = a
    g_o[...] = g * _sigmoid(g)


def _rwkv_prep(zr, shift, mu, w0, w2, a0, a2, *, batch, seq_len, tile):
    m, c = zr.shape
    d = D_MODEL
    row = lambda x: x.reshape(1, -1)
    consts = [row(mu), row(w0), w2.astype(BF16), row(a0), a2.astype(BF16)]
    const_specs = [pl.BlockSpec(x.shape, lambda *_: (0, 0)) for x in consts]
    if seq_len >= tile:
        nt = seq_len // tile
        grid = (batch, nt)
        zmap = lambda b, t: (b * nt + t, 0)
        pmap = lambda b, t: (jnp.maximum((b * seq_len + t * tile) // SUBLANES - 1, 0), 0)
        shift_spec = pl.BlockSpec((None, 1, c), lambda b, t: (b, 0, 0))
    else:
        grid = (1, m // tile)
        zmap = lambda b, t: (t, 0)
        pmap = lambda b, t: (0, 0)
        shift_spec = pl.BlockSpec((tile, c), zmap)
    out_spec = pl.BlockSpec((tile, d), zmap)
    kern = functools.partial(_rwkv_prep_kernel, seq_len=seq_len, tile=tile)
    return pl.pallas_call(
        kern, grid=grid,
        in_specs=[pl.BlockSpec((tile, c), zmap), pl.BlockSpec((SUBLANES, c), pmap), shift_spec] + const_specs,
        out_specs=[out_spec] * 6,
        out_shape=[jax.ShapeDtypeStruct((m, d), F32)] * 6,
        compiler_params=_cparams(2), name="rwkv_prep")(zr, zr, shift, *consts)


def _rwkv_scan_kernel(r_in, w_in, k_in, v_in, a_in, s0_ref, lg_ref, lb_ref, rk_ref, kkc_ref, kac_ref,
                      o_ref, s_ref, vec_ref, bf_ref, ge_ref, *, groups, live):
    nb = HEAD // SUBLANES

    @pl.when(pl.program_id(1) == 0)
    def _():
        s_ref[...] = s0_ref[...]

    shape4 = (groups, HEAD, SUBLANES, LANES)
    a = a_in[...].reshape(shape4)
    kraw = k_in[...].reshape(shape4)
    r_all = r_in[...].reshape(shape4)
    kkraw = kraw * kkc_ref[...][None]
    n2 = jnp.sum(kkraw * kkraw, axis=1, keepdims=True)
    kk = kkraw / jnp.maximum(jnp.sqrt(n2), 1e-12)
    kmod = kraw * (1.0 + (a - 1.0) * kac_ref[...][None])
    bf_ref[...] = jnp.sum(r_all * kmod * rk_ref[...][None], axis=1)
    row = lax.broadcasted_iota(jnp.int32, (HEAD, SUBLANES, LANES), 1)
    g_end = jnp.ones((HEAD, SUBLANES, LANES), F32)
    for q in range(groups):
        g = w_in[q].reshape(HEAD, SUBLANES, LANES)
        for sh in (1, 2, 4):
            g = g * jnp.where(row >= sh, pltpu.roll(g, sh, 1), 1.0)
        g = g * g_end
        g_prev = jnp.where(row >= 1, pltpu.roll(g, 1, 1), g_end)
        g_inv = 1.0 / g
        flat = (HEAD * SUBLANES, LANES)
        vec_ref[0, q] = (-kk[q] * g_prev).reshape(flat)
        vec_ref[1, q] = (kk[q] * a[q] * g_inv).reshape(flat)
        vec_ref[2, q] = (kmod[q] * g_inv).reshape(flat)
        vec_ref[3, q] = (r_all[q] * g).reshape(flat)
        last = live - 1 if q == groups - 1 else SUBLANES - 1
        g_end = jnp.broadcast_to(g[:, last:last + 1, :], (HEAD, SUBLANES, LANES))
    ge_ref[...] = g_end

    def group(q, carry):
        for r in range(live):
            def row_of(idx, j):
                return jnp.broadcast_to(vec_ref[idx, q, pl.ds(j * SUBLANES + r, 1), :], (SUBLANES, LANES))

            sa = [jnp.zeros((SUBLANES, LANES), F32) for _ in range(nb)]
            for j in range(HEAD):
                nk = row_of(0, j)
                for ib in range(nb):
                    sa[ib] = sa[ib] + s_ref[j, ib * SUBLANES:(ib + 1) * SUBLANES, :] * nk
            vt = [v_in[q, pl.ds(ib * SUBLANES * SUBLANES + r, SUBLANES, stride=SUBLANES), :] for ib in range(nb)]
            out = [jnp.zeros((SUBLANES, LANES), F32) for _ in range(nb)]
            for j in range(HEAD):
                kaj = row_of(1, j)
                kj = row_of(2, j)
                rj = row_of(3, j)
                for ib in range(nb):
                    sl = slice(ib * SUBLANES, (ib + 1) * SUBLANES)
                    sn = s_ref[j, sl, :] + sa[ib] * kaj + vt[ib] * kj
                    s_ref[j, sl, :] = sn
                    out[ib] = out[ib] + sn * rj
            o = jnp.concatenate(out, axis=0)
            mean = jnp.mean(o, axis=0, keepdims=True)
            dev = o - mean
            var = jnp.mean(dev * dev, axis=0, keepdims=True)
            y = dev * lax.rsqrt(var + LNX_EPS) * lg_ref[...] + lb_ref[...]
            o_ref[q * live + r] = y + bf_ref[q, pl.ds(r, 1), :] * jnp.concatenate(vt, axis=0)
        return carry

    lax.fori_loop(0, groups, group, 0)

    for j in range(HEAD):
        s_ref[j] = s_ref[j] * jnp.concatenate([ge_ref[j]] * nb, axis=0)


def _rwkv_scan(seqs, s0, row_consts, step_consts, *, seq_len, steps):
    t8, _, p = seqs[0].shape
    n = HEAD
    if seq_len >= SUBLANES:
        groups, live = steps // SUBLANES, SUBLANES
    else:
        groups, live = 1, seq_len
    grid = (p // LANES, t8 // groups)
    in_spec = pl.BlockSpec((groups, n * SUBLANES, LANES), lambda g, t: (t, 0, g))
    out_spec = pl.BlockSpec((groups * live, n, LANES), lambda g, t: (t, 0, g))
    st_spec = pl.BlockSpec((n, n, LANES), lambda g, t: (0, 0, g))
    rc_spec = pl.BlockSpec((n, LANES), lambda g, t: (0, g))
    sc_spec = pl.BlockSpec((n, SUBLANES, LANES), lambda g, t: (0, 0, g))
    kern = functools.partial(_rwkv_scan_kernel, groups=groups, live=live)
    return pl.pallas_call(
        kern, grid=grid,
        in_specs=[in_spec] * 5 + [st_spec] + [rc_spec] * 2 + [sc_spec] * 3,
        out_specs=[out_spec, st_spec],
        out_shape=[jax.ShapeDtypeStruct((seq_len, n, p), F32), jax.ShapeDtypeStruct((n, n, p), F32)],
        scratch_shapes=[pltpu.VMEM((4, groups, n * SUBLANES, LANES), F32), pltpu.VMEM((groups, SUBLANES, LANES), F32),
                        pltpu.VMEM((n, SUBLANES, LANES), F32)],
        compiler_params=_cparams(2), name="rwkv_scan")(*seqs, s0, *row_consts, *step_consts)


def _nh_order(x, axis):
    shape = x.shape
    x = x.reshape(shape[:axis] + (RWKV_HEADS, HEAD) + shape[axis + 1:])
    return jnp.swapaxes(x, axis, axis + 1).reshape(shape)


def _hn_order(x, axis):
    shape = x.shape
    x = x.reshape(shape[:axis] + (HEAD, RWKV_HEADS) + shape[axis + 1:])
    return jnp.swapaxes(x, axis, axis + 1).reshape(shape)


def _to_pairs_kernel(x_ref, o_ref, a_scr):
    nb = x_ref.shape[0]
    for b in range(nb):
        a_scr[b] = x_ref[b].T
    for n in range(HEAD):
        blk = a_scr[:, n * RWKV_HEADS:(n + 1) * RWKV_HEADS, :]
        y = blk.reshape(nb * RWKV_HEADS, blk.shape[2]).T
        o_ref[:, n * SUBLANES:(n + 1) * SUBLANES, :] = y.reshape(y.shape[0] // SUBLANES, SUBLANES, LANES)


def _to_pairs(x, batch, seq_len, tt):
    assert batch * RWKV_HEADS == LANES
    x3 = x.reshape(batch, seq_len, D_MODEL)
    return pl.pallas_call(
        _to_pairs_kernel, grid=(seq_len // tt,),
        in_specs=[pl.BlockSpec((batch, tt, D_MODEL), lambda t: (0, t, 0))],
        out_specs=pl.BlockSpec((tt // SUBLANES, HEAD * SUBLANES, LANES), lambda t: (t, 0, 0)),
        out_shape=jax.ShapeDtypeStruct((seq_len // SUBLANES, HEAD * SUBLANES, LANES), F32),
        scratch_shapes=[pltpu.VMEM((batch, D_MODEL, tt), F32)],
        compiler_params=_cparams(1), name="to_pairs")(x3)


def _from_pairs_kernel(o_ref, x_ref, a_scr):
    nb = x_ref.shape[0]
    for n in range(HEAD):
        blk = o_ref[:, n, :].T
        a_scr[:, n * RWKV_HEADS:(n + 1) * RWKV_HEADS, :] = blk.reshape(nb, RWKV_HEADS, blk.shape[1])
    for b in range(nb):
        x_ref[b] = a_scr[b].T


def _from_pairs(o, batch, seq_len, tt):
    assert batch * RWKV_HEADS == LANES
    out = pl.pallas_call(
        _from_pairs_kernel, grid=(seq_len // tt,),
        in_specs=[pl.BlockSpec((tt, HEAD, LANES), lambda t: (t, 0, 0))],
        out_specs=pl.BlockSpec((batch, tt, D_MODEL), lambda t: (0, t, 0)),
        out_shape=jax.ShapeDtypeStruct((batch, seq_len, D_MODEL), F32),
        scratch_shapes=[pltpu.VMEM((batch, D_MODEL, tt), F32)],
        compiler_params=_cparams(1), name="from_pairs")(o)
    return out.reshape(batch * seq_len, D_MODEL)


def _to_pairs_xla(x, batch, seq_len):
    assert seq_len < SUBLANES
    x = jnp.transpose(x.reshape(batch, seq_len, HEAD, RWKV_HEADS), (2, 1, 0, 3))
    x = x.reshape(HEAD, seq_len, batch * RWKV_HEADS)
    x = jnp.pad(x, ((0, 0), (0, SUBLANES - seq_len), (0, 0)), constant_values=1.0)
    return x.reshape(1, HEAD * SUBLANES, batch * RWKV_HEADS)


def _from_pairs_xla(o, batch, seq_len):
    o = o.reshape(seq_len, HEAD, batch, RWKV_HEADS)
    return jnp.transpose(o, (2, 0, 1, 3)).reshape(batch * seq_len, D_MODEL)


def _head_const_pairs(x, batch):
    x = x.reshape(RWKV_HEADS, HEAD).T
    return jnp.tile(x, (1, batch))


def _count_ge(score, thr):
    return jnp.sum(jnp.where(score >= thr, 1.0, 0.0), axis=-1, keepdims=True)


def _select_topk(score, kpos, k_sel, bisect_steps):
    kf = float(k_sel)
    vis = score > _NEG_INF
    nvis = jnp.sum(jnp.where(vis, 1.0, 0.0), axis=-1, keepdims=True)
    need_sel = nvis > kf
    rowmax = jnp.where(need_sel, jnp.max(score, axis=-1, keepdims=True), 0.0)
    rowmin = jnp.where(need_sel, jnp.min(jnp.where(vis, score, _POS_INF), axis=-1, keepdims=True), 0.0)

    def bis(_, c):
        lo, hi, chi = c
        hfin = jnp.where(hi == _POS_INF, rowmax, hi)
        piv = 0.5 * lo + 0.5 * hfin
        cnt = _count_ge(score, piv)
        ge = cnt >= kf
        return jnp.where(ge, piv, lo), jnp.where(ge, hi, piv), jnp.where(ge, chi, cnt)

    lo, hi, chi = lax.fori_loop(0, bisect_steps, bis,
                                (rowmin, jnp.full_like(rowmin, _POS_INF), jnp.zeros_like(rowmin)))

    def walk_cond(c):
        return jnp.min(c[4]) < 0.5

    def walk(c):
        hi, chi, tau, cgt, done, ceq = c
        pending = done < 0.5
        bmax = jnp.max(jnp.where(score < hi, score, _NEG_INF), axis=-1, keepdims=True)
        cnt = _count_ge(score, bmax)
        fin = jnp.logical_and(cnt >= kf, pending)
        tau = jnp.where(fin, bmax, tau)
        cgt = jnp.where(fin, chi, cgt)
        ceq = jnp.where(fin, cnt - chi, ceq)
        adv = jnp.logical_and(cnt < kf, pending)
        hi = jnp.where(adv, bmax, hi)
        chi = jnp.where(adv, cnt, chi)
        return hi, chi, tau, cgt, jnp.where(fin, 1.0, done), ceq

    done0 = jnp.where(need_sel, 0.0, 1.0)
    neg = jnp.full_like(rowmin, _NEG_INF)
    zero = jnp.zeros_like(rowmin)
    _, _, tau, cgt, _, ceq = lax.while_loop(walk_cond, walk, (hi, chi, neg, zero, done0, zero))

    need = kf - cgt
    excess = jnp.logical_and(need_sel, ceq > need)
    eq = score == tau
    n_keys = score.shape[-1]

    def tie_break(_):
        def body(_, c):
            plo, phi = c
            mid = lax.shift_right_arithmetic(plo + phi, 1)
            cnt = jnp.sum(jnp.where(jnp.logical_and(eq, kpos <= mid), 1.0, 0.0), axis=-1, keepdims=True)
            ge = cnt >= need
            return jnp.where(ge, plo, mid), jnp.where(ge, mid, phi)
        plo0 = jnp.full(tau.shape, -1, jnp.int32)
        phi0 = jnp.full(tau.shape, n_keys - 1, jnp.int32)
        _, phi = lax.fori_loop(0, n_keys.bit_length() + 1, body, (plo0, phi0))
        return jnp.where(excess, phi, n_keys)

    any_excess = jnp.max(jnp.where(excess, 1.0, 0.0)) > 0.0
    pcut = lax.cond(any_excess, tie_break, lambda _: jnp.full(tau.shape, n_keys, jnp.int32), 0)
    pcut = jnp.where(need_sel, pcut, -1)
    return jnp.logical_or(score > tau, jnp.logical_and(eq, kpos <= pcut))


_MASKED = -1e30


def _select_topk_chunked(s_ref, nch, ck, k_sel, nvis, bisect_steps):
    rows = s_ref.shape[0]
    kf = float(k_sel)
    nfold = ck // LANES

    def chunk(c):
        return s_ref[:, pl.ds(pl.multiple_of(c * ck, ck), ck)]

    def fold(x, op):
        out = x[:, 0:LANES]
        for i in range(1, nfold):
            out = op(out, x[:, i * LANES:(i + 1) * LANES])
        return out

    def count(pred):
        def body(c, acc):
            return acc + fold(jnp.where(pred(chunk(c), c), 1.0, 0.0), jnp.add)
        acc = lax.fori_loop(0, nch, body, jnp.zeros((rows, LANES), F32))
        return jnp.sum(acc, axis=-1, keepdims=True)

    def row_max(val):
        def body(c, acc):
            return jnp.maximum(acc, fold(val(chunk(c)), jnp.maximum))
        acc = lax.fori_loop(0, nch, body, jnp.full((rows, LANES), _NEG_INF, F32))
        return jnp.max(acc, axis=-1, keepdims=True)

    need_sel = nvis > kf
    rowmax = jnp.where(need_sel, row_max(lambda x: x), 0.0)
    rowmin = jnp.where(need_sel, -row_max(lambda x: jnp.where(x > _NEG_INF, -x, _NEG_INF)), 0.0)

    def bis(_, c):
        lo, hi, chi = c
        hfin = jnp.where(hi == _POS_INF, rowmax, hi)
        piv = 0.5 * lo + 0.5 * hfin
        cnt = count(lambda x, _c: x >= piv)
        ge = cnt >= kf
        return jnp.where(ge, piv, lo), jnp.where(ge, hi, piv), jnp.where(ge, chi, cnt)

    lo, hi, chi = lax.fori_loop(0, bisect_steps, bis,
                                (rowmin, jnp.full_like(rowmin, _POS_INF), jnp.zeros_like(rowmin)))

    def walk_cond(c):
        return jnp.min(c[4]) < 0.5

    def walk(c):
        hi, chi, tau, cgt, done = c
        pending = done < 0.5
        bmax = row_max(lambda x: jnp.where(x < hi, x, _NEG_INF))
        cnt = count(lambda x, _c: x >= bmax)
        fin = jnp.logical_and(cnt >= kf, pending)
        tau = jnp.where(fin, bmax, tau)
        cgt = jnp.where(fin, chi, cgt)
        adv = jnp.logical_and(cnt < kf, pending)
        hi = jnp.where(adv, bmax, hi)
        chi = jnp.where(adv, cnt, chi)
        return hi, chi, tau, cgt, jnp.where(fin, 1.0, done)

    neg = jnp.full_like(rowmin, _NEG_INF)
    zero = jnp.zeros_like(rowmin)
    _, _, tau, cgt, _ = lax.while_loop(walk_cond, walk, (hi, chi, neg, zero, jnp.where(need_sel, 0.0, 1.0)))

    need = jnp.where(need_sel, kf - cgt, 0.0)
    tri = jnp.where(lax.broadcasted_iota(jnp.int32, (ck, ck), 0) <= lax.broadcasted_iota(jnp.int32, (ck, ck), 1),
                    1.0, 0.0).astype(BF16)

    def write(c, ties_before):
        x = chunk(c)
        eq = x == tau
        rank = jnp.dot(jnp.where(eq, 1.0, 0.0).astype(BF16), tri, preferred_element_type=F32) + ties_before
        sel = jnp.logical_or(x > tau, jnp.logical_and(eq, rank <= need))
        s_ref[:, pl.ds(pl.multiple_of(c * ck, ck), ck)] = jnp.where(sel, 0.0, _MASKED)
        return rank[:, ck - 1:ck]

    lax.fori_loop(0, nch, write, zero)


def _attn_prep_kernel(q_ref, qi_ref, kv_ref, kiw_ref, qg_ref, kg_ref, kn_o, qb_o, qib_o, kvb_o, kib_o):
    w = ATT_KV_HEADS * HEAD
    for h in range(ATT_KV_HEADS):
        hs = slice(h * HEAD, (h + 1) * HEAD)
        x = kv_ref[:, hs]
        kn = x * lax.rsqrt(jnp.mean(x * x, axis=-1, keepdims=True) + NORM_EPS) * kg_ref[...]
        kn_o[:, hs] = kn
        kvb_o[:, hs] = kn.astype(BF16)
        v = kv_ref[:, w + h * HEAD:w + (h + 1) * HEAD].astype(BF16)
        kvb_o[:, w + h * LANES:w + (h + 1) * LANES] = jnp.concatenate([v, jnp.ones_like(v)], axis=1)
    for h in range(ATT_HEADS):
        x = q_ref[:, h * HEAD:(h + 1) * HEAD]
        qn = x * lax.rsqrt(jnp.mean(x * x, axis=-1, keepdims=True) + NORM_EPS) * (qg_ref[...] * HEAD ** -0.5)
        qb_o[h] = qn.astype(BF16)
    qib_o[...] = (qi_ref[...] * HEAD ** -0.5).astype(BF16)
    kib_o[...] = kiw_ref[:, 0:HEAD].astype(BF16)


def _attn_prep(za, qn_g, kn_g, tm):
    m = za.shape[0]
    w = ATT_KV_HEADS * HEAD
    wi = IDX_HEADS * HEAD
    row = lambda width, blk: pl.BlockSpec((tm, width), lambda i, blk=blk: (i, blk))
    gspec = pl.BlockSpec((1, HEAD), lambda i: (0, 0))
    return pl.pallas_call(
        _attn_prep_kernel, grid=(m // tm,),
        in_specs=[row(D_MODEL, ZA_Q // D_MODEL), row(wi, ZA_QI // wi), row(2 * w, ZA_KV // (2 * w)),
                  row(LANES, ZA_KIW // LANES), gspec, gspec],
        out_specs=[row(w, 0), pl.BlockSpec((ATT_HEADS, tm, HEAD), lambda i: (0, i, 0)), row(wi, 0),
                   row(w + ATT_KV_HEADS * LANES, 0), row(HEAD, 0)],
        out_shape=[jax.ShapeDtypeStruct((m, w), F32), jax.ShapeDtypeStruct((ATT_HEADS, m, HEAD), BF16),
                   jax.ShapeDtypeStruct((m, wi), BF16), jax.ShapeDtypeStruct((m, w + ATT_KV_HEADS * LANES), BF16),
                   jax.ShapeDtypeStruct((m, HEAD), BF16)],
        compiler_params=_cparams(1), name="attn_prep")(za, za, za, za, qn_g.reshape(1, HEAD), kn_g.reshape(1, HEAD))


def _bucket_edges():
    max_exact = REL_BUCKETS // 2
    d = np.arange(REL_MAX_DIST + 1)
    df = np.maximum(d, 1).astype(np.float32)
    large = max_exact + (np.log(df / max_exact) / math.log(REL_MAX_DIST / max_exact)
                         * (REL_BUCKETS - max_exact)).astype(np.int32)
    bucket = np.where(d < max_exact, d, np.minimum(large, REL_BUCKETS - 1))
    return [int(np.argmax(bucket >= b)) for b in range(REL_BUCKETS)]


_BUCKET_EDGES = _bucket_edges()


def _rel_bias_lookup(dist, value_of_bucket):
    bias = value_of_bucket(REL_BUCKETS - 1)
    for b in range(REL_BUCKETS - 2, -1, -1):
        bias = jnp.where(dist < _BUCKET_EDGES[b + 1], value_of_bucket(b), bias)
    return bias


def _attn_prompt_kernel(rb_ref, q_ref, qi_ref, kiwq_ref, g_ref, kvb_ref, kib_ref, o_ref, s_scr, tbd_ref,
                        l_scr, mx_scr, acc_scr, *, tq, k_sel, bisect_steps):
    qt = pl.program_id(0)
    q0 = qt * tq
    ck = tq
    rg_rows = LANES
    n_rg = tq // rg_rows
    w = ATT_KV_HEADS * HEAD

    @pl.when(jnp.logical_and(pl.program_id(1) == 0, qt == 0))
    def _():
        rr = lax.broadcasted_iota(jnp.int32, (tq, tq), 0) - lax.broadcasted_iota(jnp.int32, (tq, tq), 1)
        for h in range(ATT_HEADS):
            far = rb_ref[REL_BUCKETS - 1, h]
            tbd_ref[h, 0] = _rel_bias_lookup(rr, lambda b: rb_ref[b, h]) - far
            tbd_ref[h, 1] = _rel_bias_lookup(rr + tq, lambda b: rb_ref[b, h]) - far

    def chunk_start(c):
        return pl.multiple_of(c * ck, ck)

    wcol = kiwq_ref[:, HEAD:HEAD + IDX_HEADS] * (IDX_HEADS ** -0.5)
    qi_h = [[qi_ref[rg * rg_rows:(rg + 1) * rg_rows, h * HEAD:(h + 1) * HEAD] for h in range(IDX_HEADS)]
            for rg in range(n_rg)]
    w_h = [[wcol[rg * rg_rows:(rg + 1) * rg_rows, h:h + 1] for h in range(IDX_HEADS)] for rg in range(n_rg)]

    def score_chunk(c, carry):
        k0 = chunk_start(c)
        kc = kib_ref[pl.ds(k0, ck), :]
        kpos = k0 + lax.broadcasted_iota(jnp.int32, (rg_rows, ck), 1)
        for rg in range(n_rg):
            acc = jnp.zeros((rg_rows, ck), F32)
            for h in range(IDX_HEADS):
                acc = acc + jnp.maximum(_dot_nt(qi_h[rg][h], kc), 0.0) * w_h[rg][h]
            qpos = q0 + rg * rg_rows + lax.broadcasted_iota(jnp.int32, (rg_rows, ck), 0)
            s_scr[rg * rg_rows:(rg + 1) * rg_rows, pl.ds(k0, ck)] = jnp.where(kpos <= qpos, acc, _NEG_INF)
        return carry

    nch = qt + 1
    lax.fori_loop(0, nch, score_chunk, 0)

    nvis = (q0 + 1 + lax.broadcasted_iota(jnp.int32, (tq, 1), 0)).astype(F32)
    _select_topk_chunked(s_scr, nch, ck, k_sel, nvis, bisect_steps)

    g_rows = ATT_GROUP * tq
    n_far = jnp.maximum(qt - 1, 0)
    for kvh in range(ATT_KV_HEADS):
        gs = slice(kvh * ATT_GROUP, (kvh + 1) * ATT_GROUP)
        qg = q_ref[gs, :, :].reshape(g_rows, HEAD)
        mx_scr[...] = jnp.full(mx_scr.shape, _MASKED, F32)
        acc_scr[...] = jnp.zeros(acc_scr.shape, F32)

        def logits_chunk(c, carry, near, qg=qg, gs=gs, kvh=kvh):
            k0 = chunk_start(c)
            kc = kvb_ref[pl.ds(k0, ck), kvh * HEAD:(kvh + 1) * HEAD]
            s3 = _dot_nt(qg, kc).reshape(ATT_GROUP, tq, ck) + s_scr[:, pl.ds(k0, ck)][None]
            if near:
                s3 = s3 + tbd_ref[gs, qt - c, :, :]
            s = s3.reshape(g_rows, ck)
            l_scr[:, pl.ds(k0, ck)] = s
            m = mx_scr[...]
            for i in range(ck // LANES):
                m = jnp.maximum(m, s[:, i * LANES:(i + 1) * LANES])
            mx_scr[...] = m
            return carry

        lax.fori_loop(0, n_far, functools.partial(logits_chunk, near=False), 0)
        lax.fori_loop(n_far, nch, functools.partial(logits_chunk, near=True), 0)
        mx_scr[...] = jnp.broadcast_to(jnp.max(mx_scr[...], axis=-1, keepdims=True), mx_scr.shape)

        def pv_chunk(c, carry, kvh=kvh):
            k0 = chunk_start(c)
            vx = kvb_ref[pl.ds(k0, ck), w + kvh * LANES:w + (kvh + 1) * LANES]
            m = mx_scr[...]
            p = jnp.concatenate([jnp.exp(l_scr[:, pl.ds(k0 + i * LANES, LANES)] - m)
                                 for i in range(ck // LANES)], axis=1).astype(BF16)
            acc_scr[...] += jnp.dot(p, vx, preferred_element_type=F32)
            return carry

        lax.fori_loop(0, nch, pv_chunk, 0)

        acc = acc_scr[...]
        o = acc[:, 0:HEAD] / acc[:, HEAD:HEAD + 1]
        for pp in range(ATT_GROUP // 2):
            cols = slice((kvh * ATT_GROUP // 2 + pp) * LANES, (kvh * ATT_GROUP // 2 + pp + 1) * LANES)
            gh = g_ref[:, cols]
            pair = jnp.concatenate([o[2 * pp * tq:(2 * pp + 1) * tq], o[(2 * pp + 1) * tq:(2 * pp + 2) * tq]], axis=1)
            o_ref[:, cols] = pair * (gh * _sigmoid(gh))


def _attn_prompt(za, kn, qb, qib, kvb, kib, *, rel_bias, batch, seq_len, tq, k_sel):
    m = za.shape[0]
    nq = seq_len // tq
    w = ATT_KV_HEADS * HEAD
    wi = IDX_HEADS * HEAD
    assert REL_MAX_DIST <= tq and tq % LANES == 0
    row_map = lambda blk: (lambda t, b, blk=blk: (b * nq + t, blk))
    key_map = lambda t, b: (b, 0)
    kern = functools.partial(_attn_prompt_kernel, tq=tq, k_sel=k_sel, bisect_steps=_BISECT_STEPS)
    return pl.pallas_call(
        kern, grid=(nq, batch),
        in_specs=[
            pl.BlockSpec(memory_space=pltpu.SMEM),
            pl.BlockSpec((ATT_HEADS, tq, HEAD), lambda t, b: (0, b * nq + t, 0)),
            pl.BlockSpec((tq, wi), row_map(0)),
            pl.BlockSpec((tq, LANES), row_map(ZA_KIW // LANES)),
            pl.BlockSpec((tq, D_MODEL), row_map(ZA_GATT // D_MODEL)),
            pl.BlockSpec((seq_len, w + ATT_KV_HEADS * LANES), key_map),
            pl.BlockSpec((seq_len, HEAD), key_map),
        ],
        out_specs=pl.BlockSpec((tq, D_MODEL), row_map(0)),
        out_shape=jax.ShapeDtypeStruct((m, D_MODEL), F32),
        scratch_shapes=[pltpu.VMEM((tq, seq_len), F32), pltpu.VMEM((ATT_HEADS, 2, tq, tq), F32),
                        pltpu.VMEM((ATT_GROUP * tq, seq_len), F32), pltpu.VMEM((ATT_GROUP * tq, LANES), F32),
                        pltpu.VMEM((ATT_GROUP * tq, LANES), F32)],
        compiler_params=_cparams(2), name="attn_prompt")(rel_bias, qb, qib, za, za, kvb, kib)


def _sample_score_kernel(pt_ref, q_ref, w_ref, kiw_new_ref, *rest, n_pages, page, dec_seq):
    page_refs = rest[:n_pages]
    o_ref = rest[n_pages]
    past = n_pages * page
    kidx = jnp.concatenate([r[...] for r in page_refs], axis=0).astype(BF16)
    new = kiw_new_ref[:, 0:HEAD]
    new = jnp.concatenate([new, jnp.zeros((LANES - dec_seq, HEAD), F32)], axis=0).astype(BF16)
    q = (q_ref[...] * (HEAD ** -0.5)).astype(BF16)
    lg = jnp.concatenate([_dot_nt(q, kidx), _dot_nt(q, new)], axis=1)
    wr = jnp.maximum(lg, 0.0) * (w_ref[...] * (IDX_HEADS ** -0.5))
    n_keys = past + LANES
    sc = jnp.sum(wr.reshape(dec_seq, IDX_HEADS, n_keys), axis=1)
    kpos = lax.broadcasted_iota(jnp.int32, (dec_seq, n_keys), 1)
    tpos = lax.broadcasted_iota(jnp.int32, (dec_seq, n_keys), 0)
    o_ref[...] = jnp.where(kpos <= past + tpos, sc, _NEG_INF)


def _sample_select_kernel(s_ref, o_ref, *, k_sel, bisect_steps):
    score = s_ref[...]
    kpos = lax.broadcasted_iota(jnp.int32, score.shape, 1)
    sel = _select_topk(score, kpos, k_sel, bisect_steps)
    o_ref[...] = jnp.where(sel, 0.0, _NEG_INF)


def _sample_attn_kernel(pt_ref, q_ref, g_ref, kn_new_ref, kv_new_ref, mask_ref, rb_rows_ref, qg_ref, *rest,
                        n_pages, page, dec_seq):
    k_pages = rest[:n_pages]
    v_pages = rest[n_pages:2 * n_pages]
    o_ref = rest[2 * n_pages]
    bias_ref = rest[2 * n_pages + 1]
    w = ATT_KV_HEADS * HEAD
    rows = dec_seq * ATT_HEADS
    n_keys = n_pages * page + LANES

    @pl.when(pl.program_id(0) == 0)
    def _():
        t_row = lax.div(lax.broadcasted_iota(jnp.int32, (rows, n_keys), 0), ATT_HEADS)
        dist = n_pages * page + t_row - lax.broadcasted_iota(jnp.int32, (rows, n_keys), 1)
        bias_ref[...] = _rel_bias_lookup(dist, lambda b: rb_rows_ref[:, b:b + 1])

    pad = jnp.zeros((LANES - dec_seq, w), F32)
    k_all = jnp.concatenate([r[...] for r in k_pages] + [kn_new_ref[...], pad], axis=0).astype(BF16)
    v_all = jnp.concatenate([r[...] for r in v_pages] + [kv_new_ref[:, w:2 * w], pad], axis=0).astype(BF16)
    q = q_ref[...]
    q = q * lax.rsqrt(jnp.mean(q * q, axis=-1, keepdims=True) + NORM_EPS) * qg_ref[...]
    qb = q.astype(BF16)
    head = lax.rem(lax.broadcasted_iota(jnp.int32, (rows, 1), 0), ATT_HEADS)
    first = head < ATT_GROUP
    lg = jnp.where(first, _dot_nt(qb, k_all[:, 0:HEAD]), _dot_nt(qb, k_all[:, HEAD:2 * HEAD]))
    mask = jnp.broadcast_to(mask_ref[...][:, None, :], (dec_seq, ATT_HEADS, n_keys)).reshape(rows, n_keys)
    s = lg * (HEAD ** -0.5) + bias_ref[...] + mask
    p = jnp.exp(s - jnp.max(s, axis=-1, keepdims=True))
    l = jnp.sum(p, axis=-1, keepdims=True)
    pb = p.astype(BF16)
    o = jnp.where(first, jnp.dot(pb, v_all[:, 0:HEAD], preferred_element_type=F32),
                  jnp.dot(pb, v_all[:, HEAD:2 * HEAD], preferred_element_type=F32)) / l
    g = g_ref[...]
    o_ref[...] = o * (g * _sigmoid(g))


def _attn_sample(za, kn, qb, qib, kvb, kib, *, cache_k, cache_v, cache_kidx, page_table, rel_bias, qn_g,
                 dec_seq, k_sel):
    nb, n_pages = page_table.shape
    n_phys, page = cache_k.shape[0], cache_k.shape[1]
    past = n_pages * page
    n_keys = past + LANES
    w = ATT_KV_HEADS * HEAD
    ck = cache_k.reshape(n_phys, page, w)
    cv = cache_v.reshape(n_phys, page, w)
    ci = cache_kidx.reshape(n_phys, page, HEAD)
    za3 = za.reshape(nb, dec_seq, ZA_COLS)
    kn3 = kn.reshape(nb, dec_seq, w)
    qi = za[:, ZA_QI:ZA_QI + IDX_HEADS * HEAD].reshape(nb, dec_seq * IDX_HEADS, HEAD)
    wi = za[:, ZA_KIW + HEAD:ZA_KIW + HEAD + IDX_HEADS].reshape(nb, dec_seq * IDX_HEADS, 1)
    qa = za[:, ZA_Q:ZA_Q + D_MODEL].reshape(nb, dec_seq * ATT_HEADS, HEAD)
    ga = za[:, ZA_GATT:ZA_GATT + D_MODEL].reshape(nb, dec_seq * ATT_HEADS, HEAD)

    def page_specs(width):
        return [pl.BlockSpec((None, page, width), lambda b, pt, j=j: (pt[b, j], 0, 0)) for j in range(n_pages)]

    per_b = lambda r, c: pl.BlockSpec((None, r, c), lambda b, pt: (b, 0, 0))
    kiw_new = pl.BlockSpec((None, dec_seq, LANES), lambda b, pt: (b, 0, ZA_KIW // LANES))

    scores = pl.pallas_call(
        functools.partial(_sample_score_kernel, n_pages=n_pages, page=page, dec_seq=dec_seq),
        grid_spec=pltpu.PrefetchScalarGridSpec(
            num_scalar_prefetch=1, grid=(nb,),
            in_specs=[per_b(dec_seq * IDX_HEADS, HEAD), per_b(dec_seq * IDX_HEADS, 1), kiw_new] + page_specs(HEAD),
            out_specs=per_b(dec_seq, n_keys)),
        out_shape=jax.ShapeDtypeStruct((nb, dec_seq, n_keys), F32),
        compiler_params=_cparams(1), name="sample_scores")(page_table, qi, wi, za3, *([ci] * n_pages))

    rows = nb * dec_seq
    tr = min(rows, 128)
    mask = pl.pallas_call(
        functools.partial(_sample_select_kernel, k_sel=k_sel, bisect_steps=_BISECT_STEPS),
        grid=(rows // tr,),
        in_specs=[pl.BlockSpec((tr, n_keys), lambda i: (i, 0))],
        out_specs=pl.BlockSpec((tr, n_keys), lambda i: (i, 0)),
        out_shape=jax.ShapeDtypeStruct((rows, n_keys), F32),
        compiler_params=_cparams(1), name="sample_select")(scores.reshape(rows, n_keys))

    rb_rows = jnp.tile(rel_bias.T, (dec_seq, 1))

    const = lambda shape: pl.BlockSpec(shape, lambda b, pt: (0,) * len(shape))
    kv_new = pl.BlockSpec((None, dec_seq, 2 * w), lambda b, pt: (b, 0, ZA_KV // (2 * w)))
    out = pl.pallas_call(
        functools.partial(_sample_attn_kernel, n_pages=n_pages, page=page, dec_seq=dec_seq),
        grid_spec=pltpu.PrefetchScalarGridSpec(
            num_scalar_prefetch=1, grid=(nb,),
            in_specs=[per_b(dec_seq * ATT_HEADS, HEAD), per_b(dec_seq * ATT_HEADS, HEAD), per_b(dec_seq, w),
                      kv_new, per_b(dec_seq, n_keys), const((dec_seq * ATT_HEADS, REL_BUCKETS)), const((1, HEAD))]
            + page_specs(w) + page_specs(w),
            out_specs=per_b(dec_seq * ATT_HEADS, HEAD),
            scratch_shapes=[pltpu.VMEM((dec_seq * ATT_HEADS, n_keys), F32)]),
        out_shape=jax.ShapeDtypeStruct((nb, dec_seq * ATT_HEADS, HEAD), F32),
        compiler_params=_cparams(1), name="sample_attn")(
            page_table, qa, ga, kn3, za3, mask.reshape(nb, dec_seq, n_keys), rb_rows, qn_g.reshape(1, HEAD),
            *([ck] * n_pages), *([cv] * n_pages))
    return out.reshape(rows, D_MODEL)


def _merge_kernel(x_ref, oa_ref, sg_ref, ob_ref, ga_ref, gb_ref, wpa_ref, wpb_ref, wo_ref, y_ref):
    oa = (oa_ref[...] * sg_ref[...]).astype(BF16)
    pa = jnp.dot(oa, wpa_ref[...], preferred_element_type=F32)
    pb = jnp.dot(ob_ref[...].astype(BF16), wpb_ref[...], preferred_element_type=F32)
    merged = _sigmoid(ga_ref[...]) * pa + _sigmoid(gb_ref[...]) * pb
    y_ref[...] = x_ref[...] + jnp.dot(merged.astype(BF16), wo_ref[...], preferred_element_type=F32)


def _merge(x, oa, sg, ob, za, w_pa, w_pb, w_out, tm):
    m, d = x.shape
    row = pl.BlockSpec((tm, d), lambda i: (i, 0))
    wsp = pl.BlockSpec((d, d), lambda i: (0, 0))
    return pl.pallas_call(
        _merge_kernel, grid=(m // tm,),
        in_specs=[row, row, row, row,
                  pl.BlockSpec((tm, d), lambda i: (i, ZA_GA // d)), pl.BlockSpec((tm, d), lambda i: (i, ZA_GB // d)),
                  wsp, wsp, wsp],
        out_specs=row,
        out_shape=jax.ShapeDtypeStruct((m, d), F32),
        compiler_params=_cparams(1), name="merge")(x, oa, sg, ob, za, za, w_pa, w_pb, w_out)


def _layer(x, shift, s0, params, *, batch, seq_len, attend):
    (norm_g, w_r, w_a, mu, w0, w2, a0, a2, k_k, k_a, r_k, lnx_g, lnx_b, qn_g, kn_g, w_pa, w_pb, w_out) = params
    m = batch * seq_len
    tm = min(m, 512)
    xn = _rmsnorm(x, norm_g, tm)
    tm_proj = min(m, 2048)
    zr = _matmul(xn, w_r, tm_proj, RWKV_COLS // 3, "inproj_rwkv")
    za = _matmul(xn, w_a, tm_proj, ZA_COLS // 4, "inproj_attn")

    tile = min(m, 256)
    r, w, k, v, a, sg = _rwkv_prep(zr, shift, mu, w0, w2, a0, a2, batch=batch, seq_len=seq_len, tile=tile)
    relayout_tile = 128
    in_kernel_relayout = seq_len % relayout_tile == 0 and batch * RWKV_HEADS == LANES
    if in_kernel_relayout:
        seqs = [_to_pairs(t, batch, seq_len, relayout_tile) for t in (r, w, k, v, a)]
    else:
        seqs = [_to_pairs_xla(t, batch, seq_len) for t in (r, w, k, v, a)]
    row_consts = [_head_const_pairs(t, batch) for t in (lnx_g, lnx_b)]
    step_consts = [jnp.broadcast_to(_head_const_pairs(t, batch)[:, None, :], (HEAD, SUBLANES, batch * RWKV_HEADS))
                   for t in (r_k.reshape(-1), k_k, k_a)]
    o_t, s_t = _rwkv_scan(seqs, s0, row_consts, step_consts, seq_len=seq_len, steps=min(seq_len, 32))
    if in_kernel_relayout:
        oa = _from_pairs(o_t, batch, seq_len, relayout_tile)
    else:
        oa = _from_pairs_xla(o_t, batch, seq_len)

    kn, qb, qib, kvb, kib = _attn_prep(za, qn_g, kn_g, tm)
    ob = attend(za, kn, qb, qib, kvb, kib)

    y = _merge(x, oa, sg, ob, za, w_pa, w_pb, w_out, min(m, 256))
    return y, zr, za, kn, s_t


def _state_to_pairs(s):
    b, h, n, _ = s.shape
    return jnp.transpose(s, (3, 2, 0, 1)).reshape(n, n, b * h)


def _state_from_pairs(s, batch):
    n = s.shape[0]
    return jnp.transpose(s.reshape(n, n, batch, RWKV_HEADS), (2, 3, 1, 0))


def kernel(x_prompt, x_sample, cache_k, cache_v, cache_kidx, state_wkv, state_shift, page_table, norm_g, w_in,
           shift_mu, w0, w2, a0, a2, k_k, k_a, r_k, lnx_g, lnx_b, q_norm_g, k_norm_g, rel_bias, w_pa, w_pb, w_out):
    bsz, seq, d = x_prompt.shape
    dec_bsz, dec_seq, _ = x_sample.shape
    depth = w_in.shape[0]
    assert depth == 1 and d == D_MODEL
    past_len = page_table.shape[1] * cache_k.shape[2]
    topk_p = min(TOPK_MAX, seq // 4)
    topk_s = min(TOPK_MAX, (past_len + dec_seq) // 4)
    l = 0

    wl = w_in[l]
    c0 = RWKV_COLS
    q_w, kv_w, qi_w = wl[:, c0:c0 + 1024], wl[:, c0 + 1024:c0 + 1280], wl[:, c0 + 1280:c0 + 1792]
    kiw_w = wl[:, c0 + 1792:c0 + 1864]
    rest_w = wl[:, c0 + 1864:]
    zpad = lambda n: jnp.zeros((d, n), wl.dtype)
    w_a = jnp.concatenate([q_w, qi_w, kv_w, kiw_w, zpad(LANES - kiw_w.shape[1]), zpad(LANES), rest_w],
                          axis=1).astype(BF16)
    assert w_a.shape[1] == ZA_COLS
    def rwkv_cols(x, fn):
        lead = x.shape[:-1]
        main = fn(x[..., :4 * d].reshape(lead + (4, d)), len(lead) + 1).reshape(lead + (4 * d,))
        return jnp.concatenate([main, x[..., 4 * d:]], axis=-1)

    w_r = rwkv_cols(wl[:, :c0], _nh_order).astype(BF16)
    params = (norm_g[l], w_r, w_a, rwkv_cols(shift_mu[l], _nh_order), _nh_order(w0[l], 0), _nh_order(w2[l], 1),
              _nh_order(a0[l], 0), _nh_order(a2[l], 1), k_k[l], k_a[l], r_k[l],
              lnx_g[l], lnx_b[l], q_norm_g[l], k_norm_g[l],
              _nh_order(w_pa[l], 0).astype(BF16), w_pb[l].astype(BF16), w_out[l].astype(BF16))

    xp = x_prompt.reshape(bsz * seq, d)
    attend_p = functools.partial(_attn_prompt, rel_bias=rel_bias, batch=bsz, seq_len=seq,
                                 tq=min(seq, 256), k_sel=topk_p)
    yp, zr_p, za_p, kn_p, st_p = _layer(
        xp, jnp.zeros((bsz, 1, RWKV_COLS), F32), jnp.zeros((HEAD, HEAD, bsz * RWKV_HEADS), F32), params,
        batch=bsz, seq_len=seq, attend=attend_p)

    xs = x_sample.reshape(dec_bsz * dec_seq, d)
    attend_s = functools.partial(_attn_sample, cache_k=cache_k[l], cache_v=cache_v[l], cache_kidx=cache_kidx[l],
                                 page_table=page_table, rel_bias=rel_bias, qn_g=q_norm_g[l], dec_seq=dec_seq,
                                 k_sel=topk_s)
    shift_rows = jnp.repeat(rwkv_cols(state_shift[l], _nh_order), dec_seq, axis=0)
    ys, zr_s, za_s, kn_s, st_s = _layer(
        xs, shift_rows, _state_to_pairs(state_wkv[l]), params, batch=dec_bsz, seq_len=dec_seq, attend=attend_s)

    w = ATT_KV_HEADS * HEAD

    def pack(y, zr, za, kn, st, b, t):
        v = za[:, ZA_KV + w:ZA_KV + 2 * w]
        kidx = za[:, ZA_KIW:ZA_KIW + HEAD]
        return (y.reshape(b, t, d),
                kn.reshape(1, b, t, ATT_KV_HEADS, HEAD), v.reshape(1, b, t, ATT_KV_HEADS, HEAD),
                kidx.reshape(1, b, t, HEAD), _state_from_pairs(st, b)[None],
                rwkv_cols(zr.reshape(b, t, RWKV_COLS)[:, -1], _hn_order)[None])

    p = pack(yp, zr_p, za_p, kn_p, st_p, bsz, seq)
    s = pack(ys, zr_s, za_s, kn_s, st_s, dec_bsz, dec_seq)
    return (p[0], s[0]) + p[1:] + s[1:]
```

```python
import functools
import math

import numpy as np
import jax
import jax.numpy as jnp
from jax import lax
from jax.experimental import pallas as pl
from jax.experimental.pallas import tpu as pltpu

F32 = jnp.float32
BF16 = jnp.bfloat16

D_MODEL = 1024
HEAD = 64
RWKV_HEADS = D_MODEL // HEAD
LORA = 64
LNX_EPS = 64e-5
ATT_HEADS = D_MODEL // HEAD
ATT_KV_HEADS = 2
ATT_GROUP = ATT_HEADS // ATT_KV_HEADS
IDX_HEADS = 8
TOPK_MAX = 256
REL_BUCKETS = 32
REL_MAX_DIST = 128
NORM_EPS = 1e-6
RWKV_COLS = 4 * D_MODEL + 2 * LORA

LANES = 128
SUBLANES = 8
VMEM_LIMIT_BYTES = 56 * 1024 * 1024

ZA_Q = 0
ZA_QI = 1024
ZA_KV = 1536
ZA_KIW = 1792
ZA_GATT = 2048
ZA_GA = 3072
ZA_GB = 4096
ZA_COLS = 5120

_NEG_INF = float("-inf")
_POS_INF = float("inf")
_BISECT_STEPS = 14


def _cparams(n_axes):
    return pltpu.CompilerParams(dimension_semantics=("arbitrary",) * n_axes,
                                vmem_limit_bytes=VMEM_LIMIT_BYTES)


def _sigmoid(x):
    return 1.0 / (1.0 + jnp.exp(-x))


def _dot_nt(a, b):
    return lax.dot_general(a, b, (((1,), (1,)), ((), ())), preferred_element_type=F32)


def _rmsnorm_kernel(x_ref, g_ref, o_ref):
    x = x_ref[...]
    ms = jnp.mean(x * x, axis=-1, keepdims=True)
    o_ref[...] = (x * lax.rsqrt(ms + NORM_EPS) * g_ref[...]).astype(o_ref.dtype)


def _rmsnorm(x, g, tm):
    m, d = x.shape
    return pl.pallas_call(
        _rmsnorm_kernel, grid=(m // tm,),
        in_specs=[pl.BlockSpec((tm, d), lambda i: (i, 0)), pl.BlockSpec((1, d), lambda i: (0, 0))],
        out_specs=pl.BlockSpec((tm, d), lambda i: (i, 0)),
        out_shape=jax.ShapeDtypeStruct((m, d), BF16),
        compiler_params=_cparams(1), name="rmsnorm")(x, g.reshape(1, d))


def _mm_kernel(x_ref, w_ref, o_ref):
    o_ref[...] = jnp.dot(x_ref[...], w_ref[...], preferred_element_type=F32)


def _matmul(x, w, tm, tn, name):
    m, k = x.shape
    n = w.shape[1]
    return pl.pallas_call(
        _mm_kernel, grid=(m // tm, n // tn),
        in_specs=[pl.BlockSpec((tm, k), lambda i, j: (i, 0)), pl.BlockSpec((k, tn), lambda i, j: (0, j))],
        out_specs=pl.BlockSpec((tm, tn), lambda i, j: (i, j)),
        out_shape=jax.ShapeDtypeStruct((m, n), F32),
        compiler_params=_cparams(2), name=name)(x, w)


def _rwkv_prep_kernel(z_ref, prev_ref, shift_ref, mu_ref, w0_ref, w2_ref, a0_ref, a2_ref,
                      r_o, w_o, k_o, v_o, a_o, g_o, *, seq_len, tile):
    z = z_ref[...]
    rows = lax.broadcasted_iota(jnp.int32, z.shape, 0)
    rolled = pltpu.roll(z, 1, 0)
    if seq_len >= tile:
        first = jnp.where(pl.program_id(1) == 0, shift_ref[...], prev_ref[SUBLANES - 1:SUBLANES, :])
        prev = jnp.where(rows == 0, first, rolled)
    else:
        prev = jnp.where(lax.rem(rows, seq_len) == 0, shift_ref[...], rolled)
    zs = z + (prev - z) * mu_ref[...]
    d = D_MODEL
    r = zs[:, 0:d]
    k = zs[:, d:2 * d]
    v = zs[:, 2 * d:3 * d]
    g = zs[:, 3 * d:4 * d]
    wd = zs[:, 4 * d:4 * d + LORA]
    ad = zs[:, 4 * d + LORA:4 * d + 2 * LORA]
    wl = w0_ref[...] + jnp.dot(jnp.tanh(wd).astype(BF16), w2_ref[...], preferred_element_type=F32)
    decay = jnp.exp(-_sigmoid(wl) * math.exp(-0.5))
    a = _sigmoid(a0_ref[...] + jnp.dot(ad.astype(BF16), a2_ref[...], preferred_element_type=F32))
    r_o[...] = r
    w_o[...] = decay
    k_o[...] = k
    v_o[...] = v
    a_o[...] = a
    g_o[...] = g * _sigmoid(g)


def _rwkv_prep(zr, shift, mu, w0, w2, a0, a2, *, batch, seq_len, tile):
    m, c = zr.shape
    d = D_MODEL
    row = lambda x: x.reshape(1, -1)
    consts = [row(mu), row(w0), w2.astype(BF16), row(a0), a2.astype(BF16)]
    const_specs = [pl.BlockSpec(x.shape, lambda *_: (0, 0)) for x in consts]
    if seq_len >= tile:
        nt = seq_len // tile
        grid = (batch, nt)
        zmap = lambda b, t: (b * nt + t, 0)
        pmap = lambda b, t: (jnp.maximum((b * seq_len + t * tile) // SUBLANES - 1, 0), 0)
        shift_spec = pl.BlockSpec((None, 1, c), lambda b, t: (b, 0, 0))
    else:
        grid = (1, m // tile)
        zmap = lambda b, t: (t, 0)
        pmap = lambda b, t: (0, 0)
        shift_spec = pl.BlockSpec((tile, c), zmap)
    out_spec = pl.BlockSpec((tile, d), zmap)
    kern = functools.partial(_rwkv_prep_kernel, seq_len=seq_len, tile=tile)
    return pl.pallas_call(
        kern, grid=grid,
        in_specs=[pl.BlockSpec((tile, c), zmap), pl.BlockSpec((SUBLANES, c), pmap), shift_spec] + const_specs,
        out_specs=[out_spec] * 6,
        out_shape=[jax.ShapeDtypeStruct((m, d), F32)] * 6,
        compiler_params=_cparams(2), name="rwkv_prep")(zr, zr, shift, *consts)


def _rwkv_scan_kernel(r_in, w_in, k_in, v_in, a_in, s0_ref, lg_ref, lb_ref, rk_ref, kkc_ref, kac_ref,
                      o_ref, s_ref, vec_ref, bf_ref, ge_ref, *, groups, live):
    nb = HEAD // SUBLANES

    @pl.when(pl.program_id(1) == 0)
    def _():
        s_ref[...] = s0_ref[...]

    shape4 = (groups, HEAD, SUBLANES, LANES)
    a = a_in[...].reshape(shape4)
    kraw = k_in[...].reshape(shape4)
    r_all = r_in[...].reshape(shape4)
    kkraw = kraw * kkc_ref[...][None]
    n2 = jnp.sum(kkraw * kkraw, axis=1, keepdims=True)
    kk = kkraw / jnp.maximum(jnp.sqrt(n2), 1e-12)
    kmod = kraw * (1.0 + (a - 1.0) * kac_ref[...][None])
    bf_ref[...] = jnp.sum(r_all * kmod * rk_ref[...][None], axis=1)
    row = lax.broadcasted_iota(jnp.int32, (HEAD, SUBLANES, LANES), 1)
    g_end = jnp.ones((HEAD, SUBLANES, LANES), F32)
    for q in range(groups):
        g = w_in[q].reshape(HEAD, SUBLANES, LANES)
        for sh in (1, 2, 4):
            g = g * jnp.where(row >= sh, pltpu.roll(g, sh, 1), 1.0)
        g = g * g_end
        g_prev = jnp.where(row >= 1, pltpu.roll(g, 1, 1), g_end)
        g_inv = 1.0 / g
        flat = (HEAD * SUBLANES, LANES)
        vec_ref[0, q] = (-kk[q] * g_prev).reshape(flat)
        vec_ref[1, q] = (kk[q] * a[q] * g_inv).reshape(flat)
        vec_ref[2, q] = (kmod[q] * g_inv).reshape(flat)
        vec_ref[3, q] = (r_all[q] * g).reshape(flat)
        last = live - 1 if q == groups - 1 else SUBLANES - 1
        g_end = jnp.broadcast_to(g[:, last:last + 1, :], (HEAD, SUBLANES, LANES))
    ge_ref[...] = g_end

    def group(q, carry):
        for r in range(live):
            def row_of(idx, j):
                return jnp.broadcast_to(vec_ref[idx, q, pl.ds(j * SUBLANES + r, 1), :], (SUBLANES, LANES))

            sa = [jnp.zeros((SUBLANES, LANES), F32) for _ in range(nb)]
            for j in range(HEAD):
                nk = row_of(0, j)
                for ib in range(nb):
                    sa[ib] = sa[ib] + s_ref[j, ib * SUBLANES:(ib + 1) * SUBLANES, :] * nk
            vt = [v_in[q, pl.ds(ib * SUBLANES * SUBLANES + r, SUBLANES, stride=SUBLANES), :] for ib in range(nb)]
            out = [jnp.zeros((SUBLANES, LANES), F32) for _ in range(nb)]
            for j in range(HEAD):
                kaj = row_of(1, j)
                kj = row_of(2, j)
                rj = row_of(3, j)
                for ib in range(nb):
                    sl = slice(ib * SUBLANES, (ib + 1) * SUBLANES)
                    sn = s_ref[j, sl, :] + sa[ib] * kaj + vt[ib] * kj
                    s_ref[j, sl, :] = sn
                    out[ib] = out[ib] + sn * rj
            o = jnp.concatenate(out, axis=0)
            mean = jnp.mean(o, axis=0, keepdims=True)
            dev = o - mean
            var = jnp.mean(dev * dev, axis=0, keepdims=True)
            y = dev * lax.rsqrt(var + LNX_EPS) * lg_ref[...] + lb_ref[...]
            o_ref[q * live + r] = y + bf_ref[q, pl.ds(r, 1), :] * jnp.concatenate(vt, axis=0)
        return carry

    lax.fori_loop(0, groups, group, 0)

    for j in range(HEAD):
        s_ref[j] = s_ref[j] * jnp.concatenate([ge_ref[j]] * nb, axis=0)


def _rwkv_scan(seqs, s0, row_consts, step_consts, *, seq_len, steps):
    t8, _, p = seqs[0].shape
    n = HEAD
    if seq_len >= SUBLANES:
        groups, live = steps // SUBLANES, SUBLANES
    else:
        groups, live = 1, seq_len
    grid = (p // LANES, t8 // groups)
    in_spec = pl.BlockSpec((groups, n * SUBLANES, LANES), lambda g, t: (t, 0, g))
    out_spec = pl.BlockSpec((groups * live, n, LANES), lambda g, t: (t, 0, g))
    st_spec = pl.BlockSpec((n, n, LANES), lambda g, t: (0, 0, g))
    rc_spec = pl.BlockSpec((n, LANES), lambda g, t: (0, g))
    sc_spec = pl.BlockSpec((n, SUBLANES, LANES), lambda g, t: (0, 0, g))
    kern = functools.partial(_rwkv_scan_kernel, groups=groups, live=live)
    return pl.pallas_call(
        kern, grid=grid,
        in_specs=[in_spec] * 5 + [st_spec] + [rc_spec] * 2 + [sc_spec] * 3,
        out_specs=[out_spec, st_spec],
        out_shape=[jax.ShapeDtypeStruct((seq_len, n, p), F32), jax.ShapeDtypeStruct((n, n, p), F32)],
        scratch_shapes=[pltpu.VMEM((4, groups, n * SUBLANES, LANES), F32), pltpu.VMEM((groups, SUBLANES, LANES), F32),
                        pltpu.VMEM((n, SUBLANES, LANES), F32)],
        compiler_params=_cparams(2), name="rwkv_scan")(*seqs, s0, *row_consts, *step_consts)


def _nh_order(x, axis):
    shape = x.shape
    x = x.reshape(shape[:axis] + (RWKV_HEADS, HEAD) + shape[axis + 1:])
    return jnp.swapaxes(x, axis, axis + 1).reshape(shape)


def _hn_order(x, axis):
    shape = x.shape
    x = x.reshape(shape[:axis] + (HEAD, RWKV_HEADS) + shape[axis + 1:])
    return jnp.swapaxes(x, axis, axis + 1).reshape(shape)


def _to_pairs_kernel(x_ref, o_ref, a_scr):
    nb = x_ref.shape[0]
    for b in range(nb):
        a_scr[b] = x_ref[b].T
    for n in range(HEAD):
        blk = a_scr[:, n * RWKV_HEADS:(n + 1) * RWKV_HEADS, :]
        y = blk.reshape(nb * RWKV_HEADS, blk.shape[2]).T
        o_ref[:, n * SUBLANES:(n + 1) * SUBLANES, :] = y.reshape(y.shape[0] // SUBLANES, SUBLANES, LANES)


def _to_pairs(x, batch, seq_len, tt):
    assert batch * RWKV_HEADS == LANES
    x3 = x.reshape(batch, seq_len, D_MODEL)
    return pl.pallas_call(
        _to_pairs_kernel, grid=(seq_len // tt,),
        in_specs=[pl.BlockSpec((batch, tt, D_MODEL), lambda t: (0, t, 0))],
        out_specs=pl.BlockSpec((tt // SUBLANES, HEAD * SUBLANES, LANES), lambda t: (t, 0, 0)),
        out_shape=jax.ShapeDtypeStruct((seq_len // SUBLANES, HEAD * SUBLANES, LANES), F32),
        scratch_shapes=[pltpu.VMEM((batch, D_MODEL, tt), F32)],
        compiler_params=_cparams(1), name="to_pairs")(x3)


def _from_pairs_kernel(o_ref, x_ref, a_scr):
    nb = x_ref.shape[0]
    for n in range(HEAD):
        blk = o_ref[:, n, :].T
        a_scr[:, n * RWKV_HEADS:(n + 1) * RWKV_HEADS, :] = blk.reshape(nb, RWKV_HEADS, blk.shape[1])
    for b in range(nb):
        x_ref[b] = a_scr[b].T


def _from_pairs(o, batch, seq_len, tt):
    assert batch * RWKV_HEADS == LANES
    out = pl.pallas_call(
        _from_pairs_kernel, grid=(seq_len // tt,),
        in_specs=[pl.BlockSpec((tt, HEAD, LANES), lambda t: (t, 0, 0))],
        out_specs=pl.BlockSpec((batch, tt, D_MODEL), lambda t: (0, t, 0)),
        out_shape=jax.ShapeDtypeStruct((batch, seq_len, D_MODEL), F32),
        scratch_shapes=[pltpu.VMEM((batch, D_MODEL, tt), F32)],
        compiler_params=_cparams(1), name="from_pairs")(o)
    return out.reshape(batch * seq_len, D_MODEL)


def _to_pairs_xla(x, batch, seq_len):
    assert seq_len < SUBLANES
    x = jnp.transpose(x.reshape(batch, seq_len, HEAD, RWKV_HEADS), (2, 1, 0, 3))
    x = x.reshape(HEAD, seq_len, batch * RWKV_HEADS)
    x = jnp.pad(x, ((0, 0), (0, SUBLANES - seq_len), (0, 0)), constant_values=1.0)
    return x.reshape(1, HEAD * SUBLANES, batch * RWKV_HEADS)


def _from_pairs_xla(o, batch, seq_len):
    o = o.reshape(seq_len, HEAD, batch, RWKV_HEADS)
    return jnp.transpose(o, (2, 0, 1, 3)).reshape(batch * seq_len, D_MODEL)


def _head_const_pairs(x, batch):
    x = x.reshape(RWKV_HEADS, HEAD).T
    return jnp.tile(x, (1, batch))


def _count_ge(score, thr):
    return jnp.sum(jnp.where(score >= thr, 1.0, 0.0), axis=-1, keepdims=True)


def _select_topk(score, kpos, k_sel, bisect_steps):
    kf = float(k_sel)
    vis = score > _NEG_INF
    nvis = jnp.sum(jnp.where(vis, 1.0, 0.0), axis=-1, keepdims=True)
    need_sel = nvis > kf
    rowmax = jnp.where(need_sel, jnp.max(score, axis=-1, keepdims=True), 0.0)
    rowmin = jnp.where(need_sel, jnp.min(jnp.where(vis, score, _POS_INF), axis=-1, keepdims=True), 0.0)

    def bis(_, c):
        lo, hi, chi = c
        hfin = jnp.where(hi == _POS_INF, rowmax, hi)
        piv = 0.5 * lo + 0.5 * hfin
        cnt = _count_ge(score, piv)
        ge = cnt >= kf
        return jnp.where(ge, piv, lo), jnp.where(ge, hi, piv), jnp.where(ge, chi, cnt)

    lo, hi, chi = lax.fori_loop(0, bisect_steps, bis,
                                (rowmin, jnp.full_like(rowmin, _POS_INF), jnp.zeros_like(rowmin)))

    def walk_cond(c):
        return jnp.min(c[4]) < 0.5

    def walk(c):
        hi, chi, tau, cgt, done, ceq = c
        pending = done < 0.5
        bmax = jnp.max(jnp.where(score < hi, score, _NEG_INF), axis=-1, keepdims=True)
        cnt = _count_ge(score, bmax)
        fin = jnp.logical_and(cnt >= kf, pending)
        tau = jnp.where(fin, bmax, tau)
        cgt = jnp.where(fin, chi, cgt)
        ceq = jnp.where(fin, cnt - chi, ceq)
        adv = jnp.logical_and(cnt < kf, pending)
        hi = jnp.where(adv, bmax, hi)
        chi = jnp.where(adv, cnt, chi)
        return hi, chi, tau, cgt, jnp.where(fin, 1.0, done), ceq

    done0 = jnp.where(need_sel, 0.0, 1.0)
    neg = jnp.full_like(rowmin, _NEG_INF)
    zero = jnp.zeros_like(rowmin)
    _, _, tau, cgt, _, ceq = lax.while_loop(walk_cond, walk, (hi, chi, neg, zero, done0, zero))

    need = kf - cgt
    excess = jnp.logical_and(need_sel, ceq > need)
    eq = score == tau
    n_keys = score.shape[-1]

    def tie_break(_):
        def body(_, c):
            plo, phi = c
            mid = lax.shift_right_arithmetic(plo + phi, 1)
            cnt = jnp.sum(jnp.where(jnp.logical_and(eq, kpos <= mid), 1.0, 0.0), axis=-1, keepdims=True)
            ge = cnt >= need
            return jnp.where(ge, plo, mid), jnp.where(ge, mid, phi)
        plo0 = jnp.full(tau.shape, -1, jnp.int32)
        phi0 = jnp.full(tau.shape, n_keys - 1, jnp.int32)
        _, phi = lax.fori_loop(0, n_keys.bit_length() + 1, body, (plo0, phi0))
        return jnp.where(excess, phi, n_keys)

    any_excess = jnp.max(jnp.where(excess, 1.0, 0.0)) > 0.0
    pcut = lax.cond(any_excess, tie_break, lambda _: jnp.full(tau.shape, n_keys, jnp.int32), 0)
    pcut = jnp.where(need_sel, pcut, -1)
    return jnp.logical_or(score > tau, jnp.logical_and(eq, kpos <= pcut))


_MASKED = -1e30


def _select_topk_chunked(s_ref, nch, ck, k_sel, nvis, bisect_steps):
    rows = s_ref.shape[0]
    kf = float(k_sel)
    nfold = ck // LANES

    def chunk(c):
        return s_ref[:, pl.ds(pl.multiple_of(c * ck, ck), ck)]

    def fold(x, op):
        out = x[:, 0:LANES]
        for i in range(1, nfold):
            out = op(out, x[:, i * LANES:(i + 1) * LANES])
        return out

    def count(pred):
        def body(c, acc):
            return acc + fold(jnp.where(pred(chunk(c), c), 1.0, 0.0), jnp.add)
        acc = lax.fori_loop(0, nch, body, jnp.zeros((rows, LANES), F32))
        return jnp.sum(acc, axis=-1, keepdims=True)

    def row_max(val):
        def body(c, acc):
            return jnp.maximum(acc, fold(val(chunk(c)), jnp.maximum))
        acc = lax.fori_loop(0, nch, body, jnp.full((rows, LANES), _NEG_INF, F32))
        return jnp.max(acc, axis=-1, keepdims=True)

    need_sel = nvis > kf
    rowmax = jnp.where(need_sel, row_max(lambda x: x), 0.0)
    rowmin = jnp.where(need_sel, -row_max(lambda x: jnp.where(x > _NEG_INF, -x, _NEG_INF)), 0.0)

    def bis(_, c):
        lo, hi, chi = c
        hfin = jnp.where(hi == _POS_INF, rowmax, hi)
        piv = 0.5 * lo + 0.5 * hfin
        cnt = count(lambda x, _c: x >= piv)
        ge = cnt >= kf
        return jnp.where(ge, piv, lo), jnp.where(ge, hi, piv), jnp.where(ge, chi, cnt)

    lo, hi, chi = lax.fori_loop(0, bisect_steps, bis,
                                (rowmin, jnp.full_like(rowmin, _POS_INF), jnp.zeros_like(rowmin)))

    def walk_cond(c):
        return jnp.min(c[4]) < 0.5

    def walk(c):
        hi, chi, tau, cgt, done = c
        pending = done < 0.5
        bmax = row_max(lambda x: jnp.where(x < hi, x, _NEG_INF))
        cnt = count(lambda x, _c: x >= bmax)
        fin = jnp.logical_and(cnt >= kf, pending)
        tau = jnp.where(fin, bmax, tau)
        cgt = jnp.where(fin, chi, cgt)
        adv = jnp.logical_and(cnt < kf, pending)
        hi = jnp.where(adv, bmax, hi)
        chi = jnp.where(adv, cnt, chi)
        return hi, chi, tau, cgt, jnp.where(fin, 1.0, done)

    neg = jnp.full_like(rowmin, _NEG_INF)
    zero = jnp.zeros_like(rowmin)
    _, _, tau, cgt, _ = lax.while_loop(walk_cond, walk, (hi, chi, neg, zero, jnp.where(need_sel, 0.0, 1.0)))

    need = jnp.where(need_sel, kf - cgt, 0.0)
    tri = jnp.where(lax.broadcasted_iota(jnp.int32, (ck, ck), 0) <= lax.broadcasted_iota(jnp.int32, (ck, ck), 1),
                    1.0, 0.0).astype(BF16)

    def write(c, ties_before):
        x = chunk(c)
        eq = x == tau
        rank = jnp.dot(jnp.where(eq, 1.0, 0.0).astype(BF16), tri, preferred_element_type=F32) + ties_before
        sel = jnp.logical_or(x > tau, jnp.logical_and(eq, rank <= need))
        s_ref[:, pl.ds(pl.multiple_of(c * ck, ck), ck)] = jnp.where(sel, 0.0, _MASKED)
        return rank[:, ck - 1:ck]

    lax.fori_loop(0, nch, write, zero)


def _attn_prep_kernel(q_ref, qi_ref, kv_ref, kiw_ref, qg_ref, kg_ref, kn_o, qb_o, qib_o, kvb_o, kib_o):
    w = ATT_KV_HEADS * HEAD
    for h in range(ATT_KV_HEADS):
        hs = slice(h * HEAD, (h + 1) * HEAD)
        x = kv_ref[:, hs]
        kn = x * lax.rsqrt(jnp.mean(x * x, axis=-1, keepdims=True) + NORM_EPS) * kg_ref[...]
        kn_o[:, hs] = kn
        kvb_o[:, hs] = kn.astype(BF16)
        v = kv_ref[:, w + h * HEAD:w + (h + 1) * HEAD].astype(BF16)
        kvb_o[:, w + h * LANES:w + (h + 1) * LANES] = jnp.concatenate([v, jnp.ones_like(v)], axis=1)
    for h in range(ATT_HEADS):
        x = q_ref[:, h * HEAD:(h + 1) * HEAD]
        qn = x * lax.rsqrt(jnp.mean(x * x, axis=-1, keepdims=True) + NORM_EPS) * (qg_ref[...] * HEAD ** -0.5)
        qb_o[h] = qn.astype(BF16)
    qib_o[...] = (qi_ref[...] * HEAD ** -0.5).astype(BF16)
    kib_o[...] = kiw_ref[:, 0:HEAD].astype(BF16)


def _attn_prep(za, qn_g, kn_g, tm):
    m = za.shape[0]
    w = ATT_KV_HEADS * HEAD
    wi = IDX_HEADS * HEAD
    row = lambda width, blk: pl.BlockSpec((tm, width), lambda i, blk=blk: (i, blk))
    gspec = pl.BlockSpec((1, HEAD), lambda i: (0, 0))
    return pl.pallas_call(
        _attn_prep_kernel, grid=(m // tm,),
        in_specs=[row(D_MODEL, ZA_Q // D_MODEL), row(wi, ZA_QI // wi), row(2 * w, ZA_KV // (2 * w)),
                  row(LANES, ZA_KIW // LANES), gspec, gspec],
        out_specs=[row(w, 0), pl.BlockSpec((ATT_HEADS, tm, HEAD), lambda i: (0, i, 0)), row(wi, 0),
                   row(w + ATT_KV_HEADS * LANES, 0), row(HEAD, 0)],
        out_shape=[jax.ShapeDtypeStruct((m, w), F32), jax.ShapeDtypeStruct((ATT_HEADS, m, HEAD), BF16),
                   jax.ShapeDtypeStruct((m, wi), BF16), jax.ShapeDtypeStruct((m, w + ATT_KV_HEADS * LANES), BF16),
                   jax.ShapeDtypeStruct((m, HEAD), BF16)],
        compiler_params=_cparams(1), name="attn_prep")(za, za, za, za, qn_g.reshape(1, HEAD), kn_g.reshape(1, HEAD))


def _bucket_edges():
    max_exact = REL_BUCKETS // 2
    d = np.arange(REL_MAX_DIST + 1)
    df = np.maximum(d, 1).astype(np.float32)
    large = max_exact + (np.log(df / max_exact) / math.log(REL_MAX_DIST / max_exact)
                         * (REL_BUCKETS - max_exact)).astype(np.int32)
    bucket = np.where(d < max_exact, d, np.minimum(large, REL_BUCKETS - 1))
    return [int(np.argmax(bucket >= b)) for b in range(REL_BUCKETS)]


_BUCKET_EDGES = _bucket_edges()


def _rel_bias_lookup(dist, value_of_bucket):
    bias = value_of_bucket(REL_BUCKETS - 1)
    for b in range(REL_BUCKETS - 2, -1, -1):
        bias = jnp.where(dist < _BUCKET_EDGES[b + 1], value_of_bucket(b), bias)
    return bias


def _attn_prompt_kernel(rb_ref, q_ref, qi_ref, kiwq_ref, g_ref, kvb_ref, kib_ref, o_ref, s_scr, tbd_ref,
                        l_scr, mx_scr, acc_scr, *, tq, k_sel, bisect_steps):
    qt = pl.program_id(0)
    q0 = qt * tq
    ck = tq
    rg_rows = LANES
    n_rg = tq // rg_rows
    w = ATT_KV_HEADS * HEAD

    @pl.when(jnp.logical_and(pl.program_id(1) == 0, qt == 0))
    def _():
        rr = lax.broadcasted_iota(jnp.int32, (tq, tq), 0) - lax.broadcasted_iota(jnp.int32, (tq, tq), 1)
        for h in range(ATT_HEADS):
            far = rb_ref[REL_BUCKETS - 1, h]
            tbd_ref[h, 0] = _rel_bias_lookup(rr, lambda b: rb_ref[b, h]) - far
            tbd_ref[h, 1] = _rel_bias_lookup(rr + tq, lambda b: rb_ref[b, h]) - far

    def chunk_start(c):
        return pl.multiple_of(c * ck, ck)

    wcol = kiwq_ref[:, HEAD:HEAD + IDX_HEADS] * (IDX_HEADS ** -0.5)
    qi_h = [[qi_ref[rg * rg_rows:(rg + 1) * rg_rows, h * HEAD:(h + 1) * HEAD] for h in range(IDX_HEADS)]
            for rg in range(n_rg)]
    w_h = [[wcol[rg * rg_rows:(rg + 1) * rg_rows, h:h + 1] for h in range(IDX_HEADS)] for rg in range(n_rg)]

    def score_chunk(c, carry):
        k0 = chunk_start(c)
        kc = kib_ref[pl.ds(k0, ck), :]
        kpos = k0 + lax.broadcasted_iota(jnp.int32, (rg_rows, ck), 1)
        for rg in range(n_rg):
            acc = jnp.zeros((rg_rows, ck), F32)
            for h in range(IDX_HEADS):
                acc = acc + jnp.maximum(_dot_nt(qi_h[rg][h], kc), 0.0) * w_h[rg][h]
            qpos = q0 + rg * rg_rows + lax.broadcasted_iota(jnp.int32, (rg_rows, ck), 0)
            s_scr[rg * rg_rows:(rg + 1) * rg_rows, pl.ds(k0, ck)] = jnp.where(kpos <= qpos, acc, _NEG_INF)
        return carry

    nch = qt + 1
    lax.fori_loop(0, nch, score_chunk, 0)

    nvis = (q0 + 1 + lax.broadcasted_iota(jnp.int32, (tq, 1), 0)).astype(F32)
    _select_topk_chunked(s_scr, nch, ck, k_sel, nvis, bisect_steps)

    g_rows = ATT_GROUP * tq
    n_far = jnp.maximum(qt - 1, 0)
    for kvh in range(ATT_KV_HEADS):
        gs = slice(kvh * ATT_GROUP, (kvh + 1) * ATT_GROUP)
        qg = q_ref[gs, :, :].reshape(g_rows, HEAD)
        mx_scr[...] = jnp.full(mx_scr.shape, _MASKED, F32)
        acc_scr[...] = jnp.zeros(acc_scr.shape, F32)

        def logits_chunk(c, carry, near, qg=qg, gs=gs, kvh=kvh):
            k0 = chunk_start(c)
            kc = kvb_ref[pl.ds(k0, ck), kvh * HEAD:(kvh + 1) * HEAD]
            s3 = _dot_nt(qg, kc).reshape(ATT_GROUP, tq, ck) + s_scr[:, pl.ds(k0, ck)][None]
            if near:
                s3 = s3 + tbd_ref[gs, qt - c, :, :]
            s = s3.reshape(g_rows, ck)
            l_scr[:, pl.ds(k0, ck)] = s
            m = mx_scr[...]
            for i in range(ck // LANES):
                m = jnp.maximum(m, s[:, i * LANES:(i + 1) * LANES])
            mx_scr[...] = m
            return carry

        lax.fori_loop(0, n_far, functools.partial(logits_chunk, near=False), 0)
        lax.fori_loop(n_far, nch, functools.partial(logits_chunk, near=True), 0)
        mx_scr[...] = jnp.broadcast_to(jnp.max(mx_scr[...], axis=-1, keepdims=True), mx_scr.shape)

        def pv_chunk(c, carry, kvh=kvh):
            k0 = chunk_start(c)
            vx = kvb_ref[pl.ds(k0, ck), w + kvh * LANES:w + (kvh + 1) * LANES]
            m = mx_scr[...]
            p = jnp.concatenate([jnp.exp(l_scr[:, pl.ds(k0 + i * LANES, LANES)] - m)
                                 for i in range(ck // LANES)], axis=1).astype(BF16)
            acc_scr[...] += jnp.dot(p, vx, preferred_element_type=F32)
            return carry

        lax.fori_loop(0, nch, pv_chunk, 0)

        acc = acc_scr[...]
        o = acc[:, 0:HEAD] / acc[:, HEAD:HEAD + 1]
        for pp in range(ATT_GROUP // 2):
            cols = slice((kvh * ATT_GROUP // 2 + pp) * LANES, (kvh * ATT_GROUP // 2 + pp + 1) * LANES)
            gh = g_ref[:, cols]
            pair = jnp.concatenate([o[2 * pp * tq:(2 * pp + 1) * tq], o[(2 * pp + 1) * tq:(2 * pp + 2) * tq]], axis=1)
            o_ref[:, cols] = pair * (gh * _sigmoid(gh))


def _attn_prompt(za, kn, qb, qib, kvb, kib, *, rel_bias, batch, seq_len, tq, k_sel):
    m = za.shape[0]
    nq = seq_len // tq
    w = ATT_KV_HEADS * HEAD
    wi = IDX_HEADS * HEAD
    assert REL_MAX_DIST <= tq and tq % LANES == 0
    row_map = lambda blk: (lambda t, b, blk=blk: (b * nq + t, blk))
    key_map = lambda t, b: (b, 0)
    kern = functools.partial(_attn_prompt_kernel, tq=tq, k_sel=k_sel, bisect_steps=_BISECT_STEPS)
    return pl.pallas_call(
        kern, grid=(nq, batch),
        in_specs=[
            pl.BlockSpec(memory_space=pltpu.SMEM),
            pl.BlockSpec((ATT_HEADS, tq, HEAD), lambda t, b: (0, b * nq + t, 0)),
            pl.BlockSpec((tq, wi), row_map(0)),
            pl.BlockSpec((tq, LANES), row_map(ZA_KIW // LANES)),
            pl.BlockSpec((tq, D_MODEL), row_map(ZA_GATT // D_MODEL)),
            pl.BlockSpec((seq_len, w + ATT_KV_HEADS * LANES), key_map),
            pl.BlockSpec((seq_len, HEAD), key_map),
        ],
        out_specs=pl.BlockSpec((tq, D_MODEL), row_map(0)),
        out_shape=jax.ShapeDtypeStruct((m, D_MODEL), F32),
        scratch_shapes=[pltpu.VMEM((tq, seq_len), F32), pltpu.VMEM((ATT_HEADS, 2, tq, tq), F32),
                        pltpu.VMEM((ATT_GROUP * tq, seq_len), F32), pltpu.VMEM((ATT_GROUP * tq, LANES), F32),
                        pltpu.VMEM((ATT_GROUP * tq, LANES), F32)],
        compiler_params=_cparams(2), name="attn_prompt")(rel_bias, qb, qib, za, za, kvb, kib)


def _sample_score_kernel(pt_ref, q_ref, w_ref, kiw_new_ref, *rest, n_pages, page, dec_seq):
    page_refs = rest[:n_pages]
    o_ref = rest[n_pages]
    past = n_pages * page
    kidx = jnp.concatenate([r[...] for r in page_refs], axis=0).astype(BF16)
    new = kiw_new_ref[:, 0:HEAD]
    new = jnp.concatenate([new, jnp.zeros((LANES - dec_seq, HEAD), F32)], axis=0).astype(BF16)
    q = (q_ref[...] * (HEAD ** -0.5)).astype(BF16)
    lg = jnp.concatenate([_dot_nt(q, kidx), _dot_nt(q, new)], axis=1)
    wr = jnp.maximum(lg, 0.0) * (w_ref[...] * (IDX_HEADS ** -0.5))
    n_keys = past + LANES
    sc = jnp.sum(wr.reshape(dec_seq, IDX_HEADS, n_keys), axis=1)
    kpos = lax.broadcasted_iota(jnp.int32, (dec_seq, n_keys), 1)
    tpos = lax.broadcasted_iota(jnp.int32, (dec_seq, n_keys), 0)
    o_ref[...] = jnp.where(kpos <= past + tpos, sc, _NEG_INF)


def _sample_select_kernel(s_ref, o_ref, *, k_sel, bisect_steps):
    score = s_ref[...]
    kpos = lax.broadcasted_iota(jnp.int32, score.shape, 1)
    sel = _select_topk(score, kpos, k_sel, bisect_steps)
    o_ref[...] = jnp.where(sel, 0.0, _NEG_INF)


def _sample_attn_kernel(pt_ref, q_ref, g_ref, kn_new_ref, kv_new_ref, mask_ref, rb_rows_ref, qg_ref, *rest,
                        n_pages, page, dec_seq):
    k_pages = rest[:n_pages]
    v_pages = rest[n_pages:2 * n_pages]
    o_ref = rest[2 * n_pages]
    bias_ref = rest[2 * n_pages + 1]
    w = ATT_KV_HEADS * HEAD
    rows = dec_seq * ATT_HEADS
    n_keys = n_pages * page + LANES

    @pl.when(pl.program_id(0) == 0)
    def _():
        t_row = lax.div(lax.broadcasted_iota(jnp.int32, (rows, n_keys), 0), ATT_HEADS)
        dist = n_pages * page + t_row - lax.broadcasted_iota(jnp.int32, (rows, n_keys), 1)
        bias_ref[...] = _rel_bias_lookup(dist, lambda b: rb_rows_ref[:, b:b + 1])

    pad = jnp.zeros((LANES - dec_seq, w), F32)
    k_all = jnp.concatenate([r[...] for r in k_pages] + [kn_new_ref[...], pad], axis=0).astype(BF16)
    v_all = jnp.concatenate([r[...] for r in v_pages] + [kv_new_ref[:, w:2 * w], pad], axis=0).astype(BF16)
    q = q_ref[...]
    q = q * lax.rsqrt(jnp.mean(q * q, axis=-1, keepdims=True) + NORM_EPS) * qg_ref[...]
    qb = q.astype(BF16)
    head = lax.rem(lax.broadcasted_iota(jnp.int32, (rows, 1), 0), ATT_HEADS)
    first = head < ATT_GROUP
    lg = jnp.where(first, _dot_nt(qb, k_all[:, 0:HEAD]), _dot_nt(qb, k_all[:, HEAD:2 * HEAD]))
    mask = jnp.broadcast_to(mask_ref[...][:, None, :], (dec_seq, ATT_HEADS, n_keys)).reshape(rows, n_keys)
    s = lg * (HEAD ** -0.5) + bias_ref[...] + mask
    p = jnp.exp(s - jnp.max(s, axis=-1, keepdims=True))
    l = jnp.sum(p, axis=-1, keepdims=True)
    pb = p.astype(BF16)
    o = jnp.where(first, jnp.dot(pb, v_all[:, 0:HEAD], preferred_element_type=F32),
                  jnp.dot(pb, v_all[:, HEAD:2 * HEAD], preferred_element_type=F32)) / l
    g = g_ref[...]
    o_ref[...] = o * (g * _sigmoid(g))


def _attn_sample(za, kn, qb, qib, kvb, kib, *, cache_k, cache_v, cache_kidx, page_table, rel_bias, qn_g,
                 dec_seq, k_sel):
    nb, n_pages = page_table.shape
    n_phys, page = cache_k.shape[0], cache_k.shape[1]
    past = n_pages * page
    n_keys = past + LANES
    w = ATT_KV_HEADS * HEAD
    ck = cache_k.reshape(n_phys, page, w)
    cv = cache_v.reshape(n_phys, page, w)
    ci = cache_kidx.reshape(n_phys, page, HEAD)
    za3 = za.reshape(nb, dec_seq, ZA_COLS)
    kn3 = kn.reshape(nb, dec_seq, w)
    qi = za[:, ZA_QI:ZA_QI + IDX_HEADS * HEAD].reshape(nb, dec_seq * IDX_HEADS, HEAD)
    wi = za[:, ZA_KIW + HEAD:ZA_KIW + HEAD + IDX_HEADS].reshape(nb, dec_seq * IDX_HEADS, 1)
    qa = za[:, ZA_Q:ZA_Q + D_MODEL].reshape(nb, dec_seq * ATT_HEADS, HEAD)
    ga = za[:, ZA_GATT:ZA_GATT + D_MODEL].reshape(nb, dec_seq * ATT_HEADS, HEAD)

    def page_specs(width):
        return [pl.BlockSpec((None, page, width), lambda b, pt, j=j: (pt[b, j], 0, 0)) for j in range(n_pages)]

    per_b = lambda r, c: pl.BlockSpec((None, r, c), lambda b, pt: (b, 0, 0))
    kiw_new = pl.BlockSpec((None, dec_seq, LANES), lambda b, pt: (b, 0, ZA_KIW // LANES))

    scores = pl.pallas_call(
        functools.partial(_sample_score_kernel, n_pages=n_pages, page=page, dec_seq=dec_seq),
        grid_spec=pltpu.PrefetchScalarGridSpec(
            num_scalar_prefetch=1, grid=(nb,),
            in_specs=[per_b(dec_seq * IDX_HEADS, HEAD), per_b(dec_seq * IDX_HEADS, 1), kiw_new] + page_specs(HEAD),
            out_specs=per_b(dec_seq, n_keys)),
        out_shape=jax.ShapeDtypeStruct((nb, dec_seq, n_keys), F32),
        compiler_params=_cparams(1), name="sample_scores")(page_table, qi, wi, za3, *([ci] * n_pages))

    rows = nb * dec_seq
    tr = min(rows, 128)
    mask = pl.pallas_call(
        functools.partial(_sample_select_kernel, k_sel=k_sel, bisect_steps=_BISECT_STEPS),
        grid=(rows // tr,),
        in_specs=[pl.BlockSpec((tr, n_keys), lambda i: (i, 0))],
        out_specs=pl.BlockSpec((tr, n_keys), lambda i: (i, 0)),
        out_shape=jax.ShapeDtypeStruct((rows, n_keys), F32),
        compiler_params=_cparams(1), name="sample_select")(scores.reshape(rows, n_keys))

    rb_rows = jnp.tile(rel_bias.T, (dec_seq, 1))

    const = lambda shape: pl.BlockSpec(shape, lambda b, pt: (0,) * len(shape))
    kv_new = pl.BlockSpec((None, dec_seq, 2 * w), lambda b, pt: (b, 0, ZA_KV // (2 * w)))
    out = pl.pallas_call(
        functools.partial(_sample_attn_kernel, n_pages=n_pages, page=page, dec_seq=dec_seq),
        grid_spec=pltpu.PrefetchScalarGridSpec(
            num_scalar_prefetch=1, grid=(nb,),
            in_specs=[per_b(dec_seq * ATT_HEADS, HEAD), per_b(dec_seq * ATT_HEADS, HEAD), per_b(dec_seq, w),
                      kv_new, per_b(dec_seq, n_keys), const((dec_seq * ATT_HEADS, REL_BUCKETS)), const((1, HEAD))]
            + page_specs(w) + page_specs(w),
            out_specs=per_b(dec_seq * ATT_HEADS, HEAD),
            scratch_shapes=[pltpu.VMEM((dec_seq * ATT_HEADS, n_keys), F32)]),
        out_shape=jax.ShapeDtypeStruct((nb, dec_seq * ATT_HEADS, HEAD), F32),
        compiler_params=_cparams(1), name="sample_attn")(
            page_table, qa, ga, kn3, za3, mask.reshape(nb, dec_seq, n_keys), rb_rows, qn_g.reshape(1, HEAD),
            *([ck] * n_pages), *([cv] * n_pages))
    return out.reshape(rows, D_MODEL)


def _merge_kernel(x_ref, oa_ref, sg_ref, ob_ref, ga_ref, gb_ref, wpa_ref, wpb_ref, wo_ref, y_ref):
    oa = (oa_ref[...] * sg_ref[...]).astype(BF16)
    pa = jnp.dot(oa, wpa_ref[...], preferred_element_type=F32)
    pb = jnp.dot(ob_ref[...].astype(BF16), wpb_ref[...], preferred_element_type=F32)
    merged = _sigmoid(ga_ref[...]) * pa + _sigmoid(gb_ref[...]) * pb
    y_ref[...] = x_ref[...] + jnp.dot(merged.astype(BF16), wo_ref[...], preferred_element_type=F32)


def _merge(x, oa, sg, ob, za, w_pa, w_pb, w_out, tm):
    m, d = x.shape
    row = pl.BlockSpec((tm, d), lambda i: (i, 0))
    wsp = pl.BlockSpec((d, d), lambda i: (0, 0))
    return pl.pallas_call(
        _merge_kernel, grid=(m // tm,),
        in_specs=[row, row, row, row,
                  pl.BlockSpec((tm, d), lambda i: (i, ZA_GA // d)), pl.BlockSpec((tm, d), lambda i: (i, ZA_GB // d)),
                  wsp, wsp, wsp],
        out_specs=row,
        out_shape=jax.ShapeDtypeStruct((m, d), F32),
        compiler_params=_cparams(1), name="merge")(x, oa, sg, ob, za, za, w_pa, w_pb, w_out)


def _layer(x, shift, s0, params, *, batch, seq_len, attend):
    (norm_g, w_r, w_a, mu, w0, w2, a0, a2, k_k, k_a, r_k, lnx_g, lnx_b, qn_g, kn_g, w_pa, w_pb, w_out) = params
    m = batch * seq_len
    tm = min(m, 512)
    xn = _rmsnorm(x, norm_g, tm)
    tm_proj = min(m, 2048)
    zr = _matmul(xn, w_r, tm_proj, RWKV_COLS // 3, "inproj_rwkv")
    za = _matmul(xn, w_a, tm_proj, ZA_COLS // 4, "inproj_attn")

    tile = min(m, 256)
    r, w, k, v, a, sg = _rwkv_prep(zr, shift, mu, w0, w2, a0, a2, batch=batch, seq_len=seq_len, tile=tile)
    relayout_tile = 256
    in_kernel_relayout = seq_len % relayout_tile == 0 and batch * RWKV_HEADS == LANES
    if in_kernel_relayout:
        seqs = [_to_pairs(t, batch, seq_len, relayout_tile) for t in (r, w, k, v, a)]
    else:
        seqs = [_to_pairs_xla(t, batch, seq_len) for t in (r, w, k, v, a)]
    row_consts = [_head_const_pairs(t, batch) for t in (lnx_g, lnx_b)]
    step_consts = [jnp.broadcast_to(_head_const_pairs(t, batch)[:, None, :], (HEAD, SUBLANES, batch * RWKV_HEADS))
                   for t in (r_k.reshape(-1), k_k, k_a)]
    o_t, s_t = _rwkv_scan(seqs, s0, row_consts, step_consts, seq_len=seq_len, steps=min(seq_len, 64))
    if in_kernel_relayout:
        oa = _from_pairs(o_t, batch, seq_len, relayout_tile // 2)
    else:
        oa = _from_pairs_xla(o_t, batch, seq_len)

    kn, qb, qib, kvb, kib = _attn_prep(za, qn_g, kn_g, tm)
    ob = attend(za, kn, qb, qib, kvb, kib)

    y = _merge(x, oa, sg, ob, za, w_pa, w_pb, w_out, min(m, 256))
    return y, zr, za, kn, s_t


def _state_to_pairs(s):
    b, h, n, _ = s.shape
    return jnp.transpose(s, (3, 2, 0, 1)).reshape(n, n, b * h)


def _state_from_pairs(s, batch):
    n = s.shape[0]
    return jnp.transpose(s.reshape(n, n, batch, RWKV_HEADS), (2, 3, 1, 0))


def kernel(x_prompt, x_sample, cache_k, cache_v, cache_kidx, state_wkv, state_shift, page_table, norm_g, w_in,
           shift_mu, w0, w2, a0, a2, k_k, k_a, r_k, lnx_g, lnx_b, q_norm_g, k_norm_g, rel_bias, w_pa, w_pb, w_out):
    bsz, seq, d = x_prompt.shape
    dec_bsz, dec_seq, _ = x_sample.shape
    depth = w_in.shape[0]
    assert depth == 1 and d == D_MODEL
    past_len = page_table.shape[1] * cache_k.shape[2]
    topk_p = min(TOPK_MAX, seq // 4)
    topk_s = min(TOPK_MAX, (past_len + dec_seq) // 4)
    l = 0

    wl = w_in[l]
    c0 = RWKV_COLS
    q_w, kv_w, qi_w = wl[:, c0:c0 + 1024], wl[:, c0 + 1024:c0 + 1280], wl[:, c0 + 1280:c0 + 1792]
    kiw_w = wl[:, c0 + 1792:c0 + 1864]
    rest_w = wl[:, c0 + 1864:]
    zpad = lambda n: jnp.zeros((d, n), wl.dtype)
    w_a = jnp.concatenate([q_w, qi_w, kv_w, kiw_w, zpad(LANES - kiw_w.shape[1]), zpad(LANES), rest_w],
                          axis=1).astype(BF16)
    assert w_a.shape[1] == ZA_COLS
    def rwkv_cols(x, fn):
        lead = x.shape[:-1]
        main = fn(x[..., :4 * d].reshape(lead + (4, d)), len(lead) + 1).reshape(lead + (4 * d,))
        return jnp.concatenate([main, x[..., 4 * d:]], axis=-1)

    w_r = rwkv_cols(wl[:, :c0], _nh_order).astype(BF16)
    params = (norm_g[l], w_r, w_a, rwkv_cols(shift_mu[l], _nh_order), _nh_order(w0[l], 0), _nh_order(w2[l], 1),
              _nh_order(a0[l], 0), _nh_order(a2[l], 1), k_k[l], k_a[l], r_k[l],
              lnx_g[l], lnx_b[l], q_norm_g[l], k_norm_g[l],
              _nh_order(w_pa[l], 0).astype(BF16), w_pb[l].astype(BF16), w_out[l].astype(BF16))

    xp = x_prompt.reshape(bsz * seq, d)
    attend_p = functools.partial(_attn_prompt, rel_bias=rel_bias, batch=bsz, seq_len=seq,
                                 tq=min(seq, 256), k_sel=topk_p)
    yp, zr_p, za_p, kn_p, st_p = _layer(
        xp, jnp.zeros((bsz, 1, RWKV_COLS), F32), jnp.zeros((HEAD, HEAD, bsz * RWKV_HEADS), F32), params,
        batch=bsz, seq_len=seq, attend=attend_p)

    xs = x_sample.reshape(dec_bsz * dec_seq, d)
    attend_s = functools.partial(_attn_sample, cache_k=cache_k[l], cache_v=cache_v[l], cache_kidx=cache_kidx[l],
                                 page_table=page_table, rel_bias=rel_bias, qn_g=q_norm_g[l], dec_seq=dec_seq,
                                 k_sel=topk_s)
    shift_rows = jnp.repeat(rwkv_cols(state_shift[l], _nh_order), dec_seq, axis=0)
    ys, zr_s, za_s, kn_s, st_s = _layer(
        xs, shift_rows, _state_to_pairs(state_wkv[l]), params, batch=dec_bsz, seq_len=dec_seq, attend=attend_s)

    w = ATT_KV_HEADS * HEAD

    def pack(y, zr, za, kn, st, b, t):
        v = za[:, ZA_KV + w:ZA_KV + 2 * w]
        kidx = za[:, ZA_KIW:ZA_KIW + HEAD]
        return (y.reshape(b, t, d),
                kn.reshape(1, b, t, ATT_KV_HEADS, HEAD), v.reshape(1, b, t, ATT_KV_HEADS, HEAD),
                kidx.reshape(1, b, t, HEAD), _state_from_pairs(st, b)[None],
                rwkv_cols(zr.reshape(b, t, RWKV_COLS)[:, -1], _hn_order)[None])

    p = pack(yp, zr_p, za_p, kn_p, st_p, bsz, seq)
    s = pack(ys, zr_s, za_s, kn_s, st_s, dec_bsz, dec_seq)
    return (p[0], s[0]) + p[1:] + s[1:]
```

```python
import functools
import math

import numpy as np
import jax
import jax.numpy as jnp
from jax import lax
from jax.experimental import pallas as pl
from jax.experimental.pallas import tpu as pltpu

F32 = jnp.float32
BF16 = jnp.bfloat16

D_MODEL = 1024
HEAD = 64
RWKV_HEADS = D_MODEL // HEAD
LORA = 64
LNX_EPS = 64e-5
ATT_HEADS = D_MODEL // HEAD
ATT_KV_HEADS = 2
ATT_GROUP = ATT_HEADS // ATT_KV_HEADS
IDX_HEADS = 8
TOPK_MAX = 256
REL_BUCKETS = 32
REL_MAX_DIST = 128
NORM_EPS = 1e-6
RWKV_COLS = 4 * D_MODEL + 2 * LORA

LANES = 128
SUBLANES = 8
VMEM_LIMIT_BYTES = 56 * 1024 * 1024

ZA_Q = 0
ZA_QI = 1024
ZA_KV = 1536
ZA_KIW = 1792
ZA_GATT = 2048
ZA_GA = 3072
ZA_GB = 4096
ZA_COLS = 5120

_NEG_INF = float("-inf")
_POS_INF = float("inf")
_BISECT_STEPS = 14


def _cparams(n_axes):
    return pltpu.CompilerParams(dimension_semantics=("arbitrary",) * n_axes,
                                vmem_limit_bytes=VMEM_LIMIT_BYTES)


def _sigmoid(x):
    return 1.0 / (1.0 + jnp.exp(-x))


def _dot_nt(a, b):
    return lax.dot_general(a, b, (((1,), (1,)), ((), ())), preferred_element_type=F32)


def _norm_mm_kernel(x_ref, g_ref, w_ref, o_ref, xn_scr):
    @pl.when(pl.program_id(1) == 0)
    def _():
        x = x_ref[...]
        ms = jnp.mean(x * x, axis=-1, keepdims=True)
        xn_scr[...] = (x * lax.rsqrt(ms + NORM_EPS) * g_ref[...]).astype(BF16)

    o_ref[...] = jnp.dot(xn_scr[...], w_ref[...], preferred_element_type=F32)


def _norm_matmul(x, g, w, tm, tn, name):
    m, k = x.shape
    n = w.shape[1]
    return pl.pallas_call(
        _norm_mm_kernel, grid=(m // tm, n // tn),
        in_specs=[pl.BlockSpec((tm, k), lambda i, j: (i, 0)), pl.BlockSpec((1, k), lambda i, j: (0, 0)),
                  pl.BlockSpec((k, tn), lambda i, j: (0, j))],
        out_specs=pl.BlockSpec((tm, tn), lambda i, j: (i, j)),
        out_shape=jax.ShapeDtypeStruct((m, n), F32),
        scratch_shapes=[pltpu.VMEM((tm, k), BF16)],
        compiler_params=_cparams(2), name=name)(x, g.reshape(1, k), w)


def _rwkv_prep_kernel(z_ref, prev_ref, shift_ref, mu_ref, w0_ref, w2_ref, a0_ref, a2_ref,
                      r_o, w_o, k_o, v_o, a_o, g_o, *, seq_len, tile):
    z = z_ref[...]
    rows = lax.broadcasted_iota(jnp.int32, z.shape, 0)
    rolled = pltpu.roll(z, 1, 0)
    if seq_len >= tile:
        first = jnp.where(pl.program_id(1) == 0, shift_ref[...], prev_ref[SUBLANES - 1:SUBLANES, :])
        prev = jnp.where(rows == 0, first, rolled)
    else:
        prev = jnp.where(lax.rem(rows, seq_len) == 0, shift_ref[...], rolled)
    zs = z + (prev - z) * mu_ref[...]
    d = D_MODEL
    r = zs[:, 0:d]
    k = zs[:, d:2 * d]
    v = zs[:, 2 * d:3 * d]
    g = zs[:, 3 * d:4 * d]
    wd = zs[:, 4 * d:4 * d + LORA]
    ad = zs[:, 4 * d + LORA:4 * d + 2 * LORA]
    wl = w0_ref[...] + jnp.dot(jnp.tanh(wd).astype(BF16), w2_ref[...], preferred_element_type=F32)
    decay = jnp.exp(-_sigmoid(wl) * math.exp(-0.5))
    a = _sigmoid(a0_ref[...] + jnp.dot(ad.astype(BF16), a2_ref[...], preferred_element_type=F32))
    r_o[...] = r
    w_o[...] = decay
    k_o[...] = k
    v_o[...] = v
    a_o[...] = a
    g_o[...] = g * _sigmoid(g)


def _rwkv_prep(zr, shift, mu, w0, w2, a0, a2, *, batch, seq_len, tile):
    m, c = zr.shape
    d = D_MODEL
    row = lambda x: x.reshape(1, -1)
    consts = [row(mu), row(w0), w2.astype(BF16), row(a0), a2.astype(BF16)]
    const_specs = [pl.BlockSpec(x.shape, lambda *_: (0, 0)) for x in consts]
    if seq_len >= tile:
        nt = seq_len // tile
        grid = (batch, nt)
        zmap = lambda b, t: (b * nt + t, 0)
        pmap = lambda b, t: (jnp.maximum((b * seq_len + t * tile) // SUBLANES - 1, 0), 0)
        shift_spec = pl.BlockSpec((None, 1, c), lambda b, t: (b, 0, 0))
    else:
        grid = (1, m // tile)
        zmap = lambda b, t: (t, 0)
        pmap = lambda b, t: (0, 0)
        shift_spec = pl.BlockSpec((tile, c), zmap)
    out_spec = pl.BlockSpec((tile, d), zmap)
    kern = functools.partial(_rwkv_prep_kernel, seq_len=seq_len, tile=tile)
    return pl.pallas_call(
        kern, grid=grid,
        in_specs=[pl.BlockSpec((tile, c), zmap), pl.BlockSpec((SUBLANES, c), pmap), shift_spec] + const_specs,
        out_specs=[out_spec] * 6,
        out_shape=[jax.ShapeDtypeStruct((m, d), F32)] * 6,
        compiler_params=_cparams(2), name="rwkv_prep")(zr, zr, shift, *consts)


def _rwkv_scan_kernel(r_in, w_in, k_in, v_in, a_in, s0_ref, lg_ref, lb_ref, rk_ref, kkc_ref, kac_ref,
                      o_ref, s_ref, vec_ref, bf_ref, ge_ref, *, groups, live):
    nb = HEAD // SUBLANES

    @pl.when(pl.program_id(1) == 0)
    def _():
        s_ref[...] = s0_ref[...]

    shape4 = (groups, HEAD, SUBLANES, LANES)
    a = a_in[...].reshape(shape4)
    kraw = k_in[...].reshape(shape4)
    r_all = r_in[...].reshape(shape4)
    kkraw = kraw * kkc_ref[...][None]
    n2 = jnp.sum(kkraw * kkraw, axis=1, keepdims=True)
    kk = kkraw / jnp.maximum(jnp.sqrt(n2), 1e-12)
    kmod = kraw * (1.0 + (a - 1.0) * kac_ref[...][None])
    bf_ref[...] = jnp.sum(r_all * kmod * rk_ref[...][None], axis=1)
    row = lax.broadcasted_iota(jnp.int32, (HEAD, SUBLANES, LANES), 1)
    g_end = jnp.ones((HEAD, SUBLANES, LANES), F32)
    for q in range(groups):
        g = w_in[q].reshape(HEAD, SUBLANES, LANES)
        for sh in (1, 2, 4):
            g = g * jnp.where(row >= sh, pltpu.roll(g, sh, 1), 1.0)
        g = g * g_end
        g_prev = jnp.where(row >= 1, pltpu.roll(g, 1, 1), g_end)
        g_inv = 1.0 / g
        flat = (HEAD * SUBLANES, LANES)
        vec_ref[0, q] = (-kk[q] * g_prev).reshape(flat)
        vec_ref[1, q] = (kk[q] * a[q] * g_inv).reshape(flat)
        vec_ref[2, q] = (kmod[q] * g_inv).reshape(flat)
        vec_ref[3, q] = (r_all[q] * g).reshape(flat)
        last = live - 1 if q == groups - 1 else SUBLANES - 1
        g_end = jnp.broadcast_to(g[:, last:last + 1, :], (HEAD, SUBLANES, LANES))
    ge_ref[...] = g_end

    def group(q, carry):
        for r in range(live):
            def row_of(idx, j):
                return jnp.broadcast_to(vec_ref[idx, q, pl.ds(j * SUBLANES + r, 1), :], (SUBLANES, LANES))

            sa = [jnp.zeros((SUBLANES, LANES), F32) for _ in range(nb)]
            for j in range(HEAD):
                nk = row_of(0, j)
                for ib in range(nb):
                    sa[ib] = sa[ib] + s_ref[j, ib * SUBLANES:(ib + 1) * SUBLANES, :] * nk
            vt = [v_in[q, pl.ds(ib * SUBLANES * SUBLANES + r, SUBLANES, stride=SUBLANES), :] for ib in range(nb)]
            out = [jnp.zeros((SUBLANES, LANES), F32) for _ in range(nb)]
            for j in range(HEAD):
                kaj = row_of(1, j)
                kj = row_of(2, j)
                rj = row_of(3, j)
                for ib in range(nb):
                    sl = slice(ib * SUBLANES, (ib + 1) * SUBLANES)
                    sn = s_ref[j, sl, :] + sa[ib] * kaj + vt[ib] * kj
                    s_ref[j, sl, :] = sn
                    out[ib] = out[ib] + sn * rj
            o = jnp.concatenate(out, axis=0)
            mean = jnp.mean(o, axis=0, keepdims=True)
            dev = o - mean
            var = jnp.mean(dev * dev, axis=0, keepdims=True)
            y = dev * lax.rsqrt(var + LNX_EPS) * lg_ref[...] + lb_ref[...]
            o_ref[q * live + r] = y + bf_ref[q, pl.ds(r, 1), :] * jnp.concatenate(vt, axis=0)
        return carry

    lax.fori_loop(0, groups, group, 0)

    for j in range(HEAD):
        s_ref[j] = s_ref[j] * jnp.concatenate([ge_ref[j]] * nb, axis=0)


def _rwkv_scan(seqs, s0, row_consts, step_consts, *, seq_len, steps):
    t8, _, p = seqs[0].shape
    n = HEAD
    if seq_len >= SUBLANES:
        groups, live = steps // SUBLANES, SUBLANES
    else:
        groups, live = 1, seq_len
    grid = (p // LANES, t8 // groups)
    in_spec = pl.BlockSpec((groups, n * SUBLANES, LANES), lambda g, t: (t, 0, g))
    out_spec = pl.BlockSpec((groups * live, n, LANES), lambda g, t: (t, 0, g))
    st_spec = pl.BlockSpec((n, n, LANES), lambda g, t: (0, 0, g))
    rc_spec = pl.BlockSpec((n, LANES), lambda g, t: (0, g))
    sc_spec = pl.BlockSpec((n, SUBLANES, LANES), lambda g, t: (0, 0, g))
    kern = functools.partial(_rwkv_scan_kernel, groups=groups, live=live)
    return pl.pallas_call(
        kern, grid=grid,
        in_specs=[in_spec] * 5 + [st_spec] + [rc_spec] * 2 + [sc_spec] * 3,
        out_specs=[out_spec, st_spec],
        out_shape=[jax.ShapeDtypeStruct((seq_len, n, p), F32), jax.ShapeDtypeStruct((n, n, p), F32)],
        scratch_shapes=[pltpu.VMEM((4, groups, n * SUBLANES, LANES), F32), pltpu.VMEM((groups, SUBLANES, LANES), F32),
                        pltpu.VMEM((n, SUBLANES, LANES), F32)],
        compiler_params=_cparams(2), name="rwkv_scan")(*seqs, s0, *row_consts, *step_consts)


def _nh_order(x, axis):
    shape = x.shape
    x = x.reshape(shape[:axis] + (RWKV_HEADS, HEAD) + shape[axis + 1:])
    return jnp.swapaxes(x, axis, axis + 1).reshape(shape)


def _hn_order(x, axis):
    shape = x.shape
    x = x.reshape(shape[:axis] + (HEAD, RWKV_HEADS) + shape[axis + 1:])
    return jnp.swapaxes(x, axis, axis + 1).reshape(shape)


def _to_pairs_kernel(x_ref, o_ref, a_scr):
    nb = x_ref.shape[0]
    for b in range(nb):
        a_scr[b] = x_ref[b].T
    for n in range(HEAD):
        blk = a_scr[:, n * RWKV_HEADS:(n + 1) * RWKV_HEADS, :]
        y = blk.reshape(nb * RWKV_HEADS, blk.shape[2]).T
        o_ref[:, n * SUBLANES:(n + 1) * SUBLANES, :] = y.reshape(y.shape[0] // SUBLANES, SUBLANES, LANES)


def _to_pairs(x, batch, seq_len, tt):
    assert batch * RWKV_HEADS == LANES
    x3 = x.reshape(batch, seq_len, D_MODEL)
    return pl.pallas_call(
        _to_pairs_kernel, grid=(seq_len // tt,),
        in_specs=[pl.BlockSpec((batch, tt, D_MODEL), lambda t: (0, t, 0))],
        out_specs=pl.BlockSpec((tt // SUBLANES, HEAD * SUBLANES, LANES), lambda t: (t, 0, 0)),
        out_shape=jax.ShapeDtypeStruct((seq_len // SUBLANES, HEAD * SUBLANES, LANES), F32),
        scratch_shapes=[pltpu.VMEM((batch, D_MODEL, tt), F32)],
        compiler_params=_cparams(1), name="to_pairs")(x3)


def _from_pairs_kernel(o_ref, x_ref, a_scr):
    nb = x_ref.shape[0]
    for n in range(HEAD):
        blk = o_ref[:, n, :].T
        a_scr[:, n * RWKV_HEADS:(n + 1) * RWKV_HEADS, :] = blk.reshape(nb, RWKV_HEADS, blk.shape[1])
    for b in range(nb):
        x_ref[b] = a_scr[b].T


def _from_pairs(o, batch, seq_len, tt):
    assert batch * RWKV_HEADS == LANES
    out = pl.pallas_call(
        _from_pairs_kernel, grid=(seq_len // tt,),
        in_specs=[pl.BlockSpec((tt, HEAD, LANES), lambda t: (t, 0, 0))],
        out_specs=pl.BlockSpec((batch, tt, D_MODEL), lambda t: (0, t, 0)),
        out_shape=jax.ShapeDtypeStruct((batch, seq_len, D_MODEL), F32),
        scratch_shapes=[pltpu.VMEM((batch, D_MODEL, tt), F32)],
        compiler_params=_cparams(1), name="from_pairs")(o)
    return out.reshape(batch * seq_len, D_MODEL)


def _to_pairs_xla(x, batch, seq_len):
    assert seq_len < SUBLANES
    x = jnp.transpose(x.reshape(batch, seq_len, HEAD, RWKV_HEADS), (2, 1, 0, 3))
    x = x.reshape(HEAD, seq_len, batch * RWKV_HEADS)
    x = jnp.pad(x, ((0, 0), (0, SUBLANES - seq_len), (0, 0)), constant_values=1.0)
    return x.reshape(1, HEAD * SUBLANES, batch * RWKV_HEADS)


def _from_pairs_xla(o, batch, seq_len):
    o = o.reshape(seq_len, HEAD, batch, RWKV_HEADS)
    return jnp.transpose(o, (2, 0, 1, 3)).reshape(batch * seq_len, D_MODEL)


def _head_const_pairs(x, batch):
    x = x.reshape(RWKV_HEADS, HEAD).T
    return jnp.tile(x, (1, batch))


def _count_ge(score, thr):
    return jnp.sum(jnp.where(score >= thr, 1.0, 0.0), axis=-1, keepdims=True)


def _select_topk(score, kpos, k_sel, bisect_steps):
    kf = float(k_sel)
    vis = score > _NEG_INF
    nvis = jnp.sum(jnp.where(vis, 1.0, 0.0), axis=-1, keepdims=True)
    need_sel = nvis > kf
    rowmax = jnp.where(need_sel, jnp.max(score, axis=-1, keepdims=True), 0.0)
    rowmin = jnp.where(need_sel, jnp.min(jnp.where(vis, score, _POS_INF), axis=-1, keepdims=True), 0.0)

    def bis(_, c):
        lo, hi, chi = c
        hfin = jnp.where(hi == _POS_INF, rowmax, hi)
        piv = 0.5 * lo + 0.5 * hfin
        cnt = _count_ge(score, piv)
        ge = cnt >= kf
        return jnp.where(ge, piv, lo), jnp.where(ge, hi, piv), jnp.where(ge, chi, cnt)

    lo, hi, chi = lax.fori_loop(0, bisect_steps, bis,
                                (rowmin, jnp.full_like(rowmin, _POS_INF), jnp.zeros_like(rowmin)))

    def walk_cond(c):
        return jnp.min(c[4]) < 0.5

    def walk(c):
        hi, chi, tau, cgt, done, ceq = c
        pending = done < 0.5
        bmax = jnp.max(jnp.where(score < hi, score, _NEG_INF), axis=-1, keepdims=True)
        cnt = _count_ge(score, bmax)
        fin = jnp.logical_and(cnt >= kf, pending)
        tau = jnp.where(fin, bmax, tau)
        cgt = jnp.where(fin, chi, cgt)
        ceq = jnp.where(fin, cnt - chi, ceq)
        adv = jnp.logical_and(cnt < kf, pending)
        hi = jnp.where(adv, bmax, hi)
        chi = jnp.where(adv, cnt, chi)
        return hi, chi, tau, cgt, jnp.where(fin, 1.0, done), ceq

    done0 = jnp.where(need_sel, 0.0, 1.0)
    neg = jnp.full_like(rowmin, _NEG_INF)
    zero = jnp.zeros_like(rowmin)
    _, _, tau, cgt, _, ceq = lax.while_loop(walk_cond, walk, (hi, chi, neg, zero, done0, zero))

    need = kf - cgt
    excess = jnp.logical_and(need_sel, ceq > need)
    eq = score == tau
    n_keys = score.shape[-1]

    def tie_break(_):
        def body(_, c):
            plo, phi = c
            mid = lax.shift_right_arithmetic(plo + phi, 1)
            cnt = jnp.sum(jnp.where(jnp.logical_and(eq, kpos <= mid), 1.0, 0.0), axis=-1, keepdims=True)
            ge = cnt >= need
            return jnp.where(ge, plo, mid), jnp.where(ge, mid, phi)
        plo0 = jnp.full(tau.shape, -1, jnp.int32)
        phi0 = jnp.full(tau.shape, n_keys - 1, jnp.int32)
        _, phi = lax.fori_loop(0, n_keys.bit_length() + 1, body, (plo0, phi0))
        return jnp.where(excess, phi, n_keys)

    any_excess = jnp.max(jnp.where(excess, 1.0, 0.0)) > 0.0
    pcut = lax.cond(any_excess, tie_break, lambda _: jnp.full(tau.shape, n_keys, jnp.int32), 0)
    pcut = jnp.where(need_sel, pcut, -1)
    return jnp.logical_or(score > tau, jnp.logical_and(eq, kpos <= pcut))


_MASKED = -1e30


def _select_topk_chunked(s_ref, nch, ck, k_sel, nvis, bisect_steps):
    rows = s_ref.shape[0]
    kf = float(k_sel)
    nfold = ck // LANES

    def chunk(c):
        return s_ref[:, pl.ds(pl.multiple_of(c * ck, ck), ck)]

    def fold(x, op):
        out = x[:, 0:LANES]
        for i in range(1, nfold):
            out = op(out, x[:, i * LANES:(i + 1) * LANES])
        return out

    def count(pred):
        def body(c, acc):
            return acc + fold(jnp.where(pred(chunk(c), c), 1.0, 0.0), jnp.add)
        acc = lax.fori_loop(0, nch, body, jnp.zeros((rows, LANES), F32))
        return jnp.sum(acc, axis=-1, keepdims=True)

    def row_max(val):
        def body(c, acc):
            return jnp.maximum(acc, fold(val(chunk(c)), jnp.maximum))
        acc = lax.fori_loop(0, nch, body, jnp.full((rows, LANES), _NEG_INF, F32))
        return jnp.max(acc, axis=-1, keepdims=True)

    need_sel = nvis > kf
    rowmax = jnp.where(need_sel, row_max(lambda x: x), 0.0)
    rowmin = jnp.where(need_sel, -row_max(lambda x: jnp.where(x > _NEG_INF, -x, _NEG_INF)), 0.0)

    def bis(_, c):
        lo, hi, chi = c
        hfin = jnp.where(hi == _POS_INF, rowmax, hi)
        piv = 0.5 * lo + 0.5 * hfin
        cnt = count(lambda x, _c: x >= piv)
        ge = cnt >= kf
        return jnp.where(ge, piv, lo), jnp.where(ge, hi, piv), jnp.where(ge, chi, cnt)

    lo, hi, chi = lax.fori_loop(0, bisect_steps, bis,
                                (rowmin, jnp.full_like(rowmin, _POS_INF), jnp.zeros_like(rowmin)))

    def walk_cond(c):
        return jnp.min(c[4]) < 0.5

    def walk(c):
        hi, chi, tau, cgt, done = c
        pending = done < 0.5
        bmax = row_max(lambda x: jnp.where(x < hi, x, _NEG_INF))
        cnt = count(lambda x, _c: x >= bmax)
        fin = jnp.logical_and(cnt >= kf, pending)
        tau = jnp.where(fin, bmax, tau)
        cgt = jnp.where(fin, chi, cgt)
        adv = jnp.logical_and(cnt < kf, pending)
        hi = jnp.where(adv, bmax, hi)
        chi = jnp.where(adv, cnt, chi)
        return hi, chi, tau, cgt, jnp.where(fin, 1.0, done)

    neg = jnp.full_like(rowmin, _NEG_INF)
    zero = jnp.zeros_like(rowmin)
    _, _, tau, cgt, _ = lax.while_loop(walk_cond, walk, (hi, chi, neg, zero, jnp.where(need_sel, 0.0, 1.0)))

    need = jnp.where(need_sel, kf - cgt, 0.0)
    tri = jnp.where(lax.broadcasted_iota(jnp.int32, (ck, ck), 0) <= lax.broadcasted_iota(jnp.int32, (ck, ck), 1),
                    1.0, 0.0).astype(BF16)

    def write(c, ties_before):
        x = chunk(c)
        eq = x == tau
        rank = jnp.dot(jnp.where(eq, 1.0, 0.0).astype(BF16), tri, preferred_element_type=F32) + ties_before
        sel = jnp.logical_or(x > tau, jnp.logical_and(eq, rank <= need))
        s_ref[:, pl.ds(pl.multiple_of(c * ck, ck), ck)] = jnp.where(sel, 0.0, _MASKED)
        return rank[:, ck - 1:ck]

    lax.fori_loop(0, nch, write, zero)


def _attn_prep_kernel(q_ref, qi_ref, kv_ref, kiw_ref, qg_ref, kg_ref, kn_o, qb_o, qib_o, kvb_o, kib_o):
    w = ATT_KV_HEADS * HEAD
    for h in range(ATT_KV_HEADS):
        hs = slice(h * HEAD, (h + 1) * HEAD)
        x = kv_ref[:, hs]
        kn = x * lax.rsqrt(jnp.mean(x * x, axis=-1, keepdims=True) + NORM_EPS) * kg_ref[...]
        kn_o[:, hs] = kn
        kvb_o[:, hs] = kn.astype(BF16)
        v = kv_ref[:, w + h * HEAD:w + (h + 1) * HEAD].astype(BF16)
        kvb_o[:, w + h * LANES:w + (h + 1) * LANES] = jnp.concatenate([v, jnp.ones_like(v)], axis=1)
    for h in range(ATT_HEADS):
        x = q_ref[:, h * HEAD:(h + 1) * HEAD]
        qn = x * lax.rsqrt(jnp.mean(x * x, axis=-1, keepdims=True) + NORM_EPS) * (qg_ref[...] * HEAD ** -0.5)
        qb_o[h] = qn.astype(BF16)
    qib_o[...] = (qi_ref[...] * HEAD ** -0.5).astype(BF16)
    kib_o[...] = kiw_ref[:, 0:HEAD].astype(BF16)


def _attn_prep(za, qn_g, kn_g, tm):
    m = za.shape[0]
    w = ATT_KV_HEADS * HEAD
    wi = IDX_HEADS * HEAD
    row = lambda width, blk: pl.BlockSpec((tm, width), lambda i, blk=blk: (i, blk))
    gspec = pl.BlockSpec((1, HEAD), lambda i: (0, 0))
    return pl.pallas_call(
        _attn_prep_kernel, grid=(m // tm,),
        in_specs=[row(D_MODEL, ZA_Q // D_MODEL), row(wi, ZA_QI // wi), row(2 * w, ZA_KV // (2 * w)),
                  row(LANES, ZA_KIW // LANES), gspec, gspec],
        out_specs=[row(w, 0), pl.BlockSpec((ATT_HEADS, tm, HEAD), lambda i: (0, i, 0)), row(wi, 0),
                   row(w + ATT_KV_HEADS * LANES, 0), row(HEAD, 0)],
        out_shape=[jax.ShapeDtypeStruct((m, w), F32), jax.ShapeDtypeStruct((ATT_HEADS, m, HEAD), BF16),
                   jax.ShapeDtypeStruct((m, wi), BF16), jax.ShapeDtypeStruct((m, w + ATT_KV_HEADS * LANES), BF16),
                   jax.ShapeDtypeStruct((m, HEAD), BF16)],
        compiler_params=_cparams(1), name="attn_prep")(za, za, za, za, qn_g.reshape(1, HEAD), kn_g.reshape(1, HEAD))


def _bucket_edges():
    max_exact = REL_BUCKETS // 2
    d = np.arange(REL_MAX_DIST + 1)
    df = np.maximum(d, 1).astype(np.float32)
    large = max_exact + (np.log(df / max_exact) / math.log(REL_MAX_DIST / max_exact)
                         * (REL_BUCKETS - max_exact)).astype(np.int32)
    bucket = np.where(d < max_exact, d, np.minimum(large, REL_BUCKETS - 1))
    return [int(np.argmax(bucket >= b)) for b in range(REL_BUCKETS)]


_BUCKET_EDGES = _bucket_edges()


def _rel_bias_lookup(dist, value_of_bucket):
    bias = value_of_bucket(REL_BUCKETS - 1)
    for b in range(REL_BUCKETS - 2, -1, -1):
        bias = jnp.where(dist < _BUCKET_EDGES[b + 1], value_of_bucket(b), bias)
    return bias


def _attn_prompt_kernel(rb_ref, q_ref, qi_ref, kiwq_ref, g_ref, kvb_ref, kib_ref, o_ref, s_scr, tbd_ref,
                        l_scr, mx_scr, acc_scr, *, tq, k_sel, bisect_steps):
    qt = pl.program_id(0)
    q0 = qt * tq
    ck = tq
    rg_rows = LANES
    n_rg = tq // rg_rows
    w = ATT_KV_HEADS * HEAD

    @pl.when(jnp.logical_and(pl.program_id(1) == 0, qt == 0))
    def _():
        rr = lax.broadcasted_iota(jnp.int32, (tq, tq), 0) - lax.broadcasted_iota(jnp.int32, (tq, tq), 1)
        for h in range(ATT_HEADS):
            far = rb_ref[REL_BUCKETS - 1, h]
            tbd_ref[h, 0] = _rel_bias_lookup(rr, lambda b: rb_ref[b, h]) - far
            tbd_ref[h, 1] = _rel_bias_lookup(rr + tq, lambda b: rb_ref[b, h]) - far

    def chunk_start(c):
        return pl.multiple_of(c * ck, ck)

    wcol = kiwq_ref[:, HEAD:HEAD + IDX_HEADS] * (IDX_HEADS ** -0.5)
    qi_h = [[qi_ref[rg * rg_rows:(rg + 1) * rg_rows, h * HEAD:(h + 1) * HEAD] for h in range(IDX_HEADS)]
            for rg in range(n_rg)]
    w_h = [[wcol[rg * rg_rows:(rg + 1) * rg_rows, h:h + 1] for h in range(IDX_HEADS)] for rg in range(n_rg)]

    def score_chunk(c, carry):
        k0 = chunk_start(c)
        kc = kib_ref[pl.ds(k0, ck), :]
        kpos = k0 + lax.broadcasted_iota(jnp.int32, (rg_rows, ck), 1)
        for rg in range(n_rg):
            acc = jnp.zeros((rg_rows, ck), F32)
            for h in range(IDX_HEADS):
                acc = acc + jnp.maximum(_dot_nt(qi_h[rg][h], kc), 0.0) * w_h[rg][h]
            qpos = q0 + rg * rg_rows + lax.broadcasted_iota(jnp.int32, (rg_rows, ck), 0)
            s_scr[rg * rg_rows:(rg + 1) * rg_rows, pl.ds(k0, ck)] = jnp.where(kpos <= qpos, acc, _NEG_INF)
        return carry

    nch = qt + 1
    lax.fori_loop(0, nch, score_chunk, 0)

    nvis = (q0 + 1 + lax.broadcasted_iota(jnp.int32, (tq, 1), 0)).astype(F32)
    _select_topk_chunked(s_scr, nch, ck, k_sel, nvis, bisect_steps)

    g_rows = ATT_GROUP * tq
    n_far = jnp.maximum(qt - 1, 0)
    for kvh in range(ATT_KV_HEADS):
        gs = slice(kvh * ATT_GROUP, (kvh + 1) * ATT_GROUP)
        qg = q_ref[gs, :, :].reshape(g_rows, HEAD)
        mx_scr[...] = jnp.full(mx_scr.shape, _MASKED, F32)
        acc_scr[...] = jnp.zeros(acc_scr.shape, F32)

        def logits_chunk(c, carry, near, qg=qg, gs=gs, kvh=kvh):
            k0 = chunk_start(c)
            kc = kvb_ref[pl.ds(k0, ck), kvh * HEAD:(kvh + 1) * HEAD]
            s3 = _dot_nt(qg, kc).reshape(ATT_GROUP, tq, ck) + s_scr[:, pl.ds(k0, ck)][None]
            if near:
                s3 = s3 + tbd_ref[gs, qt - c, :, :]
            s = s3.reshape(g_rows, ck)
            l_scr[:, pl.ds(k0, ck)] = s
            m = mx_scr[...]
            for i in range(ck // LANES):
                m = jnp.maximum(m, s[:, i * LANES:(i + 1) * LANES])
            mx_scr[...] = m
            return carry

        lax.fori_loop(0, n_far, functools.partial(logits_chunk, near=False), 0)
        lax.fori_loop(n_far, nch, functools.partial(logits_chunk, near=True), 0)
        mx_scr[...] = jnp.broadcast_to(jnp.max(mx_scr[...], axis=-1, keepdims=True), mx_scr.shape)

        def pv_chunk(c, carry, kvh=kvh):
            k0 = chunk_start(c)
            vx = kvb_ref[pl.ds(k0, ck), w + kvh * LANES:w + (kvh + 1) * LANES]
            m = mx_scr[...]
            p = jnp.concatenate([jnp.exp(l_scr[:, pl.ds(k0 + i * LANES, LANES)] - m)
                                 for i in range(ck // LANES)], axis=1).astype(BF16)
            acc_scr[...] += jnp.dot(p, vx, preferred_element_type=F32)
            return carry

        lax.fori_loop(0, nch, pv_chunk, 0)

        acc = acc_scr[...]
        o = acc[:, 0:HEAD] / acc[:, HEAD:HEAD + 1]
        for pp in range(ATT_GROUP // 2):
            cols = slice((kvh * ATT_GROUP // 2 + pp) * LANES, (kvh * ATT_GROUP // 2 + pp + 1) * LANES)
            gh = g_ref[:, cols]
            pair = jnp.concatenate([o[2 * pp * tq:(2 * pp + 1) * tq], o[(2 * pp + 1) * tq:(2 * pp + 2) * tq]], axis=1)
            o_ref[:, cols] = pair * (gh * _sigmoid(gh))


def _attn_prompt(za, kn, qb, qib, kvb, kib, *, rel_bias, batch, seq_len, tq, k_sel):
    m = za.shape[0]
    nq = seq_len // tq
    w = ATT_KV_HEADS * HEAD
    wi = IDX_HEADS * HEAD
    assert REL_MAX_DIST <= tq and tq % LANES == 0
    row_map = lambda blk: (lambda t, b, blk=blk: (b * nq + t, blk))
    key_map = lambda t, b: (b, 0)
    kern = functools.partial(_attn_prompt_kernel, tq=tq, k_sel=k_sel, bisect_steps=_BISECT_STEPS)
    return pl.pallas_call(
        kern, grid=(nq, batch),
        in_specs=[
            pl.BlockSpec(memory_space=pltpu.SMEM),
            pl.BlockSpec((ATT_HEADS, tq, HEAD), lambda t, b: (0, b * nq + t, 0)),
            pl.BlockSpec((tq, wi), row_map(0)),
            pl.BlockSpec((tq, LANES), row_map(ZA_KIW // LANES)),
            pl.BlockSpec((tq, D_MODEL), row_map(ZA_GATT // D_MODEL)),
            pl.BlockSpec((seq_len, w + ATT_KV_HEADS * LANES), key_map),
            pl.BlockSpec((seq_len, HEAD), key_map),
        ],
        out_specs=pl.BlockSpec((tq, D_MODEL), row_map(0)),
        out_shape=jax.ShapeDtypeStruct((m, D_MODEL), F32),
        scratch_shapes=[pltpu.VMEM((tq, seq_len), F32), pltpu.VMEM((ATT_HEADS, 2, tq, tq), F32),
                        pltpu.VMEM((ATT_GROUP * tq, seq_len), F32), pltpu.VMEM((ATT_GROUP * tq, LANES), F32),
                        pltpu.VMEM((ATT_GROUP * tq, LANES), F32)],
        compiler_params=_cparams(2), name="attn_prompt")(rel_bias, qb, qib, za, za, kvb, kib)


def _sample_score_kernel(pt_ref, q_ref, w_ref, kiw_new_ref, *rest, n_pages, page, dec_seq):
    page_refs = rest[:n_pages]
    o_ref = rest[n_pages]
    past = n_pages * page
    kidx = jnp.concatenate([r[...] for r in page_refs], axis=0).astype(BF16)
    new = kiw_new_ref[:, 0:HEAD]
    new = jnp.concatenate([new, jnp.zeros((LANES - dec_seq, HEAD), F32)], axis=0).astype(BF16)
    q = (q_ref[...] * (HEAD ** -0.5)).astype(BF16)
    lg = jnp.concatenate([_dot_nt(q, kidx), _dot_nt(q, new)], axis=1)
    wr = jnp.maximum(lg, 0.0) * (w_ref[...] * (IDX_HEADS ** -0.5))
    n_keys = past + LANES
    sc = jnp.sum(wr.reshape(dec_seq, IDX_HEADS, n_keys), axis=1)
    kpos = lax.broadcasted_iota(jnp.int32, (dec_seq, n_keys), 1)
    tpos = lax.broadcasted_iota(jnp.int32, (dec_seq, n_keys), 0)
    o_ref[...] = jnp.where(kpos <= past + tpos, sc, _NEG_INF)


def _sample_select_kernel(s_ref, o_ref, *, k_sel, bisect_steps):
    score = s_ref[...]
    kpos = lax.broadcasted_iota(jnp.int32, score.shape, 1)
    sel = _select_topk(score, kpos, k_sel, bisect_steps)
    o_ref[...] = jnp.where(sel, 0.0, _NEG_INF)


def _sample_attn_kernel(pt_ref, q_ref, g_ref, kn_new_ref, kv_new_ref, mask_ref, rb_rows_ref, qg_ref, *rest,
                        n_pages, page, dec_seq):
    k_pages = rest[:n_pages]
    v_pages = rest[n_pages:2 * n_pages]
    o_ref = rest[2 * n_pages]
    bias_ref = rest[2 * n_pages + 1]
    w = ATT_KV_HEADS * HEAD
    rows = dec_seq * ATT_HEADS
    n_keys = n_pages * page + LANES

    @pl.when(pl.program_id(0) == 0)
    def _():
        t_row = lax.div(lax.broadcasted_iota(jnp.int32, (rows, n_keys), 0), ATT_HEADS)
        dist = n_pages * page + t_row - lax.broadcasted_iota(jnp.int32, (rows, n_keys), 1)
        bias_ref[...] = _rel_bias_lookup(dist, lambda b: rb_rows_ref[:, b:b + 1])

    pad = jnp.zeros((LANES - dec_seq, w), F32)
    k_all = jnp.concatenate([r[...] for r in k_pages] + [kn_new_ref[...], pad], axis=0).astype(BF16)
    v_all = jnp.concatenate([r[...] for r in v_pages] + [kv_new_ref[:, w:2 * w], pad], axis=0).astype(BF16)
    q = q_ref[...]
    q = q * lax.rsqrt(jnp.mean(q * q, axis=-1, keepdims=True) + NORM_EPS) * qg_ref[...]
    qb = q.astype(BF16)
    head = lax.rem(lax.broadcasted_iota(jnp.int32, (rows, 1), 0), ATT_HEADS)
    first = head < ATT_GROUP
    lg = jnp.where(first, _dot_nt(qb, k_all[:, 0:HEAD]), _dot_nt(qb, k_all[:, HEAD:2 * HEAD]))
    mask = jnp.broadcast_to(mask_ref[...][:, None, :], (dec_seq, ATT_HEADS, n_keys)).reshape(rows, n_keys)
    s = lg * (HEAD ** -0.5) + bias_ref[...] + mask
    p = jnp.exp(s - jnp.max(s, axis=-1, keepdims=True))
    l = jnp.sum(p, axis=-1, keepdims=True)
    pb = p.astype(BF16)
    o = jnp.where(first, jnp.dot(pb, v_all[:, 0:HEAD], preferred_element_type=F32),
                  jnp.dot(pb, v_all[:, HEAD:2 * HEAD], preferred_element_type=F32)) / l
    g = g_ref[...]
    o_ref[...] = o * (g * _sigmoid(g))


def _attn_sample(za, kn, qb, qib, kvb, kib, *, cache_k, cache_v, cache_kidx, page_table, rel_bias, qn_g,
                 dec_seq, k_sel):
    nb, n_pages = page_table.shape
    n_phys, page = cache_k.shape[0], cache_k.shape[1]
    past = n_pages * page
    n_keys = past + LANES
    w = ATT_KV_HEADS * HEAD
    ck = cache_k.reshape(n_phys, page, w)
    cv = cache_v.reshape(n_phys, page, w)
    ci = cache_kidx.reshape(n_phys, page, HEAD)
    za3 = za.reshape(nb, dec_seq, ZA_COLS)
    kn3 = kn.reshape(nb, dec_seq, w)
    qi = za[:, ZA_QI:ZA_QI + IDX_HEADS * HEAD].reshape(nb, dec_seq * IDX_HEADS, HEAD)
    wi = za[:, ZA_KIW + HEAD:ZA_KIW + HEAD + IDX_HEADS].reshape(nb, dec_seq * IDX_HEADS, 1)
    qa = za[:, ZA_Q:ZA_Q + D_MODEL].reshape(nb, dec_seq * ATT_HEADS, HEAD)
    ga = za[:, ZA_GATT:ZA_GATT + D_MODEL].reshape(nb, dec_seq * ATT_HEADS, HEAD)

    def page_specs(width):
        return [pl.BlockSpec((None, page, width), lambda b, pt, j=j: (pt[b, j], 0, 0)) for j in range(n_pages)]

    per_b = lambda r, c: pl.BlockSpec((None, r, c), lambda b, pt: (b, 0, 0))
    kiw_new = pl.BlockSpec((None, dec_seq, LANES), lambda b, pt: (b, 0, ZA_KIW // LANES))

    scores = pl.pallas_call(
        functools.partial(_sample_score_kernel, n_pages=n_pages, page=page, dec_seq=dec_seq),
        grid_spec=pltpu.PrefetchScalarGridSpec(
            num_scalar_prefetch=1, grid=(nb,),
            in_specs=[per_b(dec_seq * IDX_HEADS, HEAD), per_b(dec_seq * IDX_HEADS, 1), kiw_new] + page_specs(HEAD),
            out_specs=per_b(dec_seq, n_keys)),
        out_shape=jax.ShapeDtypeStruct((nb, dec_seq, n_keys), F32),
        compiler_params=_cparams(1), name="sample_scores")(page_table, qi, wi, za3, *([ci] * n_pages))

    rows = nb * dec_seq
    tr = min(rows, 128)
    mask = pl.pallas_call(
        functools.partial(_sample_select_kernel, k_sel=k_sel, bisect_steps=_BISECT_STEPS),
        grid=(rows // tr,),
        in_specs=[pl.BlockSpec((tr, n_keys), lambda i: (i, 0))],
        out_specs=pl.BlockSpec((tr, n_keys), lambda i: (i, 0)),
        out_shape=jax.ShapeDtypeStruct((rows, n_keys), F32),
        compiler_params=_cparams(1), name="sample_select")(scores.reshape(rows, n_keys))

    rb_rows = jnp.tile(rel_bias.T, (dec_seq, 1))

    const = lambda shape: pl.BlockSpec(shape, lambda b, pt: (0,) * len(shape))
    kv_new = pl.BlockSpec((None, dec_seq, 2 * w), lambda b, pt: (b, 0, ZA_KV // (2 * w)))
    out = pl.pallas_call(
        functools.partial(_sample_attn_kernel, n_pages=n_pages, page=page, dec_seq=dec_seq),
        grid_spec=pltpu.PrefetchScalarGridSpec(
            num_scalar_prefetch=1, grid=(nb,),
            in_specs=[per_b(dec_seq * ATT_HEADS, HEAD), per_b(dec_seq * ATT_HEADS, HEAD), per_b(dec_seq, w),
                      kv_new, per_b(dec_seq, n_keys), const((dec_seq * ATT_HEADS, REL_BUCKETS)), const((1, HEAD))]
            + page_specs(w) + page_specs(w),
            out_specs=per_b(dec_seq * ATT_HEADS, HEAD),
            scratch_shapes=[pltpu.VMEM((dec_seq * ATT_HEADS, n_keys), F32)]),
        out_shape=jax.ShapeDtypeStruct((nb, dec_seq * ATT_HEADS, HEAD), F32),
        compiler_params=_cparams(1), name="sample_attn")(
            page_table, qa, ga, kn3, za3, mask.reshape(nb, dec_seq, n_keys), rb_rows, qn_g.reshape(1, HEAD),
            *([ck] * n_pages), *([cv] * n_pages))
    return out.reshape(rows, D_MODEL)


def _merge_kernel(x_ref, oa_ref, sg_ref, ob_ref, ga_ref, gb_ref, wpa_ref, wpb_ref, wo_ref, y_ref):
    oa = (oa_ref[...] * sg_ref[...]).astype(BF16)
    pa = jnp.dot(oa, wpa_ref[...], preferred_element_type=F32)
    pb = jnp.dot(ob_ref[...].astype(BF16), wpb_ref[...], preferred_element_type=F32)
    merged = _sigmoid(ga_ref[...]) * pa + _sigmoid(gb_ref[...]) * pb
    y_ref[...] = x_ref[...] + jnp.dot(merged.astype(BF16), wo_ref[...], preferred_element_type=F32)


def _merge(x, oa, sg, ob, za, w_pa, w_pb, w_out, tm):
    m, d = x.shape
    row = pl.BlockSpec((tm, d), lambda i: (i, 0))
    wsp = pl.BlockSpec((d, d), lambda i: (0, 0))
    return pl.pallas_call(
        _merge_kernel, grid=(m // tm,),
        in_specs=[row, row, row, row,
                  pl.BlockSpec((tm, d), lambda i: (i, ZA_GA // d)), pl.BlockSpec((tm, d), lambda i: (i, ZA_GB // d)),
                  wsp, wsp, wsp],
        out_specs=row,
        out_shape=jax.ShapeDtypeStruct((m, d), F32),
        compiler_params=_cparams(1), name="merge")(x, oa, sg, ob, za, za, w_pa, w_pb, w_out)


def _layer(x, shift, s0, params, *, batch, seq_len, attend):
    (norm_g, w_r, w_a, mu, w0, w2, a0, a2, k_k, k_a, r_k, lnx_g, lnx_b, qn_g, kn_g, w_pa, w_pb, w_out) = params
    m = batch * seq_len
    tm = min(m, 512)
    tm_proj = min(m, 2048)
    zr = _norm_matmul(x, norm_g, w_r, tm_proj, RWKV_COLS // 3, "inproj_rwkv")
    za = _norm_matmul(x, norm_g, w_a, tm_proj, ZA_COLS // 4, "inproj_attn")

    tile = min(m, 256)
    r, w, k, v, a, sg = _rwkv_prep(zr, shift, mu, w0, w2, a0, a2, batch=batch, seq_len=seq_len, tile=tile)
    relayout_tile = 128
    in_kernel_relayout = seq_len % relayout_tile == 0 and batch * RWKV_HEADS == LANES
    if in_kernel_relayout:
        seqs = [_to_pairs(t, batch, seq_len, relayout_tile) for t in (r, w, k, v, a)]
    else:
        seqs = [_to_pairs_xla(t, batch, seq_len) for t in (r, w, k, v, a)]
    row_consts = [_head_const_pairs(t, batch) for t in (lnx_g, lnx_b)]
    step_consts = [jnp.broadcast_to(_head_const_pairs(t, batch)[:, None, :], (HEAD, SUBLANES, batch * RWKV_HEADS))
                   for t in (r_k.reshape(-1), k_k, k_a)]
    o_t, s_t = _rwkv_scan(seqs, s0, row_consts, step_consts, seq_len=seq_len, steps=min(seq_len, 32))
    if in_kernel_relayout:
        oa = _from_pairs(o_t, batch, seq_len, relayout_tile)
    else:
        oa = _from_pairs_xla(o_t, batch, seq_len)

    kn, qb, qib, kvb, kib = _attn_prep(za, qn_g, kn_g, tm)
    ob = attend(za, kn, qb, qib, kvb, kib)

    y = _merge(x, oa, sg, ob, za, w_pa, w_pb, w_out, min(m, 256))
    return y, zr, za, kn, s_t


def _state_to_pairs(s):
    b, h, n, _ = s.shape
    return jnp.transpose(s, (3, 2, 0, 1)).reshape(n, n, b * h)


def _state_from_pairs(s, batch):
    n = s.shape[0]
    return jnp.transpose(s.reshape(n, n, batch, RWKV_HEADS), (2, 3, 1, 0))


def kernel(x_prompt, x_sample, cache_k, cache_v, cache_kidx, state_wkv, state_shift, page_table, norm_g, w_in,
           shift_mu, w0, w2, a0, a2, k_k, k_a, r_k, lnx_g, lnx_b, q_norm_g, k_norm_g, rel_bias, w_pa, w_pb, w_out):
    bsz, seq, d = x_prompt.shape
    dec_bsz, dec_seq, _ = x_sample.shape
    depth = w_in.shape[0]
    assert depth == 1 and d == D_MODEL
    past_len = page_table.shape[1] * cache_k.shape[2]
    topk_p = min(TOPK_MAX, seq // 4)
    topk_s = min(TOPK_MAX, (past_len + dec_seq) // 4)
    l = 0

    wl = w_in[l]
    c0 = RWKV_COLS
    q_w, kv_w, qi_w = wl[:, c0:c0 + 1024], wl[:, c0 + 1024:c0 + 1280], wl[:, c0 + 1280:c0 + 1792]
    kiw_w = wl[:, c0 + 1792:c0 + 1864]
    rest_w = wl[:, c0 + 1864:]
    zpad = lambda n: jnp.zeros((d, n), wl.dtype)
    w_a = jnp.concatenate([q_w, qi_w, kv_w, kiw_w, zpad(LANES - kiw_w.shape[1]), zpad(LANES), rest_w],
                          axis=1).astype(BF16)
    assert w_a.shape[1] == ZA_COLS
    def rwkv_cols(x, fn):
        lead = x.shape[:-1]
        main = fn(x[..., :4 * d].reshape(lead + (4, d)), len(lead) + 1).reshape(lead + (4 * d,))
        return jnp.concatenate([main, x[..., 4 * d:]], axis=-1)

    w_r = rwkv_cols(wl[:, :c0], _nh_order).astype(BF16)
    params = (norm_g[l], w_r, w_a, rwkv_cols(shift_mu[l], _nh_order), _nh_order(w0[l], 0), _nh_order(w2[l], 1),
              _nh_order(a0[l], 0), _nh_order(a2[l], 1), k_k[l], k_a[l], r_k[l],
              lnx_g[l], lnx_b[l], q_norm_g[l], k_norm_g[l],
              _nh_order(w_pa[l], 0).astype(BF16), w_pb[l].astype(BF16), w_out[l].astype(BF16))

    xp = x_prompt.reshape(bsz * seq, d)
    attend_p = functools.partial(_attn_prompt, rel_bias=rel_bias, batch=bsz, seq_len=seq,
                                 tq=min(seq, 256), k_sel=topk_p)
    yp, zr_p, za_p, kn_p, st_p = _layer(
        xp, jnp.zeros((bsz, 1, RWKV_COLS), F32), jnp.zeros((HEAD, HEAD, bsz * RWKV_HEADS), F32), params,
        batch=bsz, seq_len=seq, attend=attend_p)

    xs = x_sample.reshape(dec_bsz * dec_seq, d)
    attend_s = functools.partial(_attn_sample, cache_k=cache_k[l], cache_v=cache_v[l], cache_kidx=cache_kidx[l],
                                 page_table=page_table, rel_bias=rel_bias, qn_g=q_norm_g[l], dec_seq=dec_seq,
                                 k_sel=topk_s)
    shift_rows = jnp.repeat(rwkv_cols(state_shift[l], _nh_order), dec_seq, axis=0)
    ys, zr_s, za_s, kn_s, st_s = _layer(
        xs, shift_rows, _state_to_pairs(state_wkv[l]), params, batch=dec_bsz, seq_len=dec_seq, attend=attend_s)

    w = ATT_KV_HEADS * HEAD

    def pack(y, zr, za, kn, st, b, t):
        v = za[:, ZA_KV + w:ZA_KV + 2 * w]
        kidx = za[:, ZA_KIW:ZA_KIW + HEAD]
        return (y.reshape(b, t, d),
                kn.reshape(1, b, t, ATT_KV_HEADS, HEAD), v.reshape(1, b, t, ATT_KV_HEADS, HEAD),
                kidx.reshape(1, b, t, HEAD), _state_from_pairs(st, b)[None],
                rwkv_cols(zr.reshape(b, t, RWKV_COLS)[:, -1], _hn_order)[None])

    p = pack(yp, zr_p, za_p, kn_p, st_p, bsz, seq)
    s = pack(ys, zr_s, za_s, kn_s, st_s, dec_bsz, dec_seq)
    return (p[0], s[0]) + p[1:] + s[1:]
```
